```python
import math
import jax
import jax.numpy as jnp
from jax import lax
import numpy as np

D_MODEL = 1024
BATCH = 8
SEQ = 2048
DEPTH = 2

RMS_EPS = 1e-6
CONV_WIDTH = 4
N_BRANCH = 3

LRU_WIDTH = 1024
LRU_BLOCKS = 16
LRU_BLOCK_DIM = LRU_WIDTH // LRU_BLOCKS
LRU_C = 8.0

GDN_HEADS = 8
GDN_DK = 128
GDN_DV = 128
GDN_QK_WIDTH = GDN_HEADS * GDN_DK
GDN_V_WIDTH = GDN_HEADS * GDN_DV
GDN_CHUNK = 64

RWKV_HEADS = 16
RWKV_HD = 64
RWKV_WIDTH = RWKV_HEADS * RWKV_HD
RWKV_DECAY_RANK = 64
RWKV_ICLR_RANK = 64
RWKV_GATE_RANK = 128
RWKV_GN_EPS = 64e-5
RWKV_SIZES = (RWKV_WIDTH, RWKV_WIDTH, RWKV_WIDTH, RWKV_DECAY_RANK, RWKV_ICLR_RANK, RWKV_GATE_RANK)
RWKV_IN_WIDTH = sum(RWKV_SIZES)
RWKV_OFFSETS = tuple(sum(RWKV_SIZES[:i + 1]) for i in range(len(RWKV_SIZES) - 1))

IN_SIZES = (LRU_WIDTH, LRU_WIDTH, 2 * GDN_QK_WIDTH + GDN_V_WIDTH, GDN_V_WIDTH, GDN_HEADS, GDN_HEADS, RWKV_IN_WIDTH, N_BRANCH * D_MODEL)
IN_WIDTH = sum(IN_SIZES)
IN_OFFSETS = tuple(sum(IN_SIZES[:i + 1]) for i in range(len(IN_SIZES) - 1))

PEER_HEADS = 8
PEER_NKEYS = 128
PEER_EXPERTS = PEER_NKEYS * PEER_NKEYS
PEER_DKEY = 256
PEER_HALF = PEER_DKEY // 2
PEER_TOPK = 16
PEER_BLOCK = 128

kernel_name = 'hybrid_rglru_gdn_rwkv7_peer_trunk'


def rmsnorm(x, w):
    xf = x.astype(jnp.float32)
    y = xf * lax.rsqrt(jnp.mean(xf * xf, axis=-1, keepdims=True) + RMS_EPS)
    return (y * w.astype(jnp.float32)).astype(x.dtype)


def l2norm(x):
    return x * lax.rsqrt(jnp.sum(x * x, axis=-1, keepdims=True) + RMS_EPS)


def causal_depthwise_conv(x, w):
    taps, seq = w.shape[0], x.shape[1]
    xp = jnp.pad(x, ((0, 0), (taps - 1, 0), (0, 0)))
    return sum(xp[:, j:j + seq] * w[j] for j in range(taps))


def token_shift(z, mu):
    z_prev = jnp.pad(z, ((0, 0), (1, 0), (0, 0)))[:, :-1]
    return z + mu * (z_prev - z)


def rglru_branch(u, gate, conv_w, conv_b, w_a, b_a, w_x, b_x, lam):
    bsz, seq, _ = u.shape
    f32 = jnp.float32
    xc = causal_depthwise_conv(u, conv_w) + conv_b
    xh = xc.reshape(bsz, seq, LRU_BLOCKS, LRU_BLOCK_DIM)
    r = jax.nn.sigmoid(jnp.einsum('btni,nij->btnj', xh, w_a).reshape(bsz, seq, LRU_WIDTH) + b_a)
    i = jax.nn.sigmoid(jnp.einsum('btni,nij->btnj', xh, w_x).reshape(bsz, seq, LRU_WIDTH) + b_x)
    log_a = (-LRU_C * r.astype(f32)) * jax.nn.softplus(-lam.astype(f32))
    a = jnp.exp(log_a)
    b = jnp.sqrt(-jnp.expm1(2.0 * log_a)) * (i * xc).astype(f32)

    def combine(left, right):
        a_l, b_l = left
        a_r, b_r = right
        return a_l * a_r, a_r * b_l + b_r

    _, h = lax.associative_scan(combine, (a, b), axis=1)
    return h.astype(u.dtype) * jax.nn.gelu(gate)


def chunked_gated_delta_rule(q, k, v, beta, g):
    bsz, seq, heads, dk = q.shape
    dv = v.shape[-1]
    c = GDN_CHUNK
    n = seq // c

    def to_chunks(t):
        t = t.reshape((bsz, n, c, heads) + t.shape[3:])
        return jnp.moveaxis(t, 3, 1)

    q, k, v, beta, g = (to_chunks(t) for t in (q, k, v, beta, g))
    g = jnp.cumsum(g, axis=-1)
    incl = jnp.tril(jnp.ones((c, c), dtype=bool))
    strict = jnp.tril(jnp.ones((c, c), dtype=bool), -1)
    decay = jnp.exp(jnp.where(incl, g[..., :, None] - g[..., None, :], -jnp.inf))
    k_beta = k * beta[..., None]
    a_mat = jnp.where(strict, jnp.einsum('bhnik,bhnjk->bhnij', k_beta, k) * decay, 0.0)
    lower = a_mat + jnp.eye(c, dtype=a_mat.dtype)
    rhs = jnp.concatenate([v * beta[..., None], k_beta * jnp.exp(g)[..., None]], axis=-1)
    sol = lax.linalg.triangular_solve(lower, rhs, left_side=True, lower=True, unit_diagonal=True)
    u_c, w_c = sol[..., :dv], sol[..., dv:]
    qk = jnp.einsum('bhnik,bhnjk->bhnij', q, k) * decay
    q_dec = q * jnp.exp(g)[..., None]
    k_dec = k * jnp.exp(g[..., -1:] - g)[..., None]
    g_tot = jnp.exp(g[..., -1])

    def step(state, inp):
        qk_n, qd_n, kd_n, u_n, w_n, gt_n = inp
        v_new = u_n - jnp.einsum('bhck,bhkv->bhcv', w_n, state)
        o = jnp.einsum('bhck,bhkv->bhcv', qd_n, state) + jnp.einsum('bhcs,bhsv->bhcv', qk_n, v_new)
        state = state * gt_n[..., None, None] + jnp.einsum('bhck,bhcv->bhkv', kd_n, v_new)
        return state, o

    xs = tuple(jnp.moveaxis(t, 2, 0) for t in (qk, q_dec, k_dec, u_c, w_c, g_tot))
    s0 = jnp.zeros((bsz, heads, dk, dv), q.dtype)
    _, o = lax.scan(step, s0, xs)
    return jnp.transpose(o, (1, 0, 3, 2, 4)).reshape(bsz, seq, heads, dv)


def gdn_branch(qkv, z, alpha, beta_pre, conv_w, a_log, dt_bias, norm_w):
    bsz, seq, _ = qkv.shape
    f32 = jnp.float32
    qkv = jax.nn.silu(causal_depthwise_conv(qkv, conv_w)).astype(f32)
    q, k, v = jnp.split(qkv, (GDN_QK_WIDTH, 2 * GDN_QK_WIDTH), axis=-1)
    q = l2norm(q.reshape(bsz, seq, GDN_HEADS, GDN_DK)) * (GDN_DK ** -0.5)
    k = l2norm(k.reshape(bsz, seq, GDN_HEADS, GDN_DK))
    v = v.reshape(bsz, seq, GDN_HEADS, GDN_DV)
    beta = jax.nn.sigmoid(beta_pre.astype(f32))
    g = -jnp.exp(a_log.astype(f32)) * jax.nn.softplus(alpha.astype(f32) + dt_bias.astype(f32))
    o = chunked_gated_delta_rule(q, k, v, beta, g)
    o = o * lax.rsqrt(jnp.mean(o * o, axis=-1, keepdims=True) + RMS_EPS) * norm_w.astype(f32)
    o = o * jax.nn.silu(z.astype(f32).reshape(bsz, seq, GDN_HEADS, GDN_DV))
    return o.reshape(bsz, seq, GDN_V_WIDTH).astype(z.dtype)


def rwkv7_branch(z_rw, mu, w0, w_up, a0, a_up, g_up, k_k, k_a, r_k, lnx_w, lnx_b):
    bsz, seq, _ = z_rw.shape
    f32 = jnp.float32
    zs = token_shift(z_rw, mu)
    r, k, v, wl, al, gl = jnp.split(zs, RWKV_OFFSETS, axis=-1)
    w_log = -jax.nn.softplus(-(w0 + jnp.tanh(wl) @ w_up)) - 0.5
    decay = jnp.exp(-jnp.exp(w_log.astype(f32)))
    a = jax.nn.sigmoid(a0 + al @ a_up)
    gate = jax.nn.sigmoid(gl) @ g_up
    heads = lambda t: t.astype(f32).reshape(bsz, seq, RWKV_HEADS, RWKV_HD)
    kk = l2norm(heads(k * k_k))
    k = k * (1.0 + (a - 1.0) * k_a)
    r_h, k_h, v_h, a_h, w_h = heads(r), heads(k), heads(v), heads(a), heads(decay)

    def step(state, inp):
        r_t, w_t, k_t, v_t, kk_t, kka_t = inp
        s_kk = jnp.einsum('bhvk,bhk->bhv', state, kk_t)
        state = state * w_t[:, :, None, :] - s_kk[..., None] * kka_t[:, :, None, :] + v_t[..., None] * k_t[:, :, None, :]
        return state, jnp.einsum('bhvk,bhk->bhv', state, r_t)

    xs = tuple(jnp.swapaxes(t, 0, 1) for t in (r_h, w_h, k_h, v_h, kk, kk * a_h))
    s0 = jnp.zeros((bsz, RWKV_HEADS, RWKV_HD, RWKV_HD), f32)
    _, o = lax.scan(step, s0, xs)
    o = jnp.swapaxes(o, 0, 1)
    mean = jnp.mean(o, axis=-1, keepdims=True)
    var = jnp.mean(jnp.square(o - mean), axis=-1, keepdims=True)
    o = ((o - mean) * lax.rsqrt(var + RWKV_GN_EPS)).reshape(bsz, seq, RWKV_WIDTH)
    o = o * lnx_w.astype(f32) + lnx_b.astype(f32)
    bonus = jnp.sum(r_h * k_h * r_k.astype(f32), axis=-1, keepdims=True) * v_h
    o = (o + bonus.reshape(bsz, seq, RWKV_WIDTH)) * gate.astype(f32)
    return o.astype(z_rw.dtype)


def hybrid_mixer(xn, w_in, lru_conv_w, lru_conv_b, lru_w_a, lru_b_a, lru_w_x, lru_b_x, lru_lambda,
                 gdn_conv_w, gdn_a_log, gdn_dt_bias, gdn_norm_w,
                 rwkv_mu, rwkv_w0, rwkv_w_up, rwkv_a0, rwkv_a_up, rwkv_g_up, rwkv_k_k, rwkv_k_a,
                 rwkv_r_k, rwkv_lnx_w, rwkv_lnx_b, merge_b, p_lru, p_gdn, p_rwkv, w_out):
    z = xn @ w_in
    u_lru, g_lru, qkv, z_gdn, alpha, beta_pre, z_rw, z_merge = jnp.split(z, IN_OFFSETS, axis=-1)
    y_a = rglru_branch(u_lru, g_lru, lru_conv_w, lru_conv_b, lru_w_a, lru_b_a, lru_w_x, lru_b_x, lru_lambda)
    y_b = gdn_branch(qkv, z_gdn, alpha, beta_pre, gdn_conv_w, gdn_a_log, gdn_dt_bias, gdn_norm_w)
    y_c = rwkv7_branch(z_rw, rwkv_mu, rwkv_w0, rwkv_w_up, rwkv_a0, rwkv_a_up, rwkv_g_up, rwkv_k_k,
                       rwkv_k_a, rwkv_r_k, rwkv_lnx_w, rwkv_lnx_b)
    g_a, g_b, g_c = jnp.split(jax.nn.sigmoid(z_merge + merge_b), N_BRANCH, axis=-1)
    merged = g_a * (y_a @ p_lru) + g_b * (y_b @ p_gdn) + g_c * (y_c @ p_rwkv)
    return merged @ w_out


def peer_ffn(xn, w_query, sub_keys, expert_u, expert_v):
    bsz, seq, dim = xn.shape
    q = (xn @ w_query).reshape(bsz, seq, PEER_HEADS, 2, PEER_HALF)
    s = jnp.einsum('bthsc,hsnc->bthsn', q, sub_keys).astype(jnp.float32)
    top_s, top_i = lax.top_k(s, PEER_TOPK)
    cand_s = (top_s[..., 0, :, None] + top_s[..., 1, None, :]).reshape(bsz, seq, PEER_HEADS, PEER_TOPK * PEER_TOPK)
    cand_i = (top_i[..., 0, :, None] * PEER_NKEYS + top_i[..., 1, None, :]).reshape(bsz, seq, PEER_HEADS, PEER_TOPK * PEER_TOPK)
    best_s, best_pos = lax.top_k(cand_s, PEER_TOPK)
    expert_idx = jnp.take_along_axis(cand_i, best_pos, axis=-1)
    gate = jax.nn.softmax(best_s, axis=-1)
    n_blocks = (bsz * seq) // PEER_BLOCK
    x_blk = xn.reshape(n_blocks, PEER_BLOCK, dim)
    i_blk = expert_idx.reshape(n_blocks, PEER_BLOCK, PEER_HEADS * PEER_TOPK)
    g_blk = gate.reshape(n_blocks, PEER_BLOCK, PEER_HEADS * PEER_TOPK)

    def block(args):
        xb, ib, gb = args
        u = jnp.take(expert_u, ib, axis=0)
        act = gb.astype(xb.dtype) * jax.nn.gelu(jnp.einsum('pkd,pd->pk', u, xb))
        vv = jnp.take(expert_v, ib, axis=0)
        return jnp.einsum('pk,pkd->pd', act, vv)

    y = lax.map(block, (x_blk, i_blk, g_blk))
    return y.reshape(bsz, seq, dim)


def setup_inputs(seed: int = 0) -> dict:
    key = jax.random.key(seed)
    ks = iter(jax.random.split(key, 48))
    f32 = jnp.float32
    L = DEPTH

    def normal(shape, scale):
        return scale * jax.random.normal(next(ks), shape, f32)

    def uniform(shape, lo, hi):
        return jax.random.uniform(next(ks), shape, f32, lo, hi)

    x = normal((BATCH, SEQ, D_MODEL), 1.0)
    norm_mix_w = 1.0 + normal((L, D_MODEL), 0.02)
    norm_ffn_w = 1.0 + normal((L, D_MODEL), 0.02)
    final_norm_w = 1.0 + normal((D_MODEL,), 0.02)
    w_in = normal((L, D_MODEL, IN_WIDTH), D_MODEL ** -0.5)
    lru_conv_w = normal((L, CONV_WIDTH, LRU_WIDTH), CONV_WIDTH ** -0.5)
    lru_conv_b = normal((L, LRU_WIDTH), 0.02)
    lru_w_a = normal((L, LRU_BLOCKS, LRU_BLOCK_DIM, LRU_BLOCK_DIM), LRU_BLOCK_DIM ** -0.5)
    lru_b_a = normal((L, LRU_WIDTH), 0.02)
    lru_w_x = normal((L, LRU_BLOCKS, LRU_BLOCK_DIM, LRU_BLOCK_DIM), LRU_BLOCK_DIM ** -0.5)
    lru_b_x = normal((L, LRU_WIDTH), 0.02)
    a_pow = uniform((L, LRU_WIDTH), 0.9, 0.999) ** (1.0 / LRU_C)
    lru_lambda = jnp.log(a_pow) - jnp.log1p(-a_pow)
    gdn_conv_w = normal((L, CONV_WIDTH, 2 * GDN_QK_WIDTH + GDN_V_WIDTH), CONV_WIDTH ** -0.5)
    gdn_a_log = jnp.log(uniform((L, GDN_HEADS), 1.0, 16.0))
    dt = jnp.exp(uniform((L, GDN_HEADS), math.log(1e-3), math.log(1e-1)))
    gdn_dt_bias = dt + jnp.log(-jnp.expm1(-dt))
    gdn_norm_w = 1.0 + normal((L, GDN_DV), 0.02)
    rwkv_mu = uniform((L, RWKV_IN_WIDTH), 0.0, 1.0)
    rwkv_w0 = uniform((L, RWKV_WIDTH), -6.5, -1.5)
    rwkv_w_up = normal((L, RWKV_DECAY_RANK, RWKV_WIDTH), 0.5 * RWKV_DECAY_RANK ** -0.5)
    rwkv_a0 = normal((L, RWKV_WIDTH), 0.1)
    rwkv_a_up = normal((L, RWKV_ICLR_RANK, RWKV_WIDTH), 0.5 * RWKV_ICLR_RANK ** -0.5)
    rwkv_g_up = normal((L, RWKV_GATE_RANK, RWKV_WIDTH), RWKV_GATE_RANK ** -0.5)
    rwkv_k_k = 0.85 + normal((L, RWKV_WIDTH), 0.02)
    rwkv_k_a = 1.0 + normal((L, RWKV_WIDTH), 0.02)
    rwkv_r_k = normal((L, RWKV_HEADS, RWKV_HD), 0.1)
    rwkv_lnx_w = 1.0 + normal((L, RWKV_WIDTH), 0.02)
    rwkv_lnx_b = normal((L, RWKV_WIDTH), 0.02)
    merge_b = normal((L, N_BRANCH * D_MODEL), 0.02)
    p_lru = normal((L, LRU_WIDTH, D_MODEL), LRU_WIDTH ** -0.5)
    p_gdn = normal((L, GDN_V_WIDTH, D_MODEL), GDN_V_WIDTH ** -0.5)
    p_rwkv = normal((L, RWKV_WIDTH, D_MODEL), RWKV_WIDTH ** -0.5)
    w_out = normal((L, D_MODEL, D_MODEL), D_MODEL ** -0.5)
    peer_wq = normal((L, D_MODEL, PEER_HEADS * PEER_DKEY), D_MODEL ** -0.5)
    peer_keys = normal((L, PEER_HEADS, 2, PEER_NKEYS, PEER_HALF), PEER_HALF ** -0.5)
    peer_u = normal((L, PEER_EXPERTS, D_MODEL), D_MODEL ** -0.5)
    peer_v = normal((L, PEER_EXPERTS, D_MODEL), PEER_TOPK ** -0.5)
    return {'x': x, 'norm_mix_w': norm_mix_w, 'norm_ffn_w': norm_ffn_w, 'final_norm_w': final_norm_w,
            'w_in': w_in, 'lru_conv_w': lru_conv_w, 'lru_conv_b': lru_conv_b, 'lru_w_a': lru_w_a,
            'lru_b_a': lru_b_a, 'lru_w_x': lru_w_x, 'lru_b_x': lru_b_x, 'lru_lambda': lru_lambda,
            'gdn_conv_w': gdn_conv_w, 'gdn_a_log': gdn_a_log, 'gdn_dt_bias': gdn_dt_bias,
            'gdn_norm_w': gdn_norm_w, 'rwkv_mu': rwkv_mu, 'rwkv_w0': rwkv_w0, 'rwkv_w_up': rwkv_w_up,
            'rwkv_a0': rwkv_a0, 'rwkv_a_up': rwkv_a_up, 'rwkv_g_up': rwkv_g_up, 'rwkv_k_k': rwkv_k_k,
            'rwkv_k_a': rwkv_k_a, 'rwkv_r_k': rwkv_r_k, 'rwkv_lnx_w': rwkv_lnx_w, 'rwkv_lnx_b': rwkv_lnx_b,
            'merge_b': merge_b, 'p_lru': p_lru, 'p_gdn': p_gdn, 'p_rwkv': p_rwkv, 'w_out': w_out,
            'peer_wq': peer_wq, 'peer_keys': peer_keys, 'peer_u': peer_u, 'peer_v': peer_v}


def reference(x, norm_mix_w, norm_ffn_w, final_norm_w, w_in, lru_conv_w, lru_conv_b, lru_w_a,
              lru_b_a, lru_w_x, lru_b_x, lru_lambda, gdn_conv_w, gdn_a_log, gdn_dt_bias, gdn_norm_w,
              rwkv_mu, rwkv_w0, rwkv_w_up, rwkv_a0, rwkv_a_up, rwkv_g_up, rwkv_k_k, rwkv_k_a,
              rwkv_r_k, rwkv_lnx_w, rwkv_lnx_b, merge_b, p_lru, p_gdn, p_rwkv, w_out,
              peer_wq, peer_keys, peer_u, peer_v):
    h = x
    for l in range(DEPTH):
        xn = rmsnorm(h, norm_mix_w[l])
        h = h + hybrid_mixer(xn, w_in[l], lru_conv_w[l], lru_conv_b[l], lru_w_a[l], lru_b_a[l],
                             lru_w_x[l], lru_b_x[l], lru_lambda[l], gdn_conv_w[l], gdn_a_log[l],
                             gdn_dt_bias[l], gdn_norm_w[l], rwkv_mu[l], rwkv_w0[l], rwkv_w_up[l],
                             rwkv_a0[l], rwkv_a_up[l], rwkv_g_up[l], rwkv_k_k[l], rwkv_k_a[l],
                             rwkv_r_k[l], rwkv_lnx_w[l], rwkv_lnx_b[l], merge_b[l], p_lru[l],
                             p_gdn[l], p_rwkv[l], w_out[l])
        xn = rmsnorm(h, norm_ffn_w[l])
        h = h + peer_ffn(xn, peer_wq[l], peer_keys[l], peer_u[l], peer_v[l])
    return rmsnorm(h, final_norm_w)
```

```python
import functools

import jax
import jax.numpy as jnp
from jax import lax
from jax.experimental import pallas as pl
from jax.experimental.pallas import tpu as pltpu

F32 = jnp.float32
BF16 = jnp.bfloat16
HIGHEST = lax.Precision.HIGHEST

D_MODEL = 1024
DEPTH = 2
RMS_EPS = 1e-6

LRU_WIDTH = 1024
LRU_BLOCK_DIM = 64
LRU_C = 8.0
LRU_GROUP = 256

GDN_HEADS = 8
GDN_DK = 128
GDN_DV = 128
GDN_CHUNK = 64

RWKV_HEADS = 16
RWKV_HD = 64
RWKV_WIDTH = 1024
RWKV_GN_EPS = 64e-5
RWKV_CHUNK = 64
RWKV_GROUP = 256
RWKV_LORA = 256

PEER_HEADS = 8
PEER_NKEYS = 128
PEER_EXPERTS = PEER_NKEYS * PEER_NKEYS
PEER_HALF = 128
PEER_TOPK = 16

COL_U = 0
COL_GATE = 1024
COL_QKV = 2048
COL_ZG = 5120
COL_MERGE = 6144
COL_RWKV = 9216
COL_AB = 12544
IN_WIDTH_PAD = 12672

VMEM_LIMIT = 48 * 1024 * 1024
BIG = 3.0e38


def _cparams(*sem):
    return pltpu.CompilerParams(dimension_semantics=sem, vmem_limit_bytes=VMEM_LIMIT)


def _mm(a, b):
    return jnp.dot(a.astype(BF16), b.astype(BF16), preferred_element_type=F32)


def _mm_nt(a, b):
    return lax.dot_general(a.astype(BF16), b.astype(BF16), (((1,), (1,)), ((), ())),
                           preferred_element_type=F32)


def _mm_tn(a, b):
    return lax.dot_general(a.astype(BF16), b.astype(BF16), (((0,), (0,)), ((), ())),
                           preferred_element_type=F32)


def _mm_f32(a, b):
    return jnp.dot(a, b, precision=HIGHEST, preferred_element_type=F32)


def _softplus(x):
    return jnp.maximum(x, 0.0) + jnp.log1p(jnp.exp(-jnp.abs(x)))


def _rms(x, w):
    return x * lax.rsqrt(jnp.mean(x * x, axis=-1, keepdims=True) + RMS_EPS) * w


def _shift_prev(x, prev8, s):
    r = pltpu.roll(x, s, 0)
    pr = pltpu.roll(prev8, s, 0)
    rows8 = lax.broadcasted_iota(jnp.int32, prev8.shape, 0)
    head = jnp.where(rows8 < s, pr, r[:8])
    return jnp.concatenate([head, r[8:]], axis=0)


def _shift_fill(x, d, fill):
    n, c = x.shape
    if d % 8 == 0:
        return jnp.concatenate([jnp.full((d, c), fill, x.dtype), x[:n - d]], axis=0)
    r = pltpu.roll(x, d, 0)
    rows8 = lax.broadcasted_iota(jnp.int32, (8, c), 0)
    head = jnp.where(rows8 < d, fill, r[:8])
    return jnp.concatenate([head, r[8:]], axis=0)


def _tri(n, strict):
    i = lax.broadcasted_iota(jnp.int32, (n, n), 0)
    j = lax.broadcasted_iota(jnp.int32, (n, n), 1)
    return (j < i) if strict else (j <= i)


def _inproj_body(add_y, *refs):
    if add_y:
        h_ref, y_ref, nw_ref, w_ref, hout_ref, z_ref, xn_scr = refs
    else:
        h_ref, nw_ref, w_ref, z_ref, xn_scr = refs

    @pl.when(pl.program_id(1) == 0)
    def _():
        h = h_ref[...]
        if add_y:
            h = h + y_ref[...]
            hout_ref[...] = h
        xn_scr[...] = _rms(h, nw_ref[...]).astype(BF16)

    z_ref[...] = jnp.dot(xn_scr[...], w_ref[...], preferred_element_type=F32)


def _inproj(h, y, norm_w, w_pad):
    m = h.shape[0]
    tm, tn = 512, 1152
    add_y = y is not None
    row = pl.BlockSpec((tm, D_MODEL), lambda i, j: (i, 0))
    in_specs = [row] + ([row] if add_y else []) + [
        pl.BlockSpec((1, D_MODEL), lambda i, j: (0, 0)),
        pl.BlockSpec((D_MODEL, tn), lambda i, j: (0, j)),
    ]
    z_spec = pl.BlockSpec((tm, tn), lambda i, j: (i, j))
    z_shape = jax.ShapeDtypeStruct((m, IN_WIDTH_PAD), F32)
    args = (h,) + ((y,) if add_y else ()) + (norm_w.reshape(1, D_MODEL), w_pad)
    out = pl.pallas_call(
        functools.partial(_inproj_body, add_y),
        grid=(m // tm, IN_WIDTH_PAD // tn),
        in_specs=in_specs,
        out_specs=[row, z_spec] if add_y else z_spec,
        out_shape=[jax.ShapeDtypeStruct((m, D_MODEL), F32), z_shape] if add_y else z_shape,
        scratch_shapes=[pltpu.VMEM((tm, D_MODEL), BF16)],
        compiler_params=_cparams("parallel", "arbitrary"),
        name="norm_inproj",
    )(*args)
    return (out[0], out[1]) if add_y else (h, out)


def _lru_body(zu_ref, zg_ref, cw_ref, vp_ref, wa_ref, wx_ref, o_ref, prev_scr, hc_scr):
    tt = zu_ref.shape[0]

    @pl.when(pl.program_id(1) == 0)
    def _():
        prev_scr[...] = jnp.zeros_like(prev_scr)
        hc_scr[...] = jnp.zeros_like(hc_scr)

    u = zu_ref[...]
    prev = prev_scr[...]
    cw = cw_ref[...]
    vp = vp_ref[...]
    xc = (cw[3:4] * u + cw[2:3] * _shift_prev(u, prev, 1) + cw[1:2] * _shift_prev(u, prev, 2)
          + cw[0:1] * _shift_prev(u, prev, 3) + vp[0:1])
    prev_scr[...] = u[tt - 8:]

    xcb = xc.astype(BF16)
    n_grp = LRU_WIDTH // LRU_GROUP
    pre_a = jnp.concatenate(
        [jnp.dot(xcb[:, g * LRU_GROUP:(g + 1) * LRU_GROUP], wa_ref[g], preferred_element_type=F32)
         for g in range(n_grp)], axis=1)
    pre_x = jnp.concatenate(
        [jnp.dot(xcb[:, g * LRU_GROUP:(g + 1) * LRU_GROUP], wx_ref[g], preferred_element_type=F32)
         for g in range(n_grp)], axis=1)
    r = jax.nn.sigmoid(pre_a + vp[1:2])
    i = jax.nn.sigmoid(pre_x + vp[2:3])
    log_a = (-LRU_C * r) * _softplus(-vp[3:4])
    a = jnp.exp(log_a)
    b = jnp.sqrt(-jnp.tanh(log_a) * (a * a + 1.0)) * (i * xc)

    d = 1
    while d < tt:
        b = a * _shift_fill(b, d, 0.0) + b
        a = a * _shift_fill(a, d, 1.0)
        d *= 2
    hcar = hc_scr[...]
    hval = b + a * hcar[0:1]
    hc_scr[...] = jnp.broadcast_to(hval[tt - 1:tt], hcar.shape)
    o_ref[...] = (hval * jax.nn.gelu(zg_ref[...])).astype(o_ref.dtype)


def _lru(z, bsz, seq, conv_w, vecs, wa_bd, wx_bd):
    m = z.shape[0]
    tt = 256
    nt = seq // tt
    rowmap = lambda c: (lambda b, t: (b * nt + t, c))
    const2 = lambda b, t: (0, 0)
    return pl.pallas_call(
        _lru_body,
        grid=(bsz, nt),
        in_specs=[
            pl.BlockSpec((tt, LRU_WIDTH), rowmap(COL_U // LRU_WIDTH)),
            pl.BlockSpec((tt, LRU_WIDTH), rowmap(COL_GATE // LRU_WIDTH)),
            pl.BlockSpec((4, LRU_WIDTH), const2),
            pl.BlockSpec((8, LRU_WIDTH), const2),
            pl.BlockSpec(wa_bd.shape, lambda b, t: (0, 0, 0)),
            pl.BlockSpec(wx_bd.shape, lambda b, t: (0, 0, 0)),
        ],
        out_specs=pl.BlockSpec((tt, LRU_WIDTH), rowmap(0)),
        out_shape=jax.ShapeDtypeStruct((m, LRU_WIDTH), BF16),
        scratch_shapes=[pltpu.VMEM((8, LRU_WIDTH), F32), pltpu.VMEM((8, LRU_WIDTH), F32)],
        compiler_params=_cparams("parallel", "arbitrary"),
        name="rglru",
    )(z, z, conv_w, vecs, wa_bd, wx_bd)


def _gdn_body(q_ref, k_ref, v_ref, zg_ref, ab_ref, abt_ref, cwq_ref, cwk_ref, cwv_ref,
              gp_ref, gpt_ref, nw_ref, o_ref, pq_scr, pk_scr, pv_scr, st_scr):
    tt = q_ref.shape[0]
    c = GDN_CHUNK
    h = pl.program_id(1)

    @pl.when(pl.program_id(2) == 0)
    def _():
        pq_scr[...] = jnp.zeros_like(pq_scr)
        pk_scr[...] = jnp.zeros_like(pk_scr)
        pv_scr[...] = jnp.zeros_like(pv_scr)
        st_scr[...] = jnp.zeros_like(st_scr)

    def conv_silu(x_ref, p_scr, cw_ref):
        x = x_ref[...]
        prev = p_scr[...]
        cw = cw_ref[...]
        y = (cw[3:4] * x + cw[2:3] * _shift_prev(x, prev, 1) + cw[1:2] * _shift_prev(x, prev, 2)
             + cw[0:1] * _shift_prev(x, prev, 3))
        p_scr[...] = x[tt - 8:]
        return jax.nn.silu(y)

    q = conv_silu(q_ref, pq_scr, cwq_ref)
    k = conv_silu(k_ref, pk_scr, cwk_ref)
    v = conv_silu(v_ref, pv_scr, cwv_ref)
    q = q * lax.rsqrt(jnp.sum(q * q, axis=-1, keepdims=True) + RMS_EPS) * (GDN_DK ** -0.5)
    k = k * lax.rsqrt(jnp.sum(k * k, axis=-1, keepdims=True) + RMS_EPS)

    ab = ab_ref[...]
    gp = gp_ref[...]
    lane = lax.broadcasted_iota(jnp.int32, (c, ab.shape[1]), 1)
    g_all = -jnp.exp(gp[0:1]) * _softplus(ab + gp[1:2])
    beta_all = jax.nn.sigmoid(ab)
    gpt = gpt_ref[...]
    sub = lax.broadcasted_iota(jnp.int32, (GDN_HEADS, c), 0)

    incl = _tri(c, False)
    strict = _tri(c, True)
    tril_f = incl.astype(F32)
    ones_f = jnp.ones((c, c), F32)
    triu_f =(lax.broadcasted_iota(jnp.int32, (c, c), 0) <= lax.broadcasted_iota(jnp.int32, (c, c), 1)).astype(F32)
    nw = nw_ref[...]

    for ci in range(tt // c):
        sl = slice(ci * c, (ci + 1) * c)
        qc, kc, vc = q[sl], k[sl], v[sl]
        gc_all = _mm_f32(tril_f, g_all[sl])
        gcol = jnp.sum(jnp.where(lane == h, gc_all, 0.0), axis=1, keepdims=True)
        g_rows = -jnp.exp(gpt[:, 0:1]) * _softplus(abt_ref[ci][0:GDN_HEADS] + gpt[:, 1:2])
        gr_all = _mm_f32(g_rows, triu_f)
        grow = jnp.sum(jnp.where(sub == h, gr_all, 0.0), axis=0, keepdims=True)
        bcol = jnp.sum(jnp.where(lane == h + GDN_HEADS, beta_all[sl], 0.0), axis=1, keepdims=True)
        decay = jnp.exp(jnp.where(incl, gcol - grow, -BIG))
        kb = kc * bcol
        a_mat = jnp.where(strict, _mm_nt(kb, kc) * decay, 0.0)
        p = -a_mat
        tm1 = p
        for _ in range(5):
            p = _mm(p, p)
            tm1 = tm1 + _mm(tm1, p) + p
        egc = jnp.exp(gcol)
        rhs = jnp.concatenate([vc * bcol, kb * egc], axis=1)
        sol = rhs + _mm(tm1, rhs)
        u_c, w_c = sol[:, :GDN_DV], sol[:, GDN_DV:]
        qk = _mm_nt(qc, kc) * decay
        q_dec = qc * egc
        gt_all = _mm_f32(ones_f, g_all[sl])
        g_last = jnp.sum(jnp.where(lane == h, gt_all, 0.0), axis=1, keepdims=True)
        k_dec = kc * jnp.exp(g_last - gcol)
        state = st_scr[...]
        v_new = u_c - _mm(w_c, state)
        o = _mm(q_dec, state) + _mm(qk, v_new)
        g_tot = jnp.exp(jnp.concatenate([g_last] * (GDN_DK // c), axis=0))
        st_scr[...] = state * g_tot + _mm_tn(k_dec, v_new)
        o = o * lax.rsqrt(jnp.mean(o * o, axis=-1, keepdims=True) + RMS_EPS) * nw
        o = o * jax.nn.silu(zg_ref[sl, :])
        o_ref[sl, :] = o.astype(o_ref.dtype)


def _gdn(z, abt, bsz, seq, conv_w, gp, gpt, norm_w):
    m = z.shape[0]
    tt = 256
    nt = seq // tt
    lane_blk = lambda c0: (lambda b, h, t: (b * nt + t, c0 // 128 + h))
    cw_blk = lambda c0: (lambda b, h, t: (0, c0 // 128 + h))
    const2 = lambda b, h, t: (0, 0)
    return pl.pallas_call(
        _gdn_body,
        grid=(bsz, GDN_HEADS, nt),
        in_specs=[
            pl.BlockSpec((tt, 128), lane_blk(COL_QKV)),
            pl.BlockSpec((tt, 128), lane_blk(COL_QKV + 1024)),
            pl.BlockSpec((tt, 128), lane_blk(COL_QKV + 2048)),
            pl.BlockSpec((tt, 128), lane_blk(COL_ZG)),
            pl.BlockSpec((tt, 128), lambda b, h, t: (b * nt + t, COL_AB // 128)),
            pl.BlockSpec((tt // GDN_CHUNK, 2 * GDN_HEADS, GDN_CHUNK), lambda b, h, t: (b * nt + t, 0, 0)),
            pl.BlockSpec((4, 128), cw_blk(0)),
            pl.BlockSpec((4, 128), cw_blk(1024)),
            pl.BlockSpec((4, 128), cw_blk(2048)),
            pl.BlockSpec((8, 128), const2),
            pl.BlockSpec((8, 128), const2),
            pl.BlockSpec((1, 128), const2),
        ],
        out_specs=pl.BlockSpec((tt, 128), lambda b, h, t: (b * nt + t, h)),
        out_shape=jax.ShapeDtypeStruct((m, GDN_HEADS * GDN_DV), BF16),
        scratch_shapes=[pltpu.VMEM((8, 128), F32), pltpu.VMEM((8, 128), F32), pltpu.VMEM((8, 128), F32),
                        pltpu.VMEM((GDN_DK, GDN_DV), F32)],
        compiler_params=_cparams("parallel", "parallel", "arbitrary"),
        name="gated_deltanet",
    )(z, z, z, z, z, abt, conv_w, conv_w, conv_w, gp, gpt, norm_w)


def _rwkv_body(r_ref, k_ref, v_ref, lo_ref, mur_ref, muk_ref, muv_ref, mulo_ref, vp_ref,
               wup_ref, aup_ref, gup_ref, o_ref, pr_scr, pk_scr, pv_scr, plo_scr, st_scr):
    tt = r_ref.shape[0]
    c = RWKV_CHUNK
    gw = RWKV_GROUP
    hd = RWKV_HD

    @pl.when(pl.program_id(2) == 0)
    def _():
        for s in (pr_scr, pk_scr, pv_scr, plo_scr, st_scr):
            s[...] = jnp.zeros_like(s)

    def tshift(x_ref, p_scr, mu_ref):
        x = x_ref[...]
        xs = x + mu_ref[...] * (_shift_prev(x, p_scr[...], 1) - x)
        p_scr[...] = x[tt - 8:]
        return xs

    r = tshift(r_ref, pr_scr, mur_ref)
    k = tshift(k_ref, pk_scr, muk_ref)
    v = tshift(v_ref, pv_scr, muv_ref)
    lo = tshift(lo_ref, plo_scr, mulo_ref)
    vp = vp_ref[...]
    w0, a0, k_k, k_a, r_k, lnx_w, lnx_b = (vp[i:i + 1] for i in range(7))

    lane = lax.broadcasted_iota(jnp.int32, lo.shape, 1)
    lo_act = jnp.where(lane < 64, jnp.tanh(lo), jnp.where(lane < 128, lo, jax.nn.sigmoid(lo)))
    w_pre = _mm(lo_act, wup_ref[...])
    a_pre = _mm(lo_act, aup_ref[...])
    gate = _mm(lo_act, gup_ref[...])
    w_log = -_softplus(-(w0 + w_pre)) - 0.5
    lw = -jnp.exp(w_log)
    a = jax.nn.sigmoid(a0 + a_pre)

    bi = lax.broadcasted_iota(jnp.int32, (gw, gw), 0) // hd
    bj = lax.broadcasted_iota(jnp.int32, (gw, gw), 1) // hd
    bdmask = bi == bj
    ones_bd = bdmask.astype(F32)

    def head_sum(x):
        return _mm_f32(x, ones_bd)

    kk = k * k_k
    kk = kk * lax.rsqrt(head_sum(kk * kk) + RMS_EPS)
    k2 = k * (1.0 + (a - 1.0) * k_a)
    kka = kk * a

    def bd(x):
        return jnp.where(bdmask, jnp.concatenate([x] * (gw // c), axis=0), 0.0)

    ti = lax.broadcasted_iota(jnp.int32, (c, gw), 0)
    sj = lax.broadcasted_iota(jnp.int32, (c, gw), 1) % c
    strict = sj < ti
    incl = sj <= ti
    tril_f = _tri(c, False).astype(F32)
    ones_c = jnp.ones((c, gw), F32)

    for ci in range(tt // c):
        sl = slice(ci * c, (ci + 1) * c)
        lw_c = lw[sl]
        cl = _mm_f32(tril_f, lw_c)
        wcum = jnp.exp(cl)
        inv_w = jnp.exp(-cl)
        a_h = kka[sl] * inv_w
        b_h = kk[sl] * jnp.exp(cl - lw_c)
        k_h = k2[sl] * inv_w
        r_h = r[sl] * wcum
        v_c = v[sl]
        a_bd = bd(a_h)
        k_bd = bd(k_h)
        v_bd = bd(v_c)
        br = jnp.concatenate([b_h, r_h], axis=0)
        xa = _mm_nt(br, a_bd)
        xk = _mm_nt(br, k_bd)
        l_a = jnp.where(strict, xa[:c], 0.0)
        l_k = jnp.where(strict, xk[:c], 0.0)
        ra = jnp.where(incl, xa[c:], 0.0)
        rk = jnp.where(incl, xk[c:], 0.0)
        p = -l_a
        tm1 = p
        for _ in range(5):
            p = _mm(p, bd(p))
            tm1 = tm1 + _mm(tm1, bd(p)) + p
        state = st_scr[...]
        brh = _mm(br, state)
        rhs = brh[:c] + _mm(l_k, v_bd)
        u = rhs + _mm(tm1, bd(rhs))
        o = brh[c:] - _mm(ra, bd(u)) + _mm(rk, v_bd)
        w_end = wcum[c - 1:c]
        w_col = jnp.exp(lax.dot_general(lw_c, ones_c, (((0,), (0,)), ((), ())),
                                        precision=HIGHEST, preferred_element_type=F32))
        upd = _mm_tn(jnp.concatenate([-(a_h * w_end), k_h * w_end], axis=0),
                     jnp.concatenate([u, v_c], axis=0))
        st_scr[...] = w_col * state + jnp.where(bdmask, upd, 0.0)

        mean = head_sum(o) * (1.0 / hd)
        cen = o - mean
        var = head_sum(cen * cen) * (1.0 / hd)
        on = cen * lax.rsqrt(var + RWKV_GN_EPS) * lnx_w + lnx_b
        bonus = head_sum(r[sl] * k2[sl] * r_k) * v_c
        o_ref[sl, :] = ((on + bonus) * gate[sl]).astype(o_ref.dtype)


def _rwkv(z, bsz, seq, mu, vecs, wup_pad, aup_pad, gup_pad):
    m = z.shape[0]
    tt = 256
    nt = seq // tt
    gw = RWKV_GROUP
    ngrp = RWKV_WIDTH // gw
    zblk = lambda c0: (lambda b, g, t: (b * nt + t, c0 // gw + g))
    mublk = lambda c0: (lambda b, g, t: (0, c0 // gw + g))
    lora_col = (COL_RWKV + 3 * RWKV_WIDTH) // gw
    grp_cols = lambda b, g, t: (0, g)
    return pl.pallas_call(
        _rwkv_body,
        grid=(bsz, ngrp, nt),
        in_specs=[
            pl.BlockSpec((tt, gw), zblk(COL_RWKV)),
            pl.BlockSpec((tt, gw), zblk(COL_RWKV + RWKV_WIDTH)),
            pl.BlockSpec((tt, gw), zblk(COL_RWKV + 2 * RWKV_WIDTH)),
            pl.BlockSpec((tt, gw), lambda b, g, t: (b * nt + t, lora_col)),
            pl.BlockSpec((1, gw), mublk(0)),
            pl.BlockSpec((1, gw), mublk(RWKV_WIDTH)),
            pl.BlockSpec((1, gw), mublk(2 * RWKV_WIDTH)),
            pl.BlockSpec((1, gw), lambda b, g, t: (0, 3 * RWKV_WIDTH // gw)),
            pl.BlockSpec((8, gw), grp_cols),
            pl.BlockSpec((RWKV_LORA, gw), grp_cols),
            pl.BlockSpec((RWKV_LORA, gw), grp_cols),
            pl.BlockSpec((RWKV_LORA, gw), grp_cols),
        ],
        out_specs=pl.BlockSpec((tt, gw), lambda b, g, t: (b * nt + t, g)),
        out_shape=jax.ShapeDtypeStruct((m, RWKV_WIDTH), BF16),
        scratch_shapes=[pltpu.VMEM((8, gw), F32)] * 4 + [pltpu.VMEM((gw, gw), F32)],
        compiler_params=_cparams("parallel", "parallel", "arbitrary"),
        name="rwkv7",
    )(z, z, z, z, mu, mu, mu, mu, vecs, wup_pad, aup_pad, gup_pad)


def _merge_body(h_ref, ya_ref, yb_ref, yc_ref, za_ref, zb_ref, zc_ref, mb_ref, pa_ref, pb_ref,
                pc_ref, wo_ref, nw_ref, hout_ref, xn_ref):
    mb = mb_ref[...]
    ga = jax.nn.sigmoid(za_ref[...] + mb[:, 0:D_MODEL])
    gb = jax.nn.sigmoid(zb_ref[...] + mb[:, D_MODEL:2 * D_MODEL])
    gc = jax.nn.sigmoid(zc_ref[...] + mb[:, 2 * D_MODEL:])
    merged = (ga * jnp.dot(ya_ref[...], pa_ref[...], preferred_element_type=F32)
              + gb * jnp.dot(yb_ref[...], pb_ref[...], preferred_element_type=F32)
              + gc * jnp.dot(yc_ref[...], pc_ref[...], preferred_element_type=F32))
    h = h_ref[...] + _mm(merged, wo_ref[...])
    hout_ref[...] = h
    xn_ref[...] = _rms(h, nw_ref[...]).astype(BF16)


def _merge(h, z, ya, yb, yc, merge_b, pa, pb, pc, wo, norm_w):
    m = h.shape[0]
    tm = 256
    row = lambda c: pl.BlockSpec((tm, D_MODEL), lambda i: (i, c))
    wspec = pl.BlockSpec((D_MODEL, D_MODEL), lambda i: (0, 0))
    mc = COL_MERGE // D_MODEL
    return pl.pallas_call(
        _merge_body,
        grid=(m // tm,),
        in_specs=[row(0), row(0), row(0), row(0), row(mc), row(mc + 1), row(mc + 2),
                  pl.BlockSpec((1, 3 * D_MODEL), lambda i: (0, 0)),
                  wspec, wspec, wspec, wspec,
                  pl.BlockSpec((1, D_MODEL), lambda i: (0, 0))],
        out_specs=[row(0), row(0)],
        out_shape=[jax.ShapeDtypeStruct((m, D_MODEL), F32), jax.ShapeDtypeStruct((m, D_MODEL), BF16)],
        compiler_params=_cparams("parallel"),
        name="merge_outproj",
    )(h, ya, yb, yc, z, z, z, merge_b, pa, pb, pc, wo, norm_w)


def _peer_topk_body(xnt_ref, wqt_ref, keys_ref, thr_ref, e1_ref, s2_ref, e2_ref, q_scr):
    tt = xnt_ref.shape[1]
    nk = PEER_NKEYS
    q_scr[...] = jnp.dot(wqt_ref[...], xnt_ref[...], preferred_element_type=F32)
    row_k = lax.broadcasted_iota(jnp.int32, (PEER_TOPK, tt), 0)

    def top_values(s):
        def body(rnd, carry):
            x, vals = carry
            mx = jnp.max(x, axis=0, keepdims=True)
            return jnp.where(x == mx, -BIG, x), jnp.where(row_k == rnd, mx, vals)
        return lax.fori_loop(0, PEER_TOPK, body, (s, jnp.zeros((PEER_TOPK, tt), F32)))[1]

    def head(h, carry):
        o1 = pl.multiple_of(h * (2 * nk), 2 * nk)
        s1 = _mm(keys_ref[2 * h], q_scr[pl.ds(o1, nk), :])
        s2 = _mm(keys_ref[2 * h + 1], q_scr[pl.ds(o1 + nk, nk), :])
        a1 = top_values(s1)
        a2 = top_values(s2)
        cand = jnp.concatenate([a1[p:p + 1] + a2 for p in range(PEER_TOPK)], axis=0)
        cmax = a1[0:1] + a2[0:1]

        def cbody(rnd, carry):
            cnd, zsum, c_in, c_out = carry
            mx = jnp.max(cnd, axis=0, keepdims=True)
            zsum = zsum + jnp.where(rnd < PEER_TOPK, jnp.exp(mx - cmax), 0.0)
            c_in = jnp.where(rnd == PEER_TOPK - 1, mx, c_in)
            c_out = jnp.where(rnd == PEER_TOPK, mx, c_out)
            return jnp.where(cnd == mx, -BIG, cnd), zsum, c_in, c_out

        zero = jnp.zeros((1, tt), F32)
        _, zsum, c_in, c_out = lax.fori_loop(0, PEER_TOPK + 1, cbody, (cand, zero, zero, zero))
        tau = 0.5 * (c_in + c_out)
        thr_ref[h] = jnp.where(s1 >= a1[PEER_TOPK - 1:PEER_TOPK], tau - s1, BIG)
        e1_ref[h] = jnp.exp(s1 - a1[0:1])
        s2_ref[h] = s2
        e2_ref[h] = jnp.where(s2 >= a2[PEER_TOPK - 1:PEER_TOPK], jnp.exp(s2 - a2[0:1]) / zsum, 0.0)
        return carry

    lax.fori_loop(0, PEER_HEADS, head, 0)


def _peer_topk(xnt, wqt, keys):
    m = xnt.shape[1]
    tt = 256
    out = jax.ShapeDtypeStruct((PEER_HEADS, PEER_NKEYS, m), F32)
    ospec = pl.BlockSpec((PEER_HEADS, PEER_NKEYS, tt), lambda i: (0, 0, i))
    return pl.pallas_call(
        _peer_topk_body,
        grid=(m // tt,),
        in_specs=[pl.BlockSpec((D_MODEL, tt), lambda i: (0, i)),
                  pl.BlockSpec(wqt.shape, lambda i: (0, 0)),
                  pl.BlockSpec(keys.shape, lambda i: (0, 0, 0))],
        out_specs=[ospec] * 4,
        out_shape=[out] * 4,
        scratch_shapes=[pltpu.VMEM((wqt.shape[0], tt), F32)],
        compiler_params=_cparams("parallel"),
        name="peer_topk",
    )(xnt, wqt, keys)


def _peer_dense_body(u_ref, xnt_ref, vt_ref, thr_ref, e1_ref, s2_ref, e2_ref, y_ref, act_scr, a_scr):
    nk = PEER_NKEYS
    tt = xnt_ref.shape[1]

    @pl.when(pl.program_id(1) == 0)
    def _():
        y_ref[...] = jnp.zeros_like(y_ref)

    act_scr[...] = jax.nn.gelu(jnp.dot(u_ref[...], xnt_ref[...], preferred_element_type=F32))

    def blk(ii, carry):
        w = jnp.zeros((nk, tt), F32)
        for h in range(PEER_HEADS):
            thr = thr_ref[h, pl.ds(ii, 1), :]
            e1 = e1_ref[h, pl.ds(ii, 1), :]
            w = w + jnp.where(s2_ref[h] >= thr, e2_ref[h], 0.0) * e1
        rows = pl.ds(pl.multiple_of(ii * nk, nk), nk)
        a_scr[rows, :] = (act_scr[rows, :] * w).astype(BF16)
        return carry

    lax.fori_loop(0, u_ref.shape[0] // nk, blk, 0)
    y_ref[...] += jnp.dot(vt_ref[...], a_scr[...], preferred_element_type=F32)


def _peer_dense(xnt, u_bf, vt_bf, thr, e1, s2, e2, tt):
    m = xnt.shape[1]
    te = 1024
    ib = te // PEER_NKEYS
    sel_i = pl.BlockSpec((PEER_HEADS, ib, tt), lambda i, e: (0, e, i))
    sel_all = pl.BlockSpec((PEER_HEADS, PEER_NKEYS, tt), lambda i, e: (0, 0, i))
    return pl.pallas_call(
        _peer_dense_body,
        grid=(m // tt, PEER_EXPERTS // te),
        in_specs=[pl.BlockSpec((te, D_MODEL), lambda i, e: (e, 0)),
                  pl.BlockSpec((D_MODEL, tt), lambda i, e: (0, i)),
                  pl.BlockSpec((D_MODEL, te), lambda i, e: (0, e)),
                  sel_i, sel_i, sel_all, sel_all],
        out_specs=pl.BlockSpec((D_MODEL, tt), lambda i, e: (0, i)),
        out_shape=jax.ShapeDtypeStruct((D_MODEL, m), F32),
        scratch_shapes=[pltpu.VMEM((te, tt), F32), pltpu.VMEM((te, tt), BF16)],
        compiler_params=_cparams("parallel", "arbitrary"),
        name="peer_dense",
    )(u_bf, xnt, vt_bf, thr, e1, s2, e2)


def _final_body(h_ref, y_ref, nw_ref, o_ref):
    o_ref[...] = _rms(h_ref[...] + y_ref[...], nw_ref[...])


def _final(h, y, norm_w):
    m = h.shape[0]
    tm = 512
    row = pl.BlockSpec((tm, D_MODEL), lambda i: (i, 0))
    return pl.pallas_call(
        _final_body,
        grid=(m // tm,),
        in_specs=[row, row, pl.BlockSpec((1, D_MODEL), lambda i: (0, 0))],
        out_specs=row,
        out_shape=jax.ShapeDtypeStruct((m, D_MODEL), F32),
        compiler_params=_cparams("parallel"),
        name="final_norm",
    )(h, y, norm_w.reshape(1, D_MODEL))


def _pad_rows(rows, width, n_rows=8):
    out = jnp.zeros((n_rows, width), F32)
    for i, r in enumerate(rows):
        flat = r.reshape(-1).astype(F32)
        out = out.at[i, :flat.shape[0]].set(flat)
    return out


def _chunk_rows(ab):
    m, n = ab.shape
    return jnp.transpose(ab.reshape(m // GDN_CHUNK, GDN_CHUNK, n), (0, 2, 1))


def _block_diag_groups(w):
    per = LRU_GROUP // LRU_BLOCK_DIM
    n_grp = w.shape[0] // per
    out = jnp.zeros((n_grp, LRU_GROUP, LRU_GROUP), F32)
    for g in range(n_grp):
        for j in range(per):
            o = j * LRU_BLOCK_DIM
            out = out.at[g, o:o + LRU_BLOCK_DIM, o:o + LRU_BLOCK_DIM].set(w[g * per + j])
    return out.astype(BF16)


def _prep_w_in(w):
    pad = jnp.zeros((D_MODEL, IN_WIDTH_PAD - COL_AB - 2 * GDN_HEADS), w.dtype)
    return jnp.concatenate([w[:, :6144], w[:, 9488:12560], w[:, 6160:9488], w[:, 6144:6160], pad],
                           axis=1).astype(BF16)


def _layer(l, h, y, bsz, seq, p):
    z_h, z = _inproj(h, y, p["norm_mix_w"][l], _prep_w_in(p["w_in"][l]))
    h = z_h

    lru_vecs = _pad_rows([p["lru_conv_b"][l], p["lru_b_a"][l], p["lru_b_x"][l], p["lru_lambda"][l]], LRU_WIDTH)
    ya = _lru(z, bsz, seq, p["lru_conv_w"][l], lru_vecs,
              _block_diag_groups(p["lru_w_a"][l]), _block_diag_groups(p["lru_w_x"][l]))

    abt = _chunk_rows(z[:, COL_AB:COL_AB + 2 * GDN_HEADS])
    gp = _pad_rows([p["gdn_a_log"][l], p["gdn_dt_bias"][l]], 128)
    gpt = jnp.zeros((8, 128), F32).at[:, 0].set(p["gdn_a_log"][l]).at[:, 1].set(p["gdn_dt_bias"][l])
    yb = _gdn(z, abt, bsz, seq, p["gdn_conv_w"][l], gp, gpt, p["gdn_norm_w"][l].reshape(1, GDN_DV))

    rw_vecs = _pad_rows([p["rwkv_w0"][l], p["rwkv_a0"][l], p["rwkv_k_k"][l], p["rwkv_k_a"][l],
                         p["rwkv_r_k"][l], p["rwkv_lnx_w"][l], p["rwkv_lnx_b"][l]], RWKV_WIDTH)
    zl = jnp.zeros((RWKV_LORA, RWKV_WIDTH), F32)
    wup = zl.at[0:64].set(p["rwkv_w_up"][l]).astype(BF16)
    aup = zl.at[64:128].set(p["rwkv_a_up"][l]).astype(BF16)
    gup = zl.at[128:256].set(p["rwkv_g_up"][l]).astype(BF16)
    yc = _rwkv(z, bsz, seq, p["rwkv_mu"][l].reshape(1, -1), rw_vecs, wup, aup, gup)

    h, xn = _merge(h, z, ya, yb, yc, p["merge_b"][l].reshape(1, -1), p["p_lru"][l].astype(BF16),
                   p["p_gdn"][l].astype(BF16), p["p_rwkv"][l].astype(BF16), p["w_out"][l].astype(BF16),
                   p["norm_ffn_w"][l].reshape(1, D_MODEL))

    xnt = jnp.transpose(xn)
    wqt = jnp.transpose(p["peer_wq"][l]).astype(BF16)
    keys = p["peer_keys"][l].reshape(2 * PEER_HEADS, PEER_NKEYS, PEER_HALF).astype(BF16)
    thr, e1, s2, e2 = _peer_topk(xnt, wqt, keys)
    tt = 512 if xnt.shape[1] % 512 == 0 else 256
    yt = _peer_dense(xnt, p["peer_u"][l].astype(BF16), jnp.transpose(p["peer_v"][l]).astype(BF16),
                     thr, e1, s2, e2, tt)
    return h, jnp.transpose(yt)


def kernel(x, norm_mix_w, norm_ffn_w, final_norm_w, w_in, lru_conv_w, lru_conv_b, lru_w_a, lru_b_a,
           lru_w_x, lru_b_x, lru_lambda, gdn_conv_w, gdn_a_log, gdn_dt_bias, gdn_norm_w, rwkv_mu, rwkv_w0,
           rwkv_w_up, rwkv_a0, rwkv_a_up, rwkv_g_up, rwkv_k_k, rwkv_k_a, rwkv_r_k, rwkv_lnx_w, rwkv_lnx_b,
           merge_b, p_lru, p_gdn, p_rwkv, w_out, peer_wq, peer_keys, peer_u, peer_v):
    p = dict(locals())
    bsz, seq, dim = x.shape
    h = x.reshape(bsz * seq, dim)
    y = None
    for l in range(DEPTH):
        h, y = _layer(l, h, y, bsz, seq, p)
    return _final(h, y, final_norm_w).reshape(bsz, seq, dim)
```

```python
import functools

import jax
import jax.numpy as jnp
from jax import lax
from jax.experimental import pallas as pl
from jax.experimental.pallas import tpu as pltpu

F32 = jnp.float32
BF16 = jnp.bfloat16
HIGHEST = lax.Precision.HIGHEST

D_MODEL = 1024
DEPTH = 2
RMS_EPS = 1e-6

LRU_WIDTH = 1024
LRU_BLOCK_DIM = 64
LRU_C = 8.0
LRU_GROUP = 256

GDN_HEADS = 8
GDN_DK = 128
GDN_DV = 128
GDN_CHUNK = 64

RWKV_HEADS = 16
RWKV_HD = 64
RWKV_WIDTH = 1024
RWKV_GN_EPS = 64e-5
RWKV_CHUNK = 64
RWKV_GROUP = 256
RWKV_LORA = 256

PEER_HEADS = 8
PEER_NKEYS = 128
PEER_EXPERTS = PEER_NKEYS * PEER_NKEYS
PEER_HALF = 128
PEER_TOPK = 16

COL_U = 0
COL_GATE = 1024
COL_QKV = 2048
COL_ZG = 5120
COL_MERGE = 6144
COL_RWKV = 9216
COL_AB = 12544
IN_WIDTH_PAD = 12672

VMEM_LIMIT = 48 * 1024 * 1024
BIG = 3.0e38


def _cparams(*sem):
    return pltpu.CompilerParams(dimension_semantics=sem, vmem_limit_bytes=VMEM_LIMIT)


def _mm(a, b):
    return jnp.dot(a.astype(BF16), b.astype(BF16), preferred_element_type=F32)


def _mm_nt(a, b):
    return lax.dot_general(a.astype(BF16), b.astype(BF16), (((1,), (1,)), ((), ())),
                           preferred_element_type=F32)


def _mm_tn(a, b):
    return lax.dot_general(a.astype(BF16), b.astype(BF16), (((0,), (0,)), ((), ())),
                           preferred_element_type=F32)


def _mm_f32(a, b):
    return jnp.dot(a, b, precision=HIGHEST, preferred_element_type=F32)


def _softplus(x):
    return jnp.maximum(x, 0.0) + jnp.log1p(jnp.exp(-jnp.abs(x)))


def _rms(x, w):
    return x * lax.rsqrt(jnp.mean(x * x, axis=-1, keepdims=True) + RMS_EPS) * w


def _shift_prev(x, prev8, s):
    r = pltpu.roll(x, s, 0)
    pr = pltpu.roll(prev8, s, 0)
    rows8 = lax.broadcasted_iota(jnp.int32, prev8.shape, 0)
    head = jnp.where(rows8 < s, pr, r[:8])
    return jnp.concatenate([head, r[8:]], axis=0)


def _shift_fill(x, d, fill):
    n, c = x.shape
    if d % 8 == 0:
        return jnp.concatenate([jnp.full((d, c), fill, x.dtype), x[:n - d]], axis=0)
    r = pltpu.roll(x, d, 0)
    rows8 = lax.broadcasted_iota(jnp.int32, (8, c), 0)
    head = jnp.where(rows8 < d, fill, r[:8])
    return jnp.concatenate([head, r[8:]], axis=0)


def _tri(n, strict):
    i = lax.broadcasted_iota(jnp.int32, (n, n), 0)
    j = lax.broadcasted_iota(jnp.int32, (n, n), 1)
    return (j < i) if strict else (j <= i)


def _inproj_body(add_y, *refs):
    if add_y:
        h_ref, y_ref, nw_ref, w_ref, hout_ref, z_ref, xn_scr = refs
    else:
        h_ref, nw_ref, w_ref, z_ref, xn_scr = refs

    @pl.when(pl.program_id(1) == 0)
    def _():
        h = h_ref[...]
        if add_y:
            h = h + y_ref[...]
            hout_ref[...] = h
        xn_scr[...] = _rms(h, nw_ref[...]).astype(BF16)

    z_ref[...] = jnp.dot(xn_scr[...], w_ref[...], preferred_element_type=F32)


def _inproj(h, y, norm_w, w_pad):
    m = h.shape[0]
    tm, tn = 512, 1152
    add_y = y is not None
    row = pl.BlockSpec((tm, D_MODEL), lambda i, j: (i, 0))
    in_specs = [row] + ([row] if add_y else []) + [
        pl.BlockSpec((1, D_MODEL), lambda i, j: (0, 0)),
        pl.BlockSpec((D_MODEL, tn), lambda i, j: (0, j)),
    ]
    z_spec = pl.BlockSpec((tm, tn), lambda i, j: (i, j))
    z_shape = jax.ShapeDtypeStruct((m, IN_WIDTH_PAD), F32)
    args = (h,) + ((y,) if add_y else ()) + (norm_w.reshape(1, D_MODEL), w_pad)
    out = pl.pallas_call(
        functools.partial(_inproj_body, add_y),
        grid=(m // tm, IN_WIDTH_PAD // tn),
        in_specs=in_specs,
        out_specs=[row, z_spec] if add_y else z_spec,
        out_shape=[jax.ShapeDtypeStruct((m, D_MODEL), F32), z_shape] if add_y else z_shape,
        scratch_shapes=[pltpu.VMEM((tm, D_MODEL), BF16)],
        compiler_params=_cparams("parallel", "arbitrary"),
        name="norm_inproj",
    )(*args)
    return (out[0], out[1]) if add_y else (h, out)


def _lru_body(zu_ref, zg_ref, cw_ref, vp_ref, wa_ref, wx_ref, o_ref, prev_scr, hc_scr):
    tt = zu_ref.shape[0]

    @pl.when(pl.program_id(1) == 0)
    def _():
        prev_scr[...] = jnp.zeros_like(prev_scr)
        hc_scr[...] = jnp.zeros_like(hc_scr)

    u = zu_ref[...]
    prev = prev_scr[...]
    cw = cw_ref[...]
    vp = vp_ref[...]
    xc = (cw[3:4] * u + cw[2:3] * _shift_prev(u, prev, 1) + cw[1:2] * _shift_prev(u, prev, 2)
          + cw[0:1] * _shift_prev(u, prev, 3) + vp[0:1])
    prev_scr[...] = u[tt - 8:]

    xcb = xc.astype(BF16)
    n_grp = LRU_WIDTH // LRU_GROUP
    pre_a = jnp.concatenate(
        [jnp.dot(xcb[:, g * LRU_GROUP:(g + 1) * LRU_GROUP], wa_ref[g], preferred_element_type=F32)
         for g in range(n_grp)], axis=1)
    pre_x = jnp.concatenate(
        [jnp.dot(xcb[:, g * LRU_GROUP:(g + 1) * LRU_GROUP], wx_ref[g], preferred_element_type=F32)
         for g in range(n_grp)], axis=1)
    r = jax.nn.sigmoid(pre_a + vp[1:2])
    i = jax.nn.sigmoid(pre_x + vp[2:3])
    log_a = (-LRU_C * r) * _softplus(-vp[3:4])
    a = jnp.exp(log_a)
    b = jnp.sqrt(-jnp.tanh(log_a) * (a * a + 1.0)) * (i * xc)

    d = 1
    while d < tt:
        b = a * _shift_fill(b, d, 0.0) + b
        a = a * _shift_fill(a, d, 1.0)
        d *= 2
    hcar = hc_scr[...]
    hval = b + a * hcar[0:1]
    hc_scr[...] = jnp.broadcast_to(hval[tt - 1:tt], hcar.shape)
    o_ref[...] = (hval * jax.nn.gelu(zg_ref[...])).astype(o_ref.dtype)


def _lru(z, bsz, seq, conv_w, vecs, wa_bd, wx_bd):
    m = z.shape[0]
    tt = 256
    nt = seq // tt
    rowmap = lambda c: (lambda b, t: (b * nt + t, c))
    const2 = lambda b, t: (0, 0)
    return pl.pallas_call(
        _lru_body,
        grid=(bsz, nt),
        in_specs=[
            pl.BlockSpec((tt, LRU_WIDTH), rowmap(COL_U // LRU_WIDTH)),
            pl.BlockSpec((tt, LRU_WIDTH), rowmap(COL_GATE // LRU_WIDTH)),
            pl.BlockSpec((4, LRU_WIDTH), const2),
            pl.BlockSpec((8, LRU_WIDTH), const2),
            pl.BlockSpec(wa_bd.shape, lambda b, t: (0, 0, 0)),
            pl.BlockSpec(wx_bd.shape, lambda b, t: (0, 0, 0)),
        ],
        out_specs=pl.BlockSpec((tt, LRU_WIDTH), rowmap(0)),
        out_shape=jax.ShapeDtypeStruct((m, LRU_WIDTH), BF16),
        scratch_shapes=[pltpu.VMEM((8, LRU_WIDTH), F32), pltpu.VMEM((8, LRU_WIDTH), F32)],
        compiler_params=_cparams("parallel", "arbitrary"),
        name="rglru",
    )(z, z, conv_w, vecs, wa_bd, wx_bd)


def _gdn_body(q_ref, k_ref, v_ref, zg_ref, ab_ref, abt_ref, cwq_ref, cwk_ref, cwv_ref,
              gp_ref, gpt_ref, nw_ref, o_ref, pq_scr, pk_scr, pv_scr, st_scr):
    tt = q_ref.shape[0]
    c = GDN_CHUNK
    h = pl.program_id(1)

    @pl.when(pl.program_id(2) == 0)
    def _():
        pq_scr[...] = jnp.zeros_like(pq_scr)
        pk_scr[...] = jnp.zeros_like(pk_scr)
        pv_scr[...] = jnp.zeros_like(pv_scr)
        st_scr[...] = jnp.zeros_like(st_scr)

    def conv_silu(x_ref, p_scr, cw_ref):
        x = x_ref[...]
        prev = p_scr[...]
        cw = cw_ref[...]
        y = (cw[3:4] * x + cw[2:3] * _shift_prev(x, prev, 1) + cw[1:2] * _shift_prev(x, prev, 2)
             + cw[0:1] * _shift_prev(x, prev, 3))
        p_scr[...] = x[tt - 8:]
        return jax.nn.silu(y)

    q = conv_silu(q_ref, pq_scr, cwq_ref)
    k = conv_silu(k_ref, pk_scr, cwk_ref)
    v = conv_silu(v_ref, pv_scr, cwv_ref)
    q = q * lax.rsqrt(jnp.sum(q * q, axis=-1, keepdims=True) + RMS_EPS) * (GDN_DK ** -0.5)
    k = k * lax.rsqrt(jnp.sum(k * k, axis=-1, keepdims=True) + RMS_EPS)

    ab = ab_ref[...]
    gp = gp_ref[...]
    lane = lax.broadcasted_iota(jnp.int32, (c, ab.shape[1]), 1)
    g_all = -jnp.exp(gp[0:1]) * _softplus(ab + gp[1:2])
    beta_all = jax.nn.sigmoid(ab)
    gpt = gpt_ref[...]
    sub = lax.broadcasted_iota(jnp.int32, (GDN_HEADS, c), 0)

    incl = _tri(c, False)
    strict = _tri(c, True)
    tril_f = incl.astype(F32)
    ones_f = jnp.ones((c, c), F32)
    triu_f =(lax.broadcasted_iota(jnp.int32, (c, c), 0) <= lax.broadcasted_iota(jnp.int32, (c, c), 1)).astype(F32)
    nw = nw_ref[...]

    for ci in range(tt // c):
        sl = slice(ci * c, (ci + 1) * c)
        qc, kc, vc = q[sl], k[sl], v[sl]
        gc_all = _mm_f32(tril_f, g_all[sl])
        gcol = jnp.sum(jnp.where(lane == h, gc_all, 0.0), axis=1, keepdims=True)
        g_rows = -jnp.exp(gpt[:, 0:1]) * _softplus(abt_ref[ci][0:GDN_HEADS] + gpt[:, 1:2])
        gr_all = _mm_f32(g_rows, triu_f)
        grow = jnp.sum(jnp.where(sub == h, gr_all, 0.0), axis=0, keepdims=True)
        bcol = jnp.sum(jnp.where(lane == h + GDN_HEADS, beta_all[sl], 0.0), axis=1, keepdims=True)
        decay = jnp.exp(jnp.where(incl, gcol - grow, -BIG))
        kb = kc * bcol
        a_mat = jnp.where(strict, _mm_nt(kb, kc) * decay, 0.0)
        p = -a_mat
        tm1 = p
        for _ in range(5):
            p = _mm(p, p)
            tm1 = tm1 + _mm(tm1, p) + p
        egc = jnp.exp(gcol)
        rhs = jnp.concatenate([vc * bcol, kb * egc], axis=1)
        sol = rhs + _mm(tm1, rhs)
        u_c, w_c = sol[:, :GDN_DV], sol[:, GDN_DV:]
        qk = _mm_nt(qc, kc) * decay
        q_dec = qc * egc
        gt_all = _mm_f32(ones_f, g_all[sl])
        g_last = jnp.sum(jnp.where(lane == h, gt_all, 0.0), axis=1, keepdims=True)
        k_dec = kc * jnp.exp(g_last - gcol)
        state = st_scr[...]
        v_new = u_c - _mm(w_c, state)
        o = _mm(q_dec, state) + _mm(qk, v_new)
        g_tot = jnp.exp(jnp.concatenate([g_last] * (GDN_DK // c), axis=0))
        st_scr[...] = state * g_tot + _mm_tn(k_dec, v_new)
        o = o * lax.rsqrt(jnp.mean(o * o, axis=-1, keepdims=True) + RMS_EPS) * nw
        o = o * jax.nn.silu(zg_ref[sl, :])
        o_ref[sl, :] = o.astype(o_ref.dtype)


def _gdn(z, abt, bsz, seq, conv_w, gp, gpt, norm_w):
    m = z.shape[0]
    tt = 256
    nt = seq // tt
    lane_blk = lambda c0: (lambda b, h, t: (b * nt + t, c0 // 128 + h))
    cw_blk = lambda c0: (lambda b, h, t: (0, c0 // 128 + h))
    const2 = lambda b, h, t: (0, 0)
    return pl.pallas_call(
        _gdn_body,
        grid=(bsz, GDN_HEADS, nt),
        in_specs=[
            pl.BlockSpec((tt, 128), lane_blk(COL_QKV)),
            pl.BlockSpec((tt, 128), lane_blk(COL_QKV + 1024)),
            pl.BlockSpec((tt, 128), lane_blk(COL_QKV + 2048)),
            pl.BlockSpec((tt, 128), lane_blk(COL_ZG)),
            pl.BlockSpec((tt, 128), lambda b, h, t: (b * nt + t, COL_AB // 128)),
            pl.BlockSpec((tt // GDN_CHUNK, 2 * GDN_HEADS, GDN_CHUNK), lambda b, h, t: (b * nt + t, 0, 0)),
            pl.BlockSpec((4, 128), cw_blk(0)),
            pl.BlockSpec((4, 128), cw_blk(1024)),
            pl.BlockSpec((4, 128), cw_blk(2048)),
            pl.BlockSpec((8, 128), const2),
            pl.BlockSpec((8, 128), const2),
            pl.BlockSpec((1, 128), const2),
        ],
        out_specs=pl.BlockSpec((tt, 128), lambda b, h, t: (b * nt + t, h)),
        out_shape=jax.ShapeDtypeStruct((m, GDN_HEADS * GDN_DV), BF16),
        scratch_shapes=[pltpu.VMEM((8, 128), F32), pltpu.VMEM((8, 128), F32), pltpu.VMEM((8, 128), F32),
                        pltpu.VMEM((GDN_DK, GDN_DV), F32)],
        compiler_params=_cparams("parallel", "parallel", "arbitrary"),
        name="gated_deltanet",
    )(z, z, z, z, z, abt, conv_w, conv_w, conv_w, gp, gpt, norm_w)


def _rwkv_body(r_ref, k_ref, v_ref, lo_ref, mur_ref, muk_ref, muv_ref, mulo_ref, vp_ref,
               wup_ref, aup_ref, gup_ref, o_ref, pr_scr, pk_scr, pv_scr, plo_scr, st_scr):
    tt = r_ref.shape[0]
    c = RWKV_CHUNK
    gw = RWKV_GROUP
    hd = RWKV_HD

    @pl.when(pl.program_id(2) == 0)
    def _():
        for s in (pr_scr, pk_scr, pv_scr, plo_scr, st_scr):
            s[...] = jnp.zeros_like(s)

    def tshift(x_ref, p_scr, mu_ref):
        x = x_ref[...]
        xs = x + mu_ref[...] * (_shift_prev(x, p_scr[...], 1) - x)
        p_scr[...] = x[tt - 8:]
        return xs

    r = tshift(r_ref, pr_scr, mur_ref)
    k = tshift(k_ref, pk_scr, muk_ref)
    v = tshift(v_ref, pv_scr, muv_ref)
    lo = tshift(lo_ref, plo_scr, mulo_ref)
    vp = vp_ref[...]
    w0, a0, k_k, k_a, r_k, lnx_w, lnx_b = (vp[i:i + 1] for i in range(7))

    lane = lax.broadcasted_iota(jnp.int32, lo.shape, 1)
    lo_act = jnp.where(lane < 64, jnp.tanh(lo), jnp.where(lane < 128, lo, jax.nn.sigmoid(lo)))
    w_pre = _mm(lo_act, wup_ref[...])
    a_pre = _mm(lo_act, aup_ref[...])
    gate = _mm(lo_act, gup_ref[...])
    w_log = -_softplus(-(w0 + w_pre)) - 0.5
    lw = -jnp.exp(w_log)
    a = jax.nn.sigmoid(a0 + a_pre)

    bi = lax.broadcasted_iota(jnp.int32, (gw, gw), 0) // hd
    bj = lax.broadcasted_iota(jnp.int32, (gw, gw), 1) // hd
    bdmask = bi == bj
    ones_bd = bdmask.astype(F32)

    def head_sum(x):
        return _mm_f32(x, ones_bd)

    kk = k * k_k
    kk = kk * lax.rsqrt(head_sum(kk * kk) + RMS_EPS)
    k2 = k * (1.0 + (a - 1.0) * k_a)
    kka = kk * a

    def bd(x):
        return jnp.where(bdmask, jnp.concatenate([x] * (gw // c), axis=0), 0.0)

    ti = lax.broadcasted_iota(jnp.int32, (c, gw), 0)
    sj = lax.broadcasted_iota(jnp.int32, (c, gw), 1) % c
    strict = sj < ti
    incl = sj <= ti
    tril_f = _tri(c, False).astype(F32)
    ones_c = jnp.ones((c, gw), F32)

    for ci in range(tt // c):
        sl = slice(ci * c, (ci + 1) * c)
        lw_c = lw[sl]
        cl = _mm_f32(tril_f, lw_c)
        wcum = jnp.exp(cl)
        inv_w = jnp.exp(-cl)
        a_h = kka[sl] * inv_w
        b_h = kk[sl] * jnp.exp(cl - lw_c)
        k_h = k2[sl] * inv_w
        r_h = r[sl] * wcum
        v_c = v[sl]
        a_bd = bd(a_h)
        k_bd = bd(k_h)
        v_bd = bd(v_c)
        br = jnp.concatenate([b_h, r_h], axis=0)
        xa = _mm_nt(br, a_bd)
        xk = _mm_nt(br, k_bd)
        l_a = jnp.where(strict, xa[:c], 0.0)
        l_k = jnp.where(strict, xk[:c], 0.0)
        ra = jnp.where(incl, xa[c:], 0.0)
        rk = jnp.where(incl, xk[c:], 0.0)
        p = -l_a
        tm1 = p
        for _ in range(5):
            p = _mm(p, bd(p))
            tm1 = tm1 + _mm(tm1, bd(p)) + p
        state = st_scr[...]
        brh = _mm(br, state)
        rhs = brh[:c] + _mm(l_k, v_bd)
        u = rhs + _mm(tm1, bd(rhs))
        o = brh[c:] - _mm(ra, bd(u)) + _mm(rk, v_bd)
        w_end = wcum[c - 1:c]
        w_col = jnp.exp(lax.dot_general(lw_c, ones_c, (((0,), (0,)), ((), ())),
                                        precision=HIGHEST, preferred_element_type=F32))
        upd = _mm_tn(jnp.concatenate([-(a_h * w_end), k_h * w_end], axis=0),
                     jnp.concatenate([u, v_c], axis=0))
        st_scr[...] = w_col * state + jnp.where(bdmask, upd, 0.0)

        mean = head_sum(o) * (1.0 / hd)
        cen = o - mean
        var = head_sum(cen * cen) * (1.0 / hd)
        on = cen * lax.rsqrt(var + RWKV_GN_EPS) * lnx_w + lnx_b
        bonus = head_sum(r[sl] * k2[sl] * r_k) * v_c
        o_ref[sl, :] = ((on + bonus) * gate[sl]).astype(o_ref.dtype)


def _rwkv(z, bsz, seq, mu, vecs, wup_pad, aup_pad, gup_pad):
    m = z.shape[0]
    tt = 256
    nt = seq // tt
    gw = RWKV_GROUP
    ngrp = RWKV_WIDTH // gw
    zblk = lambda c0: (lambda b, g, t: (b * nt + t, c0 // gw + g))
    mublk = lambda c0: (lambda b, g, t: (0, c0 // gw + g))
    lora_col = (COL_RWKV + 3 * RWKV_WIDTH) // gw
    grp_cols = lambda b, g, t: (0, g)
    return pl.pallas_call(
        _rwkv_body,
        grid=(bsz, ngrp, nt),
        in_specs=[
            pl.BlockSpec((tt, gw), zblk(COL_RWKV)),
            pl.BlockSpec((tt, gw), zblk(COL_RWKV + RWKV_WIDTH)),
            pl.BlockSpec((tt, gw), zblk(COL_RWKV + 2 * RWKV_WIDTH)),
            pl.BlockSpec((tt, gw), lambda b, g, t: (b * nt + t, lora_col)),
            pl.BlockSpec((1, gw), mublk(0)),
            pl.BlockSpec((1, gw), mublk(RWKV_WIDTH)),
            pl.BlockSpec((1, gw), mublk(2 * RWKV_WIDTH)),
            pl.BlockSpec((1, gw), lambda b, g, t: (0, 3 * RWKV_WIDTH // gw)),
            pl.BlockSpec((8, gw), grp_cols),
            pl.BlockSpec((RWKV_LORA, gw), grp_cols),
            pl.BlockSpec((RWKV_LORA, gw), grp_cols),
            pl.BlockSpec((RWKV_LORA, gw), grp_cols),
        ],
        out_specs=pl.BlockSpec((tt, gw), lambda b, g, t: (b * nt + t, g)),
        out_shape=jax.ShapeDtypeStruct((m, RWKV_WIDTH), BF16),
        scratch_shapes=[pltpu.VMEM((8, gw), F32)] * 4 + [pltpu.VMEM((gw, gw), F32)],
        compiler_params=_cparams("parallel", "parallel", "arbitrary"),
        name="rwkv7",
    )(z, z, z, z, mu, mu, mu, mu, vecs, wup_pad, aup_pad, gup_pad)


def _split_bf16(x, terms):
    parts = []
    rem = x
    for i in range(terms):
        p = rem.astype(BF16)
        parts.append(p)
        if i + 1 < terms:
            rem = rem - p.astype(F32)
    return parts


def _mask_mm(mask_bf, x, terms=3):
    return sum(jnp.dot(mask_bf, p, preferred_element_type=F32) for p in _split_bf16(x, terms))


def _mm_mask(x, mask_bf, terms=3):
    return sum(jnp.dot(p, mask_bf, preferred_element_type=F32) for p in _split_bf16(x, terms))


def _gdn2_body(q_ref, k_ref, v_ref, zg_ref, ab_ref, abt_ref, cw_ref, gp_ref, gpt_ref, nw_ref,
               o_ref, pq_scr, pk_scr, pv_scr, st_scr):
    tt = q_ref.shape[0]
    c = GDN_CHUNK
    width = GDN_HEADS * GDN_DK

    @pl.when(pl.program_id(1) == 0)
    def _():
        for s in (pq_scr, pk_scr, pv_scr, st_scr):
            s[...] = jnp.zeros_like(s)

    cw = cw_ref[...]

    def conv_silu(x_ref, p_scr, w):
        x = x_ref[...]
        prev = p_scr[...]
        y = (w[3:4] * x + w[2:3] * _shift_prev(x, prev, 1) + w[1:2] * _shift_prev(x, prev, 2)
             + w[0:1] * _shift_prev(x, prev, 3))
        p_scr[...] = x[tt - 8:]
        return jax.nn.silu(y)

    q = conv_silu(q_ref, pq_scr, cw[:, 0:width])
    k = conv_silu(k_ref, pk_scr, cw[:, width:2 * width])
    v = conv_silu(v_ref, pv_scr, cw[:, 2 * width:])

    ab = ab_ref[...]
    gp = gp_ref[...]
    gpt = gpt_ref[...]
    g_all = -jnp.exp(gp[0:1]) * _softplus(ab + gp[1:2])
    beta_all = jax.nn.sigmoid(ab)

    incl = _tri(c, False)
    strict = _tri(c, True)
    cum_mask = jnp.concatenate([incl.astype(BF16), jnp.ones((c, c), BF16)], axis=0)
    triu_bf = (lax.broadcasted_iota(jnp.int32, (c, c), 0) <= lax.broadcasted_iota(jnp.int32, (c, c), 1)).astype(BF16)
    nw = nw_ref[...]

    n_chunks = tt // c
    heads = range(GDN_HEADS)
    inst = [(ci, h) for ci in range(n_chunks) for h in heads]
    sls = [slice(ci * c, (ci + 1) * c) for ci in range(n_chunks)]
    hss = [slice(h * GDN_DK, (h + 1) * GDN_DK) for h in heads]
    gcs = [_mask_mm(cum_mask, g_all[sl]) for sl in sls]
    gr_all = [_mm_mask(-jnp.exp(gpt[:, 0:1]) * _softplus(abt_ref[ci][0:GDN_HEADS] + gpt[:, 1:2]), triu_bf)
              for ci in range(n_chunks)]

    qs, ks, vs, kbs, decays, egcs, kdecs, gtots, bcols = {}, {}, {}, {}, {}, {}, {}, {}, {}
    for ci, h in inst:
        sl, hs = sls[ci], hss[h]
        qc, kc = q[sl, hs], k[sl, hs]
        qs[ci, h] = qc * lax.rsqrt(jnp.sum(qc * qc, axis=-1, keepdims=True) + RMS_EPS) * (GDN_DK ** -0.5)
        ks[ci, h] = kc * lax.rsqrt(jnp.sum(kc * kc, axis=-1, keepdims=True) + RMS_EPS)
        gcol = gcs[ci][:c, h:h + 1]
        g_last = gcs[ci][c:, h:h + 1]
        grow = gr_all[ci][h:h + 1, :]
        bcols[ci, h] = beta_all[sl, GDN_HEADS + h:GDN_HEADS + h + 1]
        decays[ci, h] = jnp.exp(jnp.where(incl, gcol - grow, -BIG))
        kbs[ci, h] = ks[ci, h] * bcols[ci, h]
        egcs[ci, h] = jnp.exp(gcol)
        kdecs[ci, h] = ks[ci, h] * jnp.exp(g_last - gcol)
        gtots[ci, h] = jnp.exp(jnp.concatenate([g_last] * (GDN_DK // c), axis=0))

    kk = {i: _mm_nt(kbs[i], ks[i]) for i in inst}
    qk = {i: _mm_nt(qs[i], ks[i]) * decays[i] for i in inst}
    p = {i: -jnp.where(strict, kk[i] * decays[i], 0.0) for i in inst}
    tm1 = dict(p)
    for _ in range(5):
        p = {i: _mm(p[i], p[i]) for i in inst}
        tp = {i: _mm(tm1[i], p[i]) for i in inst}
        tm1 = {i: tm1[i] + tp[i] + p[i] for i in inst}
    rhs = {i: jnp.concatenate([v[sls[i[0]], hss[i[1]]] * bcols[i], kbs[i] * egcs[i]], axis=1) for i in inst}
    tr = {i: _mm(tm1[i], rhs[i]) for i in inst}
    sol = {i: rhs[i] + tr[i] for i in inst}

    state = [st_scr[h] for h in heads]
    for ci in range(n_chunks):
        ws = [_mm(sol[ci, h][:, GDN_DV:], state[h]) for h in heads]
        qst = [_mm(qs[ci, h] * egcs[ci, h], state[h]) for h in heads]
        v_new = [sol[ci, h][:, :GDN_DV] - ws[h] for h in heads]
        qv = [_mm(qk[ci, h], v_new[h]) for h in heads]
        kv = [_mm_tn(kdecs[ci, h], v_new[h]) for h in heads]
        state = [state[h] * gtots[ci, h] + kv[h] for h in heads]
        for h in heads:
            o = qst[h] + qv[h]
            o = o * lax.rsqrt(jnp.mean(o * o, axis=-1, keepdims=True) + RMS_EPS) * nw
            o = o * jax.nn.silu(zg_ref[sls[ci], hss[h]])
            o_ref[sls[ci], hss[h]] = o.astype(o_ref.dtype)
    for h in heads:
        st_scr[h] = state[h]


def _gdn2(z, abt, bsz, seq, conv_w, gp, gpt, norm_w):
    m = z.shape[0]
    tt = 128
    nt = seq // tt
    width = GDN_HEADS * GDN_DK
    zblk = lambda c0: pl.BlockSpec((tt, width), lambda b, t: (b * nt + t, c0 // width))
    const2 = lambda b, t: (0, 0)
    return pl.pallas_call(
        _gdn2_body,
        grid=(bsz, nt),
        in_specs=[
            zblk(COL_QKV), zblk(COL_QKV + width), zblk(COL_QKV + 2 * width), zblk(COL_ZG),
            pl.BlockSpec((tt, 128), lambda b, t: (b * nt + t, COL_AB // 128)),
            pl.BlockSpec((tt // GDN_CHUNK, 2 * GDN_HEADS, GDN_CHUNK), lambda b, t: (b * nt + t, 0, 0)),
            pl.BlockSpec((4, 3 * width), const2),
            pl.BlockSpec((8, 128), const2),
            pl.BlockSpec((8, 128), const2),
            pl.BlockSpec((1, 128), const2),
        ],
        out_specs=pl.BlockSpec((tt, width), lambda b, t: (b * nt + t, 0)),
        out_shape=jax.ShapeDtypeStruct((m, width), BF16),
        scratch_shapes=[pltpu.VMEM((8, width), F32)] * 3 + [pltpu.VMEM((GDN_HEADS, GDN_DK, GDN_DV), F32)],
        compiler_params=_cparams("parallel", "arbitrary"),
        name="gated_deltanet",
    )(z, z, z, z, z, abt, conv_w, gp, gpt, norm_w)


def _rwkv2_body(r_ref, k_ref, v_ref, lo_ref, mu_ref, vp_ref, wup_ref, aup_ref, gup_ref, o_ref,
                pr_scr, pk_scr, pv_scr, plo_scr, st_scr):
    tt = r_ref.shape[0]
    c = RWKV_CHUNK
    gw = RWKV_GROUP
    hd = RWKV_HD
    width = RWKV_WIDTH

    @pl.when(pl.program_id(1) == 0)
    def _():
        for s in (pr_scr, pk_scr, pv_scr, plo_scr, st_scr):
            s[...] = jnp.zeros_like(s)

    mu = mu_ref[...]

    def tshift(x_ref, p_scr, m):
        x = x_ref[...]
        xs = x + m * (_shift_prev(x, p_scr[...], 1) - x)
        p_scr[...] = x[tt - 8:]
        return xs

    r = tshift(r_ref, pr_scr, mu[:, 0:width])
    k = tshift(k_ref, pk_scr, mu[:, width:2 * width])
    v = tshift(v_ref, pv_scr, mu[:, 2 * width:3 * width])
    lo = tshift(lo_ref, plo_scr, mu[:, 3 * width:])
    vp = vp_ref[...]
    w0, a0, k_k, k_a, r_k, lnx_w, lnx_b = (vp[i:i + 1] for i in range(7))

    lane = lax.broadcasted_iota(jnp.int32, lo.shape, 1)
    lo_act = jnp.where(lane < 64, jnp.tanh(lo), jnp.where(lane < 128, lo, jax.nn.sigmoid(lo)))
    w_pre = _mm(lo_act, wup_ref[...])
    a_pre = _mm(lo_act, aup_ref[...])
    gate = _mm(lo_act, gup_ref[...])
    w_log = -_softplus(-(w0 + w_pre)) - 0.5
    lw = -jnp.exp(w_log)
    a = jax.nn.sigmoid(a0 + a_pre)

    bi = lax.broadcasted_iota(jnp.int32, (gw, gw), 0) // hd
    bj = lax.broadcasted_iota(jnp.int32, (gw, gw), 1) // hd
    bdmask = bi == bj
    ones_bd = bdmask.astype(BF16)

    def head_sum(x):
        return _mm_mask(x, ones_bd, terms=2)

    def bd(x):
        return jnp.where(bdmask, jnp.concatenate([x] * (gw // c), axis=0), 0.0)

    ti = lax.broadcasted_iota(jnp.int32, (c, gw), 0)
    sj = lax.broadcasted_iota(jnp.int32, (c, gw), 1) % c
    strict = sj < ti
    incl = sj <= ti
    tril_bf = _tri(c, False).astype(BF16)

    kk_all = k * k_k
    k2_all = k * (1.0 + (a - 1.0) * k_a)

    n_chunks = tt // c
    groups = range(width // gw)
    inst = [(ci, g) for ci in range(n_chunks) for g in groups]
    sls = [slice(ci * c, (ci + 1) * c) for ci in range(n_chunks)]
    gss = [slice(g * gw, (g + 1) * gw) for g in groups]
    cl_all = [_mask_mm(tril_bf, lw[sl]) for sl in sls]

    def at(x, i):
        return x[sls[i[0]], gss[i[1]]]

    kk_raw = {i: at(kk_all, i) for i in inst}
    kk_ss = {i: head_sum(kk_raw[i] * kk_raw[i]) for i in inst}
    a_h, b_h, k_h, br, w_end = {}, {}, {}, {}, {}
    for i in inst:
        cl = cl_all[i[0]][:, gss[i[1]]]
        kk = kk_raw[i] * lax.rsqrt(kk_ss[i] + RMS_EPS)
        wcum = jnp.exp(cl)
        inv_w = jnp.exp(-cl)
        a_h[i] = kk * at(a, i) * inv_w
        b_h[i] = kk * jnp.exp(cl - at(lw, i))
        k_h[i] = at(k2_all, i) * inv_w
        br[i] = jnp.concatenate([b_h[i], at(r, i) * wcum], axis=0)
        w_end[i] = wcum[c - 1:c]
    v_bd = {i: bd(at(v, i)) for i in inst}
    xa = {i: _mm_nt(br[i], bd(a_h[i])) for i in inst}
    xk = {i: _mm_nt(br[i], bd(k_h[i])) for i in inst}
    l_k = {i: jnp.where(strict, xk[i][:c], 0.0) for i in inst}
    ra = {i: jnp.where(incl, xa[i][c:], 0.0) for i in inst}
    rk = {i: jnp.where(incl, xk[i][c:], 0.0) for i in inst}
    p = {i: -jnp.where(strict, xa[i][:c], 0.0) for i in inst}
    tm1 = dict(p)
    for _ in range(5):
        p = {i: _mm(p[i], bd(p[i])) for i in inst}
        tp = {i: _mm(tm1[i], bd(p[i])) for i in inst}
        tm1 = {i: tm1[i] + tp[i] + p[i] for i in inst}
    lkv = {i: _mm(l_k[i], v_bd[i]) for i in inst}
    rkv = {i: _mm(rk[i], v_bd[i]) for i in inst}
    so = {i: head_sum(at(r, i) * at(k2_all, i) * r_k[:, gss[i[1]]]) for i in inst}

    state_t = [st_scr[g] for g in groups]
    for ci in range(n_chunks):
        ids = [(ci, g) for g in groups]
        brh = [_mm_nt(br[i], state_t[i[1]]) for i in ids]
        rhs = [brh[g][:c] + lkv[ci, g] for g in groups]
        tu = [_mm(tm1[ci, g], bd(rhs[g])) for g in groups]
        u = [rhs[g] + tu[g] for g in groups]
        rau = [_mm(ra[ci, g], bd(u[g])) for g in groups]
        upd = [_mm_tn(jnp.concatenate([u[g], at(v, (ci, g))], axis=0),
                      jnp.concatenate([-(a_h[ci, g] * w_end[ci, g]), k_h[ci, g] * w_end[ci, g]], axis=0))
               for g in groups]
        state_t = [w_end[ci, g] * state_t[g] + jnp.where(bdmask, upd[g], 0.0) for g in groups]
        o = [brh[g][c:] - rau[g] + rkv[ci, g] for g in groups]
        osum = [head_sum(o[g]) for g in groups]
        cen = [o[g] - osum[g] * (1.0 / hd) for g in groups]
        var = [head_sum(cen[g] * cen[g]) * (1.0 / hd) for g in groups]
        for g in groups:
            gs = gss[g]
            on = cen[g] * lax.rsqrt(var[g] + RWKV_GN_EPS) * lnx_w[:, gs] + lnx_b[:, gs]
            bonus = so[ci, g] * at(v, (ci, g))
            o_ref[sls[ci], gs] = ((on + bonus) * gate[sls[ci], gs]).astype(o_ref.dtype)
    for g in groups:
        st_scr[g] = state_t[g]


def _rwkv2(z, bsz, seq, mu, vecs, wup_pad, aup_pad, gup_pad):
    m = z.shape[0]
    tt = 128
    nt = seq // tt
    gw = RWKV_GROUP
    width = RWKV_WIDTH
    zblk = lambda c0: pl.BlockSpec((tt, width), lambda b, t: (b * nt + t, c0 // width))
    lora_col = (COL_RWKV + 3 * width) // RWKV_LORA
    const2 = lambda b, t: (0, 0)
    return pl.pallas_call(
        _rwkv2_body,
        grid=(bsz, nt),
        in_specs=[
            zblk(COL_RWKV), zblk(COL_RWKV + width), zblk(COL_RWKV + 2 * width),
            pl.BlockSpec((tt, RWKV_LORA), lambda b, t: (b * nt + t, lora_col)),
            pl.BlockSpec(mu.shape, const2),
            pl.BlockSpec((8, width), const2),
            pl.BlockSpec((RWKV_LORA, width), const2),
            pl.BlockSpec((RWKV_LORA, width), const2),
            pl.BlockSpec((RWKV_LORA, width), const2),
        ],
        out_specs=pl.BlockSpec((tt, width), lambda b, t: (b * nt + t, 0)),
        out_shape=jax.ShapeDtypeStruct((m, width), BF16),
        scratch_shapes=[pltpu.VMEM((8, width), F32)] * 3 + [pltpu.VMEM((8, RWKV_LORA), F32),
                                                            pltpu.VMEM((width // gw, gw, gw), F32)],
        compiler_params=_cparams("parallel", "arbitrary"),
        name="rwkv7",
    )(z, z, z, z, mu, vecs, wup_pad, aup_pad, gup_pad)


def _merge_body(h_ref, ya_ref, yb_ref, yc_ref, za_ref, zb_ref, zc_ref, mb_ref, pa_ref, pb_ref,
                pc_ref, wo_ref, nw_ref, hout_ref, xn_ref):
    mb = mb_ref[...]
    ga = jax.nn.sigmoid(za_ref[...] + mb[:, 0:D_MODEL])
    gb = jax.nn.sigmoid(zb_ref[...] + mb[:, D_MODEL:2 * D_MODEL])
    gc = jax.nn.sigmoid(zc_ref[...] + mb[:, 2 * D_MODEL:])
    merged = (ga * jnp.dot(ya_ref[...], pa_ref[...], preferred_element_type=F32)
              + gb * jnp.dot(yb_ref[...], pb_ref[...], preferred_element_type=F32)
              + gc * jnp.dot(yc_ref[...], pc_ref[...], preferred_element_type=F32))
    h = h_ref[...] + _mm(merged, wo_ref[...])
    hout_ref[...] = h
    xn_ref[...] = _rms(h, nw_ref[...]).astype(BF16)


def _merge(h, z, ya, yb, yc, merge_b, pa, pb, pc, wo, norm_w):
    m = h.shape[0]
    tm = 256
    row = lambda c: pl.BlockSpec((tm, D_MODEL), lambda i: (i, c))
    wspec = pl.BlockSpec((D_MODEL, D_MODEL), lambda i: (0, 0))
    mc = COL_MERGE // D_MODEL
    return pl.pallas_call(
        _merge_body,
        grid=(m // tm,),
        in_specs=[row(0), row(0), row(0), row(0), row(mc), row(mc + 1), row(mc + 2),
                  pl.BlockSpec((1, 3 * D_MODEL), lambda i: (0, 0)),
                  wspec, wspec, wspec, wspec,
                  pl.BlockSpec((1, D_MODEL), lambda i: (0, 0))],
        out_specs=[row(0), row(0)],
        out_shape=[jax.ShapeDtypeStruct((m, D_MODEL), F32), jax.ShapeDtypeStruct((m, D_MODEL), BF16)],
        compiler_params=_cparams("parallel"),
        name="merge_outproj",
    )(h, ya, yb, yc, z, z, z, merge_b, pa, pb, pc, wo, norm_w)


def _peer_topk_body(xnt_ref, wqt_ref, keys_ref, thr_ref, e1_ref, s2_ref, e2_ref, q_scr):
    tt = xnt_ref.shape[1]
    nk = PEER_NKEYS
    q_scr[...] = jnp.dot(wqt_ref[...], xnt_ref[...], preferred_element_type=F32)
    row_k = lax.broadcasted_iota(jnp.int32, (PEER_TOPK, tt), 0)

    def top_values(s):
        def body(rnd, carry):
            x, vals = carry
            mx = jnp.max(x, axis=0, keepdims=True)
            return jnp.where(x == mx, -BIG, x), jnp.where(row_k == rnd, mx, vals)
        return lax.fori_loop(0, PEER_TOPK, body, (s, jnp.zeros((PEER_TOPK, tt), F32)))[1]

    def head(h, carry):
        o1 = pl.multiple_of(h * (2 * nk), 2 * nk)
        s1 = _mm(keys_ref[2 * h], q_scr[pl.ds(o1, nk), :])
        s2 = _mm(keys_ref[2 * h + 1], q_scr[pl.ds(o1 + nk, nk), :])
        a1 = top_values(s1)
        a2 = top_values(s2)
        cand = jnp.concatenate([a1[p:p + 1] + a2 for p in range(PEER_TOPK)], axis=0)
        cmax = a1[0:1] + a2[0:1]

        def cbody(rnd, carry):
            cnd, zsum, c_in, c_out = carry
            mx = jnp.max(cnd, axis=0, keepdims=True)
            zsum = zsum + jnp.where(rnd < PEER_TOPK, jnp.exp(mx - cmax), 0.0)
            c_in = jnp.where(rnd == PEER_TOPK - 1, mx, c_in)
            c_out = jnp.where(rnd == PEER_TOPK, mx, c_out)
            return jnp.where(cnd == mx, -BIG, cnd), zsum, c_in, c_out

        zero = jnp.zeros((1, tt), F32)
        _, zsum, c_in, c_out = lax.fori_loop(0, PEER_TOPK + 1, cbody, (cand, zero, zero, zero))
        tau = 0.5 * (c_in + c_out)
        thr_ref[h] = jnp.where(s1 >= a1[PEER_TOPK - 1:PEER_TOPK], tau - s1, BIG)
        e1_ref[h] = jnp.exp(s1 - a1[0:1])
        s2_ref[h] = s2
        e2_ref[h] = jnp.where(s2 >= a2[PEER_TOPK - 1:PEER_TOPK], jnp.exp(s2 - a2[0:1]) / zsum, 0.0)
        return carry

    lax.fori_loop(0, PEER_HEADS, head, 0)


def _peer_topk(xnt, wqt, keys):
    m = xnt.shape[1]
    tt = 256
    out = jax.ShapeDtypeStruct((PEER_HEADS, PEER_NKEYS, m), F32)
    ospec = pl.BlockSpec((PEER_HEADS, PEER_NKEYS, tt), lambda i: (0, 0, i))
    return pl.pallas_call(
        _peer_topk_body,
        grid=(m // tt,),
        in_specs=[pl.BlockSpec((D_MODEL, tt), lambda i: (0, i)),
                  pl.BlockSpec(wqt.shape, lambda i: (0, 0)),
                  pl.BlockSpec(keys.shape, lambda i: (0, 0, 0))],
        out_specs=[ospec] * 4,
        out_shape=[out] * 4,
        scratch_shapes=[pltpu.VMEM((wqt.shape[0], tt), F32)],
        compiler_params=_cparams("parallel"),
        name="peer_topk",
    )(xnt, wqt, keys)


def _peer_dense_body(u_ref, xnt_ref, vt_ref, thr_ref, e1_ref, s2_ref, e2_ref, y_ref, act_scr, a_scr):
    nk = PEER_NKEYS
    tt = xnt_ref.shape[1]

    @pl.when(pl.program_id(1) == 0)
    def _():
        y_ref[...] = jnp.zeros_like(y_ref)

    act_scr[...] = jax.nn.gelu(jnp.dot(u_ref[...], xnt_ref[...], preferred_element_type=F32))

    def blk(ii, carry):
        w = jnp.zeros((nk, tt), F32)
        for h in range(PEER_HEADS):
            thr = thr_ref[h, pl.ds(ii, 1), :]
            e1 = e1_ref[h, pl.ds(ii, 1), :]
            w = w + jnp.where(s2_ref[h] >= thr, e2_ref[h], 0.0) * e1
        rows = pl.ds(pl.multiple_of(ii * nk, nk), nk)
        a_scr[rows, :] = (act_scr[rows, :] * w).astype(BF16)
        return carry

    lax.fori_loop(0, u_ref.shape[0] // nk, blk, 0)
    y_ref[...] += jnp.dot(vt_ref[...], a_scr[...], preferred_element_type=F32)


def _peer_dense(xnt, u_bf, vt_bf, thr, e1, s2, e2, tt):
    m = xnt.shape[1]
    te = 1024
    ib = te // PEER_NKEYS
    sel_i = pl.BlockSpec((PEER_HEADS, ib, tt), lambda i, e: (0, e, i))
    sel_all = pl.BlockSpec((PEER_HEADS, PEER_NKEYS, tt), lambda i, e: (0, 0, i))
    return pl.pallas_call(
        _peer_dense_body,
        grid=(m // tt, PEER_EXPERTS // te),
        in_specs=[pl.BlockSpec((te, D_MODEL), lambda i, e: (e, 0)),
                  pl.BlockSpec((D_MODEL, tt), lambda i, e: (0, i)),
                  pl.BlockSpec((D_MODEL, te), lambda i, e: (0, e)),
                  sel_i, sel_i, sel_all, sel_all],
        out_specs=pl.BlockSpec((D_MODEL, tt), lambda i, e: (0, i)),
        out_shape=jax.ShapeDtypeStruct((D_MODEL, m), F32),
        scratch_shapes=[pltpu.VMEM((te, tt), F32), pltpu.VMEM((te, tt), BF16)],
        compiler_params=_cparams("parallel", "arbitrary"),
        name="peer_dense",
    )(u_bf, xnt, vt_bf, thr, e1, s2, e2)


def _final_body(h_ref, y_ref, nw_ref, o_ref):
    o_ref[...] = _rms(h_ref[...] + y_ref[...], nw_ref[...])


def _final(h, y, norm_w):
    m = h.shape[0]
    tm = 512
    row = pl.BlockSpec((tm, D_MODEL), lambda i: (i, 0))
    return pl.pallas_call(
        _final_body,
        grid=(m // tm,),
        in_specs=[row, row, pl.BlockSpec((1, D_MODEL), lambda i: (0, 0))],
        out_specs=row,
        out_shape=jax.ShapeDtypeStruct((m, D_MODEL), F32),
        compiler_params=_cparams("parallel"),
        name="final_norm",
    )(h, y, norm_w.reshape(1, D_MODEL))


def _pad_rows(rows, width, n_rows=8):
    out = jnp.zeros((n_rows, width), F32)
    for i, r in enumerate(rows):
        flat = r.reshape(-1).astype(F32)
        out = out.at[i, :flat.shape[0]].set(flat)
    return out


def _chunk_rows(ab):
    m, n = ab.shape
    return jnp.transpose(ab.reshape(m // GDN_CHUNK, GDN_CHUNK, n), (0, 2, 1))


def _block_diag_groups(w):
    per = LRU_GROUP // LRU_BLOCK_DIM
    n_grp = w.shape[0] // per
    out = jnp.zeros((n_grp, LRU_GROUP, LRU_GROUP), F32)
    for g in range(n_grp):
        for j in range(per):
            o = j * LRU_BLOCK_DIM
            out = out.at[g, o:o + LRU_BLOCK_DIM, o:o + LRU_BLOCK_DIM].set(w[g * per + j])
    return out.astype(BF16)


def _prep_w_in(w):
    pad = jnp.zeros((D_MODEL, IN_WIDTH_PAD - COL_AB - 2 * GDN_HEADS), w.dtype)
    return jnp.concatenate([w[:, :6144], w[:, 9488:12560], w[:, 6160:9488], w[:, 6144:6160], pad],
                           axis=1).astype(BF16)


def _layer(l, h, y, bsz, seq, p):
    z_h, z = _inproj(h, y, p["norm_mix_w"][l], _prep_w_in(p["w_in"][l]))
    h = z_h

    lru_vecs = _pad_rows([p["lru_conv_b"][l], p["lru_b_a"][l], p["lru_b_x"][l], p["lru_lambda"][l]], LRU_WIDTH)
    ya = _lru(z, bsz, seq, p["lru_conv_w"][l], lru_vecs,
              _block_diag_groups(p["lru_w_a"][l]), _block_diag_groups(p["lru_w_x"][l]))

    abt = _chunk_rows(z[:, COL_AB:COL_AB + 2 * GDN_HEADS])
    gp = _pad_rows([p["gdn_a_log"][l], p["gdn_dt_bias"][l]], 128)
    gpt = jnp.zeros((8, 128), F32).at[:, 0].set(p["gdn_a_log"][l]).at[:, 1].set(p["gdn_dt_bias"][l])
    yb = _gdn2(z, abt, bsz, seq, p["gdn_conv_w"][l], gp, gpt, p["gdn_norm_w"][l].reshape(1, GDN_DV))

    rw_vecs = _pad_rows([p["rwkv_w0"][l], p["rwkv_a0"][l], p["rwkv_k_k"][l], p["rwkv_k_a"][l],
                         p["rwkv_r_k"][l], p["rwkv_lnx_w"][l], p["rwkv_lnx_b"][l]], RWKV_WIDTH)
    zl = jnp.zeros((RWKV_LORA, RWKV_WIDTH), F32)
    wup = zl.at[0:64].set(p["rwkv_w_up"][l]).astype(BF16)
    aup = zl.at[64:128].set(p["rwkv_a_up"][l]).astype(BF16)
    gup = zl.at[128:256].set(p["rwkv_g_up"][l]).astype(BF16)
    yc = _rwkv2(z, bsz, seq, p["rwkv_mu"][l].reshape(1, -1), rw_vecs, wup, aup, gup)

    h, xn = _merge(h, z, ya, yb, yc, p["merge_b"][l].reshape(1, -1), p["p_lru"][l].astype(BF16),
                   p["p_gdn"][l].astype(BF16), p["p_rwkv"][l].astype(BF16), p["w_out"][l].astype(BF16),
                   p["norm_ffn_w"][l].reshape(1, D_MODEL))

    xnt = jnp.transpose(xn)
    wqt = jnp.transpose(p["peer_wq"][l]).astype(BF16)
    keys = p["peer_keys"][l].reshape(2 * PEER_HEADS, PEER_NKEYS, PEER_HALF).astype(BF16)
    thr, e1, s2, e2 = _peer_topk(xnt, wqt, keys)
    tt = 512 if xnt.shape[1] % 512 == 0 else 256
    yt = _peer_dense(xnt, p["peer_u"][l].astype(BF16), jnp.transpose(p["peer_v"][l]).astype(BF16),
                     thr, e1, s2, e2, tt)
    return h, jnp.transpose(yt)


def kernel(x, norm_mix_w, norm_ffn_w, final_norm_w, w_in, lru_conv_w, lru_conv_b, lru_w_a, lru_b_a,
           lru_w_x, lru_b_x, lru_lambda, gdn_conv_w, gdn_a_log, gdn_dt_bias, gdn_norm_w, rwkv_mu, rwkv_w0,
           rwkv_w_up, rwkv_a0, rwkv_a_up, rwkv_g_up, rwkv_k_k, rwkv_k_a, rwkv_r_k, rwkv_lnx_w, rwkv_lnx_b,
           merge_b, p_lru, p_gdn, p_rwkv, w_out, peer_wq, peer_keys, peer_u, peer_v):
    p = dict(locals())
    bsz, seq, dim = x.shape
    h = x.reshape(bsz * seq, dim)
    y = None
    for l in range(DEPTH):
        h, y = _layer(l, h, y, bsz, seq, p)
    return _final(h, y, final_norm_w).reshape(bsz, seq, dim)
```

```python
import functools

import jax
import jax.numpy as jnp
from jax import lax
from jax.experimental import pallas as pl
from jax.experimental.pallas import tpu as pltpu

F32 = jnp.float32
BF16 = jnp.bfloat16
HIGHEST = lax.Precision.HIGHEST

D_MODEL = 1024
DEPTH = 2
RMS_EPS = 1e-6

LRU_WIDTH = 1024
LRU_BLOCK_DIM = 64
LRU_C = 8.0
LRU_GROUP = 256

GDN_HEADS = 8
GDN_DK = 128
GDN_DV = 128
GDN_CHUNK = 64

RWKV_HEADS = 16
RWKV_HD = 64
RWKV_WIDTH = 1024
RWKV_GN_EPS = 64e-5
RWKV_CHUNK = 64
RWKV_GROUP = 256
RWKV_LORA = 256

PEER_HEADS = 8
PEER_NKEYS = 128
PEER_EXPERTS = PEER_NKEYS * PEER_NKEYS
PEER_HALF = 128
PEER_TOPK = 16

COL_U = 0
COL_GATE = 1024
COL_QKV = 2048
COL_ZG = 5120
COL_MERGE = 6144
COL_RWKV = 9216
COL_AB = 12544
IN_WIDTH_PAD = 12672

VMEM_LIMIT = 48 * 1024 * 1024
BIG = 3.0e38


def _cparams(*sem):
    return pltpu.CompilerParams(dimension_semantics=sem, vmem_limit_bytes=VMEM_LIMIT)


def _mm(a, b):
    return jnp.dot(a.astype(BF16), b.astype(BF16), preferred_element_type=F32)


def _mm_nt(a, b):
    return lax.dot_general(a.astype(BF16), b.astype(BF16), (((1,), (1,)), ((), ())),
                           preferred_element_type=F32)


def _mm_tn(a, b):
    return lax.dot_general(a.astype(BF16), b.astype(BF16), (((0,), (0,)), ((), ())),
                           preferred_element_type=F32)


def _mm_f32(a, b):
    return jnp.dot(a, b, precision=HIGHEST, preferred_element_type=F32)


def _softplus(x):
    return jnp.maximum(x, 0.0) + jnp.log1p(jnp.exp(-jnp.abs(x)))


def _rms(x, w):
    return x * lax.rsqrt(jnp.mean(x * x, axis=-1, keepdims=True) + RMS_EPS) * w


def _shift_prev(x, prev8, s):
    r = pltpu.roll(x, s, 0)
    pr = pltpu.roll(prev8, s, 0)
    rows8 = lax.broadcasted_iota(jnp.int32, prev8.shape, 0)
    head = jnp.where(rows8 < s, pr, r[:8])
    return jnp.concatenate([head, r[8:]], axis=0)


def _shift_fill(x, d, fill):
    n, c = x.shape
    if d % 8 == 0:
        return jnp.concatenate([jnp.full((d, c), fill, x.dtype), x[:n - d]], axis=0)
    r = pltpu.roll(x, d, 0)
    rows8 = lax.broadcasted_iota(jnp.int32, (8, c), 0)
    head = jnp.where(rows8 < d, fill, r[:8])
    return jnp.concatenate([head, r[8:]], axis=0)


def _tri(n, strict):
    i = lax.broadcasted_iota(jnp.int32, (n, n), 0)
    j = lax.broadcasted_iota(jnp.int32, (n, n), 1)
    return (j < i) if strict else (j <= i)


def _inproj_body(add_y, *refs):
    if add_y:
        h_ref, y_ref, nw_ref, w_ref, hout_ref, z_ref, xn_scr = refs
    else:
        h_ref, nw_ref, w_ref, z_ref, xn_scr = refs

    @pl.when(pl.program_id(1) == 0)
    def _():
        h = h_ref[...]
        if add_y:
            h = h + y_ref[...]
            hout_ref[...] = h
        xn_scr[...] = _rms(h, nw_ref[...]).astype(BF16)

    z_ref[...] = jnp.dot(xn_scr[...], w_ref[...], preferred_element_type=F32)


def _inproj(h, y, norm_w, w_pad):
    m = h.shape[0]
    tm, tn = 512, 1152
    add_y = y is not None
    row = pl.BlockSpec((tm, D_MODEL), lambda i, j: (i, 0))
    in_specs = [row] + ([row] if add_y else []) + [
        pl.BlockSpec((1, D_MODEL), lambda i, j: (0, 0)),
        pl.BlockSpec((D_MODEL, tn), lambda i, j: (0, j)),
    ]
    z_spec = pl.BlockSpec((tm, tn), lambda i, j: (i, j))
    z_shape = jax.ShapeDtypeStruct((m, IN_WIDTH_PAD), F32)
    args = (h,) + ((y,) if add_y else ()) + (norm_w.reshape(1, D_MODEL), w_pad)
    out = pl.pallas_call(
        functools.partial(_inproj_body, add_y),
        grid=(m // tm, IN_WIDTH_PAD // tn),
        in_specs=in_specs,
        out_specs=[row, z_spec] if add_y else z_spec,
        out_shape=[jax.ShapeDtypeStruct((m, D_MODEL), F32), z_shape] if add_y else z_shape,
        scratch_shapes=[pltpu.VMEM((tm, D_MODEL), BF16)],
        compiler_params=_cparams("parallel", "arbitrary"),
        name="norm_inproj",
    )(*args)
    return (out[0], out[1]) if add_y else (h, out)


def _lru_body(zu_ref, zg_ref, cw_ref, vp_ref, wa_ref, wx_ref, o_ref, prev_scr, hc_scr):
    tt = zu_ref.shape[0]

    @pl.when(pl.program_id(1) == 0)
    def _():
        prev_scr[...] = jnp.zeros_like(prev_scr)
        hc_scr[...] = jnp.zeros_like(hc_scr)

    u = zu_ref[...]
    prev = prev_scr[...]
    cw = cw_ref[...]
    vp = vp_ref[...]
    xc = (cw[3:4] * u + cw[2:3] * _shift_prev(u, prev, 1) + cw[1:2] * _shift_prev(u, prev, 2)
          + cw[0:1] * _shift_prev(u, prev, 3) + vp[0:1])
    prev_scr[...] = u[tt - 8:]

    xcb = xc.astype(BF16)
    n_grp = LRU_WIDTH // LRU_GROUP
    pre_a = jnp.concatenate(
        [jnp.dot(xcb[:, g * LRU_GROUP:(g + 1) * LRU_GROUP], wa_ref[g], preferred_element_type=F32)
         for g in range(n_grp)], axis=1)
    pre_x = jnp.concatenate(
        [jnp.dot(xcb[:, g * LRU_GROUP:(g + 1) * LRU_GROUP], wx_ref[g], preferred_element_type=F32)
         for g in range(n_grp)], axis=1)
    r = jax.nn.sigmoid(pre_a + vp[1:2])
    i = jax.nn.sigmoid(pre_x + vp[2:3])
    log_a = (-LRU_C * r) * _softplus(-vp[3:4])
    a = jnp.exp(log_a)
    b = jnp.sqrt(-jnp.tanh(log_a) * (a * a + 1.0)) * (i * xc)

    d = 1
    while d < tt:
        b = a * _shift_fill(b, d, 0.0) + b
        a = a * _shift_fill(a, d, 1.0)
        d *= 2
    hcar = hc_scr[...]
    hval = b + a * hcar[0:1]
    hc_scr[...] = jnp.broadcast_to(hval[tt - 1:tt], hcar.shape)
    o_ref[...] = (hval * jax.nn.gelu(zg_ref[...])).astype(o_ref.dtype)


def _lru(z, bsz, seq, conv_w, vecs, wa_bd, wx_bd):
    m = z.shape[0]
    tt = 256
    nt = seq // tt
    rowmap = lambda c: (lambda b, t: (b * nt + t, c))
    const2 = lambda b, t: (0, 0)
    return pl.pallas_call(
        _lru_body,
        grid=(bsz, nt),
        in_specs=[
            pl.BlockSpec((tt, LRU_WIDTH), rowmap(COL_U // LRU_WIDTH)),
            pl.BlockSpec((tt, LRU_WIDTH), rowmap(COL_GATE // LRU_WIDTH)),
            pl.BlockSpec((4, LRU_WIDTH), const2),
            pl.BlockSpec((8, LRU_WIDTH), const2),
            pl.BlockSpec(wa_bd.shape, lambda b, t: (0, 0, 0)),
            pl.BlockSpec(wx_bd.shape, lambda b, t: (0, 0, 0)),
        ],
        out_specs=pl.BlockSpec((tt, LRU_WIDTH), rowmap(0)),
        out_shape=jax.ShapeDtypeStruct((m, LRU_WIDTH), BF16),
        scratch_shapes=[pltpu.VMEM((8, LRU_WIDTH), F32), pltpu.VMEM((8, LRU_WIDTH), F32)],
        compiler_params=_cparams("parallel", "arbitrary"),
        name="rglru",
    )(z, z, conv_w, vecs, wa_bd, wx_bd)


def _gdn_body(q_ref, k_ref, v_ref, zg_ref, ab_ref, abt_ref, cwq_ref, cwk_ref, cwv_ref,
              gp_ref, gpt_ref, nw_ref, o_ref, pq_scr, pk_scr, pv_scr, st_scr):
    tt = q_ref.shape[0]
    c = GDN_CHUNK
    h = pl.program_id(1)

    @pl.when(pl.program_id(2) == 0)
    def _():
        pq_scr[...] = jnp.zeros_like(pq_scr)
        pk_scr[...] = jnp.zeros_like(pk_scr)
        pv_scr[...] = jnp.zeros_like(pv_scr)
        st_scr[...] = jnp.zeros_like(st_scr)

    def conv_silu(x_ref, p_scr, cw_ref):
        x = x_ref[...]
        prev = p_scr[...]
        cw = cw_ref[...]
        y = (cw[3:4] * x + cw[2:3] * _shift_prev(x, prev, 1) + cw[1:2] * _shift_prev(x, prev, 2)
             + cw[0:1] * _shift_prev(x, prev, 3))
        p_scr[...] = x[tt - 8:]
        return jax.nn.silu(y)

    q = conv_silu(q_ref, pq_scr, cwq_ref)
    k = conv_silu(k_ref, pk_scr, cwk_ref)
    v = conv_silu(v_ref, pv_scr, cwv_ref)
    q = q * lax.rsqrt(jnp.sum(q * q, axis=-1, keepdims=True) + RMS_EPS) * (GDN_DK ** -0.5)
    k = k * lax.rsqrt(jnp.sum(k * k, axis=-1, keepdims=True) + RMS_EPS)

    ab = ab_ref[...]
    gp = gp_ref[...]
    lane = lax.broadcasted_iota(jnp.int32, (c, ab.shape[1]), 1)
    g_all = -jnp.exp(gp[0:1]) * _softplus(ab + gp[1:2])
    beta_all = jax.nn.sigmoid(ab)
    gpt = gpt_ref[...]
    sub = lax.broadcasted_iota(jnp.int32, (GDN_HEADS, c), 0)

    incl = _tri(c, False)
    strict = _tri(c, True)
    tril_f = incl.astype(F32)
    ones_f = jnp.ones((c, c), F32)
    triu_f =(lax.broadcasted_iota(jnp.int32, (c, c), 0) <= lax.broadcasted_iota(jnp.int32, (c, c), 1)).astype(F32)
    nw = nw_ref[...]

    for ci in range(tt // c):
        sl = slice(ci * c, (ci + 1) * c)
        qc, kc, vc = q[sl], k[sl], v[sl]
        gc_all = _mm_f32(tril_f, g_all[sl])
        gcol = jnp.sum(jnp.where(lane == h, gc_all, 0.0), axis=1, keepdims=True)
        g_rows = -jnp.exp(gpt[:, 0:1]) * _softplus(abt_ref[ci][0:GDN_HEADS] + gpt[:, 1:2])
        gr_all = _mm_f32(g_rows, triu_f)
        grow = jnp.sum(jnp.where(sub == h, gr_all, 0.0), axis=0, keepdims=True)
        bcol = jnp.sum(jnp.where(lane == h + GDN_HEADS, beta_all[sl], 0.0), axis=1, keepdims=True)
        decay = jnp.exp(jnp.where(incl, gcol - grow, -BIG))
        kb = kc * bcol
        a_mat = jnp.where(strict, _mm_nt(kb, kc) * decay, 0.0)
        p = -a_mat
        tm1 = p
        for _ in range(5):
            p = _mm(p, p)
            tm1 = tm1 + _mm(tm1, p) + p
        egc = jnp.exp(gcol)
        rhs = jnp.concatenate([vc * bcol, kb * egc], axis=1)
        sol = rhs + _mm(tm1, rhs)
        u_c, w_c = sol[:, :GDN_DV], sol[:, GDN_DV:]
        qk = _mm_nt(qc, kc) * decay
        q_dec = qc * egc
        gt_all = _mm_f32(ones_f, g_all[sl])
        g_last = jnp.sum(jnp.where(lane == h, gt_all, 0.0), axis=1, keepdims=True)
        k_dec = kc * jnp.exp(g_last - gcol)
        state = st_scr[...]
        v_new = u_c - _mm(w_c, state)
        o = _mm(q_dec, state) + _mm(qk, v_new)
        g_tot = jnp.exp(jnp.concatenate([g_last] * (GDN_DK // c), axis=0))
        st_scr[...] = state * g_tot + _mm_tn(k_dec, v_new)
        o = o * lax.rsqrt(jnp.mean(o * o, axis=-1, keepdims=True) + RMS_EPS) * nw
        o = o * jax.nn.silu(zg_ref[sl, :])
        o_ref[sl, :] = o.astype(o_ref.dtype)


def _gdn(z, abt, bsz, seq, conv_w, gp, gpt, norm_w):
    m = z.shape[0]
    tt = 256
    nt = seq // tt
    lane_blk = lambda c0: (lambda b, h, t: (b * nt + t, c0 // 128 + h))
    cw_blk = lambda c0: (lambda b, h, t: (0, c0 // 128 + h))
    const2 = lambda b, h, t: (0, 0)
    return pl.pallas_call(
        _gdn_body,
        grid=(bsz, GDN_HEADS, nt),
        in_specs=[
            pl.BlockSpec((tt, 128), lane_blk(COL_QKV)),
            pl.BlockSpec((tt, 128), lane_blk(COL_QKV + 1024)),
            pl.BlockSpec((tt, 128), lane_blk(COL_QKV + 2048)),
            pl.BlockSpec((tt, 128), lane_blk(COL_ZG)),
            pl.BlockSpec((tt, 128), lambda b, h, t: (b * nt + t, COL_AB // 128)),
            pl.BlockSpec((tt // GDN_CHUNK, 2 * GDN_HEADS, GDN_CHUNK), lambda b, h, t: (b * nt + t, 0, 0)),
            pl.BlockSpec((4, 128), cw_blk(0)),
            pl.BlockSpec((4, 128), cw_blk(1024)),
            pl.BlockSpec((4, 128), cw_blk(2048)),
            pl.BlockSpec((8, 128), const2),
            pl.BlockSpec((8, 128), const2),
            pl.BlockSpec((1, 128), const2),
        ],
        out_specs=pl.BlockSpec((tt, 128), lambda b, h, t: (b * nt + t, h)),
        out_shape=jax.ShapeDtypeStruct((m, GDN_HEADS * GDN_DV), BF16),
        scratch_shapes=[pltpu.VMEM((8, 128), F32), pltpu.VMEM((8, 128), F32), pltpu.VMEM((8, 128), F32),
                        pltpu.VMEM((GDN_DK, GDN_DV), F32)],
        compiler_params=_cparams("parallel", "parallel", "arbitrary"),
        name="gated_deltanet",
    )(z, z, z, z, z, abt, conv_w, conv_w, conv_w, gp, gpt, norm_w)


def _rwkv_body(r_ref, k_ref, v_ref, lo_ref, mur_ref, muk_ref, muv_ref, mulo_ref, vp_ref,
               wup_ref, aup_ref, gup_ref, o_ref, pr_scr, pk_scr, pv_scr, plo_scr, st_scr):
    tt = r_ref.shape[0]
    c = RWKV_CHUNK
    gw = RWKV_GROUP
    hd = RWKV_HD

    @pl.when(pl.program_id(2) == 0)
    def _():
        for s in (pr_scr, pk_scr, pv_scr, plo_scr, st_scr):
            s[...] = jnp.zeros_like(s)

    def tshift(x_ref, p_scr, mu_ref):
        x = x_ref[...]
        xs = x + mu_ref[...] * (_shift_prev(x, p_scr[...], 1) - x)
        p_scr[...] = x[tt - 8:]
        return xs

    r = tshift(r_ref, pr_scr, mur_ref)
    k = tshift(k_ref, pk_scr, muk_ref)
    v = tshift(v_ref, pv_scr, muv_ref)
    lo = tshift(lo_ref, plo_scr, mulo_ref)
    vp = vp_ref[...]
    w0, a0, k_k, k_a, r_k, lnx_w, lnx_b = (vp[i:i + 1] for i in range(7))

    lane = lax.broadcasted_iota(jnp.int32, lo.shape, 1)
    lo_act = jnp.where(lane < 64, jnp.tanh(lo), jnp.where(lane < 128, lo, jax.nn.sigmoid(lo)))
    w_pre = _mm(lo_act, wup_ref[...])
    a_pre = _mm(lo_act, aup_ref[...])
    gate = _mm(lo_act, gup_ref[...])
    w_log = -_softplus(-(w0 + w_pre)) - 0.5
    lw = -jnp.exp(w_log)
    a = jax.nn.sigmoid(a0 + a_pre)

    bi = lax.broadcasted_iota(jnp.int32, (gw, gw), 0) // hd
    bj = lax.broadcasted_iota(jnp.int32, (gw, gw), 1) // hd
    bdmask = bi == bj
    ones_bd = bdmask.astype(F32)

    def head_sum(x):
        return _mm_f32(x, ones_bd)

    kk = k * k_k
    kk = kk * lax.rsqrt(head_sum(kk * kk) + RMS_EPS)
    k2 = k * (1.0 + (a - 1.0) * k_a)
    kka = kk * a

    def bd(x):
        return jnp.where(bdmask, jnp.concatenate([x] * (gw // c), axis=0), 0.0)

    ti = lax.broadcasted_iota(jnp.int32, (c, gw), 0)
    sj = lax.broadcasted_iota(jnp.int32, (c, gw), 1) % c
    strict = sj < ti
    incl = sj <= ti
    tril_f = _tri(c, False).astype(F32)
    ones_c = jnp.ones((c, gw), F32)

    for ci in range(tt // c):
        sl = slice(ci * c, (ci + 1) * c)
        lw_c = lw[sl]
        cl = _mm_f32(tril_f, lw_c)
        wcum = jnp.exp(cl)
        inv_w = jnp.exp(-cl)
        a_h = kka[sl] * inv_w
        b_h = kk[sl] * jnp.exp(cl - lw_c)
        k_h = k2[sl] * inv_w
        r_h = r[sl] * wcum
        v_c = v[sl]
        a_bd = bd(a_h)
        k_bd = bd(k_h)
        v_bd = bd(v_c)
        br = jnp.concatenate([b_h, r_h], axis=0)
        xa = _mm_nt(br, a_bd)
        xk = _mm_nt(br, k_bd)
        l_a = jnp.where(strict, xa[:c], 0.0)
        l_k = jnp.where(strict, xk[:c], 0.0)
        ra = jnp.where(incl, xa[c:], 0.0)
        rk = jnp.where(incl, xk[c:], 0.0)
        p = -l_a
        tm1 = p
        for _ in range(5):
            p = _mm(p, bd(p))
            tm1 = tm1 + _mm(tm1, bd(p)) + p
        state = st_scr[...]
        brh = _mm(br, state)
        rhs = brh[:c] + _mm(l_k, v_bd)
        u = rhs + _mm(tm1, bd(rhs))
        o = brh[c:] - _mm(ra, bd(u)) + _mm(rk, v_bd)
        w_end = wcum[c - 1:c]
        w_col = jnp.exp(lax.dot_general(lw_c, ones_c, (((0,), (0,)), ((), ())),
                                        precision=HIGHEST, preferred_element_type=F32))
        upd = _mm_tn(jnp.concatenate([-(a_h * w_end), k_h * w_end], axis=0),
                     jnp.concatenate([u, v_c], axis=0))
        st_scr[...] = w_col * state + jnp.where(bdmask, upd, 0.0)

        mean = head_sum(o) * (1.0 / hd)
        cen = o - mean
        var = head_sum(cen * cen) * (1.0 / hd)
        on = cen * lax.rsqrt(var + RWKV_GN_EPS) * lnx_w + lnx_b
        bonus = head_sum(r[sl] * k2[sl] * r_k) * v_c
        o_ref[sl, :] = ((on + bonus) * gate[sl]).astype(o_ref.dtype)


def _rwkv(z, bsz, seq, mu, vecs, wup_pad, aup_pad, gup_pad):
    m = z.shape[0]
    tt = 256
    nt = seq // tt
    gw = RWKV_GROUP
    ngrp = RWKV_WIDTH // gw
    zblk = lambda c0: (lambda b, g, t: (b * nt + t, c0 // gw + g))
    mublk = lambda c0: (lambda b, g, t: (0, c0 // gw + g))
    lora_col = (COL_RWKV + 3 * RWKV_WIDTH) // gw
    grp_cols = lambda b, g, t: (0, g)
    return pl.pallas_call(
        _rwkv_body,
        grid=(bsz, ngrp, nt),
        in_specs=[
            pl.BlockSpec((tt, gw), zblk(COL_RWKV)),
            pl.BlockSpec((tt, gw), zblk(COL_RWKV + RWKV_WIDTH)),
            pl.BlockSpec((tt, gw), zblk(COL_RWKV + 2 * RWKV_WIDTH)),
            pl.BlockSpec((tt, gw), lambda b, g, t: (b * nt + t, lora_col)),
            pl.BlockSpec((1, gw), mublk(0)),
            pl.BlockSpec((1, gw), mublk(RWKV_WIDTH)),
            pl.BlockSpec((1, gw), mublk(2 * RWKV_WIDTH)),
            pl.BlockSpec((1, gw), lambda b, g, t: (0, 3 * RWKV_WIDTH // gw)),
            pl.BlockSpec((8, gw), grp_cols),
            pl.BlockSpec((RWKV_LORA, gw), grp_cols),
            pl.BlockSpec((RWKV_LORA, gw), grp_cols),
            pl.BlockSpec((RWKV_LORA, gw), grp_cols),
        ],
        out_specs=pl.BlockSpec((tt, gw), lambda b, g, t: (b * nt + t, g)),
        out_shape=jax.ShapeDtypeStruct((m, RWKV_WIDTH), BF16),
        scratch_shapes=[pltpu.VMEM((8, gw), F32)] * 4 + [pltpu.VMEM((gw, gw), F32)],
        compiler_params=_cparams("parallel", "parallel", "arbitrary"),
        name="rwkv7",
    )(z, z, z, z, mu, mu, mu, mu, vecs, wup_pad, aup_pad, gup_pad)


def _split_bf16(x, terms):
    parts = []
    rem = x
    for i in range(terms):
        p = rem.astype(BF16)
        parts.append(p)
        if i + 1 < terms:
            rem = rem - p.astype(F32)
    return parts


def _mask_mm(mask_bf, x, terms=3):
    return sum(jnp.dot(mask_bf, p, preferred_element_type=F32) for p in _split_bf16(x, terms))


def _mm_mask(x, mask_bf, terms=3):
    return sum(jnp.dot(p, mask_bf, preferred_element_type=F32) for p in _split_bf16(x, terms))


def _gdn2_body(q_ref, k_ref, v_ref, zg_ref, ab_ref, abt_ref, cw_ref, gp_ref, gpt_ref, nw_ref,
               o_ref, pq_scr, pk_scr, pv_scr, st_scr):
    tt = q_ref.shape[0]
    c = GDN_CHUNK
    width = GDN_HEADS * GDN_DK

    @pl.when(pl.program_id(1) == 0)
    def _():
        for s in (pq_scr, pk_scr, pv_scr, st_scr):
            s[...] = jnp.zeros_like(s)

    cw = cw_ref[...]

    def conv_silu(x_ref, p_scr, w):
        x = x_ref[...]
        prev = p_scr[...]
        y = (w[3:4] * x + w[2:3] * _shift_prev(x, prev, 1) + w[1:2] * _shift_prev(x, prev, 2)
             + w[0:1] * _shift_prev(x, prev, 3))
        p_scr[...] = x[tt - 8:]
        return jax.nn.silu(y)

    q = conv_silu(q_ref, pq_scr, cw[:, 0:width])
    k = conv_silu(k_ref, pk_scr, cw[:, width:2 * width])
    v = conv_silu(v_ref, pv_scr, cw[:, 2 * width:])

    ab = ab_ref[...]
    gp = gp_ref[...]
    gpt = gpt_ref[...]
    g_all = -jnp.exp(gp[0:1]) * _softplus(ab + gp[1:2])
    beta_all = jax.nn.sigmoid(ab)

    incl = _tri(c, False)
    strict = _tri(c, True)
    cum_mask = jnp.concatenate([incl.astype(BF16), jnp.ones((c, c), BF16)], axis=0)
    triu_bf = (lax.broadcasted_iota(jnp.int32, (c, c), 0) <= lax.broadcasted_iota(jnp.int32, (c, c), 1)).astype(BF16)
    nw = nw_ref[...]

    n_chunks = tt // c
    heads = range(GDN_HEADS)
    inst = [(ci, h) for ci in range(n_chunks) for h in heads]
    sls = [slice(ci * c, (ci + 1) * c) for ci in range(n_chunks)]
    hss = [slice(h * GDN_DK, (h + 1) * GDN_DK) for h in heads]
    gcs = [_mask_mm(cum_mask, g_all[sl]) for sl in sls]
    gr_all = [_mm_mask(-jnp.exp(gpt[:, 0:1]) * _softplus(abt_ref[ci][0:GDN_HEADS] + gpt[:, 1:2]), triu_bf)
              for ci in range(n_chunks)]

    qs, ks, vs, kbs, decays, egcs, kdecs, gtots, bcols = {}, {}, {}, {}, {}, {}, {}, {}, {}
    for ci, h in inst:
        sl, hs = sls[ci], hss[h]
        qc, kc = q[sl, hs], k[sl, hs]
        qs[ci, h] = qc * lax.rsqrt(jnp.sum(qc * qc, axis=-1, keepdims=True) + RMS_EPS) * (GDN_DK ** -0.5)
        ks[ci, h] = kc * lax.rsqrt(jnp.sum(kc * kc, axis=-1, keepdims=True) + RMS_EPS)
        gcol = gcs[ci][:c, h:h + 1]
        g_last = gcs[ci][c:, h:h + 1]
        grow = gr_all[ci][h:h + 1, :]
        bcols[ci, h] = beta_all[sl, GDN_HEADS + h:GDN_HEADS + h + 1]
        decays[ci, h] = jnp.exp(jnp.where(incl, gcol - grow, -BIG))
        kbs[ci, h] = ks[ci, h] * bcols[ci, h]
        egcs[ci, h] = jnp.exp(gcol)
        kdecs[ci, h] = ks[ci, h] * jnp.exp(g_last - gcol)
        gtots[ci, h] = jnp.exp(jnp.concatenate([g_last] * (GDN_DK // c), axis=0))

    kk = {i: _mm_nt(kbs[i], ks[i]) for i in inst}
    qk = {i: _mm_nt(qs[i], ks[i]) * decays[i] for i in inst}
    p = {i: -jnp.where(strict, kk[i] * decays[i], 0.0) for i in inst}
    tm1 = dict(p)
    for _ in range(5):
        p = {i: _mm(p[i], p[i]) for i in inst}
        tp = {i: _mm(tm1[i], p[i]) for i in inst}
        tm1 = {i: tm1[i] + tp[i] + p[i] for i in inst}
    rhs = {i: jnp.concatenate([v[sls[i[0]], hss[i[1]]] * bcols[i], kbs[i] * egcs[i]], axis=1) for i in inst}
    tr = {i: _mm(tm1[i], rhs[i]) for i in inst}
    sol = {i: rhs[i] + tr[i] for i in inst}

    state = [st_scr[h] for h in heads]
    for ci in range(n_chunks):
        ws = [_mm(sol[ci, h][:, GDN_DV:], state[h]) for h in heads]
        qst = [_mm(qs[ci, h] * egcs[ci, h], state[h]) for h in heads]
        v_new = [sol[ci, h][:, :GDN_DV] - ws[h] for h in heads]
        qv = [_mm(qk[ci, h], v_new[h]) for h in heads]
        kv = [_mm_tn(kdecs[ci, h], v_new[h]) for h in heads]
        state = [state[h] * gtots[ci, h] + kv[h] for h in heads]
        for h in heads:
            o = qst[h] + qv[h]
            o = o * lax.rsqrt(jnp.mean(o * o, axis=-1, keepdims=True) + RMS_EPS) * nw
            o = o * jax.nn.silu(zg_ref[sls[ci], hss[h]])
            o_ref[sls[ci], hss[h]] = o.astype(o_ref.dtype)
    for h in heads:
        st_scr[h] = state[h]


def _gdn2(z, abt, bsz, seq, conv_w, gp, gpt, norm_w):
    m = z.shape[0]
    tt = 128
    nt = seq // tt
    width = GDN_HEADS * GDN_DK
    zblk = lambda c0: pl.BlockSpec((tt, width), lambda b, t: (b * nt + t, c0 // width))
    const2 = lambda b, t: (0, 0)
    return pl.pallas_call(
        _gdn2_body,
        grid=(bsz, nt),
        in_specs=[
            zblk(COL_QKV), zblk(COL_QKV + width), zblk(COL_QKV + 2 * width), zblk(COL_ZG),
            pl.BlockSpec((tt, 128), lambda b, t: (b * nt + t, COL_AB // 128)),
            pl.BlockSpec((tt // GDN_CHUNK, 2 * GDN_HEADS, GDN_CHUNK), lambda b, t: (b * nt + t, 0, 0)),
            pl.BlockSpec((4, 3 * width), const2),
            pl.BlockSpec((8, 128), const2),
            pl.BlockSpec((8, 128), const2),
            pl.BlockSpec((1, 128), const2),
        ],
        out_specs=pl.BlockSpec((tt, width), lambda b, t: (b * nt + t, 0)),
        out_shape=jax.ShapeDtypeStruct((m, width), BF16),
        scratch_shapes=[pltpu.VMEM((8, width), F32)] * 3 + [pltpu.VMEM((GDN_HEADS, GDN_DK, GDN_DV), F32)],
        compiler_params=_cparams("parallel", "arbitrary"),
        name="gated_deltanet",
    )(z, z, z, z, z, abt, conv_w, gp, gpt, norm_w)


def _rwkv2_body(r_ref, k_ref, v_ref, lo_ref, mu_ref, vp_ref, wup_ref, aup_ref, gup_ref, o_ref,
                pr_scr, pk_scr, pv_scr, plo_scr, st_scr):
    tt = r_ref.shape[0]
    c = RWKV_CHUNK
    gw = RWKV_GROUP
    hd = RWKV_HD
    width = RWKV_WIDTH

    @pl.when(pl.program_id(1) == 0)
    def _():
        for s in (pr_scr, pk_scr, pv_scr, plo_scr, st_scr):
            s[...] = jnp.zeros_like(s)

    mu = mu_ref[...]

    def tshift(x_ref, p_scr, m):
        x = x_ref[...]
        xs = x + m * (_shift_prev(x, p_scr[...], 1) - x)
        p_scr[...] = x[tt - 8:]
        return xs

    r = tshift(r_ref, pr_scr, mu[:, 0:width])
    k = tshift(k_ref, pk_scr, mu[:, width:2 * width])
    v = tshift(v_ref, pv_scr, mu[:, 2 * width:3 * width])
    lo = tshift(lo_ref, plo_scr, mu[:, 3 * width:])
    vp = vp_ref[...]
    w0, a0, k_k, k_a, r_k, lnx_w, lnx_b = (vp[i:i + 1] for i in range(7))

    lane = lax.broadcasted_iota(jnp.int32, lo.shape, 1)
    lo_act = jnp.where(lane < 64, jnp.tanh(lo), jnp.where(lane < 128, lo, jax.nn.sigmoid(lo)))
    w_pre = _mm(lo_act, wup_ref[...])
    a_pre = _mm(lo_act, aup_ref[...])
    gate = _mm(lo_act, gup_ref[...])
    w_log = -_softplus(-(w0 + w_pre)) - 0.5
    lw = -jnp.exp(w_log)
    a = jax.nn.sigmoid(a0 + a_pre)

    bi = lax.broadcasted_iota(jnp.int32, (gw, gw), 0) // hd
    bj = lax.broadcasted_iota(jnp.int32, (gw, gw), 1) // hd
    bdmask = bi == bj
    ones_bd = bdmask.astype(BF16)

    def head_sum(x):
        return _mm_mask(x, ones_bd, terms=2)

    def bd(x):
        return jnp.where(bdmask, jnp.concatenate([x] * (gw // c), axis=0), 0.0)

    ti = lax.broadcasted_iota(jnp.int32, (c, gw), 0)
    sj = lax.broadcasted_iota(jnp.int32, (c, gw), 1) % c
    strict = sj < ti
    incl = sj <= ti
    tril_bf = _tri(c, False).astype(BF16)

    kk_all = k * k_k
    k2_all = k * (1.0 + (a - 1.0) * k_a)

    n_chunks = tt // c
    groups = range(width // gw)
    inst = [(ci, g) for ci in range(n_chunks) for g in groups]
    sls = [slice(ci * c, (ci + 1) * c) for ci in range(n_chunks)]
    gss = [slice(g * gw, (g + 1) * gw) for g in groups]
    cl_all = [_mask_mm(tril_bf, lw[sl]) for sl in sls]

    def at(x, i):
        return x[sls[i[0]], gss[i[1]]]

    kk_raw = {i: at(kk_all, i) for i in inst}
    kk_ss = {i: head_sum(kk_raw[i] * kk_raw[i]) for i in inst}
    a_h, b_h, k_h, br, w_end = {}, {}, {}, {}, {}
    for i in inst:
        cl = cl_all[i[0]][:, gss[i[1]]]
        kk = kk_raw[i] * lax.rsqrt(kk_ss[i] + RMS_EPS)
        wcum = jnp.exp(cl)
        inv_w = jnp.exp(-cl)
        a_h[i] = kk * at(a, i) * inv_w
        b_h[i] = kk * jnp.exp(cl - at(lw, i))
        k_h[i] = at(k2_all, i) * inv_w
        br[i] = jnp.concatenate([b_h[i], at(r, i) * wcum], axis=0)
        w_end[i] = wcum[c - 1:c]
    v_bd = {i: bd(at(v, i)) for i in inst}
    xa = {i: _mm_nt(br[i], bd(a_h[i])) for i in inst}
    xk = {i: _mm_nt(br[i], bd(k_h[i])) for i in inst}
    l_k = {i: jnp.where(strict, xk[i][:c], 0.0) for i in inst}
    ra = {i: jnp.where(incl, xa[i][c:], 0.0) for i in inst}
    rk = {i: jnp.where(incl, xk[i][c:], 0.0) for i in inst}
    p = {i: -jnp.where(strict, xa[i][:c], 0.0) for i in inst}
    tm1 = dict(p)
    for _ in range(5):
        p = {i: _mm(p[i], bd(p[i])) for i in inst}
        tp = {i: _mm(tm1[i], bd(p[i])) for i in inst}
        tm1 = {i: tm1[i] + tp[i] + p[i] for i in inst}
    lkv = {i: _mm(l_k[i], v_bd[i]) for i in inst}
    rkv = {i: _mm(rk[i], v_bd[i]) for i in inst}
    so = {i: head_sum(at(r, i) * at(k2_all, i) * r_k[:, gss[i[1]]]) for i in inst}

    state_t = [st_scr[g] for g in groups]
    for ci in range(n_chunks):
        ids = [(ci, g) for g in groups]
        brh = [_mm_nt(br[i], state_t[i[1]]) for i in ids]
        rhs = [brh[g][:c] + lkv[ci, g] for g in groups]
        tu = [_mm(tm1[ci, g], bd(rhs[g])) for g in groups]
        u = [rhs[g] + tu[g] for g in groups]
        rau = [_mm(ra[ci, g], bd(u[g])) for g in groups]
        upd = [_mm_tn(jnp.concatenate([u[g], at(v, (ci, g))], axis=0),
                      jnp.concatenate([-(a_h[ci, g] * w_end[ci, g]), k_h[ci, g] * w_end[ci, g]], axis=0))
               for g in groups]
        state_t = [w_end[ci, g] * state_t[g] + jnp.where(bdmask, upd[g], 0.0) for g in groups]
        o = [brh[g][c:] - rau[g] + rkv[ci, g] for g in groups]
        osum = [head_sum(o[g]) for g in groups]
        cen = [o[g] - osum[g] * (1.0 / hd) for g in groups]
        var = [head_sum(cen[g] * cen[g]) * (1.0 / hd) for g in groups]
        for g in groups:
            gs = gss[g]
            on = cen[g] * lax.rsqrt(var[g] + RWKV_GN_EPS) * lnx_w[:, gs] + lnx_b[:, gs]
            bonus = so[ci, g] * at(v, (ci, g))
            o_ref[sls[ci], gs] = ((on + bonus) * gate[sls[ci], gs]).astype(o_ref.dtype)
    for g in groups:
        st_scr[g] = state_t[g]


def _rwkv2(z, bsz, seq, mu, vecs, wup_pad, aup_pad, gup_pad):
    m = z.shape[0]
    tt = 128
    nt = seq // tt
    gw = RWKV_GROUP
    width = RWKV_WIDTH
    zblk = lambda c0: pl.BlockSpec((tt, width), lambda b, t: (b * nt + t, c0 // width))
    lora_col = (COL_RWKV + 3 * width) // RWKV_LORA
    const2 = lambda b, t: (0, 0)
    return pl.pallas_call(
        _rwkv2_body,
        grid=(bsz, nt),
        in_specs=[
            zblk(COL_RWKV), zblk(COL_RWKV + width), zblk(COL_RWKV + 2 * width),
            pl.BlockSpec((tt, RWKV_LORA), lambda b, t: (b * nt + t, lora_col)),
            pl.BlockSpec(mu.shape, const2),
            pl.BlockSpec((8, width), const2),
            pl.BlockSpec((RWKV_LORA, width), const2),
            pl.BlockSpec((RWKV_LORA, width), const2),
            pl.BlockSpec((RWKV_LORA, width), const2),
        ],
        out_specs=pl.BlockSpec((tt, width), lambda b, t: (b * nt + t, 0)),
        out_shape=jax.ShapeDtypeStruct((m, width), BF16),
        scratch_shapes=[pltpu.VMEM((8, width), F32)] * 3 + [pltpu.VMEM((8, RWKV_LORA), F32),
                                                            pltpu.VMEM((width // gw, gw, gw), F32)],
        compiler_params=_cparams("parallel", "arbitrary"),
        name="rwkv7",
    )(z, z, z, z, mu, vecs, wup_pad, aup_pad, gup_pad)


def _merge_body(h_ref, ya_ref, yb_ref, yc_ref, za_ref, zb_ref, zc_ref, mb_ref, pa_ref, pb_ref,
                pc_ref, wo_ref, nw_ref, hout_ref, xn_ref):
    mb = mb_ref[...]
    ga = jax.nn.sigmoid(za_ref[...] + mb[:, 0:D_MODEL])
    gb = jax.nn.sigmoid(zb_ref[...] + mb[:, D_MODEL:2 * D_MODEL])
    gc = jax.nn.sigmoid(zc_ref[...] + mb[:, 2 * D_MODEL:])
    merged = (ga * jnp.dot(ya_ref[...], pa_ref[...], preferred_element_type=F32)
              + gb * jnp.dot(yb_ref[...], pb_ref[...], preferred_element_type=F32)
              + gc * jnp.dot(yc_ref[...], pc_ref[...], preferred_element_type=F32))
    h = h_ref[...] + _mm(merged, wo_ref[...])
    hout_ref[...] = h
    xn_ref[...] = _rms(h, nw_ref[...]).astype(BF16)


def _merge(h, z, ya, yb, yc, merge_b, pa, pb, pc, wo, norm_w):
    m = h.shape[0]
    tm = 256
    row = lambda c: pl.BlockSpec((tm, D_MODEL), lambda i: (i, c))
    wspec = pl.BlockSpec((D_MODEL, D_MODEL), lambda i: (0, 0))
    mc = COL_MERGE // D_MODEL
    return pl.pallas_call(
        _merge_body,
        grid=(m // tm,),
        in_specs=[row(0), row(0), row(0), row(0), row(mc), row(mc + 1), row(mc + 2),
                  pl.BlockSpec((1, 3 * D_MODEL), lambda i: (0, 0)),
                  wspec, wspec, wspec, wspec,
                  pl.BlockSpec((1, D_MODEL), lambda i: (0, 0))],
        out_specs=[row(0), row(0)],
        out_shape=[jax.ShapeDtypeStruct((m, D_MODEL), F32), jax.ShapeDtypeStruct((m, D_MODEL), BF16)],
        compiler_params=_cparams("parallel"),
        name="merge_outproj",
    )(h, ya, yb, yc, z, z, z, merge_b, pa, pb, pc, wo, norm_w)


_CAND_SLABS = (
    (16, ((0, 0, 16),)),
    (16, ((1, 0, 8), (2, 8, 5), (4, 13, 3))),
    (16, ((3, 0, 4), (5, 4, 2), (6, 6, 2), (7, 8, 2), (8, 10, 1), (9, 11, 1), (10, 12, 1), (11, 13, 1),
          (12, 14, 1), (13, 15, 1))),
    (8, ((14, 0, 1), (15, 1, 1))),
)


def _candidate_sums(a1, a2):
    slabs = []
    for nrows, pieces in _CAND_SLABS:
        base = a2[:nrows]
        rows = lax.broadcasted_iota(jnp.int32, base.shape, 0)
        out = None
        end = 0
        for p, off, cnt in pieces:
            val = a1[p:p + 1] + (base if off == 0 else pltpu.roll(base, off, 0))
            out = val if out is None else jnp.where(rows >= off, val, out)
            end = off + cnt
        if end < nrows:
            out = jnp.where(rows >= end, -BIG, out)
        slabs.append(out)
    return jnp.concatenate(slabs, axis=0)


def _peer_topk_body(xnt_ref, wqt_ref, keys_ref, n_ref, e1_ref, r2_ref, e2_ref, q_scr):
    tt = xnt_ref.shape[1]
    nk = PEER_NKEYS
    topk = PEER_TOPK
    q_scr[...] = jnp.dot(wqt_ref[...], xnt_ref[...], preferred_element_type=F32)
    row_k = lax.broadcasted_iota(jnp.int32, (topk, tt), 0)

    def head(h, carry):
        o1 = pl.multiple_of(h * (2 * nk), 2 * nk)
        s1 = _mm(keys_ref[2 * h], q_scr[pl.ds(o1, nk), :])
        s2 = _mm(keys_ref[2 * h + 1], q_scr[pl.ds(o1 + nk, nk), :])

        def tbody(rnd, carry):
            x1, x2, v1, v2 = carry
            m1 = jnp.max(x1, axis=0, keepdims=True)
            m2 = jnp.max(x2, axis=0, keepdims=True)
            return (jnp.where(x1 == m1, -BIG, x1), jnp.where(x2 == m2, -BIG, x2),
                    jnp.where(row_k == rnd, m1, v1), jnp.where(row_k == rnd, m2, v2))

        zk = jnp.zeros((topk, tt), F32)
        _, _, a1, a2 = lax.fori_loop(0, topk, tbody, (s1, s2, zk, zk))
        cand = _candidate_sums(a1, a2)
        cmax = a1[0:1] + a2[0:1]

        def cbody(rnd, carry):
            cnd, zsum, c_in, c_out = carry
            mx = jnp.max(cnd, axis=0, keepdims=True)
            zsum = zsum + jnp.where(rnd < topk, jnp.exp(mx - cmax), 0.0)
            c_in = jnp.where(rnd == topk - 1, mx, c_in)
            c_out = jnp.where(rnd == topk, mx, c_out)
            return jnp.where(cnd == mx, -BIG, cnd), zsum, c_in, c_out

        zero = jnp.zeros((1, tt), F32)
        _, zsum, c_in, c_out = lax.fori_loop(0, topk + 1, cbody, (cand, zero, zero, zero))
        tau = 0.5 * (c_in + c_out)
        n_sel = jnp.zeros((nk, tt), F32)
        r2 = jnp.zeros((nk, tt), F32)
        for qq in range(topk):
            aq = a2[qq:qq + 1]
            n_sel = n_sel + jnp.where(s1 + aq >= tau, 1.0, 0.0)
            r2 = r2 + jnp.where(s2 < aq, 1.0, 0.0)
        n_ref[h] = jnp.where(s1 >= a1[topk - 1:topk], n_sel, 0.0)
        e1_ref[h] = jnp.exp(s1 - a1[0:1])
        r2_ref[h] = r2.astype(BF16)
        e2_ref[h] = (jnp.exp(s2 - a2[0:1]) / zsum).astype(BF16)
        return carry

    lax.fori_loop(0, PEER_HEADS, head, 0)


def _peer_topk(xnt, wqt, keys):
    m = xnt.shape[1]
    tt = 512 if m % 512 == 0 else 256
    shape = (PEER_HEADS, PEER_NKEYS, m)
    ospec = pl.BlockSpec((PEER_HEADS, PEER_NKEYS, tt), lambda i: (0, 0, i))
    return pl.pallas_call(
        _peer_topk_body,
        grid=(m // tt,),
        in_specs=[pl.BlockSpec((D_MODEL, tt), lambda i: (0, i)),
                  pl.BlockSpec(wqt.shape, lambda i: (0, 0)),
                  pl.BlockSpec(keys.shape, lambda i: (0, 0, 0))],
        out_specs=[ospec] * 4,
        out_shape=[jax.ShapeDtypeStruct(shape, F32), jax.ShapeDtypeStruct(shape, F32),
                   jax.ShapeDtypeStruct(shape, BF16), jax.ShapeDtypeStruct(shape, BF16)],
        scratch_shapes=[pltpu.VMEM((wqt.shape[0], tt), F32)],
        compiler_params=_cparams("parallel"),
        name="peer_topk",
    )(xnt, wqt, keys)


PEER_TOKEN_CHUNK = 256


def _peer_dense_body(u_ref, xnt_ref, vt_ref, n_ref, e1_ref, r2_ref, e2_ref, y_ref):
    nk = PEER_NKEYS
    tt = xnt_ref.shape[1]
    tc = PEER_TOKEN_CHUNK
    n_blk = u_ref.shape[0] // nk
    pack = 16

    @pl.when(pl.program_id(1) == 0)
    def _():
        y_ref[...] = jnp.zeros_like(y_ref)

    def rows_bf16(ref, h, ii, cs):
        row = jnp.broadcast_to(ref[h, ii:ii + 1, cs], (pack, tc)).astype(BF16)
        return jnp.concatenate([row] * (nk // pack), axis=0)

    chunks = [slice(ci * tc, (ci + 1) * tc) for ci in range(tt // tc)]
    pre = [jnp.dot(u_ref[...], xnt_ref[:, cs], preferred_element_type=F32) for cs in chunks]
    for ci, cs in enumerate(chunks):
        act = jax.nn.gelu(pre[ci]).astype(BF16)
        blocks = []
        for ii in range(n_blk):
            w = None
            for h in range(PEER_HEADS):
                sel = jnp.where(r2_ref[h, :, cs] < rows_bf16(n_ref, h, ii, cs), e2_ref[h, :, cs],
                                jnp.zeros((), BF16))
                term = sel * rows_bf16(e1_ref, h, ii, cs)
                w = term if w is None else w + term
            blocks.append(act[ii * nk:(ii + 1) * nk] * w)
        a = jnp.concatenate(blocks, axis=0)
        y_ref[:, cs] += jnp.dot(vt_ref[...], a, preferred_element_type=F32)


def _peer_dense(xnt, u_bf, vt_bf, n_sel, e1, r2, e2, tt):
    m = xnt.shape[1]
    te = 1024
    ib = te // PEER_NKEYS
    sel_i = pl.BlockSpec((PEER_HEADS, ib, tt), lambda i, e: (0, e, i))
    sel_all = pl.BlockSpec((PEER_HEADS, PEER_NKEYS, tt), lambda i, e: (0, 0, i))
    return pl.pallas_call(
        _peer_dense_body,
        grid=(m // tt, PEER_EXPERTS // te),
        in_specs=[pl.BlockSpec((te, D_MODEL), lambda i, e: (e, 0)),
                  pl.BlockSpec((D_MODEL, tt), lambda i, e: (0, i)),
                  pl.BlockSpec((D_MODEL, te), lambda i, e: (0, e)),
                  sel_i, sel_i, sel_all, sel_all],
        out_specs=pl.BlockSpec((D_MODEL, tt), lambda i, e: (0, i)),
        out_shape=jax.ShapeDtypeStruct((D_MODEL, m), F32),
        compiler_params=_cparams("parallel", "arbitrary"),
        name="peer_dense",
    )(u_bf, xnt, vt_bf, n_sel, e1, r2, e2)


def _final_body(h_ref, y_ref, nw_ref, o_ref):
    o_ref[...] = _rms(h_ref[...] + y_ref[...], nw_ref[...])


def _final(h, y, norm_w):
    m = h.shape[0]
    tm = 512
    row = pl.BlockSpec((tm, D_MODEL), lambda i: (i, 0))
    return pl.pallas_call(
        _final_body,
        grid=(m // tm,),
        in_specs=[row, row, pl.BlockSpec((1, D_MODEL), lambda i: (0, 0))],
        out_specs=row,
        out_shape=jax.ShapeDtypeStruct((m, D_MODEL), F32),
        compiler_params=_cparams("parallel"),
        name="final_norm",
    )(h, y, norm_w.reshape(1, D_MODEL))


def _pad_rows(rows, width, n_rows=8):
    out = jnp.zeros((n_rows, width), F32)
    for i, r in enumerate(rows):
        flat = r.reshape(-1).astype(F32)
        out = out.at[i, :flat.shape[0]].set(flat)
    return out


def _chunk_rows(ab):
    m, n = ab.shape
    return jnp.transpose(ab.reshape(m // GDN_CHUNK, GDN_CHUNK, n), (0, 2, 1))


def _block_diag_groups(w):
    per = LRU_GROUP // LRU_BLOCK_DIM
    n_grp = w.shape[0] // per
    out = jnp.zeros((n_grp, LRU_GROUP, LRU_GROUP), F32)
    for g in range(n_grp):
        for j in range(per):
            o = j * LRU_BLOCK_DIM
            out = out.at[g, o:o + LRU_BLOCK_DIM, o:o + LRU_BLOCK_DIM].set(w[g * per + j])
    return out.astype(BF16)


def _prep_w_in(w):
    pad = jnp.zeros((D_MODEL, IN_WIDTH_PAD - COL_AB - 2 * GDN_HEADS), w.dtype)
    return jnp.concatenate([w[:, :6144], w[:, 9488:12560], w[:, 6160:9488], w[:, 6144:6160], pad],
                           axis=1).astype(BF16)


def _layer(l, h, y, bsz, seq, p):
    z_h, z = _inproj(h, y, p["norm_mix_w"][l], _prep_w_in(p["w_in"][l]))
    h = z_h

    lru_vecs = _pad_rows([p["lru_conv_b"][l], p["lru_b_a"][l], p["lru_b_x"][l], p["lru_lambda"][l]], LRU_WIDTH)
    ya = _lru(z, bsz, seq, p["lru_conv_w"][l], lru_vecs,
              _block_diag_groups(p["lru_w_a"][l]), _block_diag_groups(p["lru_w_x"][l]))

    abt = _chunk_rows(z[:, COL_AB:COL_AB + 2 * GDN_HEADS])
    gp = _pad_rows([p["gdn_a_log"][l], p["gdn_dt_bias"][l]], 128)
    gpt = jnp.zeros((8, 128), F32).at[:, 0].set(p["gdn_a_log"][l]).at[:, 1].set(p["gdn_dt_bias"][l])
    yb = _gdn2(z, abt, bsz, seq, p["gdn_conv_w"][l], gp, gpt, p["gdn_norm_w"][l].reshape(1, GDN_DV))

    rw_vecs = _pad_rows([p["rwkv_w0"][l], p["rwkv_a0"][l], p["rwkv_k_k"][l], p["rwkv_k_a"][l],
                         p["rwkv_r_k"][l], p["rwkv_lnx_w"][l], p["rwkv_lnx_b"][l]], RWKV_WIDTH)
    zl = jnp.zeros((RWKV_LORA, RWKV_WIDTH), F32)
    wup = zl.at[0:64].set(p["rwkv_w_up"][l]).astype(BF16)
    aup = zl.at[64:128].set(p["rwkv_a_up"][l]).astype(BF16)
    gup = zl.at[128:256].set(p["rwkv_g_up"][l]).astype(BF16)
    yc = _rwkv2(z, bsz, seq, p["rwkv_mu"][l].reshape(1, -1), rw_vecs, wup, aup, gup)

    h, xn = _merge(h, z, ya, yb, yc, p["merge_b"][l].reshape(1, -1), p["p_lru"][l].astype(BF16),
                   p["p_gdn"][l].astype(BF16), p["p_rwkv"][l].astype(BF16), p["w_out"][l].astype(BF16),
                   p["norm_ffn_w"][l].reshape(1, D_MODEL))

    xnt = jnp.transpose(xn)
    wqt = jnp.transpose(p["peer_wq"][l]).astype(BF16)
    keys = p["peer_keys"][l].reshape(2 * PEER_HEADS, PEER_NKEYS, PEER_HALF).astype(BF16)
    thr, e1, s2, e2 = _peer_topk(xnt, wqt, keys)
    tt = 512 if xnt.shape[1] % 512 == 0 else 256
    yt = _peer_dense(xnt, p["peer_u"][l].astype(BF16), jnp.transpose(p["peer_v"][l]).astype(BF16),
                     thr, e1, s2, e2, tt)
    return h, jnp.transpose(yt)


def kernel(x, norm_mix_w, norm_ffn_w, final_norm_w, w_in, lru_conv_w, lru_conv_b, lru_w_a, lru_b_a,
           lru_w_x, lru_b_x, lru_lambda, gdn_conv_w, gdn_a_log, gdn_dt_bias, gdn_norm_w, rwkv_mu, rwkv_w0,
           rwkv_w_up, rwkv_a0, rwkv_a_up, rwkv_g_up, rwkv_k_k, rwkv_k_a, rwkv_r_k, rwkv_lnx_w, rwkv_lnx_b,
           merge_b, p_lru, p_gdn, p_rwkv, w_out, peer_wq, peer_keys, peer_u, peer_v):
    p = dict(locals())
    bsz, seq, dim = x.shape
    h = x.reshape(bsz * seq, dim)
    y = None
    for l in range(DEPTH):
        h, y = _layer(l, h, y, bsz, seq, p)
    return _final(h, y, final_norm_w).reshape(bsz, seq, dim)
```

```python
import functools

import jax
import jax.numpy as jnp
from jax import lax
from jax.experimental import pallas as pl
from jax.experimental.pallas import tpu as pltpu

F32 = jnp.float32
BF16 = jnp.bfloat16
HIGHEST = lax.Precision.HIGHEST

D_MODEL = 1024
DEPTH = 2
RMS_EPS = 1e-6

LRU_WIDTH = 1024
LRU_BLOCK_DIM = 64
LRU_C = 8.0
LRU_GROUP = 256

GDN_HEADS = 8
GDN_DK = 128
GDN_DV = 128
GDN_CHUNK = 64

RWKV_HEADS = 16
RWKV_HD = 64
RWKV_WIDTH = 1024
RWKV_GN_EPS = 64e-5
RWKV_CHUNK = 64
RWKV_GROUP = 256
RWKV_LORA = 256

PEER_HEADS = 8
PEER_NKEYS = 128
PEER_EXPERTS = PEER_NKEYS * PEER_NKEYS
PEER_HALF = 128
PEER_TOPK = 16

COL_U = 0
COL_GATE = 1024
COL_QKV = 2048
COL_ZG = 5120
COL_MERGE = 6144
COL_RWKV = 9216
COL_AB = 12544
IN_WIDTH_PAD = 12672

VMEM_LIMIT = 48 * 1024 * 1024
BIG = 3.0e38


def _cparams(*sem, flags=None):
    return pltpu.CompilerParams(dimension_semantics=sem, vmem_limit_bytes=VMEM_LIMIT, flags=flags)


def _mm(a, b):
    return jnp.dot(a.astype(BF16), b.astype(BF16), preferred_element_type=F32)


def _mm_nt(a, b):
    return lax.dot_general(a.astype(BF16), b.astype(BF16), (((1,), (1,)), ((), ())),
                           preferred_element_type=F32)


def _mm_tn(a, b):
    return lax.dot_general(a.astype(BF16), b.astype(BF16), (((0,), (0,)), ((), ())),
                           preferred_element_type=F32)


def _mm_f32(a, b):
    return jnp.dot(a, b, precision=HIGHEST, preferred_element_type=F32)


def _softplus(x):
    return jnp.maximum(x, 0.0) + jnp.log1p(jnp.exp(-jnp.abs(x)))


def _rms(x, w):
    return x * lax.rsqrt(jnp.mean(x * x, axis=-1, keepdims=True) + RMS_EPS) * w


def _shift_prev(x, prev8, s):
    r = pltpu.roll(x, s, 0)
    pr = pltpu.roll(prev8, s, 0)
    rows8 = lax.broadcasted_iota(jnp.int32, prev8.shape, 0)
    head = jnp.where(rows8 < s, pr, r[:8])
    return jnp.concatenate([head, r[8:]], axis=0)


def _shift_fill(x, d, fill):
    n, c = x.shape
    if d % 8 == 0:
        return jnp.concatenate([jnp.full((d, c), fill, x.dtype), x[:n - d]], axis=0)
    r = pltpu.roll(x, d, 0)
    rows8 = lax.broadcasted_iota(jnp.int32, (8, c), 0)
    head = jnp.where(rows8 < d, fill, r[:8])
    return jnp.concatenate([head, r[8:]], axis=0)


def _tri(n, strict):
    i = lax.broadcasted_iota(jnp.int32, (n, n), 0)
    j = lax.broadcasted_iota(jnp.int32, (n, n), 1)
    return (j < i) if strict else (j <= i)


def _inproj_body(add_y, *refs):
    if add_y:
        h_ref, y_ref, nw_ref, w_ref, hout_ref, z_ref, xn_scr = refs
    else:
        h_ref, nw_ref, w_ref, z_ref, xn_scr = refs

    @pl.when(pl.program_id(1) == 0)
    def _():
        h = h_ref[...]
        if add_y:
            h = h + y_ref[...]
            hout_ref[...] = h
        xn_scr[...] = _rms(h, nw_ref[...]).astype(BF16)

    z_ref[...] = jnp.dot(xn_scr[...], w_ref[...], preferred_element_type=F32)


def _inproj(h, y, norm_w, w_pad):
    m = h.shape[0]
    tm, tn = 512, 1152
    add_y = y is not None
    row = pl.BlockSpec((tm, D_MODEL), lambda i, j: (i, 0))
    in_specs = [row] + ([row] if add_y else []) + [
        pl.BlockSpec((1, D_MODEL), lambda i, j: (0, 0)),
        pl.BlockSpec((D_MODEL, tn), lambda i, j: (0, j)),
    ]
    z_spec = pl.BlockSpec((tm, tn), lambda i, j: (i, j))
    z_shape = jax.ShapeDtypeStruct((m, IN_WIDTH_PAD), F32)
    args = (h,) + ((y,) if add_y else ()) + (norm_w.reshape(1, D_MODEL), w_pad)
    out = pl.pallas_call(
        functools.partial(_inproj_body, add_y),
        grid=(m // tm, IN_WIDTH_PAD // tn),
        in_specs=in_specs,
        out_specs=[row, z_spec] if add_y else z_spec,
        out_shape=[jax.ShapeDtypeStruct((m, D_MODEL), F32), z_shape] if add_y else z_shape,
        scratch_shapes=[pltpu.VMEM((tm, D_MODEL), BF16)],
        compiler_params=_cparams("parallel", "arbitrary"),
        name="norm_inproj",
    )(*args)
    return (out[0], out[1]) if add_y else (h, out)


def _lru_body(zu_ref, zg_ref, cw_ref, vp_ref, wa_ref, wx_ref, o_ref, prev_scr, hc_scr):
    tt = zu_ref.shape[0]

    @pl.when(pl.program_id(1) == 0)
    def _():
        prev_scr[...] = jnp.zeros_like(prev_scr)
        hc_scr[...] = jnp.zeros_like(hc_scr)

    u = zu_ref[...]
    prev = prev_scr[...]
    cw = cw_ref[...]
    vp = vp_ref[...]
    xc = (cw[3:4] * u + cw[2:3] * _shift_prev(u, prev, 1) + cw[1:2] * _shift_prev(u, prev, 2)
          + cw[0:1] * _shift_prev(u, prev, 3) + vp[0:1])
    prev_scr[...] = u[tt - 8:]

    xcb = xc.astype(BF16)
    n_grp = LRU_WIDTH // LRU_GROUP
    pre_a = jnp.concatenate(
        [jnp.dot(xcb[:, g * LRU_GROUP:(g + 1) * LRU_GROUP], wa_ref[g], preferred_element_type=F32)
         for g in range(n_grp)], axis=1)
    pre_x = jnp.concatenate(
        [jnp.dot(xcb[:, g * LRU_GROUP:(g + 1) * LRU_GROUP], wx_ref[g], preferred_element_type=F32)
         for g in range(n_grp)], axis=1)
    r = jax.nn.sigmoid(pre_a + vp[1:2])
    i = jax.nn.sigmoid(pre_x + vp[2:3])
    log_a = (-LRU_C * r) * _softplus(-vp[3:4])
    a = jnp.exp(log_a)
    b = jnp.sqrt(-jnp.tanh(log_a) * (a * a + 1.0)) * (i * xc)

    d = 1
    while d < tt:
        b = a * _shift_fill(b, d, 0.0) + b
        a = a * _shift_fill(a, d, 1.0)
        d *= 2
    hcar = hc_scr[...]
    hval = b + a * hcar[0:1]
    hc_scr[...] = jnp.broadcast_to(hval[tt - 1:tt], hcar.shape)
    o_ref[...] = (hval * jax.nn.gelu(zg_ref[...])).astype(o_ref.dtype)


def _lru(z, bsz, seq, conv_w, vecs, wa_bd, wx_bd):
    m = z.shape[0]
    tt = 256
    nt = seq // tt
    rowmap = lambda c: (lambda b, t: (b * nt + t, c))
    const2 = lambda b, t: (0, 0)
    return pl.pallas_call(
        _lru_body,
        grid=(bsz, nt),
        in_specs=[
            pl.BlockSpec((tt, LRU_WIDTH), rowmap(COL_U // LRU_WIDTH)),
            pl.BlockSpec((tt, LRU_WIDTH), rowmap(COL_GATE // LRU_WIDTH)),
            pl.BlockSpec((4, LRU_WIDTH), const2),
            pl.BlockSpec((8, LRU_WIDTH), const2),
            pl.BlockSpec(wa_bd.shape, lambda b, t: (0, 0, 0)),
            pl.BlockSpec(wx_bd.shape, lambda b, t: (0, 0, 0)),
        ],
        out_specs=pl.BlockSpec((tt, LRU_WIDTH), rowmap(0)),
        out_shape=jax.ShapeDtypeStruct((m, LRU_WIDTH), BF16),
        scratch_shapes=[pltpu.VMEM((8, LRU_WIDTH), F32), pltpu.VMEM((8, LRU_WIDTH), F32)],
        compiler_params=_cparams("parallel", "arbitrary"),
        name="rglru",
    )(z, z, conv_w, vecs, wa_bd, wx_bd)


def _gdn_body(q_ref, k_ref, v_ref, zg_ref, ab_ref, abt_ref, cwq_ref, cwk_ref, cwv_ref,
              gp_ref, gpt_ref, nw_ref, o_ref, pq_scr, pk_scr, pv_scr, st_scr):
    tt = q_ref.shape[0]
    c = GDN_CHUNK
    h = pl.program_id(1)

    @pl.when(pl.program_id(2) == 0)
    def _():
        pq_scr[...] = jnp.zeros_like(pq_scr)
        pk_scr[...] = jnp.zeros_like(pk_scr)
        pv_scr[...] = jnp.zeros_like(pv_scr)
        st_scr[...] = jnp.zeros_like(st_scr)

    def conv_silu(x_ref, p_scr, cw_ref):
        x = x_ref[...]
        prev = p_scr[...]
        cw = cw_ref[...]
        y = (cw[3:4] * x + cw[2:3] * _shift_prev(x, prev, 1) + cw[1:2] * _shift_prev(x, prev, 2)
             + cw[0:1] * _shift_prev(x, prev, 3))
        p_scr[...] = x[tt - 8:]
        return jax.nn.silu(y)

    q = conv_silu(q_ref, pq_scr, cwq_ref)
    k = conv_silu(k_ref, pk_scr, cwk_ref)
    v = conv_silu(v_ref, pv_scr, cwv_ref)
    q = q * lax.rsqrt(jnp.sum(q * q, axis=-1, keepdims=True) + RMS_EPS) * (GDN_DK ** -0.5)
    k = k * lax.rsqrt(jnp.sum(k * k, axis=-1, keepdims=True) + RMS_EPS)

    ab = ab_ref[...]
    gp = gp_ref[...]
    lane = lax.broadcasted_iota(jnp.int32, (c, ab.shape[1]), 1)
    g_all = -jnp.exp(gp[0:1]) * _softplus(ab + gp[1:2])
    beta_all = jax.nn.sigmoid(ab)
    gpt = gpt_ref[...]
    sub = lax.broadcasted_iota(jnp.int32, (GDN_HEADS, c), 0)

    incl = _tri(c, False)
    strict = _tri(c, True)
    tril_f = incl.astype(F32)
    ones_f = jnp.ones((c, c), F32)
    triu_f =(lax.broadcasted_iota(jnp.int32, (c, c), 0) <= lax.broadcasted_iota(jnp.int32, (c, c), 1)).astype(F32)
    nw = nw_ref[...]

    for ci in range(tt // c):
        sl = slice(ci * c, (ci + 1) * c)
        qc, kc, vc = q[sl], k[sl], v[sl]
        gc_all = _mm_f32(tril_f, g_all[sl])
        gcol = jnp.sum(jnp.where(lane == h, gc_all, 0.0), axis=1, keepdims=True)
        g_rows = -jnp.exp(gpt[:, 0:1]) * _softplus(abt_ref[ci][0:GDN_HEADS] + gpt[:, 1:2])
        gr_all = _mm_f32(g_rows, triu_f)
        grow = jnp.sum(jnp.where(sub == h, gr_all, 0.0), axis=0, keepdims=True)
        bcol = jnp.sum(jnp.where(lane == h + GDN_HEADS, beta_all[sl], 0.0), axis=1, keepdims=True)
        decay = jnp.exp(jnp.where(incl, gcol - grow, -BIG))
        kb = kc * bcol
        a_mat = jnp.where(strict, _mm_nt(kb, kc) * decay, 0.0)
        p = -a_mat
        tm1 = p
        for _ in range(5):
            p = _mm(p, p)
            tm1 = tm1 + _mm(tm1, p) + p
        egc = jnp.exp(gcol)
        rhs = jnp.concatenate([vc * bcol, kb * egc], axis=1)
        sol = rhs + _mm(tm1, rhs)
        u_c, w_c = sol[:, :GDN_DV], sol[:, GDN_DV:]
        qk = _mm_nt(qc, kc) * decay
        q_dec = qc * egc
        gt_all = _mm_f32(ones_f, g_all[sl])
        g_last = jnp.sum(jnp.where(lane == h, gt_all, 0.0), axis=1, keepdims=True)
        k_dec = kc * jnp.exp(g_last - gcol)
        state = st_scr[...]
        v_new = u_c - _mm(w_c, state)
        o = _mm(q_dec, state) + _mm(qk, v_new)
        g_tot = jnp.exp(jnp.concatenate([g_last] * (GDN_DK // c), axis=0))
        st_scr[...] = state * g_tot + _mm_tn(k_dec, v_new)
        o = o * lax.rsqrt(jnp.mean(o * o, axis=-1, keepdims=True) + RMS_EPS) * nw
        o = o * jax.nn.silu(zg_ref[sl, :])
        o_ref[sl, :] = o.astype(o_ref.dtype)


def _gdn(z, abt, bsz, seq, conv_w, gp, gpt, norm_w):
    m = z.shape[0]
    tt = 256
    nt = seq // tt
    lane_blk = lambda c0: (lambda b, h, t: (b * nt + t, c0 // 128 + h))
    cw_blk = lambda c0: (lambda b, h, t: (0, c0 // 128 + h))
    const2 = lambda b, h, t: (0, 0)
    return pl.pallas_call(
        _gdn_body,
        grid=(bsz, GDN_HEADS, nt),
        in_specs=[
            pl.BlockSpec((tt, 128), lane_blk(COL_QKV)),
            pl.BlockSpec((tt, 128), lane_blk(COL_QKV + 1024)),
            pl.BlockSpec((tt, 128), lane_blk(COL_QKV + 2048)),
            pl.BlockSpec((tt, 128), lane_blk(COL_ZG)),
            pl.BlockSpec((tt, 128), lambda b, h, t: (b * nt + t, COL_AB // 128)),
            pl.BlockSpec((tt // GDN_CHUNK, 2 * GDN_HEADS, GDN_CHUNK), lambda b, h, t: (b * nt + t, 0, 0)),
            pl.BlockSpec((4, 128), cw_blk(0)),
            pl.BlockSpec((4, 128), cw_blk(1024)),
            pl.BlockSpec((4, 128), cw_blk(2048)),
            pl.BlockSpec((8, 128), const2),
            pl.BlockSpec((8, 128), const2),
            pl.BlockSpec((1, 128), const2),
        ],
        out_specs=pl.BlockSpec((tt, 128), lambda b, h, t: (b * nt + t, h)),
        out_shape=jax.ShapeDtypeStruct((m, GDN_HEADS * GDN_DV), BF16),
        scratch_shapes=[pltpu.VMEM((8, 128), F32), pltpu.VMEM((8, 128), F32), pltpu.VMEM((8, 128), F32),
                        pltpu.VMEM((GDN_DK, GDN_DV), F32)],
        compiler_params=_cparams("parallel", "parallel", "arbitrary"),
        name="gated_deltanet",
    )(z, z, z, z, z, abt, conv_w, conv_w, conv_w, gp, gpt, norm_w)


def _rwkv_body(r_ref, k_ref, v_ref, lo_ref, mur_ref, muk_ref, muv_ref, mulo_ref, vp_ref,
               wup_ref, aup_ref, gup_ref, o_ref, pr_scr, pk_scr, pv_scr, plo_scr, st_scr):
    tt = r_ref.shape[0]
    c = RWKV_CHUNK
    gw = RWKV_GROUP
    hd = RWKV_HD

    @pl.when(pl.program_id(2) == 0)
    def _():
        for s in (pr_scr, pk_scr, pv_scr, plo_scr, st_scr):
            s[...] = jnp.zeros_like(s)

    def tshift(x_ref, p_scr, mu_ref):
        x = x_ref[...]
        xs = x + mu_ref[...] * (_shift_prev(x, p_scr[...], 1) - x)
        p_scr[...] = x[tt - 8:]
        return xs

    r = tshift(r_ref, pr_scr, mur_ref)
    k = tshift(k_ref, pk_scr, muk_ref)
    v = tshift(v_ref, pv_scr, muv_ref)
    lo = tshift(lo_ref, plo_scr, mulo_ref)
    vp = vp_ref[...]
    w0, a0, k_k, k_a, r_k, lnx_w, lnx_b = (vp[i:i + 1] for i in range(7))

    lane = lax.broadcasted_iota(jnp.int32, lo.shape, 1)
    lo_act = jnp.where(lane < 64, jnp.tanh(lo), jnp.where(lane < 128, lo, jax.nn.sigmoid(lo)))
    w_pre = _mm(lo_act, wup_ref[...])
    a_pre = _mm(lo_act, aup_ref[...])
    gate = _mm(lo_act, gup_ref[...])
    w_log = -_softplus(-(w0 + w_pre)) - 0.5
    lw = -jnp.exp(w_log)
    a = jax.nn.sigmoid(a0 + a_pre)

    bi = lax.broadcasted_iota(jnp.int32, (gw, gw), 0) // hd
    bj = lax.broadcasted_iota(jnp.int32, (gw, gw), 1) // hd
    bdmask = bi == bj
    ones_bd = bdmask.astype(F32)

    def head_sum(x):
        return _mm_f32(x, ones_bd)

    kk = k * k_k
    kk = kk * lax.rsqrt(head_sum(kk * kk) + RMS_EPS)
    k2 = k * (1.0 + (a - 1.0) * k_a)
    kka = kk * a

    def bd(x):
        return jnp.where(bdmask, jnp.concatenate([x] * (gw // c), axis=0), 0.0)

    ti = lax.broadcasted_iota(jnp.int32, (c, gw), 0)
    sj = lax.broadcasted_iota(jnp.int32, (c, gw), 1) % c
    strict = sj < ti
    incl = sj <= ti
    tril_f = _tri(c, False).astype(F32)
    ones_c = jnp.ones((c, gw), F32)

    for ci in range(tt // c):
        sl = slice(ci * c, (ci + 1) * c)
        lw_c = lw[sl]
        cl = _mm_f32(tril_f, lw_c)
        wcum = jnp.exp(cl)
        inv_w = jnp.exp(-cl)
        a_h = kka[sl] * inv_w
        b_h = kk[sl] * jnp.exp(cl - lw_c)
        k_h = k2[sl] * inv_w
        r_h = r[sl] * wcum
        v_c = v[sl]
        a_bd = bd(a_h)
        k_bd = bd(k_h)
        v_bd = bd(v_c)
        br = jnp.concatenate([b_h, r_h], axis=0)
        xa = _mm_nt(br, a_bd)
        xk = _mm_nt(br, k_bd)
        l_a = jnp.where(strict, xa[:c], 0.0)
        l_k = jnp.where(strict, xk[:c], 0.0)
        ra = jnp.where(incl, xa[c:], 0.0)
        rk = jnp.where(incl, xk[c:], 0.0)
        p = -l_a
        tm1 = p
        for _ in range(5):
            p = _mm(p, bd(p))
            tm1 = tm1 + _mm(tm1, bd(p)) + p
        state = st_scr[...]
        brh = _mm(br, state)
        rhs = brh[:c] + _mm(l_k, v_bd)
        u = rhs + _mm(tm1, bd(rhs))
        o = brh[c:] - _mm(ra, bd(u)) + _mm(rk, v_bd)
        w_end = wcum[c - 1:c]
        w_col = jnp.exp(lax.dot_general(lw_c, ones_c, (((0,), (0,)), ((), ())),
                                        precision=HIGHEST, preferred_element_type=F32))
        upd = _mm_tn(jnp.concatenate([-(a_h * w_end), k_h * w_end], axis=0),
                     jnp.concatenate([u, v_c], axis=0))
        st_scr[...] = w_col * state + jnp.where(bdmask, upd, 0.0)

        mean = head_sum(o) * (1.0 / hd)
        cen = o - mean
        var = head_sum(cen * cen) * (1.0 / hd)
        on = cen * lax.rsqrt(var + RWKV_GN_EPS) * lnx_w + lnx_b
        bonus = head_sum(r[sl] * k2[sl] * r_k) * v_c
        o_ref[sl, :] = ((on + bonus) * gate[sl]).astype(o_ref.dtype)


def _rwkv(z, bsz, seq, mu, vecs, wup_pad, aup_pad, gup_pad):
    m = z.shape[0]
    tt = 256
    nt = seq // tt
    gw = RWKV_GROUP
    ngrp = RWKV_WIDTH // gw
    zblk = lambda c0: (lambda b, g, t: (b * nt + t, c0 // gw + g))
    mublk = lambda c0: (lambda b, g, t: (0, c0 // gw + g))
    lora_col = (COL_RWKV + 3 * RWKV_WIDTH) // gw
    grp_cols = lambda b, g, t: (0, g)
    return pl.pallas_call(
        _rwkv_body,
        grid=(bsz, ngrp, nt),
        in_specs=[
            pl.BlockSpec((tt, gw), zblk(COL_RWKV)),
            pl.BlockSpec((tt, gw), zblk(COL_RWKV + RWKV_WIDTH)),
            pl.BlockSpec((tt, gw), zblk(COL_RWKV + 2 * RWKV_WIDTH)),
            pl.BlockSpec((tt, gw), lambda b, g, t: (b * nt + t, lora_col)),
            pl.BlockSpec((1, gw), mublk(0)),
            pl.BlockSpec((1, gw), mublk(RWKV_WIDTH)),
            pl.BlockSpec((1, gw), mublk(2 * RWKV_WIDTH)),
            pl.BlockSpec((1, gw), lambda b, g, t: (0, 3 * RWKV_WIDTH // gw)),
            pl.BlockSpec((8, gw), grp_cols),
            pl.BlockSpec((RWKV_LORA, gw), grp_cols),
            pl.BlockSpec((RWKV_LORA, gw), grp_cols),
            pl.BlockSpec((RWKV_LORA, gw), grp_cols),
        ],
        out_specs=pl.BlockSpec((tt, gw), lambda b, g, t: (b * nt + t, g)),
        out_shape=jax.ShapeDtypeStruct((m, RWKV_WIDTH), BF16),
        scratch_shapes=[pltpu.VMEM((8, gw), F32)] * 4 + [pltpu.VMEM((gw, gw), F32)],
        compiler_params=_cparams("parallel", "parallel", "arbitrary"),
        name="rwkv7",
    )(z, z, z, z, mu, mu, mu, mu, vecs, wup_pad, aup_pad, gup_pad)


def _split_bf16(x, terms):
    parts = []
    rem = x
    for i in range(terms):
        p = rem.astype(BF16)
        parts.append(p)
        if i + 1 < terms:
            rem = rem - p.astype(F32)
    return parts


def _mask_mm(mask_bf, x, terms=3):
    return sum(jnp.dot(mask_bf, p, preferred_element_type=F32) for p in _split_bf16(x, terms))


def _mm_mask(x, mask_bf, terms=3):
    return sum(jnp.dot(p, mask_bf, preferred_element_type=F32) for p in _split_bf16(x, terms))


def _gdn2_body(q_ref, k_ref, v_ref, zg_ref, ab_ref, abt_ref, cw_ref, gp_ref, gpt_ref, nw_ref,
               o_ref, pq_scr, pk_scr, pv_scr, st_scr):
    tt = q_ref.shape[0]
    c = GDN_CHUNK
    width = GDN_HEADS * GDN_DK

    @pl.when(pl.program_id(1) == 0)
    def _():
        for s in (pq_scr, pk_scr, pv_scr, st_scr):
            s[...] = jnp.zeros_like(s)

    cw = cw_ref[...]

    def conv_silu(x_ref, p_scr, w):
        x = x_ref[...]
        prev = p_scr[...]
        y = (w[3:4] * x + w[2:3] * _shift_prev(x, prev, 1) + w[1:2] * _shift_prev(x, prev, 2)
             + w[0:1] * _shift_prev(x, prev, 3))
        p_scr[...] = x[tt - 8:]
        return jax.nn.silu(y)

    q = conv_silu(q_ref, pq_scr, cw[:, 0:width])
    k = conv_silu(k_ref, pk_scr, cw[:, width:2 * width])
    v = conv_silu(v_ref, pv_scr, cw[:, 2 * width:])

    ab = ab_ref[...]
    gp = gp_ref[...]
    gpt = gpt_ref[...]
    g_all = -jnp.exp(gp[0:1]) * _softplus(ab + gp[1:2])
    beta_all = jax.nn.sigmoid(ab)

    incl = _tri(c, False)
    strict = _tri(c, True)
    cum_mask = jnp.concatenate([incl.astype(BF16), jnp.ones((c, c), BF16)], axis=0)
    triu_bf = (lax.broadcasted_iota(jnp.int32, (c, c), 0) <= lax.broadcasted_iota(jnp.int32, (c, c), 1)).astype(BF16)
    nw = nw_ref[...]

    n_chunks = tt // c
    heads = range(GDN_HEADS)
    inst = [(ci, h) for ci in range(n_chunks) for h in heads]
    sls = [slice(ci * c, (ci + 1) * c) for ci in range(n_chunks)]
    hss = [slice(h * GDN_DK, (h + 1) * GDN_DK) for h in heads]
    gcs = [_mask_mm(cum_mask, g_all[sl]) for sl in sls]
    gr_all = [_mm_mask(-jnp.exp(gpt[:, 0:1]) * _softplus(abt_ref[ci][0:GDN_HEADS] + gpt[:, 1:2]), triu_bf)
              for ci in range(n_chunks)]

    qs, ks, vs, kbs, decays, egcs, kdecs, gtots, bcols = {}, {}, {}, {}, {}, {}, {}, {}, {}
    for ci, h in inst:
        sl, hs = sls[ci], hss[h]
        qc, kc = q[sl, hs], k[sl, hs]
        qs[ci, h] = qc * lax.rsqrt(jnp.sum(qc * qc, axis=-1, keepdims=True) + RMS_EPS) * (GDN_DK ** -0.5)
        ks[ci, h] = kc * lax.rsqrt(jnp.sum(kc * kc, axis=-1, keepdims=True) + RMS_EPS)
        gcol = gcs[ci][:c, h:h + 1]
        g_last = gcs[ci][c:, h:h + 1]
        grow = gr_all[ci][h:h + 1, :]
        bcols[ci, h] = beta_all[sl, GDN_HEADS + h:GDN_HEADS + h + 1]
        decays[ci, h] = jnp.exp(jnp.where(incl, gcol - grow, -BIG))
        kbs[ci, h] = ks[ci, h] * bcols[ci, h]
        egcs[ci, h] = jnp.exp(gcol)
        kdecs[ci, h] = ks[ci, h] * jnp.exp(g_last - gcol)
        gtots[ci, h] = jnp.exp(jnp.concatenate([g_last] * (GDN_DK // c), axis=0))

    kk = {i: _mm_nt(kbs[i], ks[i]) for i in inst}
    qk = {i: _mm_nt(qs[i], ks[i]) * decays[i] for i in inst}
    p = {i: -jnp.where(strict, kk[i] * decays[i], 0.0) for i in inst}
    tm1 = dict(p)
    for _ in range(5):
        p = {i: _mm(p[i], p[i]) for i in inst}
        tp = {i: _mm(tm1[i], p[i]) for i in inst}
        tm1 = {i: tm1[i] + tp[i] + p[i] for i in inst}
    rhs = {i: jnp.concatenate([v[sls[i[0]], hss[i[1]]] * bcols[i], kbs[i] * egcs[i]], axis=1) for i in inst}
    tr = {i: _mm(tm1[i], rhs[i]) for i in inst}
    sol = {i: rhs[i] + tr[i] for i in inst}

    state = [st_scr[h] for h in heads]
    for ci in range(n_chunks):
        ws = [_mm(sol[ci, h][:, GDN_DV:], state[h]) for h in heads]
        qst = [_mm(qs[ci, h] * egcs[ci, h], state[h]) for h in heads]
        v_new = [sol[ci, h][:, :GDN_DV] - ws[h] for h in heads]
        qv = [_mm(qk[ci, h], v_new[h]) for h in heads]
        kv = [_mm_tn(kdecs[ci, h], v_new[h]) for h in heads]
        state = [state[h] * gtots[ci, h] + kv[h] for h in heads]
        for h in heads:
            o = qst[h] + qv[h]
            o = o * lax.rsqrt(jnp.mean(o * o, axis=-1, keepdims=True) + RMS_EPS) * nw
            o = o * jax.nn.silu(zg_ref[sls[ci], hss[h]])
            o_ref[sls[ci], hss[h]] = o.astype(o_ref.dtype)
    for h in heads:
        st_scr[h] = state[h]


def _gdn2(z, abt, bsz, seq, conv_w, gp, gpt, norm_w):
    m = z.shape[0]
    tt = 128
    nt = seq // tt
    width = GDN_HEADS * GDN_DK
    zblk = lambda c0: pl.BlockSpec((tt, width), lambda b, t: (b * nt + t, c0 // width))
    const2 = lambda b, t: (0, 0)
    return pl.pallas_call(
        _gdn2_body,
        grid=(bsz, nt),
        in_specs=[
            zblk(COL_QKV), zblk(COL_QKV + width), zblk(COL_QKV + 2 * width), zblk(COL_ZG),
            pl.BlockSpec((tt, 128), lambda b, t: (b * nt + t, COL_AB // 128)),
            pl.BlockSpec((tt // GDN_CHUNK, 2 * GDN_HEADS, GDN_CHUNK), lambda b, t: (b * nt + t, 0, 0)),
            pl.BlockSpec((4, 3 * width), const2),
            pl.BlockSpec((8, 128), const2),
            pl.BlockSpec((8, 128), const2),
            pl.BlockSpec((1, 128), const2),
        ],
        out_specs=pl.BlockSpec((tt, width), lambda b, t: (b * nt + t, 0)),
        out_shape=jax.ShapeDtypeStruct((m, width), BF16),
        scratch_shapes=[pltpu.VMEM((8, width), F32)] * 3 + [pltpu.VMEM((GDN_HEADS, GDN_DK, GDN_DV), F32)],
        compiler_params=_cparams("parallel", "arbitrary"),
        name="gated_deltanet",
    )(z, z, z, z, z, abt, conv_w, gp, gpt, norm_w)


def _rwkv2_body(r_ref, k_ref, v_ref, lo_ref, mu_ref, vp_ref, wup_ref, aup_ref, gup_ref, o_ref,
                pr_scr, pk_scr, pv_scr, plo_scr, st_scr):
    tt = r_ref.shape[0]
    c = RWKV_CHUNK
    gw = RWKV_GROUP
    hd = RWKV_HD
    width = RWKV_WIDTH

    @pl.when(pl.program_id(1) == 0)
    def _():
        for s in (pr_scr, pk_scr, pv_scr, plo_scr, st_scr):
            s[...] = jnp.zeros_like(s)

    mu = mu_ref[...]

    def tshift(x_ref, p_scr, m):
        x = x_ref[...]
        xs = x + m * (_shift_prev(x, p_scr[...], 1) - x)
        p_scr[...] = x[tt - 8:]
        return xs

    r = tshift(r_ref, pr_scr, mu[:, 0:width])
    k = tshift(k_ref, pk_scr, mu[:, width:2 * width])
    v = tshift(v_ref, pv_scr, mu[:, 2 * width:3 * width])
    lo = tshift(lo_ref, plo_scr, mu[:, 3 * width:])
    vp = vp_ref[...]
    w0, a0, k_k, k_a, r_k, lnx_w, lnx_b = (vp[i:i + 1] for i in range(7))

    lane = lax.broadcasted_iota(jnp.int32, lo.shape, 1)
    lo_act = jnp.where(lane < 64, jnp.tanh(lo), jnp.where(lane < 128, lo, jax.nn.sigmoid(lo)))
    w_pre = _mm(lo_act, wup_ref[...])
    a_pre = _mm(lo_act, aup_ref[...])
    gate = _mm(lo_act, gup_ref[...])
    w_log = -_softplus(-(w0 + w_pre)) - 0.5
    lw = -jnp.exp(w_log)
    a = jax.nn.sigmoid(a0 + a_pre)

    bi = lax.broadcasted_iota(jnp.int32, (gw, gw), 0) // hd
    bj = lax.broadcasted_iota(jnp.int32, (gw, gw), 1) // hd
    bdmask = bi == bj
    ones_bd = bdmask.astype(BF16)

    def head_sum(x):
        return _mm_mask(x, ones_bd, terms=2)

    def bd(x):
        return jnp.where(bdmask, jnp.concatenate([x] * (gw // c), axis=0), 0.0)

    ti = lax.broadcasted_iota(jnp.int32, (c, gw), 0)
    sj = lax.broadcasted_iota(jnp.int32, (c, gw), 1) % c
    strict = sj < ti
    incl = sj <= ti
    tril_bf = _tri(c, False).astype(BF16)

    kk_all = k * k_k
    k2_all = k * (1.0 + (a - 1.0) * k_a)

    n_chunks = tt // c
    groups = range(width // gw)
    inst = [(ci, g) for ci in range(n_chunks) for g in groups]
    sls = [slice(ci * c, (ci + 1) * c) for ci in range(n_chunks)]
    gss = [slice(g * gw, (g + 1) * gw) for g in groups]
    cl_all = [_mask_mm(tril_bf, lw[sl]) for sl in sls]

    def at(x, i):
        return x[sls[i[0]], gss[i[1]]]

    kk_raw = {i: at(kk_all, i) for i in inst}
    kk_ss = {i: head_sum(kk_raw[i] * kk_raw[i]) for i in inst}
    a_h, b_h, k_h, br, w_end = {}, {}, {}, {}, {}
    for i in inst:
        cl = cl_all[i[0]][:, gss[i[1]]]
        kk = kk_raw[i] * lax.rsqrt(kk_ss[i] + RMS_EPS)
        wcum = jnp.exp(cl)
        inv_w = jnp.exp(-cl)
        a_h[i] = kk * at(a, i) * inv_w
        b_h[i] = kk * jnp.exp(cl - at(lw, i))
        k_h[i] = at(k2_all, i) * inv_w
        br[i] = jnp.concatenate([b_h[i], at(r, i) * wcum], axis=0)
        w_end[i] = wcum[c - 1:c]
    v_bd = {i: bd(at(v, i)) for i in inst}
    xa = {i: _mm_nt(br[i], bd(a_h[i])) for i in inst}
    xk = {i: _mm_nt(br[i], bd(k_h[i])) for i in inst}
    l_k = {i: jnp.where(strict, xk[i][:c], 0.0) for i in inst}
    ra = {i: jnp.where(incl, xa[i][c:], 0.0) for i in inst}
    rk = {i: jnp.where(incl, xk[i][c:], 0.0) for i in inst}
    p = {i: -jnp.where(strict, xa[i][:c], 0.0) for i in inst}
    tm1 = dict(p)
    for _ in range(5):
        p = {i: _mm(p[i], bd(p[i])) for i in inst}
        tp = {i: _mm(tm1[i], bd(p[i])) for i in inst}
        tm1 = {i: tm1[i] + tp[i] + p[i] for i in inst}
    lkv = {i: _mm(l_k[i], v_bd[i]) for i in inst}
    rkv = {i: _mm(rk[i], v_bd[i]) for i in inst}
    so = {i: head_sum(at(r, i) * at(k2_all, i) * r_k[:, gss[i[1]]]) for i in inst}

    state_t = [st_scr[g] for g in groups]
    for ci in range(n_chunks):
        ids = [(ci, g) for g in groups]
        brh = [_mm_nt(br[i], state_t[i[1]]) for i in ids]
        rhs = [brh[g][:c] + lkv[ci, g] for g in groups]
        tu = [_mm(tm1[ci, g], bd(rhs[g])) for g in groups]
        u = [rhs[g] + tu[g] for g in groups]
        rau = [_mm(ra[ci, g], bd(u[g])) for g in groups]
        upd = [_mm_tn(jnp.concatenate([u[g], at(v, (ci, g))], axis=0),
                      jnp.concatenate([-(a_h[ci, g] * w_end[ci, g]), k_h[ci, g] * w_end[ci, g]], axis=0))
               for g in groups]
        state_t = [w_end[ci, g] * state_t[g] + jnp.where(bdmask, upd[g], 0.0) for g in groups]
        o = [brh[g][c:] - rau[g] + rkv[ci, g] for g in groups]
        osum = [head_sum(o[g]) for g in groups]
        cen = [o[g] - osum[g] * (1.0 / hd) for g in groups]
        var = [head_sum(cen[g] * cen[g]) * (1.0 / hd) for g in groups]
        for g in groups:
            gs = gss[g]
            on = cen[g] * lax.rsqrt(var[g] + RWKV_GN_EPS) * lnx_w[:, gs] + lnx_b[:, gs]
            bonus = so[ci, g] * at(v, (ci, g))
            o_ref[sls[ci], gs] = ((on + bonus) * gate[sls[ci], gs]).astype(o_ref.dtype)
    for g in groups:
        st_scr[g] = state_t[g]


def _rwkv2(z, bsz, seq, mu, vecs, wup_pad, aup_pad, gup_pad):
    m = z.shape[0]
    tt = 128
    nt = seq // tt
    gw = RWKV_GROUP
    width = RWKV_WIDTH
    zblk = lambda c0: pl.BlockSpec((tt, width), lambda b, t: (b * nt + t, c0 // width))
    lora_col = (COL_RWKV + 3 * width) // RWKV_LORA
    const2 = lambda b, t: (0, 0)
    return pl.pallas_call(
        _rwkv2_body,
        grid=(bsz, nt),
        in_specs=[
            zblk(COL_RWKV), zblk(COL_RWKV + width), zblk(COL_RWKV + 2 * width),
            pl.BlockSpec((tt, RWKV_LORA), lambda b, t: (b * nt + t, lora_col)),
            pl.BlockSpec(mu.shape, const2),
            pl.BlockSpec((8, width), const2),
            pl.BlockSpec((RWKV_LORA, width), const2),
            pl.BlockSpec((RWKV_LORA, width), const2),
            pl.BlockSpec((RWKV_LORA, width), const2),
        ],
        out_specs=pl.BlockSpec((tt, width), lambda b, t: (b * nt + t, 0)),
        out_shape=jax.ShapeDtypeStruct((m, width), BF16),
        scratch_shapes=[pltpu.VMEM((8, width), F32)] * 3 + [pltpu.VMEM((8, RWKV_LORA), F32),
                                                            pltpu.VMEM((width // gw, gw, gw), F32)],
        compiler_params=_cparams("parallel", "arbitrary"),
        name="rwkv7",
    )(z, z, z, z, mu, vecs, wup_pad, aup_pad, gup_pad)


def _merge_body(h_ref, ya_ref, yb_ref, yc_ref, za_ref, zb_ref, zc_ref, mb_ref, pa_ref, pb_ref,
                pc_ref, wo_ref, nw_ref, hout_ref, xn_ref):
    mb = mb_ref[...]
    ga = jax.nn.sigmoid(za_ref[...] + mb[:, 0:D_MODEL])
    gb = jax.nn.sigmoid(zb_ref[...] + mb[:, D_MODEL:2 * D_MODEL])
    gc = jax.nn.sigmoid(zc_ref[...] + mb[:, 2 * D_MODEL:])
    merged = (ga * jnp.dot(ya_ref[...], pa_ref[...], preferred_element_type=F32)
              + gb * jnp.dot(yb_ref[...], pb_ref[...], preferred_element_type=F32)
              + gc * jnp.dot(yc_ref[...], pc_ref[...], preferred_element_type=F32))
    h = h_ref[...] + _mm(merged, wo_ref[...])
    hout_ref[...] = h
    xn_ref[...] = _rms(h, nw_ref[...]).astype(BF16)


def _merge(h, z, ya, yb, yc, merge_b, pa, pb, pc, wo, norm_w):
    m = h.shape[0]
    tm = 256
    row = lambda c: pl.BlockSpec((tm, D_MODEL), lambda i: (i, c))
    wspec = pl.BlockSpec((D_MODEL, D_MODEL), lambda i: (0, 0))
    mc = COL_MERGE // D_MODEL
    return pl.pallas_call(
        _merge_body,
        grid=(m // tm,),
        in_specs=[row(0), row(0), row(0), row(0), row(mc), row(mc + 1), row(mc + 2),
                  pl.BlockSpec((1, 3 * D_MODEL), lambda i: (0, 0)),
                  wspec, wspec, wspec, wspec,
                  pl.BlockSpec((1, D_MODEL), lambda i: (0, 0))],
        out_specs=[row(0), row(0)],
        out_shape=[jax.ShapeDtypeStruct((m, D_MODEL), F32), jax.ShapeDtypeStruct((m, D_MODEL), BF16)],
        compiler_params=_cparams("parallel"),
        name="merge_outproj",
    )(h, ya, yb, yc, z, z, z, merge_b, pa, pb, pc, wo, norm_w)


_CAND_SLABS = (
    (16, ((0, 0, 16),)),
    (16, ((1, 0, 8), (2, 8, 5), (4, 13, 3))),
    (16, ((3, 0, 4), (5, 4, 2), (6, 6, 2), (7, 8, 2), (8, 10, 1), (9, 11, 1), (10, 12, 1), (11, 13, 1),
          (12, 14, 1), (13, 15, 1))),
    (8, ((14, 0, 1), (15, 1, 1))),
)


def _odd_even_merge_sort_pairs(n):
    pairs = []
    p = 1
    while p < n:
        k = p
        while k >= 1:
            for j in range(k % p, n - k, 2 * k):
                for i in range(min(k, n - j - k)):
                    if (i + j) // (2 * p) == (i + j + k) // (2 * p):
                        pairs.append((i + j, i + j + k))
            k //= 2
        p *= 2
    return tuple(pairs)


_SORT_PAIRS = _odd_even_merge_sort_pairs(PEER_TOPK)


def _candidate_sums(a1, a2):
    slabs = []
    for nrows, pieces in _CAND_SLABS:
        base = a2[:nrows]
        rows = lax.broadcasted_iota(jnp.int32, base.shape, 0)
        out = None
        end = 0
        for p, off, cnt in pieces:
            val = a1[p:p + 1] + (base if off == 0 else pltpu.roll(base, off, 0))
            out = val if out is None else jnp.where(rows >= off, val, out)
            end = off + cnt
        if end < nrows:
            out = jnp.where(rows >= end, -BIG, out)
        slabs.append(out)
    return jnp.concatenate(slabs, axis=0)


def _peer_topk_body(xnt_ref, wqt_ref, keys_ref, n_ref, e1_ref, r2_ref, e2_ref, q_scr):
    tt = xnt_ref.shape[1]
    nk = PEER_NKEYS
    topk = PEER_TOPK
    q_scr[...] = jnp.dot(wqt_ref[...], xnt_ref[...], preferred_element_type=F32)
    lanes = 128
    sub = 8
    assert nk == topk * sub
    row8 = lax.broadcasted_iota(jnp.int32, (sub, lanes), 0)

    def top_sorted(s):
        v = [s[k * sub:(k + 1) * sub] for k in range(topk)]
        for i, j in _SORT_PAIRS:
            v[i], v[j] = jnp.maximum(v[i], v[j]), jnp.minimum(v[i], v[j])
        for shift in (4, 2, 1):
            other = [pltpu.roll(x, shift, 0) for x in v]
            v = [jnp.maximum(v[k], other[topk - 1 - k]) for k in range(topk)]
            d = topk // 2
            while d >= 1:
                for i in range(topk):
                    if i & d == 0:
                        v[i], v[i + d] = jnp.maximum(v[i], v[i + d]), jnp.minimum(v[i], v[i + d])
                d //= 2
        return v

    def compact(v):
        tiles = []
        for t0 in range(0, topk, sub):
            out = v[t0]
            for k in range(1, sub):
                out = jnp.where(row8 == k, v[t0 + k], out)
            tiles.append(out)
        return jnp.concatenate(tiles, axis=0)

    def head(h, carry):
        o1 = pl.multiple_of(h * (2 * nk), 2 * nk)
        s1 = _mm(keys_ref[2 * h], q_scr[pl.ds(o1, nk), :])
        s2 = _mm(keys_ref[2 * h + 1], q_scr[pl.ds(o1 + nk, nk), :])
        chunks = [slice(l0, l0 + lanes) for l0 in range(0, tt, lanes)]
        a1 = [top_sorted(s1[:, ls]) for ls in chunks]
        a2 = [top_sorted(s2[:, ls]) for ls in chunks]
        a1c = jnp.concatenate([compact(v) for v in a1], axis=1)
        a2c = jnp.concatenate([compact(v) for v in a2], axis=1)
        cand = _candidate_sums(a1c, a2c)
        cmax = a1c[0:1] + a2c[0:1]

        def cbody(rnd, carry):
            cnd, zsum, c_in, c_out = carry
            mx = jnp.max(cnd, axis=0, keepdims=True)
            zsum = zsum + jnp.where(rnd < topk, jnp.exp(mx - cmax), 0.0)
            c_in = jnp.where(rnd == topk - 1, mx, c_in)
            c_out = jnp.where(rnd == topk, mx, c_out)
            return jnp.where(cnd == mx, -BIG, cnd), zsum, c_in, c_out

        zero = jnp.zeros((1, tt), F32)
        _, zsum, c_in, c_out = lax.fori_loop(0, topk + 1, cbody, (cand, zero, zero, zero))
        tau = 0.5 * (c_in + c_out)
        inv_z = 1.0 / zsum
        pack = 16

        def rep(x):
            return jnp.concatenate([x] * (pack // sub), axis=0)

        for ci, ls in enumerate(chunks):
            tau8 = jnp.broadcast_to(tau[:, ls], (sub, lanes))
            need = [rep(tau8 - a2[ci][qq]) for qq in range(topk)]
            val2 = [rep(a2[ci][qq]) for qq in range(topk)]
            last1 = rep(a1[ci][topk - 1])
            top1 = rep(a1[ci][0])
            top2 = val2[0]
            for k in range(nk // pack):
                rows = slice(k * pack, (k + 1) * pack)
                x1 = s1[rows, ls]
                x2 = s2[rows, ls]
                n_sel = jnp.zeros((pack, lanes), F32)
                r2 = jnp.zeros((pack, lanes), F32)
                for qq in range(topk):
                    n_sel = jnp.where(x1 >= need[qq], n_sel + 1.0, n_sel)
                    r2 = jnp.where(x2 < val2[qq], r2 + 1.0, r2)
                n_ref[h, rows, ls] = jnp.where(x1 >= last1, n_sel, 0.0)
                e1_ref[h, rows, ls] = jnp.exp(x1 - top1)
                r2_ref[h, rows, ls] = r2.astype(BF16)
                e2_ref[h, rows, ls] = (jnp.exp(x2 - top2) * inv_z[:, ls]).astype(BF16)
        return carry

    lax.fori_loop(0, PEER_HEADS, head, 0)


def _peer_topk(xnt, wqt, keys):
    m = xnt.shape[1]
    tt = 512 if m % 512 == 0 else 256
    shape = (PEER_HEADS, PEER_NKEYS, m)
    ospec = pl.BlockSpec((PEER_HEADS, PEER_NKEYS, tt), lambda i: (0, 0, i))
    return pl.pallas_call(
        _peer_topk_body,
        grid=(m // tt,),
        in_specs=[pl.BlockSpec((D_MODEL, tt), lambda i: (0, i)),
                  pl.BlockSpec(wqt.shape, lambda i: (0, 0)),
                  pl.BlockSpec(keys.shape, lambda i: (0, 0, 0))],
        out_specs=[ospec] * 4,
        out_shape=[jax.ShapeDtypeStruct(shape, F32), jax.ShapeDtypeStruct(shape, F32),
                   jax.ShapeDtypeStruct(shape, BF16), jax.ShapeDtypeStruct(shape, BF16)],
        scratch_shapes=[pltpu.VMEM((wqt.shape[0], tt), F32)],
        compiler_params=_cparams("parallel"),
        name="peer_topk",
    )(xnt, wqt, keys)


PEER_TOKEN_CHUNK = 256


def _peer_dense_body(u_ref, xnt_ref, vt_ref, n_ref, e1_ref, r2_ref, e2_ref, y_ref):
    nk = PEER_NKEYS
    tt = xnt_ref.shape[1]
    tc = PEER_TOKEN_CHUNK
    n_blk = u_ref.shape[0] // nk
    pack = 16

    @pl.when(pl.program_id(1) == 0)
    def _():
        y_ref[...] = jnp.zeros_like(y_ref)

    def rows_bf16(ref, h, ii, ls):
        row = jnp.broadcast_to(ref[h, ii:ii + 1, ls], (pack, tc)).astype(BF16)
        return jnp.concatenate([row] * (nk // pack), axis=0)

    def select_weights(ii, c0):
        ls = slice(c0, c0 + tc)
        w = None
        for h in range(PEER_HEADS):
            sel = jnp.where(r2_ref[h, :, ls] < rows_bf16(n_ref, h, ii, ls), e2_ref[h, :, ls],
                            jnp.zeros((), BF16))
            term = sel * rows_bf16(e1_ref, h, ii, ls)
            w = term if w is None else w + term
        return w

    def gelu_tanh(x):
        c = 0.7978845608028654
        inner = x * ((x * x) * (c * 0.044715) + c)
        hx = 0.5 * x
        return hx * jnp.tanh(inner) + hx

    starts = list(range(0, tt, tc))
    pre = [jnp.dot(u_ref[...], xnt_ref[:, c0:c0 + tc], preferred_element_type=F32) for c0 in starts]
    for idx, c0 in enumerate(starts):
        act = gelu_tanh(pre[idx]).astype(BF16)
        a = jnp.concatenate([act[ii * nk:(ii + 1) * nk] * select_weights(ii, c0) for ii in range(n_blk)], axis=0)
        y_ref[:, c0:c0 + tc] += jnp.dot(vt_ref[...], a, preferred_element_type=F32)


def _peer_dense(xnt, u_bf, vt_bf, n_sel, e1, r2, e2, tt):
    m = xnt.shape[1]
    te = 1024
    ib = te // PEER_NKEYS
    sel_i = pl.BlockSpec((PEER_HEADS, ib, tt), lambda i, e: (0, e, i))
    sel_all = pl.BlockSpec((PEER_HEADS, PEER_NKEYS, tt), lambda i, e: (0, 0, i))
    return pl.pallas_call(
        _peer_dense_body,
        grid=(m // tt, PEER_EXPERTS // te),
        in_specs=[pl.BlockSpec((te, D_MODEL), lambda i, e: (e, 0)),
                  pl.BlockSpec((D_MODEL, tt), lambda i, e: (0, i)),
                  pl.BlockSpec((D_MODEL, te), lambda i, e: (0, e)),
                  sel_i, sel_i, sel_all, sel_all],
        out_specs=pl.BlockSpec((D_MODEL, tt), lambda i, e: (0, i)),
        out_shape=jax.ShapeDtypeStruct((D_MODEL, m), F32),
        compiler_params=_cparams("parallel", "arbitrary"),
        name="peer_dense",
    )(u_bf, xnt, vt_bf, n_sel, e1, r2, e2)


def _final_body(h_ref, y_ref, nw_ref, o_ref):
    o_ref[...] = _rms(h_ref[...] + y_ref[...], nw_ref[...])


def _final(h, y, norm_w):
    m = h.shape[0]
    tm = 512
    row = pl.BlockSpec((tm, D_MODEL), lambda i: (i, 0))
    return pl.pallas_call(
        _final_body,
        grid=(m // tm,),
        in_specs=[row, row, pl.BlockSpec((1, D_MODEL), lambda i: (0, 0))],
        out_specs=row,
        out_shape=jax.ShapeDtypeStruct((m, D_MODEL), F32),
        compiler_params=_cparams("parallel"),
        name="final_norm",
    )(h, y, norm_w.reshape(1, D_MODEL))


def _pad_rows(rows, width, n_rows=8):
    out = jnp.zeros((n_rows, width), F32)
    for i, r in enumerate(rows):
        flat = r.reshape(-1).astype(F32)
        out = out.at[i, :flat.shape[0]].set(flat)
    return out


def _chunk_rows(ab):
    m, n = ab.shape
    return jnp.transpose(ab.reshape(m // GDN_CHUNK, GDN_CHUNK, n), (0, 2, 1))


def _block_diag_groups(w):
    per = LRU_GROUP // LRU_BLOCK_DIM
    n_grp = w.shape[0] // per
    out = jnp.zeros((n_grp, LRU_GROUP, LRU_GROUP), F32)
    for g in range(n_grp):
        for j in range(per):
            o = j * LRU_BLOCK_DIM
            out = out.at[g, o:o + LRU_BLOCK_DIM, o:o + LRU_BLOCK_DIM].set(w[g * per + j])
    return out.astype(BF16)


def _prep_w_in(w):
    pad = jnp.zeros((D_MODEL, IN_WIDTH_PAD - COL_AB - 2 * GDN_HEADS), w.dtype)
    return jnp.concatenate([w[:, :6144], w[:, 9488:12560], w[:, 6160:9488], w[:, 6144:6160], pad],
                           axis=1).astype(BF16)


def _layer(l, h, y, bsz, seq, p):
    z_h, z = _inproj(h, y, p["norm_mix_w"][l], _prep_w_in(p["w_in"][l]))
    h = z_h

    lru_vecs = _pad_rows([p["lru_conv_b"][l], p["lru_b_a"][l], p["lru_b_x"][l], p["lru_lambda"][l]], LRU_WIDTH)
    ya = _lru(z, bsz, seq, p["lru_conv_w"][l], lru_vecs,
              _block_diag_groups(p["lru_w_a"][l]), _block_diag_groups(p["lru_w_x"][l]))

    abt = _chunk_rows(z[:, COL_AB:COL_AB + 2 * GDN_HEADS])
    gp = _pad_rows([p["gdn_a_log"][l], p["gdn_dt_bias"][l]], 128)
    gpt = jnp.zeros((8, 128), F32).at[:, 0].set(p["gdn_a_log"][l]).at[:, 1].set(p["gdn_dt_bias"][l])
    yb = _gdn2(z, abt, bsz, seq, p["gdn_conv_w"][l], gp, gpt, p["gdn_norm_w"][l].reshape(1, GDN_DV))

    rw_vecs = _pad_rows([p["rwkv_w0"][l], p["rwkv_a0"][l], p["rwkv_k_k"][l], p["rwkv_k_a"][l],
                         p["rwkv_r_k"][l], p["rwkv_lnx_w"][l], p["rwkv_lnx_b"][l]], RWKV_WIDTH)
    zl = jnp.zeros((RWKV_LORA, RWKV_WIDTH), F32)
    wup = zl.at[0:64].set(p["rwkv_w_up"][l]).astype(BF16)
    aup = zl.at[64:128].set(p["rwkv_a_up"][l]).astype(BF16)
    gup = zl.at[128:256].set(p["rwkv_g_up"][l]).astype(BF16)
    yc = _rwkv2(z, bsz, seq, p["rwkv_mu"][l].reshape(1, -1), rw_vecs, wup, aup, gup)

    h, xn = _merge(h, z, ya, yb, yc, p["merge_b"][l].reshape(1, -1), p["p_lru"][l].astype(BF16),
                   p["p_gdn"][l].astype(BF16), p["p_rwkv"][l].astype(BF16), p["w_out"][l].astype(BF16),
                   p["norm_ffn_w"][l].reshape(1, D_MODEL))

    xnt = jnp.transpose(xn)
    wqt = jnp.transpose(p["peer_wq"][l]).astype(BF16)
    keys = p["peer_keys"][l].reshape(2 * PEER_HEADS, PEER_NKEYS, PEER_HALF).astype(BF16)
    thr, e1, s2, e2 = _peer_topk(xnt, wqt, keys)
    tt = 512 if xnt.shape[1] % 512 == 0 else 256
    yt = _peer_dense(xnt, p["peer_u"][l].astype(BF16), jnp.transpose(p["peer_v"][l]).astype(BF16),
                     thr, e1, s2, e2, tt)
    return h, jnp.transpose(yt)


def kernel(x, norm_mix_w, norm_ffn_w, final_norm_w, w_in, lru_conv_w, lru_conv_b, lru_w_a, lru_b_a,
           lru_w_x, lru_b_x, lru_lambda, gdn_conv_w, gdn_a_log, gdn_dt_bias, gdn_norm_w, rwkv_mu, rwkv_w0,
           rwkv_w_up, rwkv_a0, rwkv_a_up, rwkv_g_up, rwkv_k_k, rwkv_k_a, rwkv_r_k, rwkv_lnx_w, rwkv_lnx_b,
           merge_b, p_lru, p_gdn, p_rwkv, w_out, peer_wq, peer_keys, peer_u, peer_v):
    p = dict(locals())
    bsz, seq, dim = x.shape
    h = x.reshape(bsz * seq, dim)
    y = None
    for l in range(DEPTH):
        h, y = _layer(l, h, y, bsz, seq, p)
    return _final(h, y, final_norm_w).reshape(bsz, seq, dim)
```

```python
import functools

import jax
import jax.numpy as jnp
from jax import lax
from jax.experimental import pallas as pl
from jax.experimental.pallas import tpu as pltpu

F32 = jnp.float32
BF16 = jnp.bfloat16
HIGHEST = lax.Precision.HIGHEST

D_MODEL = 1024
DEPTH = 2
RMS_EPS = 1e-6

LRU_WIDTH = 1024
LRU_BLOCK_DIM = 64
LRU_C = 8.0
LRU_GROUP = 256

GDN_HEADS = 8
GDN_DK = 128
GDN_DV = 128
GDN_CHUNK = 64

RWKV_HEADS = 16
RWKV_HD = 64
RWKV_WIDTH = 1024
RWKV_GN_EPS = 64e-5
RWKV_CHUNK = 64
RWKV_GROUP = 256
RWKV_LORA = 256

PEER_HEADS = 8
PEER_NKEYS = 128
PEER_EXPERTS = PEER_NKEYS * PEER_NKEYS
PEER_HALF = 128
PEER_TOPK = 16

COL_U = 0
COL_GATE = 1024
COL_QKV = 2048
COL_ZG = 5120
COL_MERGE = 6144
COL_RWKV = 9216
COL_AB = 12544
IN_WIDTH_PAD = 12672

VMEM_LIMIT = 48 * 1024 * 1024
BIG = 3.0e38


def _cparams(*sem, flags=None):
    return pltpu.CompilerParams(dimension_semantics=sem, vmem_limit_bytes=VMEM_LIMIT, flags=flags)


def _mm(a, b):
    return jnp.dot(a.astype(BF16), b.astype(BF16), preferred_element_type=F32)


def _mm_nt(a, b):
    return lax.dot_general(a.astype(BF16), b.astype(BF16), (((1,), (1,)), ((), ())),
                           preferred_element_type=F32)


def _mm_tn(a, b):
    return lax.dot_general(a.astype(BF16), b.astype(BF16), (((0,), (0,)), ((), ())),
                           preferred_element_type=F32)


def _mm_f32(a, b):
    return jnp.dot(a, b, precision=HIGHEST, preferred_element_type=F32)


def _softplus(x):
    return jnp.maximum(x, 0.0) + jnp.log1p(jnp.exp(-jnp.abs(x)))


def _rms(x, w):
    return x * lax.rsqrt(jnp.mean(x * x, axis=-1, keepdims=True) + RMS_EPS) * w


def _shift_prev(x, prev8, s):
    r = pltpu.roll(x, s, 0)
    pr = pltpu.roll(prev8, s, 0)
    rows8 = lax.broadcasted_iota(jnp.int32, prev8.shape, 0)
    head = jnp.where(rows8 < s, pr, r[:8])
    return jnp.concatenate([head, r[8:]], axis=0)


def _shift_fill(x, d, fill):
    n, c = x.shape
    if d % 8 == 0:
        return jnp.concatenate([jnp.full((d, c), fill, x.dtype), x[:n - d]], axis=0)
    r = pltpu.roll(x, d, 0)
    rows8 = lax.broadcasted_iota(jnp.int32, (8, c), 0)
    head = jnp.where(rows8 < d, fill, r[:8])
    return jnp.concatenate([head, r[8:]], axis=0)


def _tri(n, strict):
    i = lax.broadcasted_iota(jnp.int32, (n, n), 0)
    j = lax.broadcasted_iota(jnp.int32, (n, n), 1)
    return (j < i) if strict else (j <= i)


def _inproj_body(add_y, *refs):
    if add_y:
        h_ref, y_ref, nw_ref, w_ref, hout_ref, z_ref, xn_scr = refs
    else:
        h_ref, nw_ref, w_ref, z_ref, xn_scr = refs

    @pl.when(pl.program_id(1) == 0)
    def _():
        h = h_ref[...]
        if add_y:
            h = h + y_ref[...]
            hout_ref[...] = h
        xn_scr[...] = _rms(h, nw_ref[...]).astype(BF16)

    z_ref[...] = jnp.dot(xn_scr[...], w_ref[...], preferred_element_type=F32).astype(z_ref.dtype)


def _inproj(h, y, norm_w, w_pad):
    m = h.shape[0]
    tm, tn = (1024 if m % 1024 == 0 else 512), 1152
    add_y = y is not None
    row = pl.BlockSpec((tm, D_MODEL), lambda i, j: (i, 0))
    in_specs = [row] + ([row] if add_y else []) + [
        pl.BlockSpec((1, D_MODEL), lambda i, j: (0, 0)),
        pl.BlockSpec((D_MODEL, tn), lambda i, j: (0, j)),
    ]
    z_spec = pl.BlockSpec((tm, tn), lambda i, j: (i, j))
    z_shape = jax.ShapeDtypeStruct((m, IN_WIDTH_PAD), BF16)
    args = (h,) + ((y,) if add_y else ()) + (norm_w.reshape(1, D_MODEL), w_pad)
    out = pl.pallas_call(
        functools.partial(_inproj_body, add_y),
        grid=(m // tm, IN_WIDTH_PAD // tn),
        in_specs=in_specs,
        out_specs=[row, z_spec] if add_y else z_spec,
        out_shape=[jax.ShapeDtypeStruct((m, D_MODEL), F32), z_shape] if add_y else z_shape,
        scratch_shapes=[pltpu.VMEM((tm, D_MODEL), BF16)],
        compiler_params=_cparams("parallel", "arbitrary"),
        name="norm_inproj",
    )(*args)
    return (out[0], out[1]) if add_y else (h, out)


def _lru_body(zu_ref, zg_ref, cw_ref, vp_ref, wa_ref, wx_ref, o_ref, prev_scr, hc_scr):
    tt = zu_ref.shape[0]

    @pl.when(pl.program_id(1) == 0)
    def _():
        prev_scr[...] = jnp.zeros_like(prev_scr)
        hc_scr[...] = jnp.zeros_like(hc_scr)

    u = zu_ref[...].astype(F32)
    prev = prev_scr[...]
    cw = cw_ref[...]
    vp = vp_ref[...]
    xc = (cw[3:4] * u + cw[2:3] * _shift_prev(u, prev, 1) + cw[1:2] * _shift_prev(u, prev, 2)
          + cw[0:1] * _shift_prev(u, prev, 3) + vp[0:1])
    prev_scr[...] = u[tt - 8:]

    xcb = xc.astype(BF16)
    n_grp = LRU_WIDTH // LRU_GROUP
    pre_a = jnp.concatenate(
        [jnp.dot(xcb[:, g * LRU_GROUP:(g + 1) * LRU_GROUP], wa_ref[g], preferred_element_type=F32)
         for g in range(n_grp)], axis=1)
    pre_x = jnp.concatenate(
        [jnp.dot(xcb[:, g * LRU_GROUP:(g + 1) * LRU_GROUP], wx_ref[g], preferred_element_type=F32)
         for g in range(n_grp)], axis=1)
    r = jax.nn.sigmoid(pre_a + vp[1:2])
    i = jax.nn.sigmoid(pre_x + vp[2:3])
    log_a = (-LRU_C * r) * _softplus(-vp[3:4])
    a = jnp.exp(log_a)
    b = jnp.sqrt(-jnp.tanh(log_a) * (a * a + 1.0)) * (i * xc)

    d = 1
    while d < tt:
        b = a * _shift_fill(b, d, 0.0) + b
        a = a * _shift_fill(a, d, 1.0)
        d *= 2
    hcar = hc_scr[...]
    hval = b + a * hcar[0:1]
    hc_scr[...] = jnp.broadcast_to(hval[tt - 1:tt], hcar.shape)
    o_ref[...] = (hval * jax.nn.gelu(zg_ref[...].astype(F32))).astype(o_ref.dtype)


def _lru(z, bsz, seq, conv_w, vecs, wa_bd, wx_bd):
    m = z.shape[0]
    tt = 256
    nt = seq // tt
    rowmap = lambda c: (lambda b, t: (b * nt + t, c))
    const2 = lambda b, t: (0, 0)
    return pl.pallas_call(
        _lru_body,
        grid=(bsz, nt),
        in_specs=[
            pl.BlockSpec((tt, LRU_WIDTH), rowmap(COL_U // LRU_WIDTH)),
            pl.BlockSpec((tt, LRU_WIDTH), rowmap(COL_GATE // LRU_WIDTH)),
            pl.BlockSpec((4, LRU_WIDTH), const2),
            pl.BlockSpec((8, LRU_WIDTH), const2),
            pl.BlockSpec(wa_bd.shape, lambda b, t: (0, 0, 0)),
            pl.BlockSpec(wx_bd.shape, lambda b, t: (0, 0, 0)),
        ],
        out_specs=pl.BlockSpec((tt, LRU_WIDTH), rowmap(0)),
        out_shape=jax.ShapeDtypeStruct((m, LRU_WIDTH), BF16),
        scratch_shapes=[pltpu.VMEM((8, LRU_WIDTH), F32), pltpu.VMEM((8, LRU_WIDTH), F32)],
        compiler_params=_cparams("parallel", "arbitrary"),
        name="rglru",
    )(z, z, conv_w, vecs, wa_bd, wx_bd)


def _gdn_body(q_ref, k_ref, v_ref, zg_ref, ab_ref, abt_ref, cwq_ref, cwk_ref, cwv_ref,
              gp_ref, gpt_ref, nw_ref, o_ref, pq_scr, pk_scr, pv_scr, st_scr):
    tt = q_ref.shape[0]
    c = GDN_CHUNK
    h = pl.program_id(1)

    @pl.when(pl.program_id(2) == 0)
    def _():
        pq_scr[...] = jnp.zeros_like(pq_scr)
        pk_scr[...] = jnp.zeros_like(pk_scr)
        pv_scr[...] = jnp.zeros_like(pv_scr)
        st_scr[...] = jnp.zeros_like(st_scr)

    def conv_silu(x_ref, p_scr, cw_ref):
        x = x_ref[...]
        prev = p_scr[...]
        cw = cw_ref[...]
        y = (cw[3:4] * x + cw[2:3] * _shift_prev(x, prev, 1) + cw[1:2] * _shift_prev(x, prev, 2)
             + cw[0:1] * _shift_prev(x, prev, 3))
        p_scr[...] = x[tt - 8:]
        return jax.nn.silu(y)

    q = conv_silu(q_ref, pq_scr, cwq_ref)
    k = conv_silu(k_ref, pk_scr, cwk_ref)
    v = conv_silu(v_ref, pv_scr, cwv_ref)
    q = q * lax.rsqrt(jnp.sum(q * q, axis=-1, keepdims=True) + RMS_EPS) * (GDN_DK ** -0.5)
    k = k * lax.rsqrt(jnp.sum(k * k, axis=-1, keepdims=True) + RMS_EPS)

    ab = ab_ref[...]
    gp = gp_ref[...]
    lane = lax.broadcasted_iota(jnp.int32, (c, ab.shape[1]), 1)
    g_all = -jnp.exp(gp[0:1]) * _softplus(ab + gp[1:2])
    beta_all = jax.nn.sigmoid(ab)
    gpt = gpt_ref[...]
    sub = lax.broadcasted_iota(jnp.int32, (GDN_HEADS, c), 0)

    incl = _tri(c, False)
    strict = _tri(c, True)
    tril_f = incl.astype(F32)
    ones_f = jnp.ones((c, c), F32)
    triu_f =(lax.broadcasted_iota(jnp.int32, (c, c), 0) <= lax.broadcasted_iota(jnp.int32, (c, c), 1)).astype(F32)
    nw = nw_ref[...]

    for ci in range(tt // c):
        sl = slice(ci * c, (ci + 1) * c)
        qc, kc, vc = q[sl], k[sl], v[sl]
        gc_all = _mm_f32(tril_f, g_all[sl])
        gcol = jnp.sum(jnp.where(lane == h, gc_all, 0.0), axis=1, keepdims=True)
        g_rows = -jnp.exp(gpt[:, 0:1]) * _softplus(abt_ref[ci][0:GDN_HEADS] + gpt[:, 1:2])
        gr_all = _mm_f32(g_rows, triu_f)
        grow = jnp.sum(jnp.where(sub == h, gr_all, 0.0), axis=0, keepdims=True)
        bcol = jnp.sum(jnp.where(lane == h + GDN_HEADS, beta_all[sl], 0.0), axis=1, keepdims=True)
        decay = jnp.exp(jnp.where(incl, gcol - grow, -BIG))
        kb = kc * bcol
        a_mat = jnp.where(strict, _mm_nt(kb, kc) * decay, 0.0)
        p = -a_mat
        tm1 = p
        for _ in range(5):
            p = _mm(p, p)
            tm1 = tm1 + _mm(tm1, p) + p
        egc = jnp.exp(gcol)
        rhs = jnp.concatenate([vc * bcol, kb * egc], axis=1)
        sol = rhs + _mm(tm1, rhs)
        u_c, w_c = sol[:, :GDN_DV], sol[:, GDN_DV:]
        qk = _mm_nt(qc, kc) * decay
        q_dec = qc * egc
        gt_all = _mm_f32(ones_f, g_all[sl])
        g_last = jnp.sum(jnp.where(lane == h, gt_all, 0.0), axis=1, keepdims=True)
        k_dec = kc * jnp.exp(g_last - gcol)
        state = st_scr[...]
        v_new = u_c - _mm(w_c, state)
        o = _mm(q_dec, state) + _mm(qk, v_new)
        g_tot = jnp.exp(jnp.concatenate([g_last] * (GDN_DK // c), axis=0))
        st_scr[...] = state * g_tot + _mm_tn(k_dec, v_new)
        o = o * lax.rsqrt(jnp.mean(o * o, axis=-1, keepdims=True) + RMS_EPS) * nw
        o = o * jax.nn.silu(zg_ref[sl, :])
        o_ref[sl, :] = o.astype(o_ref.dtype)


def _gdn(z, abt, bsz, seq, conv_w, gp, gpt, norm_w):
    m = z.shape[0]
    tt = 256
    nt = seq // tt
    lane_blk = lambda c0: (lambda b, h, t: (b * nt + t, c0 // 128 + h))
    cw_blk = lambda c0: (lambda b, h, t: (0, c0 // 128 + h))
    const2 = lambda b, h, t: (0, 0)
    return pl.pallas_call(
        _gdn_body,
        grid=(bsz, GDN_HEADS, nt),
        in_specs=[
            pl.BlockSpec((tt, 128), lane_blk(COL_QKV)),
            pl.BlockSpec((tt, 128), lane_blk(COL_QKV + 1024)),
            pl.BlockSpec((tt, 128), lane_blk(COL_QKV + 2048)),
            pl.BlockSpec((tt, 128), lane_blk(COL_ZG)),
            pl.BlockSpec((tt, 128), lambda b, h, t: (b * nt + t, COL_AB // 128)),
            pl.BlockSpec((tt // GDN_CHUNK, 2 * GDN_HEADS, GDN_CHUNK), lambda b, h, t: (b * nt + t, 0, 0)),
            pl.BlockSpec((4, 128), cw_blk(0)),
            pl.BlockSpec((4, 128), cw_blk(1024)),
            pl.BlockSpec((4, 128), cw_blk(2048)),
            pl.BlockSpec((8, 128), const2),
            pl.BlockSpec((8, 128), const2),
            pl.BlockSpec((1, 128), const2),
        ],
        out_specs=pl.BlockSpec((tt, 128), lambda b, h, t: (b * nt + t, h)),
        out_shape=jax.ShapeDtypeStruct((m, GDN_HEADS * GDN_DV), BF16),
        scratch_shapes=[pltpu.VMEM((8, 128), F32), pltpu.VMEM((8, 128), F32), pltpu.VMEM((8, 128), F32),
                        pltpu.VMEM((GDN_DK, GDN_DV), F32)],
        compiler_params=_cparams("parallel", "parallel", "arbitrary"),
        name="gated_deltanet",
    )(z, z, z, z, z, abt, conv_w, conv_w, conv_w, gp, gpt, norm_w)


def _rwkv_body(r_ref, k_ref, v_ref, lo_ref, mur_ref, muk_ref, muv_ref, mulo_ref, vp_ref,
               wup_ref, aup_ref, gup_ref, o_ref, pr_scr, pk_scr, pv_scr, plo_scr, st_scr):
    tt = r_ref.shape[0]
    c = RWKV_CHUNK
    gw = RWKV_GROUP
    hd = RWKV_HD

    @pl.when(pl.program_id(2) == 0)
    def _():
        for s in (pr_scr, pk_scr, pv_scr, plo_scr, st_scr):
            s[...] = jnp.zeros_like(s)

    def tshift(x_ref, p_scr, mu_ref):
        x = x_ref[...]
        xs = x + mu_ref[...] * (_shift_prev(x, p_scr[...], 1) - x)
        p_scr[...] = x[tt - 8:]
        return xs

    r = tshift(r_ref, pr_scr, mur_ref)
    k = tshift(k_ref, pk_scr, muk_ref)
    v = tshift(v_ref, pv_scr, muv_ref)
    lo = tshift(lo_ref, plo_scr, mulo_ref)
    vp = vp_ref[...]
    w0, a0, k_k, k_a, r_k, lnx_w, lnx_b = (vp[i:i + 1] for i in range(7))

    lane = lax.broadcasted_iota(jnp.int32, lo.shape, 1)
    lo_act = jnp.where(lane < 64, jnp.tanh(lo), jnp.where(lane < 128, lo, jax.nn.sigmoid(lo)))
    w_pre = _mm(lo_act, wup_ref[...])
    a_pre = _mm(lo_act, aup_ref[...])
    gate = _mm(lo_act, gup_ref[...])
    w_log = -_softplus(-(w0 + w_pre)) - 0.5
    lw = -jnp.exp(w_log)
    a = jax.nn.sigmoid(a0 + a_pre)

    bi = lax.broadcasted_iota(jnp.int32, (gw, gw), 0) // hd
    bj = lax.broadcasted_iota(jnp.int32, (gw, gw), 1) // hd
    bdmask = bi == bj
    ones_bd = bdmask.astype(F32)

    def head_sum(x):
        return _mm_f32(x, ones_bd)

    kk = k * k_k
    kk = kk * lax.rsqrt(head_sum(kk * kk) + RMS_EPS)
    k2 = k * (1.0 + (a - 1.0) * k_a)
    kka = kk * a

    def bd(x):
        return jnp.where(bdmask, jnp.concatenate([x] * (gw // c), axis=0), 0.0)

    ti = lax.broadcasted_iota(jnp.int32, (c, gw), 0)
    sj = lax.broadcasted_iota(jnp.int32, (c, gw), 1) % c
    strict = sj < ti
    incl = sj <= ti
    tril_f = _tri(c, False).astype(F32)
    ones_c = jnp.ones((c, gw), F32)

    for ci in range(tt // c):
        sl = slice(ci * c, (ci + 1) * c)
        lw_c = lw[sl]
        cl = _mm_f32(tril_f, lw_c)
        wcum = jnp.exp(cl)
        inv_w = jnp.exp(-cl)
        a_h = kka[sl] * inv_w
        b_h = kk[sl] * jnp.exp(cl - lw_c)
        k_h = k2[sl] * inv_w
        r_h = r[sl] * wcum
        v_c = v[sl]
        a_bd = bd(a_h)
        k_bd = bd(k_h)
        v_bd = bd(v_c)
        br = jnp.concatenate([b_h, r_h], axis=0)
        xa = _mm_nt(br, a_bd)
        xk = _mm_nt(br, k_bd)
        l_a = jnp.where(strict, xa[:c], 0.0)
        l_k = jnp.where(strict, xk[:c], 0.0)
        ra = jnp.where(incl, xa[c:], 0.0)
        rk = jnp.where(incl, xk[c:], 0.0)
        p = -l_a
        tm1 = p
        for _ in range(5):
            p = _mm(p, bd(p))
            tm1 = tm1 + _mm(tm1, bd(p)) + p
        state = st_scr[...]
        brh = _mm(br, state)
        rhs = brh[:c] + _mm(l_k, v_bd)
        u = rhs + _mm(tm1, bd(rhs))
        o = brh[c:] - _mm(ra, bd(u)) + _mm(rk, v_bd)
        w_end = wcum[c - 1:c]
        w_col = jnp.exp(lax.dot_general(lw_c, ones_c, (((0,), (0,)), ((), ())),
                                        precision=HIGHEST, preferred_element_type=F32))
        upd = _mm_tn(jnp.concatenate([-(a_h * w_end), k_h * w_end], axis=0),
                     jnp.concatenate([u, v_c], axis=0))
        st_scr[...] = w_col * state + jnp.where(bdmask, upd, 0.0)

        mean = head_sum(o) * (1.0 / hd)
        cen = o - mean
        var = head_sum(cen * cen) * (1.0 / hd)
        on = cen * lax.rsqrt(var + RWKV_GN_EPS) * lnx_w + lnx_b
        bonus = head_sum(r[sl] * k2[sl] * r_k) * v_c
        o_ref[sl, :] = ((on + bonus) * gate[sl]).astype(o_ref.dtype)


def _rwkv(z, bsz, seq, mu, vecs, wup_pad, aup_pad, gup_pad):
    m = z.shape[0]
    tt = 256
    nt = seq // tt
    gw = RWKV_GROUP
    ngrp = RWKV_WIDTH // gw
    zblk = lambda c0: (lambda b, g, t: (b * nt + t, c0 // gw + g))
    mublk = lambda c0: (lambda b, g, t: (0, c0 // gw + g))
    lora_col = (COL_RWKV + 3 * RWKV_WIDTH) // gw
    grp_cols = lambda b, g, t: (0, g)
    return pl.pallas_call(
        _rwkv_body,
        grid=(bsz, ngrp, nt),
        in_specs=[
            pl.BlockSpec((tt, gw), zblk(COL_RWKV)),
            pl.BlockSpec((tt, gw), zblk(COL_RWKV + RWKV_WIDTH)),
            pl.BlockSpec((tt, gw), zblk(COL_RWKV + 2 * RWKV_WIDTH)),
            pl.BlockSpec((tt, gw), lambda b, g, t: (b * nt + t, lora_col)),
            pl.BlockSpec((1, gw), mublk(0)),
            pl.BlockSpec((1, gw), mublk(RWKV_WIDTH)),
            pl.BlockSpec((1, gw), mublk(2 * RWKV_WIDTH)),
            pl.BlockSpec((1, gw), lambda b, g, t: (0, 3 * RWKV_WIDTH // gw)),
            pl.BlockSpec((8, gw), grp_cols),
            pl.BlockSpec((RWKV_LORA, gw), grp_cols),
            pl.BlockSpec((RWKV_LORA, gw), grp_cols),
            pl.BlockSpec((RWKV_LORA, gw), grp_cols),
        ],
        out_specs=pl.BlockSpec((tt, gw), lambda b, g, t: (b * nt + t, g)),
        out_shape=jax.ShapeDtypeStruct((m, RWKV_WIDTH), BF16),
        scratch_shapes=[pltpu.VMEM((8, gw), F32)] * 4 + [pltpu.VMEM((gw, gw), F32)],
        compiler_params=_cparams("parallel", "parallel", "arbitrary"),
        name="rwkv7",
    )(z, z, z, z, mu, mu, mu, mu, vecs, wup_pad, aup_pad, gup_pad)


def _split_bf16(x, terms):
    parts = []
    rem = x
    for i in range(terms):
        p = rem.astype(BF16)
        parts.append(p)
        if i + 1 < terms:
            rem = rem - p.astype(F32)
    return parts


def _mask_mm(mask_bf, x, terms=3):
    return sum(jnp.dot(mask_bf, p, preferred_element_type=F32) for p in _split_bf16(x, terms))


def _mm_mask(x, mask_bf, terms=3):
    return sum(jnp.dot(p, mask_bf, preferred_element_type=F32) for p in _split_bf16(x, terms))


def _gdn2_body(q_ref, k_ref, v_ref, zg_ref, ab_ref, abt_ref, cw_ref, gp_ref, gpt_ref, nw_ref,
               o_ref, pq_scr, pk_scr, pv_scr, st_scr):
    tt = q_ref.shape[0]
    c = GDN_CHUNK
    width = GDN_HEADS * GDN_DK

    @pl.when(pl.program_id(1) == 0)
    def _():
        for s in (pq_scr, pk_scr, pv_scr, st_scr):
            s[...] = jnp.zeros_like(s)

    cw = cw_ref[...]

    def conv_silu(x_ref, p_scr, w):
        x = x_ref[...].astype(F32)
        prev = p_scr[...]
        y = (w[3:4] * x + w[2:3] * _shift_prev(x, prev, 1) + w[1:2] * _shift_prev(x, prev, 2)
             + w[0:1] * _shift_prev(x, prev, 3))
        p_scr[...] = x[tt - 8:]
        return jax.nn.silu(y)

    q = conv_silu(q_ref, pq_scr, cw[:, 0:width])
    k = conv_silu(k_ref, pk_scr, cw[:, width:2 * width])
    v = conv_silu(v_ref, pv_scr, cw[:, 2 * width:])

    ab = ab_ref[...].astype(F32)
    gp = gp_ref[...]
    gpt = gpt_ref[...]
    g_all = -jnp.exp(gp[0:1]) * _softplus(ab + gp[1:2])
    beta_all = jax.nn.sigmoid(ab)

    incl = _tri(c, False)
    strict = _tri(c, True)
    cum_mask = jnp.concatenate([incl.astype(BF16), jnp.ones((c, c), BF16)], axis=0)
    triu_bf = (lax.broadcasted_iota(jnp.int32, (c, c), 0) <= lax.broadcasted_iota(jnp.int32, (c, c), 1)).astype(BF16)
    nw = nw_ref[...]

    n_chunks = tt // c
    heads = range(GDN_HEADS)
    inst = [(ci, h) for ci in range(n_chunks) for h in heads]
    sls = [slice(ci * c, (ci + 1) * c) for ci in range(n_chunks)]
    hss = [slice(h * GDN_DK, (h + 1) * GDN_DK) for h in heads]
    gcs = [_mask_mm(cum_mask, g_all[sl]) for sl in sls]
    gr_all = [_mm_mask(-jnp.exp(gpt[:, 0:1]) * _softplus(abt_ref[ci][0:GDN_HEADS] + gpt[:, 1:2]), triu_bf)
              for ci in range(n_chunks)]

    qs, ks, vs, kbs, decays, egcs, kdecs, gtots, bcols = {}, {}, {}, {}, {}, {}, {}, {}, {}
    for ci, h in inst:
        sl, hs = sls[ci], hss[h]
        qc, kc = q[sl, hs], k[sl, hs]
        qs[ci, h] = qc * lax.rsqrt(jnp.sum(qc * qc, axis=-1, keepdims=True) + RMS_EPS) * (GDN_DK ** -0.5)
        ks[ci, h] = kc * lax.rsqrt(jnp.sum(kc * kc, axis=-1, keepdims=True) + RMS_EPS)
        gcol = gcs[ci][:c, h:h + 1]
        g_last = gcs[ci][c:, h:h + 1]
        grow = gr_all[ci][h:h + 1, :]
        bcols[ci, h] = beta_all[sl, GDN_HEADS + h:GDN_HEADS + h + 1]
        decays[ci, h] = jnp.exp(jnp.where(incl, gcol - grow, -BIG))
        kbs[ci, h] = ks[ci, h] * bcols[ci, h]
        egcs[ci, h] = jnp.exp(gcol)
        kdecs[ci, h] = ks[ci, h] * jnp.exp(g_last - gcol)
        gtots[ci, h] = jnp.exp(jnp.concatenate([g_last] * (GDN_DK // c), axis=0))

    kk = {i: _mm_nt(kbs[i], ks[i]) for i in inst}
    qk = {i: _mm_nt(qs[i], ks[i]) * decays[i] for i in inst}
    p = {i: -jnp.where(strict, kk[i] * decays[i], 0.0) for i in inst}
    tm1 = dict(p)
    for _ in range(5):
        p = {i: _mm(p[i], p[i]) for i in inst}
        tp = {i: _mm(tm1[i], p[i]) for i in inst}
        tm1 = {i: tm1[i] + tp[i] + p[i] for i in inst}
    rhs = {i: jnp.concatenate([v[sls[i[0]], hss[i[1]]] * bcols[i], kbs[i] * egcs[i]], axis=1) for i in inst}
    tr = {i: _mm(tm1[i], rhs[i]) for i in inst}
    sol = {i: rhs[i] + tr[i] for i in inst}

    state = [st_scr[h] for h in heads]
    for ci in range(n_chunks):
        ws = [_mm(sol[ci, h][:, GDN_DV:], state[h]) for h in heads]
        qst = [_mm(qs[ci, h] * egcs[ci, h], state[h]) for h in heads]
        v_new = [sol[ci, h][:, :GDN_DV] - ws[h] for h in heads]
        qv = [_mm(qk[ci, h], v_new[h]) for h in heads]
        kv = [_mm_tn(kdecs[ci, h], v_new[h]) for h in heads]
        state = [state[h] * gtots[ci, h] + kv[h] for h in heads]
        for h in heads:
            o = qst[h] + qv[h]
            o = o * lax.rsqrt(jnp.mean(o * o, axis=-1, keepdims=True) + RMS_EPS) * nw
            o = o * jax.nn.silu(zg_ref[sls[ci], hss[h]].astype(F32))
            o_ref[sls[ci], hss[h]] = o.astype(o_ref.dtype)
    for h in heads:
        st_scr[h] = state[h]


def _gdn2(z, abt, bsz, seq, conv_w, gp, gpt, norm_w):
    m = z.shape[0]
    tt = 128
    nt = seq // tt
    width = GDN_HEADS * GDN_DK
    zblk = lambda c0: pl.BlockSpec((tt, width), lambda b, t: (b * nt + t, c0 // width))
    const2 = lambda b, t: (0, 0)
    return pl.pallas_call(
        _gdn2_body,
        grid=(bsz, nt),
        in_specs=[
            zblk(COL_QKV), zblk(COL_QKV + width), zblk(COL_QKV + 2 * width), zblk(COL_ZG),
            pl.BlockSpec((tt, 128), lambda b, t: (b * nt + t, COL_AB // 128)),
            pl.BlockSpec((tt // GDN_CHUNK, 2 * GDN_HEADS, GDN_CHUNK), lambda b, t: (b * nt + t, 0, 0)),
            pl.BlockSpec((4, 3 * width), const2),
            pl.BlockSpec((8, 128), const2),
            pl.BlockSpec((8, 128), const2),
            pl.BlockSpec((1, 128), const2),
        ],
        out_specs=pl.BlockSpec((tt, width), lambda b, t: (b * nt + t, 0)),
        out_shape=jax.ShapeDtypeStruct((m, width), BF16),
        scratch_shapes=[pltpu.VMEM((8, width), F32)] * 3 + [pltpu.VMEM((GDN_HEADS, GDN_DK, GDN_DV), F32)],
        compiler_params=_cparams("parallel", "arbitrary"),
        name="gated_deltanet",
    )(z, z, z, z, z, abt, conv_w, gp, gpt, norm_w)


def _rwkv2_body(r_ref, k_ref, v_ref, lo_ref, mu_ref, vp_ref, wup_ref, aup_ref, gup_ref, o_ref,
                pr_scr, pk_scr, pv_scr, plo_scr, st_scr):
    tt = r_ref.shape[0]
    c = RWKV_CHUNK
    gw = RWKV_GROUP
    hd = RWKV_HD
    width = RWKV_WIDTH

    @pl.when(pl.program_id(1) == 0)
    def _():
        for s in (pr_scr, pk_scr, pv_scr, plo_scr, st_scr):
            s[...] = jnp.zeros_like(s)

    mu = mu_ref[...]

    def tshift(x_ref, p_scr, m):
        x = x_ref[...].astype(F32)
        xs = x + m * (_shift_prev(x, p_scr[...], 1) - x)
        p_scr[...] = x[tt - 8:]
        return xs

    r = tshift(r_ref, pr_scr, mu[:, 0:width])
    k = tshift(k_ref, pk_scr, mu[:, width:2 * width])
    v = tshift(v_ref, pv_scr, mu[:, 2 * width:3 * width])
    lo = tshift(lo_ref, plo_scr, mu[:, 3 * width:])
    vp = vp_ref[...]
    w0, a0, k_k, k_a, r_k, lnx_w, lnx_b = (vp[i:i + 1] for i in range(7))

    lane = lax.broadcasted_iota(jnp.int32, lo.shape, 1)
    lo_act = jnp.where(lane < 64, jnp.tanh(lo), jnp.where(lane < 128, lo, jax.nn.sigmoid(lo)))
    w_pre = _mm(lo_act, wup_ref[...])
    a_pre = _mm(lo_act, aup_ref[...])
    gate = _mm(lo_act, gup_ref[...])
    w_log = -_softplus(-(w0 + w_pre)) - 0.5
    lw = -jnp.exp(w_log)
    a = jax.nn.sigmoid(a0 + a_pre)

    bi = lax.broadcasted_iota(jnp.int32, (gw, gw), 0) // hd
    bj = lax.broadcasted_iota(jnp.int32, (gw, gw), 1) // hd
    bdmask = bi == bj
    ones_bd = bdmask.astype(BF16)

    def head_sum(x):
        return _mm_mask(x, ones_bd, terms=2)

    def bd(x):
        return jnp.where(bdmask, jnp.concatenate([x] * (gw // c), axis=0), 0.0)

    ti = lax.broadcasted_iota(jnp.int32, (c, gw), 0)
    sj = lax.broadcasted_iota(jnp.int32, (c, gw), 1) % c
    strict = sj < ti
    incl = sj <= ti
    tril_bf = _tri(c, False).astype(BF16)

    kk_all = k * k_k
    k2_all = k * (1.0 + (a - 1.0) * k_a)

    n_chunks = tt // c
    groups = range(width // gw)
    inst = [(ci, g) for ci in range(n_chunks) for g in groups]
    sls = [slice(ci * c, (ci + 1) * c) for ci in range(n_chunks)]
    gss = [slice(g * gw, (g + 1) * gw) for g in groups]
    cl_all = [_mask_mm(tril_bf, lw[sl]) for sl in sls]

    def at(x, i):
        return x[sls[i[0]], gss[i[1]]]

    kk_raw = {i: at(kk_all, i) for i in inst}
    kk_ss = {i: head_sum(kk_raw[i] * kk_raw[i]) for i in inst}
    a_h, b_h, k_h, br, w_end = {}, {}, {}, {}, {}
    for i in inst:
        cl = cl_all[i[0]][:, gss[i[1]]]
        kk = kk_raw[i] * lax.rsqrt(kk_ss[i] + RMS_EPS)
        wcum = jnp.exp(cl)
        inv_w = jnp.exp(-cl)
        a_h[i] = kk * at(a, i) * inv_w
        b_h[i] = kk * jnp.exp(cl - at(lw, i))
        k_h[i] = at(k2_all, i) * inv_w
        br[i] = jnp.concatenate([b_h[i], at(r, i) * wcum], axis=0)
        w_end[i] = wcum[c - 1:c]
    v_bd = {i: bd(at(v, i)) for i in inst}
    xa = {i: _mm_nt(br[i], bd(a_h[i])) for i in inst}
    xk = {i: _mm_nt(br[i], bd(k_h[i])) for i in inst}
    l_k = {i: jnp.where(strict, xk[i][:c], 0.0) for i in inst}
    ra = {i: jnp.where(incl, xa[i][c:], 0.0) for i in inst}
    rk = {i: jnp.where(incl, xk[i][c:], 0.0) for i in inst}
    p = {i: -jnp.where(strict, xa[i][:c], 0.0) for i in inst}
    tm1 = dict(p)
    for _ in range(5):
        p = {i: _mm(p[i], bd(p[i])) for i in inst}
        tp = {i: _mm(tm1[i], bd(p[i])) for i in inst}
        tm1 = {i: tm1[i] + tp[i] + p[i] for i in inst}
    lkv = {i: _mm(l_k[i], v_bd[i]) for i in inst}
    rkv = {i: _mm(rk[i], v_bd[i]) for i in inst}
    so = {i: head_sum(at(r, i) * at(k2_all, i) * r_k[:, gss[i[1]]]) for i in inst}

    state_t = [st_scr[g] for g in groups]
    for ci in range(n_chunks):
        ids = [(ci, g) for g in groups]
        brh = [_mm_nt(br[i], state_t[i[1]]) for i in ids]
        rhs = [brh[g][:c] + lkv[ci, g] for g in groups]
        tu = [_mm(tm1[ci, g], bd(rhs[g])) for g in groups]
        u = [rhs[g] + tu[g] for g in groups]
        rau = [_mm(ra[ci, g], bd(u[g])) for g in groups]
        upd = [_mm_tn(jnp.concatenate([u[g], at(v, (ci, g))], axis=0),
                      jnp.concatenate([-(a_h[ci, g] * w_end[ci, g]), k_h[ci, g] * w_end[ci, g]], axis=0))
               for g in groups]
        state_t = [w_end[ci, g] * state_t[g] + jnp.where(bdmask, upd[g], 0.0) for g in groups]
        o = [brh[g][c:] - rau[g] + rkv[ci, g] for g in groups]
        osum = [head_sum(o[g]) for g in groups]
        cen = [o[g] - osum[g] * (1.0 / hd) for g in groups]
        var = [head_sum(cen[g] * cen[g]) * (1.0 / hd) for g in groups]
        for g in groups:
            gs = gss[g]
            on = cen[g] * lax.rsqrt(var[g] + RWKV_GN_EPS) * lnx_w[:, gs] + lnx_b[:, gs]
            bonus = so[ci, g] * at(v, (ci, g))
            o_ref[sls[ci], gs] = ((on + bonus) * gate[sls[ci], gs]).astype(o_ref.dtype)
    for g in groups:
        st_scr[g] = state_t[g]


def _rwkv2(z, bsz, seq, mu, vecs, wup_pad, aup_pad, gup_pad):
    m = z.shape[0]
    tt = 128
    nt = seq // tt
    gw = RWKV_GROUP
    width = RWKV_WIDTH
    zblk = lambda c0: pl.BlockSpec((tt, width), lambda b, t: (b * nt + t, c0 // width))
    lora_col = (COL_RWKV + 3 * width) // RWKV_LORA
    const2 = lambda b, t: (0, 0)
    return pl.pallas_call(
        _rwkv2_body,
        grid=(bsz, nt),
        in_specs=[
            zblk(COL_RWKV), zblk(COL_RWKV + width), zblk(COL_RWKV + 2 * width),
            pl.BlockSpec((tt, RWKV_LORA), lambda b, t: (b * nt + t, lora_col)),
            pl.BlockSpec(mu.shape, const2),
            pl.BlockSpec((8, width), const2),
            pl.BlockSpec((RWKV_LORA, width), const2),
            pl.BlockSpec((RWKV_LORA, width), const2),
            pl.BlockSpec((RWKV_LORA, width), const2),
        ],
        out_specs=pl.BlockSpec((tt, width), lambda b, t: (b * nt + t, 0)),
        out_shape=jax.ShapeDtypeStruct((m, width), BF16),
        scratch_shapes=[pltpu.VMEM((8, width), F32)] * 3 + [pltpu.VMEM((8, RWKV_LORA), F32),
                                                            pltpu.VMEM((width // gw, gw, gw), F32)],
        compiler_params=_cparams("parallel", "arbitrary"),
        name="rwkv7",
    )(z, z, z, z, mu, vecs, wup_pad, aup_pad, gup_pad)


def _merge_body(h_ref, ya_ref, yb_ref, yc_ref, za_ref, zb_ref, zc_ref, mb_ref, pa_ref, pb_ref,
                pc_ref, wo_ref, nw_ref, hout_ref, xnt_ref):
    mb = mb_ref[...]
    ga = jax.nn.sigmoid(za_ref[...].astype(F32) + mb[:, 0:D_MODEL])
    gb = jax.nn.sigmoid(zb_ref[...].astype(F32) + mb[:, D_MODEL:2 * D_MODEL])
    gc = jax.nn.sigmoid(zc_ref[...].astype(F32) + mb[:, 2 * D_MODEL:])
    merged = (ga * jnp.dot(ya_ref[...], pa_ref[...], preferred_element_type=F32)
              + gb * jnp.dot(yb_ref[...], pb_ref[...], preferred_element_type=F32)
              + gc * jnp.dot(yc_ref[...], pc_ref[...], preferred_element_type=F32))
    h = h_ref[...] + _mm(merged, wo_ref[...])
    hout_ref[...] = h
    xnt_ref[...] = jnp.transpose(_rms(h, nw_ref[...])).astype(BF16)


def _merge(h, z, ya, yb, yc, merge_b, pa, pb, pc, wo, norm_w):
    m = h.shape[0]
    tm = 256
    row = lambda c: pl.BlockSpec((tm, D_MODEL), lambda i: (i, c))
    wspec = pl.BlockSpec((D_MODEL, D_MODEL), lambda i: (0, 0))
    mc = COL_MERGE // D_MODEL
    return pl.pallas_call(
        _merge_body,
        grid=(m // tm,),
        in_specs=[row(0), row(0), row(0), row(0), row(mc), row(mc + 1), row(mc + 2),
                  pl.BlockSpec((1, 3 * D_MODEL), lambda i: (0, 0)),
                  wspec, wspec, wspec, wspec,
                  pl.BlockSpec((1, D_MODEL), lambda i: (0, 0))],
        out_specs=[row(0), pl.BlockSpec((D_MODEL, tm), lambda i: (0, i))],
        out_shape=[jax.ShapeDtypeStruct((m, D_MODEL), F32), jax.ShapeDtypeStruct((D_MODEL, m), BF16)],
        compiler_params=_cparams("parallel"),
        name="merge_outproj",
    )(h, ya, yb, yc, z, z, z, merge_b, pa, pb, pc, wo, norm_w)


_CAND_SLABS = (
    (16, ((0, 0, 16),)),
    (16, ((1, 0, 8), (2, 8, 5), (4, 13, 3))),
    (16, ((3, 0, 4), (5, 4, 2), (6, 6, 2), (7, 8, 2), (8, 10, 1), (9, 11, 1), (10, 12, 1), (11, 13, 1),
          (12, 14, 1), (13, 15, 1))),
    (8, ((14, 0, 1), (15, 1, 1))),
)


def _odd_even_merge_sort_pairs(n):
    pairs = []
    p = 1
    while p < n:
        k = p
        while k >= 1:
            for j in range(k % p, n - k, 2 * k):
                for i in range(min(k, n - j - k)):
                    if (i + j) // (2 * p) == (i + j + k) // (2 * p):
                        pairs.append((i + j, i + j + k))
            k //= 2
        p *= 2
    return tuple(pairs)


_SORT_PAIRS = _odd_even_merge_sort_pairs(PEER_TOPK)


def _candidate_sums(a1, a2):
    slabs = []
    for nrows, pieces in _CAND_SLABS:
        base = a2[:nrows]
        rows = lax.broadcasted_iota(jnp.int32, base.shape, 0)
        out = None
        end = 0
        for p, off, cnt in pieces:
            val = a1[p:p + 1] + (base if off == 0 else pltpu.roll(base, off, 0))
            out = val if out is None else jnp.where(rows >= off, val, out)
            end = off + cnt
        if end < nrows:
            out = jnp.where(rows >= end, -BIG, out)
        slabs.append(out)
    return jnp.concatenate(slabs, axis=0)


def _peer_topk_body(xnt_ref, wqt_ref, keys_ref, n_ref, e1_ref, r2_ref, e2_ref, q_scr):
    tt = xnt_ref.shape[1]
    nk = PEER_NKEYS
    topk = PEER_TOPK
    q_scr[...] = jnp.dot(wqt_ref[...], xnt_ref[...], preferred_element_type=F32)
    lanes = 128
    sub = 8
    assert nk == topk * sub
    row8 = lax.broadcasted_iota(jnp.int32, (sub, lanes), 0)

    def top_sorted(s):
        v = [s[k * sub:(k + 1) * sub] for k in range(topk)]
        for i, j in _SORT_PAIRS:
            v[i], v[j] = jnp.maximum(v[i], v[j]), jnp.minimum(v[i], v[j])
        for shift in (4, 2, 1):
            other = [pltpu.roll(x, shift, 0) for x in v]
            v = [jnp.maximum(v[k], other[topk - 1 - k]) for k in range(topk)]
            d = topk // 2
            while d >= 1:
                for i in range(topk):
                    if i & d == 0:
                        v[i], v[i + d] = jnp.maximum(v[i], v[i + d]), jnp.minimum(v[i], v[i + d])
                d //= 2
        return v

    def compact(v):
        tiles = []
        for t0 in range(0, topk, sub):
            out = v[t0]
            for k in range(1, sub):
                out = jnp.where(row8 == k, v[t0 + k], out)
            tiles.append(out)
        return jnp.concatenate(tiles, axis=0)

    def head(h, carry):
        o1 = pl.multiple_of(h * (2 * nk), 2 * nk)
        s1 = _mm(keys_ref[2 * h], q_scr[pl.ds(o1, nk), :])
        s2 = _mm(keys_ref[2 * h + 1], q_scr[pl.ds(o1 + nk, nk), :])
        chunks = [slice(l0, l0 + lanes) for l0 in range(0, tt, lanes)]
        a1 = [top_sorted(s1[:, ls]) for ls in chunks]
        a2 = [top_sorted(s2[:, ls]) for ls in chunks]
        a1c = jnp.concatenate([compact(v) for v in a1], axis=1)
        a2c = jnp.concatenate([compact(v) for v in a2], axis=1)
        cand = _candidate_sums(a1c, a2c)
        cmax = a1c[0:1] + a2c[0:1]

        def cbody(rnd, carry):
            cnd, zsum, c_in, c_out = carry
            mx = jnp.max(cnd, axis=0, keepdims=True)
            zsum = zsum + jnp.where(rnd < topk, jnp.exp(mx - cmax), 0.0)
            c_in = jnp.where(rnd == topk - 1, mx, c_in)
            c_out = jnp.where(rnd == topk, mx, c_out)
            return jnp.where(cnd == mx, -BIG, cnd), zsum, c_in, c_out

        zero = jnp.zeros((1, tt), F32)
        _, zsum, c_in, c_out = lax.fori_loop(0, topk + 1, cbody, (cand, zero, zero, zero))
        tau = 0.5 * (c_in + c_out)
        inv_z = 1.0 / zsum
        pack = 16

        def rep(x):
            return jnp.concatenate([x] * (pack // sub), axis=0)

        for ci, ls in enumerate(chunks):
            tau8 = jnp.broadcast_to(tau[:, ls], (sub, lanes))
            need = [rep(tau8 - a2[ci][qq]) for qq in range(topk)]
            val2 = [rep(a2[ci][qq]) for qq in range(topk)]
            last1 = rep(a1[ci][topk - 1])
            top1 = rep(a1[ci][0])
            top2 = val2[0]
            for k in range(nk // pack):
                rows = slice(k * pack, (k + 1) * pack)
                x1 = s1[rows, ls]
                x2 = s2[rows, ls]
                n_sel = jnp.zeros((pack, lanes), F32)
                r2 = jnp.zeros((pack, lanes), F32)
                for qq in range(topk):
                    n_sel = jnp.where(x1 >= need[qq], n_sel + 1.0, n_sel)
                    r2 = jnp.where(x2 < val2[qq], r2 + 1.0, r2)
                n_ref[h, rows, ls] = jnp.where(x1 >= last1, n_sel, 0.0)
                e1_ref[h, rows, ls] = jnp.exp(x1 - top1)
                r2_ref[h, rows, ls] = r2.astype(BF16)
                e2_ref[h, rows, ls] = (jnp.exp(x2 - top2) * inv_z[:, ls]).astype(BF16)
        return carry

    lax.fori_loop(0, PEER_HEADS, head, 0)


def _peer_topk(xnt, wqt, keys):
    m = xnt.shape[1]
    tt = 512 if m % 512 == 0 else 256
    shape = (PEER_HEADS, PEER_NKEYS, m)
    ospec = pl.BlockSpec((PEER_HEADS, PEER_NKEYS, tt), lambda i: (0, 0, i))
    return pl.pallas_call(
        _peer_topk_body,
        grid=(m // tt,),
        in_specs=[pl.BlockSpec((D_MODEL, tt), lambda i: (0, i)),
                  pl.BlockSpec(wqt.shape, lambda i: (0, 0)),
                  pl.BlockSpec(keys.shape, lambda i: (0, 0, 0))],
        out_specs=[ospec] * 4,
        out_shape=[jax.ShapeDtypeStruct(shape, F32), jax.ShapeDtypeStruct(shape, F32),
                   jax.ShapeDtypeStruct(shape, BF16), jax.ShapeDtypeStruct(shape, BF16)],
        scratch_shapes=[pltpu.VMEM((wqt.shape[0], tt), F32)],
        compiler_params=_cparams("parallel"),
        name="peer_topk",
    )(xnt, wqt, keys)


PEER_TOKEN_CHUNK = 256


def _peer_dense_body(u_ref, xnt_ref, vt_ref, n_ref, e1_ref, r2_ref, e2_ref, y_ref, acc_scr):
    nk = PEER_NKEYS
    tt = xnt_ref.shape[1]
    tc = PEER_TOKEN_CHUNK
    n_blk = u_ref.shape[0] // nk
    pack = 16

    @pl.when(pl.program_id(1) == 0)
    def _():
        acc_scr[...] = jnp.zeros_like(acc_scr)

    def rows_bf16(ref, h, ii, ls):
        row = jnp.broadcast_to(ref[h, ii:ii + 1, ls], (pack, tc)).astype(BF16)
        return jnp.concatenate([row] * (nk // pack), axis=0)

    def select_weights(ii, c0):
        ls = slice(c0, c0 + tc)
        w = None
        for h in range(PEER_HEADS):
            sel = jnp.where(r2_ref[h, :, ls] < rows_bf16(n_ref, h, ii, ls), e2_ref[h, :, ls],
                            jnp.zeros((), BF16))
            term = sel * rows_bf16(e1_ref, h, ii, ls)
            w = term if w is None else w + term
        return w

    def gelu_tanh(x):
        c = 0.7978845608028654
        inner = x * ((x * x) * (c * 0.044715) + c)
        hx = 0.5 * x
        return hx * jnp.tanh(inner) + hx

    starts = list(range(0, tt, tc))
    pre = [jnp.dot(u_ref[...], xnt_ref[:, c0:c0 + tc], preferred_element_type=F32) for c0 in starts]
    for idx, c0 in enumerate(starts):
        act = gelu_tanh(pre[idx]).astype(BF16)
        a = jnp.concatenate([act[ii * nk:(ii + 1) * nk] * select_weights(ii, c0) for ii in range(n_blk)], axis=0)
        acc_scr[:, c0:c0 + tc] += jnp.dot(vt_ref[...], a, preferred_element_type=F32)

    @pl.when(pl.program_id(1) == pl.num_programs(1) - 1)
    def _():
        y_ref[...] = jnp.transpose(acc_scr[...])


def _peer_dense(xnt, u_bf, vt_bf, n_sel, e1, r2, e2, tt):
    m = xnt.shape[1]
    te = 1024
    ib = te // PEER_NKEYS
    sel_i = pl.BlockSpec((PEER_HEADS, ib, tt), lambda i, e: (0, e, i))
    sel_all = pl.BlockSpec((PEER_HEADS, PEER_NKEYS, tt), lambda i, e: (0, 0, i))
    return pl.pallas_call(
        _peer_dense_body,
        grid=(m // tt, PEER_EXPERTS // te),
        in_specs=[pl.BlockSpec((te, D_MODEL), lambda i, e: (e, 0)),
                  pl.BlockSpec((D_MODEL, tt), lambda i, e: (0, i)),
                  pl.BlockSpec((D_MODEL, te), lambda i, e: (0, e)),
                  sel_i, sel_i, sel_all, sel_all],
        out_specs=pl.BlockSpec((tt, D_MODEL), lambda i, e: (i, 0)),
        out_shape=jax.ShapeDtypeStruct((m, D_MODEL), F32),
        scratch_shapes=[pltpu.VMEM((D_MODEL, tt), F32)],
        compiler_params=_cparams("parallel", "arbitrary"),
        name="peer_dense",
    )(u_bf, xnt, vt_bf, n_sel, e1, r2, e2)


def _final_body(h_ref, y_ref, nw_ref, o_ref):
    o_ref[...] = _rms(h_ref[...] + y_ref[...], nw_ref[...])


def _final(h, y, norm_w):
    m = h.shape[0]
    tm = 512
    row = pl.BlockSpec((tm, D_MODEL), lambda i: (i, 0))
    return pl.pallas_call(
        _final_body,
        grid=(m // tm,),
        in_specs=[row, row, pl.BlockSpec((1, D_MODEL), lambda i: (0, 0))],
        out_specs=row,
        out_shape=jax.ShapeDtypeStruct((m, D_MODEL), F32),
        compiler_params=_cparams("parallel"),
        name="final_norm",
    )(h, y, norm_w.reshape(1, D_MODEL))


def _pad_rows(rows, width, n_rows=8):
    out = jnp.zeros((n_rows, width), F32)
    for i, r in enumerate(rows):
        flat = r.reshape(-1).astype(F32)
        out = out.at[i, :flat.shape[0]].set(flat)
    return out


def _chunk_rows(ab):
    m, n = ab.shape
    return jnp.transpose(ab.reshape(m // GDN_CHUNK, GDN_CHUNK, n), (0, 2, 1)).astype(F32)


def _block_diag_groups(w):
    per = LRU_GROUP // LRU_BLOCK_DIM
    n_grp = w.shape[0] // per
    out = jnp.zeros((n_grp, LRU_GROUP, LRU_GROUP), F32)
    for g in range(n_grp):
        for j in range(per):
            o = j * LRU_BLOCK_DIM
            out = out.at[g, o:o + LRU_BLOCK_DIM, o:o + LRU_BLOCK_DIM].set(w[g * per + j])
    return out.astype(BF16)


def _prep_w_in(w):
    pad = jnp.zeros((D_MODEL, IN_WIDTH_PAD - COL_AB - 2 * GDN_HEADS), w.dtype)
    return jnp.concatenate([w[:, :6144], w[:, 9488:12560], w[:, 6160:9488], w[:, 6144:6160], pad],
                           axis=1).astype(BF16)


def _layer(l, h, y, bsz, seq, p):
    z_h, z = _inproj(h, y, p["norm_mix_w"][l], _prep_w_in(p["w_in"][l]))
    h = z_h

    lru_vecs = _pad_rows([p["lru_conv_b"][l], p["lru_b_a"][l], p["lru_b_x"][l], p["lru_lambda"][l]], LRU_WIDTH)
    ya = _lru(z, bsz, seq, p["lru_conv_w"][l], lru_vecs,
              _block_diag_groups(p["lru_w_a"][l]), _block_diag_groups(p["lru_w_x"][l]))

    abt = _chunk_rows(z[:, COL_AB:COL_AB + 2 * GDN_HEADS])
    gp = _pad_rows([p["gdn_a_log"][l], p["gdn_dt_bias"][l]], 128)
    gpt = jnp.zeros((8, 128), F32).at[:, 0].set(p["gdn_a_log"][l]).at[:, 1].set(p["gdn_dt_bias"][l])
    yb = _gdn2(z, abt, bsz, seq, p["gdn_conv_w"][l], gp, gpt, p["gdn_norm_w"][l].reshape(1, GDN_DV))

    rw_vecs = _pad_rows([p["rwkv_w0"][l], p["rwkv_a0"][l], p["rwkv_k_k"][l], p["rwkv_k_a"][l],
                         p["rwkv_r_k"][l], p["rwkv_lnx_w"][l], p["rwkv_lnx_b"][l]], RWKV_WIDTH)
    zl = jnp.zeros((RWKV_LORA, RWKV_WIDTH), F32)
    wup = zl.at[0:64].set(p["rwkv_w_up"][l]).astype(BF16)
    aup = zl.at[64:128].set(p["rwkv_a_up"][l]).astype(BF16)
    gup = zl.at[128:256].set(p["rwkv_g_up"][l]).astype(BF16)
    yc = _rwkv2(z, bsz, seq, p["rwkv_mu"][l].reshape(1, -1), rw_vecs, wup, aup, gup)

    h, xnt = _merge(h, z, ya, yb, yc, p["merge_b"][l].reshape(1, -1), p["p_lru"][l].astype(BF16),
                    p["p_gdn"][l].astype(BF16), p["p_rwkv"][l].astype(BF16), p["w_out"][l].astype(BF16),
                    p["norm_ffn_w"][l].reshape(1, D_MODEL))

    wqt = jnp.transpose(p["peer_wq"][l]).astype(BF16)
    keys = p["peer_keys"][l].reshape(2 * PEER_HEADS, PEER_NKEYS, PEER_HALF).astype(BF16)
    n_sel, e1, r2, e2 = _peer_topk(xnt, wqt, keys)
    tt = 512 if xnt.shape[1] % 512 == 0 else 256
    y = _peer_dense(xnt, p["peer_u"][l].astype(BF16), jnp.transpose(p["peer_v"][l]).astype(BF16),
                    n_sel, e1, r2, e2, tt)
    return h, y


def kernel(x, norm_mix_w, norm_ffn_w, final_norm_w, w_in, lru_conv_w, lru_conv_b, lru_w_a, lru_b_a,
           lru_w_x, lru_b_x, lru_lambda, gdn_conv_w, gdn_a_log, gdn_dt_bias, gdn_norm_w, rwkv_mu, rwkv_w0,
           rwkv_w_up, rwkv_a0, rwkv_a_up, rwkv_g_up, rwkv_k_k, rwkv_k_a, rwkv_r_k, rwkv_lnx_w, rwkv_lnx_b,
           merge_b, p_lru, p_gdn, p_rwkv, w_out, peer_wq, peer_keys, peer_u, peer_v):
    p = dict(locals())
    bsz, seq, dim = x.shape
    h = x.reshape(bsz * seq, dim)
    y = None
    for l in range(DEPTH):
        h, y = _layer(l, h, y, bsz, seq, p)
    return _final(h, y, final_norm_w).reshape(bsz, seq, dim)
```

```python
import functools

import jax
import jax.numpy as jnp
from jax import lax
from jax.experimental import pallas as pl
from jax.experimental.pallas import tpu as pltpu

F32 = jnp.float32
BF16 = jnp.bfloat16

D_MODEL = 1024
DEPTH = 2
RMS_EPS = 1e-6

LRU_WIDTH = 1024
LRU_BLOCK_DIM = 64
LRU_C = 8.0
LRU_GROUP = 256

GDN_HEADS = 8
GDN_DK = 128
GDN_DV = 128
GDN_CHUNK = 64

RWKV_HEADS = 16
RWKV_HD = 64
RWKV_WIDTH = 1024
RWKV_GN_EPS = 64e-5
RWKV_CHUNK = 64
RWKV_GROUP = 256
RWKV_LORA = 256

PEER_HEADS = 8
PEER_NKEYS = 128
PEER_EXPERTS = PEER_NKEYS * PEER_NKEYS
PEER_HALF = 128
PEER_TOPK = 16

COL_U = 0
COL_GATE = 1024
COL_QKV = 2048
COL_ZG = 5120
COL_MERGE = 6144
COL_RWKV = 9216
COL_AB = 12544
IN_WIDTH_PAD = 12672

VMEM_LIMIT = 48 * 1024 * 1024
BIG = 3.0e38


def _cparams(*sem):
    return pltpu.CompilerParams(dimension_semantics=sem, vmem_limit_bytes=VMEM_LIMIT)


def _mm(a, b):
    return jnp.dot(a.astype(BF16), b.astype(BF16), preferred_element_type=F32)


def _mm_nt(a, b):
    return lax.dot_general(a.astype(BF16), b.astype(BF16), (((1,), (1,)), ((), ())),
                           preferred_element_type=F32)


def _mm_tn(a, b):
    return lax.dot_general(a.astype(BF16), b.astype(BF16), (((0,), (0,)), ((), ())),
                           preferred_element_type=F32)


def _softplus(x):
    return jnp.maximum(x, 0.0) + jnp.log1p(jnp.exp(-jnp.abs(x)))


def _rms(x, w):
    return x * lax.rsqrt(jnp.mean(x * x, axis=-1, keepdims=True) + RMS_EPS) * w


def _shift_prev(x, prev8, s):
    r = pltpu.roll(x, s, 0)
    pr = pltpu.roll(prev8, s, 0)
    rows8 = lax.broadcasted_iota(jnp.int32, prev8.shape, 0)
    head = jnp.where(rows8 < s, pr, r[:8])
    return jnp.concatenate([head, r[8:]], axis=0)


def _shift_fill(x, d, fill):
    n, c = x.shape
    if d % 8 == 0:
        return jnp.concatenate([jnp.full((d, c), fill, x.dtype), x[:n - d]], axis=0)
    r = pltpu.roll(x, d, 0)
    rows8 = lax.broadcasted_iota(jnp.int32, (8, c), 0)
    head = jnp.where(rows8 < d, fill, r[:8])
    return jnp.concatenate([head, r[8:]], axis=0)


def _tri(n, strict):
    i = lax.broadcasted_iota(jnp.int32, (n, n), 0)
    j = lax.broadcasted_iota(jnp.int32, (n, n), 1)
    return (j < i) if strict else (j <= i)


def _split_bf16(x, terms):
    parts = []
    rem = x
    for i in range(terms):
        p = rem.astype(BF16)
        parts.append(p)
        if i + 1 < terms:
            rem = rem - p.astype(F32)
    return parts


def _mask_mm(mask_bf, x, terms=3):
    return sum(jnp.dot(mask_bf, p, preferred_element_type=F32) for p in _split_bf16(x, terms))


def _mm_mask(x, mask_bf, terms=3):
    return sum(jnp.dot(p, mask_bf, preferred_element_type=F32) for p in _split_bf16(x, terms))


def _inproj_body(add_y, *refs):
    if add_y:
        h_ref, y_ref, nw_ref, w_ref, hout_ref, z_ref, xn_scr = refs
    else:
        h_ref, nw_ref, w_ref, z_ref, xn_scr = refs

    @pl.when(pl.program_id(1) == 0)
    def _():
        h = h_ref[...]
        if add_y:
            h = h + y_ref[...]
            hout_ref[...] = h
        xn_scr[...] = _rms(h, nw_ref[...]).astype(BF16)

    z_ref[...] = jnp.dot(xn_scr[...], w_ref[...], preferred_element_type=F32).astype(z_ref.dtype)


def _inproj(h, y, norm_w, w_pad):
    m = h.shape[0]
    tm, tn = (1024 if m % 1024 == 0 else 512), 1152
    add_y = y is not None
    row = pl.BlockSpec((tm, D_MODEL), lambda i, j: (i, 0))
    in_specs = [row] + ([row] if add_y else []) + [
        pl.BlockSpec((1, D_MODEL), lambda i, j: (0, 0)),
        pl.BlockSpec((D_MODEL, tn), lambda i, j: (0, j)),
    ]
    z_spec = pl.BlockSpec((tm, tn), lambda i, j: (i, j))
    z_shape = jax.ShapeDtypeStruct((m, IN_WIDTH_PAD), BF16)
    args = (h,) + ((y,) if add_y else ()) + (norm_w.reshape(1, D_MODEL), w_pad)
    out = pl.pallas_call(
        functools.partial(_inproj_body, add_y),
        grid=(m // tm, IN_WIDTH_PAD // tn),
        in_specs=in_specs,
        out_specs=[row, z_spec] if add_y else z_spec,
        out_shape=[jax.ShapeDtypeStruct((m, D_MODEL), F32), z_shape] if add_y else z_shape,
        scratch_shapes=[pltpu.VMEM((tm, D_MODEL), BF16)],
        compiler_params=_cparams("parallel", "arbitrary"),
        name="norm_inproj",
    )(*args)
    return (out[0], out[1]) if add_y else (h, out)


def _lru_body(zu_ref, zg_ref, cw_ref, vp_ref, wa_ref, wx_ref, o_ref, prev_scr, hc_scr):
    tt = zu_ref.shape[0]

    @pl.when(pl.program_id(1) == 0)
    def _():
        prev_scr[...] = jnp.zeros_like(prev_scr)
        hc_scr[...] = jnp.zeros_like(hc_scr)

    u = zu_ref[...].astype(F32)
    prev = prev_scr[...]
    cw = cw_ref[...]
    vp = vp_ref[...]
    xc = (cw[3:4] * u + cw[2:3] * _shift_prev(u, prev, 1) + cw[1:2] * _shift_prev(u, prev, 2)
          + cw[0:1] * _shift_prev(u, prev, 3) + vp[0:1])
    prev_scr[...] = u[tt - 8:]

    xcb = xc.astype(BF16)
    n_grp = LRU_WIDTH // LRU_GROUP
    pre_a = jnp.concatenate(
        [jnp.dot(xcb[:, g * LRU_GROUP:(g + 1) * LRU_GROUP], wa_ref[g], preferred_element_type=F32)
         for g in range(n_grp)], axis=1)
    pre_x = jnp.concatenate(
        [jnp.dot(xcb[:, g * LRU_GROUP:(g + 1) * LRU_GROUP], wx_ref[g], preferred_element_type=F32)
         for g in range(n_grp)], axis=1)
    r = jax.nn.sigmoid(pre_a + vp[1:2])
    i = jax.nn.sigmoid(pre_x + vp[2:3])
    log_a = (-LRU_C * r) * _softplus(-vp[3:4])
    a = jnp.exp(log_a)
    b = jnp.sqrt(-jnp.tanh(log_a) * (a * a + 1.0)) * (i * xc)

    d = 1
    while d < tt:
        b = a * _shift_fill(b, d, 0.0) + b
        a = a * _shift_fill(a, d, 1.0)
        d *= 2
    hcar = hc_scr[...]
    hval = b + a * hcar[0:1]
    hc_scr[...] = jnp.broadcast_to(hval[tt - 1:tt], hcar.shape)
    o_ref[...] = (hval * jax.nn.gelu(zg_ref[...].astype(F32))).astype(o_ref.dtype)


def _lru(z, bsz, seq, conv_w, vecs, wa_bd, wx_bd):
    m = z.shape[0]
    tt = 256
    nt = seq // tt
    rowmap = lambda c: (lambda b, t: (b * nt + t, c))
    const2 = lambda b, t: (0, 0)
    return pl.pallas_call(
        _lru_body,
        grid=(bsz, nt),
        in_specs=[
            pl.BlockSpec((tt, LRU_WIDTH), rowmap(COL_U // LRU_WIDTH)),
            pl.BlockSpec((tt, LRU_WIDTH), rowmap(COL_GATE // LRU_WIDTH)),
            pl.BlockSpec((4, LRU_WIDTH), const2),
            pl.BlockSpec((8, LRU_WIDTH), const2),
            pl.BlockSpec(wa_bd.shape, lambda b, t: (0, 0, 0)),
            pl.BlockSpec(wx_bd.shape, lambda b, t: (0, 0, 0)),
        ],
        out_specs=pl.BlockSpec((tt, LRU_WIDTH), rowmap(0)),
        out_shape=jax.ShapeDtypeStruct((m, LRU_WIDTH), BF16),
        scratch_shapes=[pltpu.VMEM((8, LRU_WIDTH), F32), pltpu.VMEM((8, LRU_WIDTH), F32)],
        compiler_params=_cparams("parallel", "arbitrary"),
        name="rglru",
    )(z, z, conv_w, vecs, wa_bd, wx_bd)


def _gdn_body(q_ref, k_ref, v_ref, zg_ref, ab_ref, abt_ref, cw_ref, gp_ref, gpt_ref, nw_ref,
              o_ref, pq_scr, pk_scr, pv_scr, st_scr):
    tt = q_ref.shape[0]
    c = GDN_CHUNK
    width = GDN_HEADS * GDN_DK

    @pl.when(pl.program_id(1) == 0)
    def _():
        for s in (pq_scr, pk_scr, pv_scr, st_scr):
            s[...] = jnp.zeros_like(s)

    cw = cw_ref[...]

    def conv_silu(x_ref, p_scr, w):
        x = x_ref[...].astype(F32)
        prev = p_scr[...]
        y = (w[3:4] * x + w[2:3] * _shift_prev(x, prev, 1) + w[1:2] * _shift_prev(x, prev, 2)
             + w[0:1] * _shift_prev(x, prev, 3))
        p_scr[...] = x[tt - 8:]
        return jax.nn.silu(y)

    q = conv_silu(q_ref, pq_scr, cw[:, 0:width])
    k = conv_silu(k_ref, pk_scr, cw[:, width:2 * width])
    v = conv_silu(v_ref, pv_scr, cw[:, 2 * width:])

    ab = ab_ref[...].astype(F32)
    gp = gp_ref[...]
    gpt = gpt_ref[...]
    g_all = -jnp.exp(gp[0:1]) * _softplus(ab + gp[1:2])
    beta_all = jax.nn.sigmoid(ab)

    incl = _tri(c, False)
    strict = _tri(c, True)
    cum_mask = jnp.concatenate([incl.astype(BF16), jnp.ones((c, c), BF16)], axis=0)
    triu_bf = (lax.broadcasted_iota(jnp.int32, (c, c), 0) <= lax.broadcasted_iota(jnp.int32, (c, c), 1)).astype(BF16)
    nw = nw_ref[...]

    n_chunks = tt // c
    heads = range(GDN_HEADS)
    inst = [(ci, h) for ci in range(n_chunks) for h in heads]
    sls = [slice(ci * c, (ci + 1) * c) for ci in range(n_chunks)]
    hss = [slice(h * GDN_DK, (h + 1) * GDN_DK) for h in heads]
    gcs = [_mask_mm(cum_mask, g_all[sl]) for sl in sls]
    gr_all = [_mm_mask(-jnp.exp(gpt[:, 0:1]) * _softplus(abt_ref[ci][0:GDN_HEADS] + gpt[:, 1:2]), triu_bf)
              for ci in range(n_chunks)]

    qs, ks, kbs, decays, egcs, kdecs, gtots, bcols = {}, {}, {}, {}, {}, {}, {}, {}
    for ci, h in inst:
        sl, hs = sls[ci], hss[h]
        qc, kc = q[sl, hs], k[sl, hs]
        qs[ci, h] = qc * lax.rsqrt(jnp.sum(qc * qc, axis=-1, keepdims=True) + RMS_EPS) * (GDN_DK ** -0.5)
        ks[ci, h] = kc * lax.rsqrt(jnp.sum(kc * kc, axis=-1, keepdims=True) + RMS_EPS)
        gcol = gcs[ci][:c, h:h + 1]
        g_last = gcs[ci][c:, h:h + 1]
        grow = gr_all[ci][h:h + 1, :]
        bcols[ci, h] = beta_all[sl, GDN_HEADS + h:GDN_HEADS + h + 1]
        decays[ci, h] = jnp.exp(jnp.where(incl, gcol - grow, -BIG))
        kbs[ci, h] = ks[ci, h] * bcols[ci, h]
        egcs[ci, h] = jnp.exp(gcol)
        kdecs[ci, h] = ks[ci, h] * jnp.exp(g_last - gcol)
        gtots[ci, h] = jnp.exp(jnp.concatenate([g_last] * (GDN_DK // c), axis=0))

    kk = {i: _mm_nt(kbs[i], ks[i]) for i in inst}
    qk = {i: _mm_nt(qs[i], ks[i]) * decays[i] for i in inst}
    p = {i: -jnp.where(strict, kk[i] * decays[i], 0.0) for i in inst}
    tm1 = dict(p)
    for _ in range(5):
        p = {i: _mm(p[i], p[i]) for i in inst}
        tp = {i: _mm(tm1[i], p[i]) for i in inst}
        tm1 = {i: tm1[i] + tp[i] + p[i] for i in inst}
    rhs = {i: jnp.concatenate([v[sls[i[0]], hss[i[1]]] * bcols[i], kbs[i] * egcs[i]], axis=1) for i in inst}
    tr = {i: _mm(tm1[i], rhs[i]) for i in inst}
    sol = {i: rhs[i] + tr[i] for i in inst}

    state = [st_scr[h] for h in heads]
    for ci in range(n_chunks):
        ws = [_mm(sol[ci, h][:, GDN_DV:], state[h]) for h in heads]
        qst = [_mm(qs[ci, h] * egcs[ci, h], state[h]) for h in heads]
        v_new = [sol[ci, h][:, :GDN_DV] - ws[h] for h in heads]
        qv = [_mm(qk[ci, h], v_new[h]) for h in heads]
        kv = [_mm_tn(kdecs[ci, h], v_new[h]) for h in heads]
        state = [state[h] * gtots[ci, h] + kv[h] for h in heads]
        for h in heads:
            o = qst[h] + qv[h]
            o = o * lax.rsqrt(jnp.mean(o * o, axis=-1, keepdims=True) + RMS_EPS) * nw
            o = o * jax.nn.silu(zg_ref[sls[ci], hss[h]].astype(F32))
            o_ref[sls[ci], hss[h]] = o.astype(o_ref.dtype)
    for h in heads:
        st_scr[h] = state[h]


def _gdn(z, abt, bsz, seq, conv_w, gp, gpt, norm_w):
    m = z.shape[0]
    tt = 128
    nt = seq // tt
    width = GDN_HEADS * GDN_DK
    zblk = lambda c0: pl.BlockSpec((tt, width), lambda b, t: (b * nt + t, c0 // width))
    const2 = lambda b, t: (0, 0)
    return pl.pallas_call(
        _gdn_body,
        grid=(bsz, nt),
        in_specs=[
            zblk(COL_QKV), zblk(COL_QKV + width), zblk(COL_QKV + 2 * width), zblk(COL_ZG),
            pl.BlockSpec((tt, 128), lambda b, t: (b * nt + t, COL_AB // 128)),
            pl.BlockSpec((tt // GDN_CHUNK, 2 * GDN_HEADS, GDN_CHUNK), lambda b, t: (b * nt + t, 0, 0)),
            pl.BlockSpec((4, 3 * width), const2),
            pl.BlockSpec((8, 128), const2),
            pl.BlockSpec((8, 128), const2),
            pl.BlockSpec((1, 128), const2),
        ],
        out_specs=pl.BlockSpec((tt, width), lambda b, t: (b * nt + t, 0)),
        out_shape=jax.ShapeDtypeStruct((m, width), BF16),
        scratch_shapes=[pltpu.VMEM((8, width), F32)] * 3 + [pltpu.VMEM((GDN_HEADS, GDN_DK, GDN_DV), F32)],
        compiler_params=_cparams("parallel", "arbitrary"),
        name="gated_deltanet",
    )(z, z, z, z, z, abt, conv_w, gp, gpt, norm_w)


def _rwkv_body(r_ref, k_ref, v_ref, lo_ref, mu_ref, vp_ref, wup_ref, aup_ref, gup_ref, o_ref,
               pr_scr, pk_scr, pv_scr, plo_scr, st_scr):
    tt = r_ref.shape[0]
    c = RWKV_CHUNK
    gw = RWKV_GROUP
    hd = RWKV_HD
    width = RWKV_WIDTH

    @pl.when(pl.program_id(1) == 0)
    def _():
        for s in (pr_scr, pk_scr, pv_scr, plo_scr, st_scr):
            s[...] = jnp.zeros_like(s)

    mu = mu_ref[...]

    def tshift(x_ref, p_scr, m):
        x = x_ref[...].astype(F32)
        xs = x + m * (_shift_prev(x, p_scr[...], 1) - x)
        p_scr[...] = x[tt - 8:]
        return xs

    r = tshift(r_ref, pr_scr, mu[:, 0:width])
    k = tshift(k_ref, pk_scr, mu[:, width:2 * width])
    v = tshift(v_ref, pv_scr, mu[:, 2 * width:3 * width])
    lo = tshift(lo_ref, plo_scr, mu[:, 3 * width:])
    vp = vp_ref[...]
    w0, a0, k_k, k_a, r_k, lnx_w, lnx_b = (vp[i:i + 1] for i in range(7))

    lane = lax.broadcasted_iota(jnp.int32, lo.shape, 1)
    lo_act = jnp.where(lane < 64, jnp.tanh(lo), jnp.where(lane < 128, lo, jax.nn.sigmoid(lo)))
    w_pre = _mm(lo_act, wup_ref[...])
    a_pre = _mm(lo_act, aup_ref[...])
    gate = _mm(lo_act, gup_ref[...])
    w_log = -_softplus(-(w0 + w_pre)) - 0.5
    lw = -jnp.exp(w_log)
    a = jax.nn.sigmoid(a0 + a_pre)

    bi = lax.broadcasted_iota(jnp.int32, (gw, gw), 0) // hd
    bj = lax.broadcasted_iota(jnp.int32, (gw, gw), 1) // hd
    bdmask = bi == bj
    ones_bd = bdmask.astype(BF16)

    def head_sum(x):
        return _mm_mask(x, ones_bd, terms=2)

    def bd(x):
        return jnp.where(bdmask, jnp.concatenate([x] * (gw // c), axis=0), 0.0)

    ti = lax.broadcasted_iota(jnp.int32, (c, gw), 0)
    sj = lax.broadcasted_iota(jnp.int32, (c, gw), 1) % c
    strict = sj < ti
    incl = sj <= ti
    tril_bf = _tri(c, False).astype(BF16)

    kk_all = k * k_k
    k2_all = k * (1.0 + (a - 1.0) * k_a)

    n_chunks = tt // c
    groups = range(width // gw)
    inst = [(ci, g) for ci in range(n_chunks) for g in groups]
    sls = [slice(ci * c, (ci + 1) * c) for ci in range(n_chunks)]
    gss = [slice(g * gw, (g + 1) * gw) for g in groups]
    cl_all = [_mask_mm(tril_bf, lw[sl]) for sl in sls]

    def at(x, i):
        return x[sls[i[0]], gss[i[1]]]

    kk_raw = {i: at(kk_all, i) for i in inst}
    kk_ss = {i: head_sum(kk_raw[i] * kk_raw[i]) for i in inst}
    a_h, b_h, k_h, br, w_end = {}, {}, {}, {}, {}
    for i in inst:
        cl = cl_all[i[0]][:, gss[i[1]]]
        kk = kk_raw[i] * lax.rsqrt(kk_ss[i] + RMS_EPS)
        wcum = jnp.exp(cl)
        inv_w = jnp.exp(-cl)
        a_h[i] = kk * at(a, i) * inv_w
        b_h[i] = kk * jnp.exp(cl - at(lw, i))
        k_h[i] = at(k2_all, i) * inv_w
        br[i] = jnp.concatenate([b_h[i], at(r, i) * wcum], axis=0)
        w_end[i] = wcum[c - 1:c]
    v_bd = {i: bd(at(v, i)) for i in inst}
    xa = {i: _mm_nt(br[i], bd(a_h[i])) for i in inst}
    xk = {i: _mm_nt(br[i], bd(k_h[i])) for i in inst}
    l_k = {i: jnp.where(strict, xk[i][:c], 0.0) for i in inst}
    ra = {i: jnp.where(incl, xa[i][c:], 0.0) for i in inst}
    rk = {i: jnp.where(incl, xk[i][c:], 0.0) for i in inst}
    p = {i: -jnp.where(strict, xa[i][:c], 0.0) for i in inst}
    tm1 = dict(p)
    for _ in range(5):
        p = {i: _mm(p[i], bd(p[i])) for i in inst}
        tp = {i: _mm(tm1[i], bd(p[i])) for i in inst}
        tm1 = {i: tm1[i] + tp[i] + p[i] for i in inst}
    lkv = {i: _mm(l_k[i], v_bd[i]) for i in inst}
    rkv = {i: _mm(rk[i], v_bd[i]) for i in inst}
    so = {i: head_sum(at(r, i) * at(k2_all, i) * r_k[:, gss[i[1]]]) for i in inst}

    state_t = [st_scr[g] for g in groups]
    for ci in range(n_chunks):
        ids = [(ci, g) for g in groups]
        brh = [_mm_nt(br[i], state_t[i[1]]) for i in ids]
        rhs = [brh[g][:c] + lkv[ci, g] for g in groups]
        tu = [_mm(tm1[ci, g], bd(rhs[g])) for g in groups]
        u = [rhs[g] + tu[g] for g in groups]
        rau = [_mm(ra[ci, g], bd(u[g])) for g in groups]
        upd = [_mm_tn(jnp.concatenate([u[g], at(v, (ci, g))], axis=0),
                      jnp.concatenate([-(a_h[ci, g] * w_end[ci, g]), k_h[ci, g] * w_end[ci, g]], axis=0))
               for g in groups]
        state_t = [w_end[ci, g] * state_t[g] + jnp.where(bdmask, upd[g], 0.0) for g in groups]
        o = [brh[g][c:] - rau[g] + rkv[ci, g] for g in groups]
        osum = [head_sum(o[g]) for g in groups]
        cen = [o[g] - osum[g] * (1.0 / hd) for g in groups]
        var = [head_sum(cen[g] * cen[g]) * (1.0 / hd) for g in groups]
        for g in groups:
            gs = gss[g]
            on = cen[g] * lax.rsqrt(var[g] + RWKV_GN_EPS) * lnx_w[:, gs] + lnx_b[:, gs]
            bonus = so[ci, g] * at(v, (ci, g))
            o_ref[sls[ci], gs] = ((on + bonus) * gate[sls[ci], gs]).astype(o_ref.dtype)
    for g in groups:
        st_scr[g] = state_t[g]


def _rwkv(z, bsz, seq, mu, vecs, wup_pad, aup_pad, gup_pad):
    m = z.shape[0]
    tt = 256
    nt = seq // tt
    gw = RWKV_GROUP
    width = RWKV_WIDTH
    zblk = lambda c0: pl.BlockSpec((tt, width), lambda b, t: (b * nt + t, c0 // width))
    lora_col = (COL_RWKV + 3 * width) // RWKV_LORA
    const2 = lambda b, t: (0, 0)
    return pl.pallas_call(
        _rwkv_body,
        grid=(bsz, nt),
        in_specs=[
            zblk(COL_RWKV), zblk(COL_RWKV + width), zblk(COL_RWKV + 2 * width),
            pl.BlockSpec((tt, RWKV_LORA), lambda b, t: (b * nt + t, lora_col)),
            pl.BlockSpec(mu.shape, const2),
            pl.BlockSpec((8, width), const2),
            pl.BlockSpec((RWKV_LORA, width), const2),
            pl.BlockSpec((RWKV_LORA, width), const2),
            pl.BlockSpec((RWKV_LORA, width), const2),
        ],
        out_specs=pl.BlockSpec((tt, width), lambda b, t: (b * nt + t, 0)),
        out_shape=jax.ShapeDtypeStruct((m, width), BF16),
        scratch_shapes=[pltpu.VMEM((8, width), F32)] * 3 + [pltpu.VMEM((8, RWKV_LORA), F32),
                                                            pltpu.VMEM((width // gw, gw, gw), F32)],
        compiler_params=_cparams("parallel", "arbitrary"),
        name="rwkv7",
    )(z, z, z, z, mu, vecs, wup_pad, aup_pad, gup_pad)


def _merge_body(h_ref, ya_ref, yb_ref, yc_ref, za_ref, zb_ref, zc_ref, mb_ref, pa_ref, pb_ref,
                pc_ref, wo_ref, nw_ref, hout_ref, xnt_ref):
    mb = mb_ref[...]
    ga = jax.nn.sigmoid(za_ref[...].astype(F32) + mb[:, 0:D_MODEL])
    gb = jax.nn.sigmoid(zb_ref[...].astype(F32) + mb[:, D_MODEL:2 * D_MODEL])
    gc = jax.nn.sigmoid(zc_ref[...].astype(F32) + mb[:, 2 * D_MODEL:])
    merged = (ga * jnp.dot(ya_ref[...], pa_ref[...], preferred_element_type=F32)
              + gb * jnp.dot(yb_ref[...], pb_ref[...], preferred_element_type=F32)
              + gc * jnp.dot(yc_ref[...], pc_ref[...], preferred_element_type=F32))
    h = h_ref[...] + _mm(merged, wo_ref[...])
    hout_ref[...] = h
    xnt_ref[...] = jnp.transpose(_rms(h, nw_ref[...])).astype(BF16)


def _merge(h, z, ya, yb, yc, merge_b, pa, pb, pc, wo, norm_w):
    m = h.shape[0]
    tm = 256
    row = lambda c: pl.BlockSpec((tm, D_MODEL), lambda i: (i, c))
    wspec = pl.BlockSpec((D_MODEL, D_MODEL), lambda i: (0, 0))
    mc = COL_MERGE // D_MODEL
    return pl.pallas_call(
        _merge_body,
        grid=(m // tm,),
        in_specs=[row(0), row(0), row(0), row(0), row(mc), row(mc + 1), row(mc + 2),
                  pl.BlockSpec((1, 3 * D_MODEL), lambda i: (0, 0)),
                  wspec, wspec, wspec, wspec,
                  pl.BlockSpec((1, D_MODEL), lambda i: (0, 0))],
        out_specs=[row(0), pl.BlockSpec((D_MODEL, tm), lambda i: (0, i))],
        out_shape=[jax.ShapeDtypeStruct((m, D_MODEL), F32), jax.ShapeDtypeStruct((D_MODEL, m), BF16)],
        compiler_params=_cparams("parallel"),
        name="merge_outproj",
    )(h, ya, yb, yc, z, z, z, merge_b, pa, pb, pc, wo, norm_w)


_CAND_SLABS = (
    (16, ((0, 0, 16),)),
    (16, ((1, 0, 8), (2, 8, 5), (4, 13, 3))),
    (16, ((3, 0, 4), (5, 4, 2), (6, 6, 2), (7, 8, 2), (8, 10, 1), (9, 11, 1), (10, 12, 1), (11, 13, 1),
          (12, 14, 1), (13, 15, 1))),
    (8, ((14, 0, 1), (15, 1, 1))),
)


def _odd_even_merge_sort_pairs(n):
    pairs = []
    p = 1
    while p < n:
        k = p
        while k >= 1:
            for j in range(k % p, n - k, 2 * k):
                for i in range(min(k, n - j - k)):
                    if (i + j) // (2 * p) == (i + j + k) // (2 * p):
                        pairs.append((i + j, i + j + k))
            k //= 2
        p *= 2
    return tuple(pairs)


_SORT_PAIRS = _odd_even_merge_sort_pairs(PEER_TOPK)


def _candidate_sums(a1, a2):
    slabs = []
    for nrows, pieces in _CAND_SLABS:
        base = a2[:nrows]
        rows = lax.broadcasted_iota(jnp.int32, base.shape, 0)
        out = None
        end = 0
        for p, off, cnt in pieces:
            val = a1[p:p + 1] + (base if off == 0 else pltpu.roll(base, off, 0))
            out = val if out is None else jnp.where(rows >= off, val, out)
            end = off + cnt
        if end < nrows:
            out = jnp.where(rows >= end, -BIG, out)
        slabs.append(out)
    return jnp.concatenate(slabs, axis=0)


def _peer_topk_body(xnt_ref, wqt_ref, keys_ref, n_ref, e1_ref, r2_ref, e2_ref, q_scr):
    tt = xnt_ref.shape[1]
    nk = PEER_NKEYS
    topk = PEER_TOPK
    q_scr[...] = jnp.dot(wqt_ref[...], xnt_ref[...], preferred_element_type=F32)
    lanes = 128
    sub = 8
    assert nk == topk * sub
    row8 = lax.broadcasted_iota(jnp.int32, (sub, lanes), 0)

    def top_sorted(s):
        v = [s[k * sub:(k + 1) * sub] for k in range(topk)]
        for i, j in _SORT_PAIRS:
            v[i], v[j] = jnp.maximum(v[i], v[j]), jnp.minimum(v[i], v[j])
        for shift in (4, 2, 1):
            other = [pltpu.roll(x, shift, 0) for x in v]
            v = [jnp.maximum(v[k], other[topk - 1 - k]) for k in range(topk)]
            d = topk // 2
            while d >= 1:
                for i in range(topk):
                    if i & d == 0:
                        v[i], v[i + d] = jnp.maximum(v[i], v[i + d]), jnp.minimum(v[i], v[i + d])
                d //= 2
        return v

    def compact(v):
        tiles = []
        for t0 in range(0, topk, sub):
            out = v[t0]
            for k in range(1, sub):
                out = jnp.where(row8 == k, v[t0 + k], out)
            tiles.append(out)
        return jnp.concatenate(tiles, axis=0)

    def head(h, carry):
        o1 = pl.multiple_of(h * (2 * nk), 2 * nk)
        s1 = _mm(keys_ref[2 * h], q_scr[pl.ds(o1, nk), :])
        s2 = _mm(keys_ref[2 * h + 1], q_scr[pl.ds(o1 + nk, nk), :])
        chunks = [slice(l0, l0 + lanes) for l0 in range(0, tt, lanes)]
        a1 = [top_sorted(s1[:, ls]) for ls in chunks]
        a2 = [top_sorted(s2[:, ls]) for ls in chunks]
        a1c = jnp.concatenate([compact(v) for v in a1], axis=1)
        a2c = jnp.concatenate([compact(v) for v in a2], axis=1)
        cand = _candidate_sums(a1c, a2c)
        cmax = a1c[0:1] + a2c[0:1]

        def cbody(rnd, carry):
            cnd, zsum, c_in, c_out = carry
            mx = jnp.max(cnd, axis=0, keepdims=True)
            zsum = zsum + jnp.where(rnd < topk, jnp.exp(mx - cmax), 0.0)
            c_in = jnp.where(rnd == topk - 1, mx, c_in)
            c_out = jnp.where(rnd == topk, mx, c_out)
            return jnp.where(cnd == mx, -BIG, cnd), zsum, c_in, c_out

        zero = jnp.zeros((1, tt), F32)
        _, zsum, c_in, c_out = lax.fori_loop(0, topk + 1, cbody, (cand, zero, zero, zero))
        tau = 0.5 * (c_in + c_out)
        inv_z = 1.0 / zsum
        pack = 16

        def rep(x):
            return jnp.concatenate([x] * (pack // sub), axis=0)

        for ci, ls in enumerate(chunks):
            tau8 = jnp.broadcast_to(tau[:, ls], (sub, lanes))
            need = [rep(tau8 - a2[ci][qq]) for qq in range(topk)]
            val2 = [rep(a2[ci][qq]) for qq in range(topk)]
            last1 = rep(a1[ci][topk - 1])
            top1 = rep(a1[ci][0])
            top2 = val2[0]
            for k in range(nk // pack):
                rows = slice(k * pack, (k + 1) * pack)
                x1 = s1[rows, ls]
                x2 = s2[rows, ls]
                n_sel = jnp.zeros((pack, lanes), F32)
                r2 = jnp.zeros((pack, lanes), F32)
                for qq in range(topk):
                    n_sel = jnp.where(x1 >= need[qq], n_sel + 1.0, n_sel)
                    r2 = jnp.where(x2 < val2[qq], r2 + 1.0, r2)
                n_ref[h, rows, ls] = jnp.where(x1 >= last1, n_sel, 0.0)
                e1_ref[h, rows, ls] = jnp.exp(x1 - top1)
                r2_ref[h, rows, ls] = r2.astype(BF16)
                e2_ref[h, rows, ls] = (jnp.exp(x2 - top2) * inv_z[:, ls]).astype(BF16)
        return carry

    lax.fori_loop(0, PEER_HEADS, head, 0)


def _peer_topk(xnt, wqt, keys):
    m = xnt.shape[1]
    tt = 512 if m % 512 == 0 else 256
    shape = (PEER_HEADS, PEER_NKEYS, m)
    ospec = pl.BlockSpec((PEER_HEADS, PEER_NKEYS, tt), lambda i: (0, 0, i))
    return pl.pallas_call(
        _peer_topk_body,
        grid=(m // tt,),
        in_specs=[pl.BlockSpec((D_MODEL, tt), lambda i: (0, i)),
                  pl.BlockSpec(wqt.shape, lambda i: (0, 0)),
                  pl.BlockSpec(keys.shape, lambda i: (0, 0, 0))],
        out_specs=[ospec] * 4,
        out_shape=[jax.ShapeDtypeStruct(shape, F32), jax.ShapeDtypeStruct(shape, F32),
                   jax.ShapeDtypeStruct(shape, BF16), jax.ShapeDtypeStruct(shape, BF16)],
        scratch_shapes=[pltpu.VMEM((wqt.shape[0], tt), F32)],
        compiler_params=_cparams("parallel"),
        name="peer_topk",
    )(xnt, wqt, keys)


PEER_TOKEN_CHUNK = 256


def _peer_dense_body(u_ref, xnt_ref, vt_ref, n_ref, e1_ref, r2_ref, e2_ref, y_ref, acc_scr):
    nk = PEER_NKEYS
    tt = xnt_ref.shape[1]
    tc = PEER_TOKEN_CHUNK
    n_blk = u_ref.shape[0] // nk
    pack = 16

    @pl.when(pl.program_id(1) == 0)
    def _():
        acc_scr[...] = jnp.zeros_like(acc_scr)

    def rows_bf16(ref, h, ii, ls):
        row = jnp.broadcast_to(ref[h, ii:ii + 1, ls], (pack, tc)).astype(BF16)
        return jnp.concatenate([row] * (nk // pack), axis=0)

    def select_weights(ii, c0):
        ls = slice(c0, c0 + tc)
        w = None
        for h in range(PEER_HEADS):
            sel = jnp.where(r2_ref[h, :, ls] < rows_bf16(n_ref, h, ii, ls), e2_ref[h, :, ls],
                            jnp.zeros((), BF16))
            term = sel * rows_bf16(e1_ref, h, ii, ls)
            w = term if w is None else w + term
        return w

    def gelu_tanh(x):
        c = 0.7978845608028654
        inner = x * ((x * x) * (c * 0.044715) + c)
        hx = 0.5 * x
        return hx * jnp.tanh(inner) + hx

    starts = list(range(0, tt, tc))
    pre = [jnp.dot(u_ref[...], xnt_ref[:, c0:c0 + tc], preferred_element_type=F32) for c0 in starts]
    for idx, c0 in enumerate(starts):
        act = gelu_tanh(pre[idx]).astype(BF16)
        a = jnp.concatenate([act[ii * nk:(ii + 1) * nk] * select_weights(ii, c0) for ii in range(n_blk)], axis=0)
        acc_scr[:, c0:c0 + tc] += jnp.dot(vt_ref[...], a, preferred_element_type=F32)

    @pl.when(pl.program_id(1) == pl.num_programs(1) - 1)
    def _():
        y_ref[...] = jnp.transpose(acc_scr[...])


def _peer_dense(xnt, u_bf, vt_bf, n_sel, e1, r2, e2, tt):
    m = xnt.shape[1]
    te = 1024
    ib = te // PEER_NKEYS
    sel_i = pl.BlockSpec((PEER_HEADS, ib, tt), lambda i, e: (0, e, i))
    sel_all = pl.BlockSpec((PEER_HEADS, PEER_NKEYS, tt), lambda i, e: (0, 0, i))
    return pl.pallas_call(
        _peer_dense_body,
        grid=(m // tt, PEER_EXPERTS // te),
        in_specs=[pl.BlockSpec((te, D_MODEL), lambda i, e: (e, 0)),
                  pl.BlockSpec((D_MODEL, tt), lambda i, e: (0, i)),
                  pl.BlockSpec((D_MODEL, te), lambda i, e: (0, e)),
                  sel_i, sel_i, sel_all, sel_all],
        out_specs=pl.BlockSpec((tt, D_MODEL), lambda i, e: (i, 0)),
        out_shape=jax.ShapeDtypeStruct((m, D_MODEL), F32),
        scratch_shapes=[pltpu.VMEM((D_MODEL, tt), F32)],
        compiler_params=_cparams("parallel", "arbitrary"),
        name="peer_dense",
    )(u_bf, xnt, vt_bf, n_sel, e1, r2, e2)


def _final_body(h_ref, y_ref, nw_ref, o_ref):
    o_ref[...] = _rms(h_ref[...] + y_ref[...], nw_ref[...])


def _final(h, y, norm_w):
    m = h.shape[0]
    tm = 512
    row = pl.BlockSpec((tm, D_MODEL), lambda i: (i, 0))
    return pl.pallas_call(
        _final_body,
        grid=(m // tm,),
        in_specs=[row, row, pl.BlockSpec((1, D_MODEL), lambda i: (0, 0))],
        out_specs=row,
        out_shape=jax.ShapeDtypeStruct((m, D_MODEL), F32),
        compiler_params=_cparams("parallel"),
        name="final_norm",
    )(h, y, norm_w.reshape(1, D_MODEL))


def _pad_rows(rows, width, n_rows=8):
    out = jnp.zeros((n_rows, width), F32)
    for i, r in enumerate(rows):
        flat = r.reshape(-1).astype(F32)
        out = out.at[i, :flat.shape[0]].set(flat)
    return out


def _chunk_rows(ab):
    m, n = ab.shape
    return jnp.transpose(ab.reshape(m // GDN_CHUNK, GDN_CHUNK, n), (0, 2, 1)).astype(F32)


def _block_diag_groups(w):
    per = LRU_GROUP // LRU_BLOCK_DIM
    n_grp = w.shape[0] // per
    out = jnp.zeros((n_grp, LRU_GROUP, LRU_GROUP), F32)
    for g in range(n_grp):
        for j in range(per):
            o = j * LRU_BLOCK_DIM
            out = out.at[g, o:o + LRU_BLOCK_DIM, o:o + LRU_BLOCK_DIM].set(w[g * per + j])
    return out.astype(BF16)


def _prep_w_in(w):
    pad = jnp.zeros((D_MODEL, IN_WIDTH_PAD - COL_AB - 2 * GDN_HEADS), w.dtype)
    return jnp.concatenate([w[:, :6144], w[:, 9488:12560], w[:, 6160:9488], w[:, 6144:6160], pad],
                           axis=1).astype(BF16)


def _layer(l, h, y, bsz, seq, p):
    z_h, z = _inproj(h, y, p["norm_mix_w"][l], _prep_w_in(p["w_in"][l]))
    h = z_h

    lru_vecs = _pad_rows([p["lru_conv_b"][l], p["lru_b_a"][l], p["lru_b_x"][l], p["lru_lambda"][l]], LRU_WIDTH)
    ya = _lru(z, bsz, seq, p["lru_conv_w"][l], lru_vecs,
              _block_diag_groups(p["lru_w_a"][l]), _block_diag_groups(p["lru_w_x"][l]))

    abt = _chunk_rows(z[:, COL_AB:COL_AB + 2 * GDN_HEADS])
    gp = _pad_rows([p["gdn_a_log"][l], p["gdn_dt_bias"][l]], 128)
    gpt = jnp.zeros((8, 128), F32).at[:, 0].set(p["gdn_a_log"][l]).at[:, 1].set(p["gdn_dt_bias"][l])
    yb = _gdn(z, abt, bsz, seq, p["gdn_conv_w"][l], gp, gpt, p["gdn_norm_w"][l].reshape(1, GDN_DV))

    rw_vecs = _pad_rows([p["rwkv_w0"][l], p["rwkv_a0"][l], p["rwkv_k_k"][l], p["rwkv_k_a"][l],
                         p["rwkv_r_k"][l], p["rwkv_lnx_w"][l], p["rwkv_lnx_b"][l]], RWKV_WIDTH)
    zl = jnp.zeros((RWKV_LORA, RWKV_WIDTH), F32)
    wup = zl.at[0:64].set(p["rwkv_w_up"][l]).astype(BF16)
    aup = zl.at[64:128].set(p["rwkv_a_up"][l]).astype(BF16)
    gup = zl.at[128:256].set(p["rwkv_g_up"][l]).astype(BF16)
    yc = _rwkv(z, bsz, seq, p["rwkv_mu"][l].reshape(1, -1), rw_vecs, wup, aup, gup)

    h, xnt = _merge(h, z, ya, yb, yc, p["merge_b"][l].reshape(1, -1), p["p_lru"][l].astype(BF16),
                    p["p_gdn"][l].astype(BF16), p["p_rwkv"][l].astype(BF16), p["w_out"][l].astype(BF16),
                    p["norm_ffn_w"][l].reshape(1, D_MODEL))

    wqt = jnp.transpose(p["peer_wq"][l]).astype(BF16)
    keys = p["peer_keys"][l].reshape(2 * PEER_HEADS, PEER_NKEYS, PEER_HALF).astype(BF16)
    n_sel, e1, r2, e2 = _peer_topk(xnt, wqt, keys)
    tt = 512 if xnt.shape[1] % 512 == 0 else 256
    y = _peer_dense(xnt, p["peer_u"][l].astype(BF16), jnp.transpose(p["peer_v"][l]).astype(BF16),
                    n_sel, e1, r2, e2, tt)
    return h, y


def kernel(x, norm_mix_w, norm_ffn_w, final_norm_w, w_in, lru_conv_w, lru_conv_b, lru_w_a, lru_b_a,
           lru_w_x, lru_b_x, lru_lambda, gdn_conv_w, gdn_a_log, gdn_dt_bias, gdn_norm_w, rwkv_mu, rwkv_w0,
           rwkv_w_up, rwkv_a0, rwkv_a_up, rwkv_g_up, rwkv_k_k, rwkv_k_a, rwkv_r_k, rwkv_lnx_w, rwkv_lnx_b,
           merge_b, p_lru, p_gdn, p_rwkv, w_out, peer_wq, peer_keys, peer_u, peer_v):
    p = dict(locals())
    bsz, seq, dim = x.shape
    h = x.reshape(bsz * seq, dim)
    y = None
    for l in range(DEPTH):
        h, y = _layer(l, h, y, bsz, seq, p)
    return _final(h, y, final_norm_w).reshape(bsz, seq, dim)
```

```python
import functools

import jax
import jax.numpy as jnp
from jax import lax
from jax.experimental import pallas as pl
from jax.experimental.pallas import tpu as pltpu

F32 = jnp.float32
BF16 = jnp.bfloat16

D_MODEL = 1024
DEPTH = 2
RMS_EPS = 1e-6

LRU_WIDTH = 1024
LRU_BLOCK_DIM = 64
LRU_C = 8.0
LRU_GROUP = 256

GDN_HEADS = 8
GDN_DK = 128
GDN_DV = 128
GDN_CHUNK = 64

RWKV_HEADS = 16
RWKV_HD = 64
RWKV_WIDTH = 1024
RWKV_GN_EPS = 64e-5
RWKV_CHUNK = 64
RWKV_GROUP = 256
RWKV_LORA = 256

PEER_HEADS = 8
PEER_NKEYS = 128
PEER_EXPERTS = PEER_NKEYS * PEER_NKEYS
PEER_HALF = 128
PEER_TOPK = 16

COL_U = 0
COL_GATE = 1024
COL_QKV = 2048
COL_ZG = 5120
COL_MERGE = 6144
COL_RWKV = 9216
COL_AB = 12544
IN_WIDTH_PAD = 12672

VMEM_LIMIT = 48 * 1024 * 1024
BIG = 3.0e38


def _cparams(*sem):
    return pltpu.CompilerParams(dimension_semantics=sem, vmem_limit_bytes=VMEM_LIMIT)


def _mm(a, b):
    return jnp.dot(a.astype(BF16), b.astype(BF16), preferred_element_type=F32)


def _mm_nt(a, b):
    return lax.dot_general(a.astype(BF16), b.astype(BF16), (((1,), (1,)), ((), ())),
                           preferred_element_type=F32)


def _mm_tn(a, b):
    return lax.dot_general(a.astype(BF16), b.astype(BF16), (((0,), (0,)), ((), ())),
                           preferred_element_type=F32)


def _softplus(x):
    return jnp.maximum(x, 0.0) + jnp.log1p(jnp.exp(-jnp.abs(x)))


def _rms(x, w):
    return x * lax.rsqrt(jnp.mean(x * x, axis=-1, keepdims=True) + RMS_EPS) * w


def _shift_prev(x, prev8, s):
    r = pltpu.roll(x, s, 0)
    pr = pltpu.roll(prev8, s, 0)
    rows8 = lax.broadcasted_iota(jnp.int32, prev8.shape, 0)
    head = jnp.where(rows8 < s, pr, r[:8])
    return jnp.concatenate([head, r[8:]], axis=0)


def _shift_fill(x, d, fill):
    n, c = x.shape
    if d % 8 == 0:
        return jnp.concatenate([jnp.full((d, c), fill, x.dtype), x[:n - d]], axis=0)
    r = pltpu.roll(x, d, 0)
    rows8 = lax.broadcasted_iota(jnp.int32, (8, c), 0)
    head = jnp.where(rows8 < d, fill, r[:8])
    return jnp.concatenate([head, r[8:]], axis=0)


def _tri(n, strict):
    i = lax.broadcasted_iota(jnp.int32, (n, n), 0)
    j = lax.broadcasted_iota(jnp.int32, (n, n), 1)
    return (j < i) if strict else (j <= i)


def _split_bf16(x, terms):
    parts = []
    rem = x
    for i in range(terms):
        p = rem.astype(BF16)
        parts.append(p)
        if i + 1 < terms:
            rem = rem - p.astype(F32)
    return parts


def _mask_mm(mask_bf, x, terms=3):
    return sum(jnp.dot(mask_bf, p, preferred_element_type=F32) for p in _split_bf16(x, terms))


def _mm_mask(x, mask_bf, terms=3):
    return sum(jnp.dot(p, mask_bf, preferred_element_type=F32) for p in _split_bf16(x, terms))


def _inproj_body(add_y, *refs):
    if add_y:
        h_ref, y_ref, nw_ref, w_ref, hout_ref, z_ref, xn_scr = refs
    else:
        h_ref, nw_ref, w_ref, z_ref, xn_scr = refs

    @pl.when(pl.program_id(1) == 0)
    def _():
        h = h_ref[...]
        if add_y:
            h = h + y_ref[...]
            hout_ref[...] = h
        xn_scr[...] = _rms(h, nw_ref[...]).astype(BF16)

    z_ref[...] = jnp.dot(xn_scr[...], w_ref[...], preferred_element_type=F32).astype(z_ref.dtype)


def _inproj(h, y, norm_w, w_pad):
    m = h.shape[0]
    tm, tn = (1024 if m % 1024 == 0 else 512), 1152
    add_y = y is not None
    row = pl.BlockSpec((tm, D_MODEL), lambda i, j: (i, 0))
    in_specs = [row] + ([row] if add_y else []) + [
        pl.BlockSpec((1, D_MODEL), lambda i, j: (0, 0)),
        pl.BlockSpec((D_MODEL, tn), lambda i, j: (0, j)),
    ]
    z_spec = pl.BlockSpec((tm, tn), lambda i, j: (i, j))
    z_shape = jax.ShapeDtypeStruct((m, IN_WIDTH_PAD), BF16)
    args = (h,) + ((y,) if add_y else ()) + (norm_w.reshape(1, D_MODEL), w_pad)
    out = pl.pallas_call(
        functools.partial(_inproj_body, add_y),
        grid=(m // tm, IN_WIDTH_PAD // tn),
        in_specs=in_specs,
        out_specs=[row, z_spec] if add_y else z_spec,
        out_shape=[jax.ShapeDtypeStruct((m, D_MODEL), F32), z_shape] if add_y else z_shape,
        scratch_shapes=[pltpu.VMEM((tm, D_MODEL), BF16)],
        compiler_params=_cparams("parallel", "arbitrary"),
        name="norm_inproj",
    )(*args)
    return (out[0], out[1]) if add_y else (h, out)


def _lru_body(zu_ref, zg_ref, cw_ref, vp_ref, wa_ref, wx_ref, o_ref, prev_scr, hc_scr):
    tt = zu_ref.shape[0]

    @pl.when(pl.program_id(1) == 0)
    def _():
        prev_scr[...] = jnp.zeros_like(prev_scr)
        hc_scr[...] = jnp.zeros_like(hc_scr)

    u = zu_ref[...].astype(F32)
    prev = prev_scr[...]
    cw = cw_ref[...]
    vp = vp_ref[...]
    xc = (cw[3:4] * u + cw[2:3] * _shift_prev(u, prev, 1) + cw[1:2] * _shift_prev(u, prev, 2)
          + cw[0:1] * _shift_prev(u, prev, 3) + vp[0:1])
    prev_scr[...] = u[tt - 8:]

    xcb = xc.astype(BF16)
    n_grp = LRU_WIDTH // LRU_GROUP
    pre_a = jnp.concatenate(
        [jnp.dot(xcb[:, g * LRU_GROUP:(g + 1) * LRU_GROUP], wa_ref[g], preferred_element_type=F32)
         for g in range(n_grp)], axis=1)
    pre_x = jnp.concatenate(
        [jnp.dot(xcb[:, g * LRU_GROUP:(g + 1) * LRU_GROUP], wx_ref[g], preferred_element_type=F32)
         for g in range(n_grp)], axis=1)
    r = jax.nn.sigmoid(pre_a + vp[1:2])
    i = jax.nn.sigmoid(pre_x + vp[2:3])
    log_a = (-LRU_C * r) * _softplus(-vp[3:4])
    a = jnp.exp(log_a)
    b = jnp.sqrt(-jnp.tanh(log_a) * (a * a + 1.0)) * (i * xc)

    d = 1
    while d < tt:
        b = a * _shift_fill(b, d, 0.0) + b
        a = a * _shift_fill(a, d, 1.0)
        d *= 2
    hcar = hc_scr[...]
    hval = b + a * hcar[0:1]
    hc_scr[...] = jnp.broadcast_to(hval[tt - 1:tt], hcar.shape)
    o_ref[...] = (hval * jax.nn.gelu(zg_ref[...].astype(F32))).astype(o_ref.dtype)


def _lru(z, bsz, seq, conv_w, vecs, wa_bd, wx_bd):
    m = z.shape[0]
    tt = 256
    nt = seq // tt
    rowmap = lambda c: (lambda b, t: (b * nt + t, c))
    const2 = lambda b, t: (0, 0)
    return pl.pallas_call(
        _lru_body,
        grid=(bsz, nt),
        in_specs=[
            pl.BlockSpec((tt, LRU_WIDTH), rowmap(COL_U // LRU_WIDTH)),
            pl.BlockSpec((tt, LRU_WIDTH), rowmap(COL_GATE // LRU_WIDTH)),
            pl.BlockSpec((4, LRU_WIDTH), const2),
            pl.BlockSpec((8, LRU_WIDTH), const2),
            pl.BlockSpec(wa_bd.shape, lambda b, t: (0, 0, 0)),
            pl.BlockSpec(wx_bd.shape, lambda b, t: (0, 0, 0)),
        ],
        out_specs=pl.BlockSpec((tt, LRU_WIDTH), rowmap(0)),
        out_shape=jax.ShapeDtypeStruct((m, LRU_WIDTH), BF16),
        scratch_shapes=[pltpu.VMEM((8, LRU_WIDTH), F32), pltpu.VMEM((8, LRU_WIDTH), F32)],
        compiler_params=_cparams("parallel", "arbitrary"),
        name="rglru",
    )(z, z, conv_w, vecs, wa_bd, wx_bd)


def _gdn_body(q_ref, k_ref, v_ref, zg_ref, ab_ref, abt_ref, cw_ref, gp_ref, gpt_ref, nw_ref,
              o_ref, pq_scr, pk_scr, pv_scr, st_scr):
    tt = q_ref.shape[0]
    c = GDN_CHUNK
    width = GDN_HEADS * GDN_DK

    @pl.when(pl.program_id(1) == 0)
    def _():
        for s in (pq_scr, pk_scr, pv_scr, st_scr):
            s[...] = jnp.zeros_like(s)

    cw = cw_ref[...]

    def conv_silu(x_ref, p_scr, w):
        x = x_ref[...].astype(F32)
        prev = p_scr[...]
        y = (w[3:4] * x + w[2:3] * _shift_prev(x, prev, 1) + w[1:2] * _shift_prev(x, prev, 2)
             + w[0:1] * _shift_prev(x, prev, 3))
        p_scr[...] = x[tt - 8:]
        return jax.nn.silu(y)

    q = conv_silu(q_ref, pq_scr, cw[:, 0:width])
    k = conv_silu(k_ref, pk_scr, cw[:, width:2 * width])
    v = conv_silu(v_ref, pv_scr, cw[:, 2 * width:])

    ab = ab_ref[...].astype(F32)
    gp = gp_ref[...]
    gpt = gpt_ref[...]
    g_all = -jnp.exp(gp[0:1]) * _softplus(ab + gp[1:2])
    beta_all = jax.nn.sigmoid(ab)

    incl = _tri(c, False)
    strict = _tri(c, True)
    cum_mask = jnp.concatenate([incl.astype(BF16), jnp.ones((c, c), BF16)], axis=0)
    triu_bf = (lax.broadcasted_iota(jnp.int32, (c, c), 0) <= lax.broadcasted_iota(jnp.int32, (c, c), 1)).astype(BF16)
    nw = nw_ref[...]

    n_chunks = tt // c
    heads = range(GDN_HEADS)
    inst = [(ci, h) for ci in range(n_chunks) for h in heads]
    sls = [slice(ci * c, (ci + 1) * c) for ci in range(n_chunks)]
    hss = [slice(h * GDN_DK, (h + 1) * GDN_DK) for h in heads]
    gcs = [_mask_mm(cum_mask, g_all[sl]) for sl in sls]
    gr_all = [_mm_mask(-jnp.exp(gpt[:, 0:1]) * _softplus(abt_ref[ci][0:GDN_HEADS] + gpt[:, 1:2]), triu_bf)
              for ci in range(n_chunks)]

    qs, ks, kbs, decays, egcs, kdecs, gtots, bcols = {}, {}, {}, {}, {}, {}, {}, {}
    for ci, h in inst:
        sl, hs = sls[ci], hss[h]
        qc, kc = q[sl, hs], k[sl, hs]
        qs[ci, h] = qc * lax.rsqrt(jnp.sum(qc * qc, axis=-1, keepdims=True) + RMS_EPS) * (GDN_DK ** -0.5)
        ks[ci, h] = kc * lax.rsqrt(jnp.sum(kc * kc, axis=-1, keepdims=True) + RMS_EPS)
        gcol = gcs[ci][:c, h:h + 1]
        g_last = gcs[ci][c:, h:h + 1]
        grow = gr_all[ci][h:h + 1, :]
        bcols[ci, h] = beta_all[sl, GDN_HEADS + h:GDN_HEADS + h + 1]
        decays[ci, h] = jnp.exp(jnp.where(incl, gcol - grow, -BIG))
        kbs[ci, h] = ks[ci, h] * bcols[ci, h]
        egcs[ci, h] = jnp.exp(gcol)
        kdecs[ci, h] = ks[ci, h] * jnp.exp(g_last - gcol)
        gtots[ci, h] = jnp.exp(jnp.concatenate([g_last] * (GDN_DK // c), axis=0))

    kk = {i: _mm_nt(kbs[i], ks[i]) for i in inst}
    qk = {i: _mm_nt(qs[i], ks[i]) * decays[i] for i in inst}
    p = {i: -jnp.where(strict, kk[i] * decays[i], 0.0) for i in inst}
    tm1 = dict(p)
    for _ in range(5):
        p = {i: _mm(p[i], p[i]) for i in inst}
        tp = {i: _mm(tm1[i], p[i]) for i in inst}
        tm1 = {i: tm1[i] + tp[i] + p[i] for i in inst}
    rhs = {i: jnp.concatenate([v[sls[i[0]], hss[i[1]]] * bcols[i], kbs[i] * egcs[i]], axis=1) for i in inst}
    tr = {i: _mm(tm1[i], rhs[i]) for i in inst}
    sol = {i: rhs[i] + tr[i] for i in inst}

    state = [st_scr[h] for h in heads]
    for ci in range(n_chunks):
        ws = [_mm(sol[ci, h][:, GDN_DV:], state[h]) for h in heads]
        qst = [_mm(qs[ci, h] * egcs[ci, h], state[h]) for h in heads]
        v_new = [sol[ci, h][:, :GDN_DV] - ws[h] for h in heads]
        qv = [_mm(qk[ci, h], v_new[h]) for h in heads]
        kv = [_mm_tn(kdecs[ci, h], v_new[h]) for h in heads]
        state = [state[h] * gtots[ci, h] + kv[h] for h in heads]
        for h in heads:
            o = qst[h] + qv[h]
            o = o * lax.rsqrt(jnp.mean(o * o, axis=-1, keepdims=True) + RMS_EPS) * nw
            o = o * jax.nn.silu(zg_ref[sls[ci], hss[h]].astype(F32))
            o_ref[sls[ci], hss[h]] = o.astype(o_ref.dtype)
    for h in heads:
        st_scr[h] = state[h]


def _gdn(z, abt, bsz, seq, conv_w, gp, gpt, norm_w):
    m = z.shape[0]
    tt = 128
    nt = seq // tt
    width = GDN_HEADS * GDN_DK
    zblk = lambda c0: pl.BlockSpec((tt, width), lambda b, t: (b * nt + t, c0 // width))
    const2 = lambda b, t: (0, 0)
    return pl.pallas_call(
        _gdn_body,
        grid=(bsz, nt),
        in_specs=[
            zblk(COL_QKV), zblk(COL_QKV + width), zblk(COL_QKV + 2 * width), zblk(COL_ZG),
            pl.BlockSpec((tt, 128), lambda b, t: (b * nt + t, COL_AB // 128)),
            pl.BlockSpec((tt // GDN_CHUNK, 2 * GDN_HEADS, GDN_CHUNK), lambda b, t: (b * nt + t, 0, 0)),
            pl.BlockSpec((4, 3 * width), const2),
            pl.BlockSpec((8, 128), const2),
            pl.BlockSpec((8, 128), const2),
            pl.BlockSpec((1, 128), const2),
        ],
        out_specs=pl.BlockSpec((tt, width), lambda b, t: (b * nt + t, 0)),
        out_shape=jax.ShapeDtypeStruct((m, width), BF16),
        scratch_shapes=[pltpu.VMEM((8, width), F32)] * 3 + [pltpu.VMEM((GDN_HEADS, GDN_DK, GDN_DV), F32)],
        compiler_params=_cparams("parallel", "arbitrary"),
        name="gated_deltanet",
    )(z, z, z, z, z, abt, conv_w, gp, gpt, norm_w)


def _rwkv_body(r_ref, k_ref, v_ref, lo_ref, mu_ref, vp_ref, wup_ref, aup_ref, gup_ref, o_ref,
               pr_scr, pk_scr, pv_scr, plo_scr, st_scr):
    tt = r_ref.shape[0]
    c = RWKV_CHUNK
    gw = RWKV_GROUP
    hd = RWKV_HD
    width = RWKV_WIDTH

    @pl.when(pl.program_id(1) == 0)
    def _():
        for s in (pr_scr, pk_scr, pv_scr, plo_scr, st_scr):
            s[...] = jnp.zeros_like(s)

    mu = mu_ref[...]

    def tshift(x_ref, p_scr, m):
        x = x_ref[...].astype(F32)
        xs = x + m * (_shift_prev(x, p_scr[...], 1) - x)
        p_scr[...] = x[tt - 8:]
        return xs

    r = tshift(r_ref, pr_scr, mu[:, 0:width])
    k = tshift(k_ref, pk_scr, mu[:, width:2 * width])
    v = tshift(v_ref, pv_scr, mu[:, 2 * width:3 * width])
    lo = tshift(lo_ref, plo_scr, mu[:, 3 * width:])
    vp = vp_ref[...]
    w0, a0, k_k, k_a, r_k, lnx_w, lnx_b = (vp[i:i + 1] for i in range(7))

    lane = lax.broadcasted_iota(jnp.int32, lo.shape, 1)
    lo_act = jnp.where(lane < 64, jnp.tanh(lo), jnp.where(lane < 128, lo, jax.nn.sigmoid(lo)))
    w_pre = _mm(lo_act, wup_ref[...])
    a_pre = _mm(lo_act, aup_ref[...])
    gate = _mm(lo_act, gup_ref[...])
    w_log = -_softplus(-(w0 + w_pre)) - 0.5
    lw = -jnp.exp(w_log)
    a = jax.nn.sigmoid(a0 + a_pre)

    bi = lax.broadcasted_iota(jnp.int32, (gw, gw), 0) // hd
    bj = lax.broadcasted_iota(jnp.int32, (gw, gw), 1) // hd
    bdmask = bi == bj
    ones_bd = bdmask.astype(BF16)

    def head_sum(x):
        return _mm_mask(x, ones_bd, terms=2)

    def bd(x):
        return jnp.where(bdmask, jnp.concatenate([x] * (gw // c), axis=0), 0.0)

    ti = lax.broadcasted_iota(jnp.int32, (c, gw), 0)
    sj = lax.broadcasted_iota(jnp.int32, (c, gw), 1) % c
    strict = sj < ti
    incl = sj <= ti
    tril_bf = _tri(c, False).astype(BF16)

    kk_all = k * k_k
    k2_all = k * (1.0 + (a - 1.0) * k_a)

    n_chunks = tt // c
    groups = range(width // gw)
    inst = [(ci, g) for ci in range(n_chunks) for g in groups]
    sls = [slice(ci * c, (ci + 1) * c) for ci in range(n_chunks)]
    gss = [slice(g * gw, (g + 1) * gw) for g in groups]
    cl_all = [_mask_mm(tril_bf, lw[sl]) for sl in sls]

    def at(x, i):
        return x[sls[i[0]], gss[i[1]]]

    kk_raw = {i: at(kk_all, i) for i in inst}
    kk_ss = {i: head_sum(kk_raw[i] * kk_raw[i]) for i in inst}
    a_h, b_h, k_h, br, w_end = {}, {}, {}, {}, {}
    for i in inst:
        cl = cl_all[i[0]][:, gss[i[1]]]
        kk = kk_raw[i] * lax.rsqrt(kk_ss[i] + RMS_EPS)
        wcum = jnp.exp(cl)
        inv_w = jnp.exp(-cl)
        a_h[i] = kk * at(a, i) * inv_w
        b_h[i] = kk * jnp.exp(cl - at(lw, i))
        k_h[i] = at(k2_all, i) * inv_w
        br[i] = jnp.concatenate([b_h[i], at(r, i) * wcum], axis=0)
        w_end[i] = wcum[c - 1:c]
    v_bd = {i: bd(at(v, i)) for i in inst}
    xa = {i: _mm_nt(br[i], bd(a_h[i])) for i in inst}
    xk = {i: _mm_nt(br[i], bd(k_h[i])) for i in inst}
    l_k = {i: jnp.where(strict, xk[i][:c], 0.0) for i in inst}
    ra = {i: jnp.where(incl, xa[i][c:], 0.0) for i in inst}
    rk = {i: jnp.where(incl, xk[i][c:], 0.0) for i in inst}
    p = {i: -jnp.where(strict, xa[i][:c], 0.0) for i in inst}
    tm1 = dict(p)
    for _ in range(5):
        p = {i: _mm(p[i], bd(p[i])) for i in inst}
        tp = {i: _mm(tm1[i], bd(p[i])) for i in inst}
        tm1 = {i: tm1[i] + tp[i] + p[i] for i in inst}
    lkv = {i: _mm(l_k[i], v_bd[i]) for i in inst}
    rkv = {i: _mm(rk[i], v_bd[i]) for i in inst}
    so = {i: head_sum(at(r, i) * at(k2_all, i) * r_k[:, gss[i[1]]]) for i in inst}

    state_t = [st_scr[g] for g in groups]
    for ci in range(n_chunks):
        ids = [(ci, g) for g in groups]
        brh = [_mm_nt(br[i], state_t[i[1]]) for i in ids]
        rhs = [brh[g][:c] + lkv[ci, g] for g in groups]
        tu = [_mm(tm1[ci, g], bd(rhs[g])) for g in groups]
        u = [rhs[g] + tu[g] for g in groups]
        rau = [_mm(ra[ci, g], bd(u[g])) for g in groups]
        upd = [_mm_tn(jnp.concatenate([u[g], at(v, (ci, g))], axis=0),
                      jnp.concatenate([-(a_h[ci, g] * w_end[ci, g]), k_h[ci, g] * w_end[ci, g]], axis=0))
               for g in groups]
        state_t = [w_end[ci, g] * state_t[g] + jnp.where(bdmask, upd[g], 0.0) for g in groups]
        o = [brh[g][c:] - rau[g] + rkv[ci, g] for g in groups]
        osum = [head_sum(o[g]) for g in groups]
        cen = [o[g] - osum[g] * (1.0 / hd) for g in groups]
        var = [head_sum(cen[g] * cen[g]) * (1.0 / hd) for g in groups]
        for g in groups:
            gs = gss[g]
            on = cen[g] * lax.rsqrt(var[g] + RWKV_GN_EPS) * lnx_w[:, gs] + lnx_b[:, gs]
            bonus = so[ci, g] * at(v, (ci, g))
            o_ref[sls[ci], gs] = ((on + bonus) * gate[sls[ci], gs]).astype(o_ref.dtype)
    for g in groups:
        st_scr[g] = state_t[g]


def _rwkv(z, bsz, seq, mu, vecs, wup_pad, aup_pad, gup_pad):
    m = z.shape[0]
    tt = 256
    nt = seq // tt
    gw = RWKV_GROUP
    width = RWKV_WIDTH
    zblk = lambda c0: pl.BlockSpec((tt, width), lambda b, t: (b * nt + t, c0 // width))
    lora_col = (COL_RWKV + 3 * width) // RWKV_LORA
    const2 = lambda b, t: (0, 0)
    return pl.pallas_call(
        _rwkv_body,
        grid=(bsz, nt),
        in_specs=[
            zblk(COL_RWKV), zblk(COL_RWKV + width), zblk(COL_RWKV + 2 * width),
            pl.BlockSpec((tt, RWKV_LORA), lambda b, t: (b * nt + t, lora_col)),
            pl.BlockSpec(mu.shape, const2),
            pl.BlockSpec((8, width), const2),
            pl.BlockSpec((RWKV_LORA, width), const2),
            pl.BlockSpec((RWKV_LORA, width), const2),
            pl.BlockSpec((RWKV_LORA, width), const2),
        ],
        out_specs=pl.BlockSpec((tt, width), lambda b, t: (b * nt + t, 0)),
        out_shape=jax.ShapeDtypeStruct((m, width), BF16),
        scratch_shapes=[pltpu.VMEM((8, width), F32)] * 3 + [pltpu.VMEM((8, RWKV_LORA), F32),
                                                            pltpu.VMEM((width // gw, gw, gw), F32)],
        compiler_params=_cparams("parallel", "arbitrary"),
        name="rwkv7",
    )(z, z, z, z, mu, vecs, wup_pad, aup_pad, gup_pad)


def _merge_body(h_ref, ya_ref, yb_ref, yc_ref, za_ref, zb_ref, zc_ref, mb_ref, pa_ref, pb_ref,
                pc_ref, wo_ref, nw_ref, hout_ref, xnt_ref):
    mb = mb_ref[...]
    ga = jax.nn.sigmoid(za_ref[...].astype(F32) + mb[:, 0:D_MODEL])
    gb = jax.nn.sigmoid(zb_ref[...].astype(F32) + mb[:, D_MODEL:2 * D_MODEL])
    gc = jax.nn.sigmoid(zc_ref[...].astype(F32) + mb[:, 2 * D_MODEL:])
    merged = (ga * jnp.dot(ya_ref[...], pa_ref[...], preferred_element_type=F32)
              + gb * jnp.dot(yb_ref[...], pb_ref[...], preferred_element_type=F32)
              + gc * jnp.dot(yc_ref[...], pc_ref[...], preferred_element_type=F32))
    h = h_ref[...] + _mm(merged, wo_ref[...])
    hout_ref[...] = h
    xnt_ref[...] = jnp.transpose(_rms(h, nw_ref[...])).astype(BF16)


def _merge(h, z, ya, yb, yc, merge_b, pa, pb, pc, wo, norm_w):
    m = h.shape[0]
    tm = 256
    row = lambda c: pl.BlockSpec((tm, D_MODEL), lambda i: (i, c))
    wspec = pl.BlockSpec((D_MODEL, D_MODEL), lambda i: (0, 0))
    mc = COL_MERGE // D_MODEL
    return pl.pallas_call(
        _merge_body,
        grid=(m // tm,),
        in_specs=[row(0), row(0), row(0), row(0), row(mc), row(mc + 1), row(mc + 2),
                  pl.BlockSpec((1, 3 * D_MODEL), lambda i: (0, 0)),
                  wspec, wspec, wspec, wspec,
                  pl.BlockSpec((1, D_MODEL), lambda i: (0, 0))],
        out_specs=[row(0), pl.BlockSpec((D_MODEL, tm), lambda i: (0, i))],
        out_shape=[jax.ShapeDtypeStruct((m, D_MODEL), F32), jax.ShapeDtypeStruct((D_MODEL, m), BF16)],
        compiler_params=_cparams("parallel"),
        name="merge_outproj",
    )(h, ya, yb, yc, z, z, z, merge_b, pa, pb, pc, wo, norm_w)


_CAND_SLABS = (
    (16, ((0, 0, 16),)),
    (16, ((1, 0, 8), (2, 8, 5), (4, 13, 3))),
    (16, ((3, 0, 4), (5, 4, 2), (6, 6, 2), (7, 8, 2), (8, 10, 1), (9, 11, 1), (10, 12, 1), (11, 13, 1),
          (12, 14, 1), (13, 15, 1))),
    (8, ((14, 0, 1), (15, 1, 1))),
)


def _odd_even_merge_sort_pairs(n):
    pairs = []
    p = 1
    while p < n:
        k = p
        while k >= 1:
            for j in range(k % p, n - k, 2 * k):
                for i in range(min(k, n - j - k)):
                    if (i + j) // (2 * p) == (i + j + k) // (2 * p):
                        pairs.append((i + j, i + j + k))
            k //= 2
        p *= 2
    return tuple(pairs)


_SORT_PAIRS = _odd_even_merge_sort_pairs(PEER_TOPK)


def _candidate_sums(a1, a2):
    slabs = []
    for nrows, pieces in _CAND_SLABS:
        base = a2[:nrows]
        rows = lax.broadcasted_iota(jnp.int32, base.shape, 0)
        out = None
        end = 0
        for p, off, cnt in pieces:
            val = a1[p:p + 1] + (base if off == 0 else pltpu.roll(base, off, 0))
            out = val if out is None else jnp.where(rows >= off, val, out)
            end = off + cnt
        if end < nrows:
            out = jnp.where(rows >= end, -BIG, out)
        slabs.append(out)
    return jnp.concatenate(slabs, axis=0)


def _peer_topk_body(xnt_ref, wqt_ref, keys_ref, n_ref, e1_ref, r2_ref, e2_ref, q_scr):
    tt = xnt_ref.shape[1]
    nk = PEER_NKEYS
    topk = PEER_TOPK
    q_scr[...] = jnp.dot(wqt_ref[...], xnt_ref[...], preferred_element_type=F32)
    lanes = 128
    sub = 8
    assert nk == topk * sub
    row8 = lax.broadcasted_iota(jnp.int32, (sub, lanes), 0)

    def top_sorted(s):
        v = [s[k * sub:(k + 1) * sub] for k in range(topk)]
        for i, j in _SORT_PAIRS:
            v[i], v[j] = jnp.maximum(v[i], v[j]), jnp.minimum(v[i], v[j])
        for shift in (4, 2, 1):
            other = [pltpu.roll(x, shift, 0) for x in v]
            v = [jnp.maximum(v[k], other[topk - 1 - k]) for k in range(topk)]
            d = topk // 2
            while d >= 1:
                for i in range(topk):
                    if i & d == 0:
                        v[i], v[i + d] = jnp.maximum(v[i], v[i + d]), jnp.minimum(v[i], v[i + d])
                d //= 2
        return v

    def compact(v):
        tiles = []
        for t0 in range(0, topk, sub):
            out = v[t0]
            for k in range(1, sub):
                out = jnp.where(row8 == k, v[t0 + k], out)
            tiles.append(out)
        return jnp.concatenate(tiles, axis=0)

    def head(h, carry):
        o1 = pl.multiple_of(h * (2 * nk), 2 * nk)
        s1 = _mm(keys_ref[2 * h], q_scr[pl.ds(o1, nk), :])
        s2 = _mm(keys_ref[2 * h + 1], q_scr[pl.ds(o1 + nk, nk), :])
        chunks = [slice(l0, l0 + lanes) for l0 in range(0, tt, lanes)]
        a1 = [top_sorted(s1[:, ls]) for ls in chunks]
        a2 = [top_sorted(s2[:, ls]) for ls in chunks]
        a1c = jnp.concatenate([compact(v) for v in a1], axis=1)
        a2c = jnp.concatenate([compact(v) for v in a2], axis=1)
        cand = _candidate_sums(a1c, a2c)
        cmax = a1c[0:1] + a2c[0:1]

        def cbody(rnd, carry):
            cnd, zsum, c_in, c_out = carry
            mx = jnp.max(cnd, axis=0, keepdims=True)
            zsum = zsum + jnp.where(rnd < topk, jnp.exp(mx - cmax), 0.0)
            c_in = jnp.where(rnd == topk - 1, mx, c_in)
            c_out = jnp.where(rnd == topk, mx, c_out)
            return jnp.where(cnd == mx, -BIG, cnd), zsum, c_in, c_out

        zero = jnp.zeros((1, tt), F32)
        _, zsum, c_in, c_out = lax.fori_loop(0, topk + 1, cbody, (cand, zero, zero, zero))
        tau = 0.5 * (c_in + c_out)
        inv_z = 1.0 / zsum
        pack = 16

        def rep(x):
            return jnp.concatenate([x] * (pack // sub), axis=0)

        for ci, ls in enumerate(chunks):
            tau8 = jnp.broadcast_to(tau[:, ls], (sub, lanes))
            need = [rep(tau8 - a2[ci][qq]) for qq in range(topk)]
            val2 = [rep(a2[ci][qq]) for qq in range(topk)]
            last1 = rep(a1[ci][topk - 1])
            top1 = rep(a1[ci][0])
            top2 = val2[0]
            for k in range(nk // pack):
                rows = slice(k * pack, (k + 1) * pack)
                x1 = s1[rows, ls]
                x2 = s2[rows, ls]
                n_sel = jnp.zeros((pack, lanes), F32)
                r2 = jnp.zeros((pack, lanes), F32)
                for qq in range(topk):
                    n_sel = jnp.where(x1 >= need[qq], n_sel + 1.0, n_sel)
                    r2 = jnp.where(x2 < val2[qq], r2 + 1.0, r2)
                n_ref[h, rows, ls] = jnp.where(x1 >= last1, n_sel, 0.0)
                e1_ref[h, rows, ls] = jnp.exp(x1 - top1)
                r2_ref[h, rows, ls] = r2.astype(BF16)
                e2_ref[h, rows, ls] = (jnp.exp(x2 - top2) * inv_z[:, ls]).astype(BF16)
        return carry

    lax.fori_loop(0, PEER_HEADS, head, 0)


def _peer_topk(xnt, wqt, keys):
    m = xnt.shape[1]
    tt = 512 if m % 512 == 0 else 256
    shape = (PEER_HEADS, PEER_NKEYS, m)
    ospec = pl.BlockSpec((PEER_HEADS, PEER_NKEYS, tt), lambda i: (0, 0, i))
    return pl.pallas_call(
        _peer_topk_body,
        grid=(m // tt,),
        in_specs=[pl.BlockSpec((D_MODEL, tt), lambda i: (0, i)),
                  pl.BlockSpec(wqt.shape, lambda i: (0, 0)),
                  pl.BlockSpec(keys.shape, lambda i: (0, 0, 0))],
        out_specs=[ospec] * 4,
        out_shape=[jax.ShapeDtypeStruct(shape, F32), jax.ShapeDtypeStruct(shape, F32),
                   jax.ShapeDtypeStruct(shape, BF16), jax.ShapeDtypeStruct(shape, BF16)],
        scratch_shapes=[pltpu.VMEM((wqt.shape[0], tt), F32)],
        compiler_params=_cparams("parallel"),
        name="peer_topk",
    )(xnt, wqt, keys)


PEER_TOKEN_CHUNK = 256


def _peer_dense_body(u_ref, xnt_ref, vt_ref, n_ref, e1_ref, r2_ref, e2_ref, y_ref, acc_scr):
    nk = PEER_NKEYS
    tt = xnt_ref.shape[1]
    tc = PEER_TOKEN_CHUNK
    n_blk = u_ref.shape[0] // nk
    pack = 16

    @pl.when(pl.program_id(1) == 0)
    def _():
        acc_scr[...] = jnp.zeros_like(acc_scr)

    def rows_bf16(ref, h, ii, ls):
        row = jnp.broadcast_to(ref[h, ii:ii + 1, ls], (pack, tc)).astype(BF16)
        return jnp.concatenate([row] * (nk // pack), axis=0)

    def select_weights(ii, c0):
        ls = slice(c0, c0 + tc)
        w = None
        for h in range(PEER_HEADS):
            sel = jnp.where(r2_ref[h, :, ls] < rows_bf16(n_ref, h, ii, ls), e2_ref[h, :, ls],
                            jnp.zeros((), BF16))
            term = sel * rows_bf16(e1_ref, h, ii, ls)
            w = term if w is None else w + term
        return w

    def gelu_tanh(x):
        c = 0.7978845608028654
        inner = x * ((x * x) * (c * 0.044715) + c)
        hx = 0.5 * x
        return hx * jnp.tanh(inner) + hx

    starts = list(range(0, tt, tc))
    pre = [jnp.dot(u_ref[...], xnt_ref[:, c0:c0 + tc], preferred_element_type=F32) for c0 in starts]
    for idx, c0 in enumerate(starts):
        act = gelu_tanh(pre[idx].astype(BF16))
        a = jnp.concatenate([act[ii * nk:(ii + 1) * nk] * select_weights(ii, c0) for ii in range(n_blk)], axis=0)
        acc_scr[:, c0:c0 + tc] += jnp.dot(vt_ref[...], a, preferred_element_type=F32)

    @pl.when(pl.program_id(1) == pl.num_programs(1) - 1)
    def _():
        y_ref[...] = jnp.transpose(acc_scr[...])


def _peer_dense(xnt, u_bf, vt_bf, n_sel, e1, r2, e2, tt):
    m = xnt.shape[1]
    te = 1024
    ib = te // PEER_NKEYS
    sel_i = pl.BlockSpec((PEER_HEADS, ib, tt), lambda i, e: (0, e, i))
    sel_all = pl.BlockSpec((PEER_HEADS, PEER_NKEYS, tt), lambda i, e: (0, 0, i))
    return pl.pallas_call(
        _peer_dense_body,
        grid=(m // tt, PEER_EXPERTS // te),
        in_specs=[pl.BlockSpec((te, D_MODEL), lambda i, e: (e, 0)),
                  pl.BlockSpec((D_MODEL, tt), lambda i, e: (0, i)),
                  pl.BlockSpec((D_MODEL, te), lambda i, e: (0, e)),
                  sel_i, sel_i, sel_all, sel_all],
        out_specs=pl.BlockSpec((tt, D_MODEL), lambda i, e: (i, 0)),
        out_shape=jax.ShapeDtypeStruct((m, D_MODEL), F32),
        scratch_shapes=[pltpu.VMEM((D_MODEL, tt), F32)],
        compiler_params=_cparams("parallel", "arbitrary"),
        name="peer_dense",
    )(u_bf, xnt, vt_bf, n_sel, e1, r2, e2)


def _final_body(h_ref, y_ref, nw_ref, o_ref):
    o_ref[...] = _rms(h_ref[...] + y_ref[...], nw_ref[...])


def _final(h, y, norm_w):
    m = h.shape[0]
    tm = 512
    row = pl.BlockSpec((tm, D_MODEL), lambda i: (i, 0))
    return pl.pallas_call(
        _final_body,
        grid=(m // tm,),
        in_specs=[row, row, pl.BlockSpec((1, D_MODEL), lambda i: (0, 0))],
        out_specs=row,
        out_shape=jax.ShapeDtypeStruct((m, D_MODEL), F32),
        compiler_params=_cparams("parallel"),
        name="final_norm",
    )(h, y, norm_w.reshape(1, D_MODEL))


def _pad_rows(rows, width, n_rows=8):
    out = jnp.zeros((n_rows, width), F32)
    for i, r in enumerate(rows):
        flat = r.reshape(-1).astype(F32)
        out = out.at[i, :flat.shape[0]].set(flat)
    return out


def _chunk_rows(ab):
    m, n = ab.shape
    return jnp.transpose(ab.reshape(m // GDN_CHUNK, GDN_CHUNK, n), (0, 2, 1)).astype(F32)


def _block_diag_groups(w):
    per = LRU_GROUP // LRU_BLOCK_DIM
    n_grp = w.shape[0] // per
    out = jnp.zeros((n_grp, LRU_GROUP, LRU_GROUP), F32)
    for g in range(n_grp):
        for j in range(per):
            o = j * LRU_BLOCK_DIM
            out = out.at[g, o:o + LRU_BLOCK_DIM, o:o + LRU_BLOCK_DIM].set(w[g * per + j])
    return out.astype(BF16)


def _prep_w_in(w):
    pad = jnp.zeros((D_MODEL, IN_WIDTH_PAD - COL_AB - 2 * GDN_HEADS), w.dtype)
    return jnp.concatenate([w[:, :6144], w[:, 9488:12560], w[:, 6160:9488], w[:, 6144:6160], pad],
                           axis=1).astype(BF16)


def _layer(l, h, y, bsz, seq, p):
    z_h, z = _inproj(h, y, p["norm_mix_w"][l], _prep_w_in(p["w_in"][l]))
    h = z_h

    lru_vecs = _pad_rows([p["lru_conv_b"][l], p["lru_b_a"][l], p["lru_b_x"][l], p["lru_lambda"][l]], LRU_WIDTH)
    ya = _lru(z, bsz, seq, p["lru_conv_w"][l], lru_vecs,
              _block_diag_groups(p["lru_w_a"][l]), _block_diag_groups(p["lru_w_x"][l]))

    abt = _chunk_rows(z[:, COL_AB:COL_AB + 2 * GDN_HEADS])
    gp = _pad_rows([p["gdn_a_log"][l], p["gdn_dt_bias"][l]], 128)
    gpt = jnp.zeros((8, 128), F32).at[:, 0].set(p["gdn_a_log"][l]).at[:, 1].set(p["gdn_dt_bias"][l])
    yb = _gdn(z, abt, bsz, seq, p["gdn_conv_w"][l], gp, gpt, p["gdn_norm_w"][l].reshape(1, GDN_DV))

    rw_vecs = _pad_rows([p["rwkv_w0"][l], p["rwkv_a0"][l], p["rwkv_k_k"][l], p["rwkv_k_a"][l],
                         p["rwkv_r_k"][l], p["rwkv_lnx_w"][l], p["rwkv_lnx_b"][l]], RWKV_WIDTH)
    zl = jnp.zeros((RWKV_LORA, RWKV_WIDTH), F32)
    wup = zl.at[0:64].set(p["rwkv_w_up"][l]).astype(BF16)
    aup = zl.at[64:128].set(p["rwkv_a_up"][l]).astype(BF16)
    gup = zl.at[128:256].set(p["rwkv_g_up"][l]).astype(BF16)
    yc = _rwkv(z, bsz, seq, p["rwkv_mu"][l].reshape(1, -1), rw_vecs, wup, aup, gup)

    h, xnt = _merge(h, z, ya, yb, yc, p["merge_b"][l].reshape(1, -1), p["p_lru"][l].astype(BF16),
                    p["p_gdn"][l].astype(BF16), p["p_rwkv"][l].astype(BF16), p["w_out"][l].astype(BF16),
                    p["norm_ffn_w"][l].reshape(1, D_MODEL))

    wqt = jnp.transpose(p["peer_wq"][l]).astype(BF16)
    keys = p["peer_keys"][l].reshape(2 * PEER_HEADS, PEER_NKEYS, PEER_HALF).astype(BF16)
    n_sel, e1, r2, e2 = _peer_topk(xnt, wqt, keys)
    tt = 512 if xnt.shape[1] % 512 == 0 else 256
    y = _peer_dense(xnt, p["peer_u"][l].astype(BF16), jnp.transpose(p["peer_v"][l]).astype(BF16),
                    n_sel, e1, r2, e2, tt)
    return h, y


def kernel(x, norm_mix_w, norm_ffn_w, final_norm_w, w_in, lru_conv_w, lru_conv_b, lru_w_a, lru_b_a,
           lru_w_x, lru_b_x, lru_lambda, gdn_conv_w, gdn_a_log, gdn_dt_bias, gdn_norm_w, rwkv_mu, rwkv_w0,
           rwkv_w_up, rwkv_a0, rwkv_a_up, rwkv_g_up, rwkv_k_k, rwkv_k_a, rwkv_r_k, rwkv_lnx_w, rwkv_lnx_b,
           merge_b, p_lru, p_gdn, p_rwkv, w_out, peer_wq, peer_keys, peer_u, peer_v):
    p = dict(locals())
    bsz, seq, dim = x.shape
    h = x.reshape(bsz * seq, dim)
    y = None
    for l in range(DEPTH):
        h, y = _layer(l, h, y, bsz, seq, p)
    return _final(h, y, final_norm_w).reshape(bsz, seq, dim)
```

```python
import functools

import jax
import jax.numpy as jnp
from jax import lax
from jax.experimental import pallas as pl
from jax.experimental.pallas import tpu as pltpu

F32 = jnp.float32
BF16 = jnp.bfloat16

D_MODEL = 1024
DEPTH = 2
RMS_EPS = 1e-6

LRU_WIDTH = 1024
LRU_BLOCK_DIM = 64
LRU_C = 8.0
LRU_GROUP = 256

GDN_HEADS = 8
GDN_DK = 128
GDN_DV = 128
GDN_CHUNK = 64

RWKV_HEADS = 16
RWKV_HD = 64
RWKV_WIDTH = 1024
RWKV_GN_EPS = 64e-5
RWKV_CHUNK = 64
RWKV_GROUP = 256
RWKV_LORA = 256

PEER_HEADS = 8
PEER_NKEYS = 128
PEER_EXPERTS = PEER_NKEYS * PEER_NKEYS
PEER_HALF = 128
PEER_TOPK = 16

COL_U = 0
COL_GATE = 1024
COL_QKV = 2048
COL_ZG = 5120
COL_MERGE = 6144
COL_RWKV = 9216
COL_AB = 12544
IN_WIDTH_PAD = 12672

VMEM_LIMIT = 48 * 1024 * 1024
BIG = 3.0e38


def _cparams(*sem):
    return pltpu.CompilerParams(dimension_semantics=sem, vmem_limit_bytes=VMEM_LIMIT)


def _mm(a, b):
    return jnp.dot(a.astype(BF16), b.astype(BF16), preferred_element_type=F32)


def _mm_nt(a, b):
    return lax.dot_general(a.astype(BF16), b.astype(BF16), (((1,), (1,)), ((), ())),
                           preferred_element_type=F32)


def _mm_tn(a, b):
    return lax.dot_general(a.astype(BF16), b.astype(BF16), (((0,), (0,)), ((), ())),
                           preferred_element_type=F32)


def _softplus(x):
    return jnp.maximum(x, 0.0) + jnp.log1p(jnp.exp(-jnp.abs(x)))


def _rms(x, w):
    return x * lax.rsqrt(jnp.mean(x * x, axis=-1, keepdims=True) + RMS_EPS) * w


def _shift_prev(x, prev8, s):
    r = pltpu.roll(x, s, 0)
    pr = pltpu.roll(prev8, s, 0)
    rows8 = lax.broadcasted_iota(jnp.int32, prev8.shape, 0)
    head = jnp.where(rows8 < s, pr, r[:8])
    return jnp.concatenate([head, r[8:]], axis=0)


def _shift_fill(x, d, fill):
    n, c = x.shape
    if d % 8 == 0:
        return jnp.concatenate([jnp.full((d, c), fill, x.dtype), x[:n - d]], axis=0)
    r = pltpu.roll(x, d, 0)
    rows8 = lax.broadcasted_iota(jnp.int32, (8, c), 0)
    head = jnp.where(rows8 < d, fill, r[:8])
    return jnp.concatenate([head, r[8:]], axis=0)


def _tri(n, strict):
    i = lax.broadcasted_iota(jnp.int32, (n, n), 0)
    j = lax.broadcasted_iota(jnp.int32, (n, n), 1)
    return (j < i) if strict else (j <= i)


def _split_bf16(x, terms):
    parts = []
    rem = x
    for i in range(terms):
        p = rem.astype(BF16)
        parts.append(p)
        if i + 1 < terms:
            rem = rem - p.astype(F32)
    return parts


def _mask_mm(mask_bf, x, terms=3):
    return sum(jnp.dot(mask_bf, p, preferred_element_type=F32) for p in _split_bf16(x, terms))


def _mm_mask(x, mask_bf, terms=3):
    return sum(jnp.dot(p, mask_bf, preferred_element_type=F32) for p in _split_bf16(x, terms))


def _inproj_body(add_y, *refs):
    if add_y:
        h_ref, y_ref, nw_ref, w_ref, hout_ref, z_ref, xn_scr = refs
    else:
        h_ref, nw_ref, w_ref, z_ref, xn_scr = refs

    @pl.when(pl.program_id(1) == 0)
    def _():
        h = h_ref[...]
        if add_y:
            h = h + y_ref[...]
            hout_ref[...] = h
        xn_scr[...] = _rms(h, nw_ref[...]).astype(BF16)

    z_ref[...] = jnp.dot(xn_scr[...], w_ref[...], preferred_element_type=F32).astype(z_ref.dtype)


def _inproj(h, y, norm_w, w_pad):
    m = h.shape[0]
    tm, tn = (1024 if m % 1024 == 0 else 512), 1152
    add_y = y is not None
    row = pl.BlockSpec((tm, D_MODEL), lambda i, j: (i, 0))
    in_specs = [row] + ([row] if add_y else []) + [
        pl.BlockSpec((1, D_MODEL), lambda i, j: (0, 0)),
        pl.BlockSpec((D_MODEL, tn), lambda i, j: (0, j)),
    ]
    z_spec = pl.BlockSpec((tm, tn), lambda i, j: (i, j))
    z_shape = jax.ShapeDtypeStruct((m, IN_WIDTH_PAD), BF16)
    args = (h,) + ((y,) if add_y else ()) + (norm_w.reshape(1, D_MODEL), w_pad)
    out = pl.pallas_call(
        functools.partial(_inproj_body, add_y),
        grid=(m // tm, IN_WIDTH_PAD // tn),
        in_specs=in_specs,
        out_specs=[row, z_spec] if add_y else z_spec,
        out_shape=[jax.ShapeDtypeStruct((m, D_MODEL), F32), z_shape] if add_y else z_shape,
        scratch_shapes=[pltpu.VMEM((tm, D_MODEL), BF16)],
        compiler_params=_cparams("parallel", "arbitrary"),
        name="norm_inproj",
    )(*args)
    return (out[0], out[1]) if add_y else (h, out)


def _lru_body(zu_ref, zg_ref, cw_ref, vp_ref, wa_ref, wx_ref, o_ref, prev_scr, hc_scr):
    tt = zu_ref.shape[0]

    @pl.when(pl.program_id(1) == 0)
    def _():
        prev_scr[...] = jnp.zeros_like(prev_scr)
        hc_scr[...] = jnp.zeros_like(hc_scr)

    u = zu_ref[...].astype(F32)
    prev = prev_scr[...]
    cw = cw_ref[...]
    vp = vp_ref[...]
    xc = (cw[3:4] * u + cw[2:3] * _shift_prev(u, prev, 1) + cw[1:2] * _shift_prev(u, prev, 2)
          + cw[0:1] * _shift_prev(u, prev, 3) + vp[0:1])
    prev_scr[...] = u[tt - 8:]

    xcb = xc.astype(BF16)
    n_grp = LRU_WIDTH // LRU_GROUP
    pre_a = jnp.concatenate(
        [jnp.dot(xcb[:, g * LRU_GROUP:(g + 1) * LRU_GROUP], wa_ref[g], preferred_element_type=F32)
         for g in range(n_grp)], axis=1)
    pre_x = jnp.concatenate(
        [jnp.dot(xcb[:, g * LRU_GROUP:(g + 1) * LRU_GROUP], wx_ref[g], preferred_element_type=F32)
         for g in range(n_grp)], axis=1)
    r = jax.nn.sigmoid(pre_a + vp[1:2])
    i = jax.nn.sigmoid(pre_x + vp[2:3])
    log_a = (-LRU_C * r) * _softplus(-vp[3:4])
    a = jnp.exp(log_a)
    b = jnp.sqrt(-jnp.tanh(log_a) * (a * a + 1.0)) * (i * xc)

    d = 1
    while d < tt:
        b = a * _shift_fill(b, d, 0.0) + b
        a = a * _shift_fill(a, d, 1.0)
        d *= 2
    hcar = hc_scr[...]
    hval = b + a * hcar[0:1]
    hc_scr[...] = jnp.broadcast_to(hval[tt - 1:tt], hcar.shape)
    o_ref[...] = (hval * jax.nn.gelu(zg_ref[...].astype(F32))).astype(o_ref.dtype)


def _lru(z, bsz, seq, conv_w, vecs, wa_bd, wx_bd):
    m = z.shape[0]
    tt = 256
    nt = seq // tt
    rowmap = lambda c: (lambda b, t: (b * nt + t, c))
    const2 = lambda b, t: (0, 0)
    return pl.pallas_call(
        _lru_body,
        grid=(bsz, nt),
        in_specs=[
            pl.BlockSpec((tt, LRU_WIDTH), rowmap(COL_U // LRU_WIDTH)),
            pl.BlockSpec((tt, LRU_WIDTH), rowmap(COL_GATE // LRU_WIDTH)),
            pl.BlockSpec((4, LRU_WIDTH), const2),
            pl.BlockSpec((8, LRU_WIDTH), const2),
            pl.BlockSpec(wa_bd.shape, lambda b, t: (0, 0, 0)),
            pl.BlockSpec(wx_bd.shape, lambda b, t: (0, 0, 0)),
        ],
        out_specs=pl.BlockSpec((tt, LRU_WIDTH), rowmap(0)),
        out_shape=jax.ShapeDtypeStruct((m, LRU_WIDTH), BF16),
        scratch_shapes=[pltpu.VMEM((8, LRU_WIDTH), F32), pltpu.VMEM((8, LRU_WIDTH), F32)],
        compiler_params=_cparams("parallel", "arbitrary"),
        name="rglru",
    )(z, z, conv_w, vecs, wa_bd, wx_bd)


def _gdn_body(q_ref, k_ref, v_ref, zg_ref, ab_ref, abt_ref, cw_ref, gp_ref, gpt_ref, nw_ref,
              o_ref, pq_scr, pk_scr, pv_scr, st_scr):
    tt = q_ref.shape[0]
    c = GDN_CHUNK
    width = GDN_HEADS * GDN_DK

    @pl.when(pl.program_id(1) == 0)
    def _():
        for s in (pq_scr, pk_scr, pv_scr, st_scr):
            s[...] = jnp.zeros_like(s)

    cw = cw_ref[...]

    def conv_silu(x_ref, p_scr, w):
        x = x_ref[...].astype(F32)
        prev = p_scr[...]
        y = (w[3:4] * x + w[2:3] * _shift_prev(x, prev, 1) + w[1:2] * _shift_prev(x, prev, 2)
             + w[0:1] * _shift_prev(x, prev, 3))
        p_scr[...] = x[tt - 8:]
        return jax.nn.silu(y)

    q = conv_silu(q_ref, pq_scr, cw[:, 0:width])
    k = conv_silu(k_ref, pk_scr, cw[:, width:2 * width])
    v = conv_silu(v_ref, pv_scr, cw[:, 2 * width:])

    ab = ab_ref[...].astype(F32)
    gp = gp_ref[...]
    gpt = gpt_ref[...]
    g_all = -jnp.exp(gp[0:1]) * _softplus(ab + gp[1:2])
    beta_all = jax.nn.sigmoid(ab)

    incl = _tri(c, False)
    strict = _tri(c, True)
    cum_mask = jnp.concatenate([incl.astype(BF16), jnp.ones((c, c), BF16)], axis=0)
    triu_bf = (lax.broadcasted_iota(jnp.int32, (c, c), 0) <= lax.broadcasted_iota(jnp.int32, (c, c), 1)).astype(BF16)
    nw = nw_ref[...]

    n_chunks = tt // c
    heads = range(GDN_HEADS)
    inst = [(ci, h) for ci in range(n_chunks) for h in heads]
    sls = [slice(ci * c, (ci + 1) * c) for ci in range(n_chunks)]
    hss = [slice(h * GDN_DK, (h + 1) * GDN_DK) for h in heads]
    gcs = [_mask_mm(cum_mask, g_all[sl]) for sl in sls]
    gr_all = [_mm_mask(-jnp.exp(gpt[:, 0:1]) * _softplus(abt_ref[ci][0:GDN_HEADS] + gpt[:, 1:2]), triu_bf)
              for ci in range(n_chunks)]

    qs, ks, kbs, decays, egcs, kdecs, gtots, bcols = {}, {}, {}, {}, {}, {}, {}, {}
    for ci, h in inst:
        sl, hs = sls[ci], hss[h]
        qc, kc = q[sl, hs], k[sl, hs]
        qs[ci, h] = qc * lax.rsqrt(jnp.sum(qc * qc, axis=-1, keepdims=True) + RMS_EPS) * (GDN_DK ** -0.5)
        ks[ci, h] = kc * lax.rsqrt(jnp.sum(kc * kc, axis=-1, keepdims=True) + RMS_EPS)
        gcol = gcs[ci][:c, h:h + 1]
        g_last = gcs[ci][c:, h:h + 1]
        grow = gr_all[ci][h:h + 1, :]
        bcols[ci, h] = beta_all[sl, GDN_HEADS + h:GDN_HEADS + h + 1]
        decays[ci, h] = jnp.exp(jnp.where(incl, gcol - grow, -BIG))
        kbs[ci, h] = ks[ci, h] * bcols[ci, h]
        egcs[ci, h] = jnp.exp(gcol)
        kdecs[ci, h] = ks[ci, h] * jnp.exp(g_last - gcol)
        gtots[ci, h] = jnp.exp(jnp.concatenate([g_last] * (GDN_DK // c), axis=0))

    kk = {i: _mm_nt(kbs[i], ks[i]) for i in inst}
    qk = {i: _mm_nt(qs[i], ks[i]) * decays[i] for i in inst}
    p = {i: -jnp.where(strict, kk[i] * decays[i], 0.0) for i in inst}
    tm1 = dict(p)
    for _ in range(5):
        p = {i: _mm(p[i], p[i]) for i in inst}
        tp = {i: _mm(tm1[i], p[i]) for i in inst}
        tm1 = {i: tm1[i] + tp[i] + p[i] for i in inst}
    rhs = {i: jnp.concatenate([v[sls[i[0]], hss[i[1]]] * bcols[i], kbs[i] * egcs[i]], axis=1) for i in inst}
    tr = {i: _mm(tm1[i], rhs[i]) for i in inst}
    sol = {i: rhs[i] + tr[i] for i in inst}

    state = [st_scr[h] for h in heads]
    for ci in range(n_chunks):
        ws = [_mm(sol[ci, h][:, GDN_DV:], state[h]) for h in heads]
        qst = [_mm(qs[ci, h] * egcs[ci, h], state[h]) for h in heads]
        v_new = [sol[ci, h][:, :GDN_DV] - ws[h] for h in heads]
        qv = [_mm(qk[ci, h], v_new[h]) for h in heads]
        kv = [_mm_tn(kdecs[ci, h], v_new[h]) for h in heads]
        state = [state[h] * gtots[ci, h] + kv[h] for h in heads]
        for h in heads:
            o = qst[h] + qv[h]
            o = o * lax.rsqrt(jnp.mean(o * o, axis=-1, keepdims=True) + RMS_EPS) * nw
            o = o * jax.nn.silu(zg_ref[sls[ci], hss[h]].astype(F32))
            o_ref[sls[ci], hss[h]] = o.astype(o_ref.dtype)
    for h in heads:
        st_scr[h] = state[h]


def _gdn(z, abt, bsz, seq, conv_w, gp, gpt, norm_w):
    m = z.shape[0]
    tt = 256
    nt = seq // tt
    width = GDN_HEADS * GDN_DK
    zblk = lambda c0: pl.BlockSpec((tt, width), lambda b, t: (b * nt + t, c0 // width))
    const2 = lambda b, t: (0, 0)
    return pl.pallas_call(
        _gdn_body,
        grid=(bsz, nt),
        in_specs=[
            zblk(COL_QKV), zblk(COL_QKV + width), zblk(COL_QKV + 2 * width), zblk(COL_ZG),
            pl.BlockSpec((tt, 128), lambda b, t: (b * nt + t, COL_AB // 128)),
            pl.BlockSpec((tt // GDN_CHUNK, 2 * GDN_HEADS, GDN_CHUNK), lambda b, t: (b * nt + t, 0, 0)),
            pl.BlockSpec((4, 3 * width), const2),
            pl.BlockSpec((8, 128), const2),
            pl.BlockSpec((8, 128), const2),
            pl.BlockSpec((1, 128), const2),
        ],
        out_specs=pl.BlockSpec((tt, width), lambda b, t: (b * nt + t, 0)),
        out_shape=jax.ShapeDtypeStruct((m, width), BF16),
        scratch_shapes=[pltpu.VMEM((8, width), F32)] * 3 + [pltpu.VMEM((GDN_HEADS, GDN_DK, GDN_DV), F32)],
        compiler_params=_cparams("parallel", "arbitrary"),
        name="gated_deltanet",
    )(z, z, z, z, z, abt, conv_w, gp, gpt, norm_w)


def _rwkv_body(r_ref, k_ref, v_ref, lo_ref, mu_ref, vp_ref, wup_ref, aup_ref, gup_ref, o_ref,
               pr_scr, pk_scr, pv_scr, plo_scr, st_scr):
    tt = r_ref.shape[0]
    c = RWKV_CHUNK
    gw = RWKV_GROUP
    hd = RWKV_HD
    width = RWKV_WIDTH

    @pl.when(pl.program_id(1) == 0)
    def _():
        for s in (pr_scr, pk_scr, pv_scr, plo_scr, st_scr):
            s[...] = jnp.zeros_like(s)

    mu = mu_ref[...]

    def tshift(x_ref, p_scr, m):
        x = x_ref[...].astype(F32)
        xs = x + m * (_shift_prev(x, p_scr[...], 1) - x)
        p_scr[...] = x[tt - 8:]
        return xs

    r = tshift(r_ref, pr_scr, mu[:, 0:width])
    k = tshift(k_ref, pk_scr, mu[:, width:2 * width])
    v = tshift(v_ref, pv_scr, mu[:, 2 * width:3 * width])
    lo = tshift(lo_ref, plo_scr, mu[:, 3 * width:])
    vp = vp_ref[...]
    w0, a0, k_k, k_a, r_k, lnx_w, lnx_b = (vp[i:i + 1] for i in range(7))

    lane = lax.broadcasted_iota(jnp.int32, lo.shape, 1)
    lo_act = jnp.where(lane < 64, jnp.tanh(lo), jnp.where(lane < 128, lo, jax.nn.sigmoid(lo)))
    w_pre = _mm(lo_act, wup_ref[...])
    a_pre = _mm(lo_act, aup_ref[...])
    gate = _mm(lo_act, gup_ref[...])
    w_log = -_softplus(-(w0 + w_pre)) - 0.5
    lw = -jnp.exp(w_log)
    a = jax.nn.sigmoid(a0 + a_pre)

    bi = lax.broadcasted_iota(jnp.int32, (gw, gw), 0) // hd
    bj = lax.broadcasted_iota(jnp.int32, (gw, gw), 1) // hd
    bdmask = bi == bj
    ones_bd = bdmask.astype(BF16)

    def head_sum(x):
        return _mm_mask(x, ones_bd, terms=2)

    def bd(x):
        return jnp.where(bdmask, jnp.concatenate([x] * (gw // c), axis=0), 0.0)

    ti = lax.broadcasted_iota(jnp.int32, (c, gw), 0)
    sj = lax.broadcasted_iota(jnp.int32, (c, gw), 1) % c
    strict = sj < ti
    incl = sj <= ti
    tril_bf = _tri(c, False).astype(BF16)

    kk_all = k * k_k
    k2_all = k * (1.0 + (a - 1.0) * k_a)

    n_chunks = tt // c
    groups = range(width // gw)
    inst = [(ci, g) for ci in range(n_chunks) for g in groups]
    sls = [slice(ci * c, (ci + 1) * c) for ci in range(n_chunks)]
    gss = [slice(g * gw, (g + 1) * gw) for g in groups]
    cl_all = [_mask_mm(tril_bf, lw[sl]) for sl in sls]

    def at(x, i):
        return x[sls[i[0]], gss[i[1]]]

    kk_raw = {i: at(kk_all, i) for i in inst}
    kk_ss = {i: head_sum(kk_raw[i] * kk_raw[i]) for i in inst}
    a_h, b_h, k_h, br, w_end = {}, {}, {}, {}, {}
    for i in inst:
        cl = cl_all[i[0]][:, gss[i[1]]]
        kk = kk_raw[i] * lax.rsqrt(kk_ss[i] + RMS_EPS)
        wcum = jnp.exp(cl)
        inv_w = jnp.exp(-cl)
        a_h[i] = kk * at(a, i) * inv_w
        b_h[i] = kk * jnp.exp(cl - at(lw, i))
        k_h[i] = at(k2_all, i) * inv_w
        br[i] = jnp.concatenate([b_h[i], at(r, i) * wcum], axis=0)
        w_end[i] = wcum[c - 1:c]
    v_bd = {i: bd(at(v, i)) for i in inst}
    xa = {i: _mm_nt(br[i], bd(a_h[i])) for i in inst}
    xk = {i: _mm_nt(br[i], bd(k_h[i])) for i in inst}
    l_k = {i: jnp.where(strict, xk[i][:c], 0.0) for i in inst}
    ra = {i: jnp.where(incl, xa[i][c:], 0.0) for i in inst}
    rk = {i: jnp.where(incl, xk[i][c:], 0.0) for i in inst}
    p = {i: -jnp.where(strict, xa[i][:c], 0.0) for i in inst}
    tm1 = dict(p)
    for _ in range(5):
        p = {i: _mm(p[i], bd(p[i])) for i in inst}
        tp = {i: _mm(tm1[i], bd(p[i])) for i in inst}
        tm1 = {i: tm1[i] + tp[i] + p[i] for i in inst}
    lkv = {i: _mm(l_k[i], v_bd[i]) for i in inst}
    rkv = {i: _mm(rk[i], v_bd[i]) for i in inst}
    so = {i: head_sum(at(r, i) * at(k2_all, i) * r_k[:, gss[i[1]]]) for i in inst}

    state_t = [st_scr[g] for g in groups]
    for ci in range(n_chunks):
        ids = [(ci, g) for g in groups]
        brh = [_mm_nt(br[i], state_t[i[1]]) for i in ids]
        rhs = [brh[g][:c] + lkv[ci, g] for g in groups]
        tu = [_mm(tm1[ci, g], bd(rhs[g])) for g in groups]
        u = [rhs[g] + tu[g] for g in groups]
        rau = [_mm(ra[ci, g], bd(u[g])) for g in groups]
        upd = [_mm_tn(jnp.concatenate([u[g], at(v, (ci, g))], axis=0),
                      jnp.concatenate([-(a_h[ci, g] * w_end[ci, g]), k_h[ci, g] * w_end[ci, g]], axis=0))
               for g in groups]
        state_t = [w_end[ci, g] * state_t[g] + jnp.where(bdmask, upd[g], 0.0) for g in groups]
        o = [brh[g][c:] - rau[g] + rkv[ci, g] for g in groups]
        osum = [head_sum(o[g]) for g in groups]
        cen = [o[g] - osum[g] * (1.0 / hd) for g in groups]
        var = [head_sum(cen[g] * cen[g]) * (1.0 / hd) for g in groups]
        for g in groups:
            gs = gss[g]
            on = cen[g] * lax.rsqrt(var[g] + RWKV_GN_EPS) * lnx_w[:, gs] + lnx_b[:, gs]
            bonus = so[ci, g] * at(v, (ci, g))
            o_ref[sls[ci], gs] = ((on + bonus) * gate[sls[ci], gs]).astype(o_ref.dtype)
    for g in groups:
        st_scr[g] = state_t[g]


def _rwkv(z, bsz, seq, mu, vecs, wup_pad, aup_pad, gup_pad):
    m = z.shape[0]
    tt = 256
    nt = seq // tt
    gw = RWKV_GROUP
    width = RWKV_WIDTH
    zblk = lambda c0: pl.BlockSpec((tt, width), lambda b, t: (b * nt + t, c0 // width))
    lora_col = (COL_RWKV + 3 * width) // RWKV_LORA
    const2 = lambda b, t: (0, 0)
    return pl.pallas_call(
        _rwkv_body,
        grid=(bsz, nt),
        in_specs=[
            zblk(COL_RWKV), zblk(COL_RWKV + width), zblk(COL_RWKV + 2 * width),
            pl.BlockSpec((tt, RWKV_LORA), lambda b, t: (b * nt + t, lora_col)),
            pl.BlockSpec(mu.shape, const2),
            pl.BlockSpec((8, width), const2),
            pl.BlockSpec((RWKV_LORA, width), const2),
            pl.BlockSpec((RWKV_LORA, width), const2),
            pl.BlockSpec((RWKV_LORA, width), const2),
        ],
        out_specs=pl.BlockSpec((tt, width), lambda b, t: (b * nt + t, 0)),
        out_shape=jax.ShapeDtypeStruct((m, width), BF16),
        scratch_shapes=[pltpu.VMEM((8, width), F32)] * 3 + [pltpu.VMEM((8, RWKV_LORA), F32),
                                                            pltpu.VMEM((width // gw, gw, gw), F32)],
        compiler_params=_cparams("parallel", "arbitrary"),
        name="rwkv7",
    )(z, z, z, z, mu, vecs, wup_pad, aup_pad, gup_pad)


def _merge_body(h_ref, ya_ref, yb_ref, yc_ref, za_ref, zb_ref, zc_ref, mb_ref, pa_ref, pb_ref,
                pc_ref, wo_ref, nw_ref, hout_ref, xnt_ref):
    mb = mb_ref[...]
    ga = jax.nn.sigmoid(za_ref[...].astype(F32) + mb[:, 0:D_MODEL])
    gb = jax.nn.sigmoid(zb_ref[...].astype(F32) + mb[:, D_MODEL:2 * D_MODEL])
    gc = jax.nn.sigmoid(zc_ref[...].astype(F32) + mb[:, 2 * D_MODEL:])
    merged = (ga * jnp.dot(ya_ref[...], pa_ref[...], preferred_element_type=F32)
              + gb * jnp.dot(yb_ref[...], pb_ref[...], preferred_element_type=F32)
              + gc * jnp.dot(yc_ref[...], pc_ref[...], preferred_element_type=F32))
    h = h_ref[...] + _mm(merged, wo_ref[...])
    hout_ref[...] = h
    xnt_ref[...] = jnp.transpose(_rms(h, nw_ref[...])).astype(BF16)


def _merge(h, z, ya, yb, yc, merge_b, pa, pb, pc, wo, norm_w):
    m = h.shape[0]
    tm = 256
    row = lambda c: pl.BlockSpec((tm, D_MODEL), lambda i: (i, c))
    wspec = pl.BlockSpec((D_MODEL, D_MODEL), lambda i: (0, 0))
    mc = COL_MERGE // D_MODEL
    return pl.pallas_call(
        _merge_body,
        grid=(m // tm,),
        in_specs=[row(0), row(0), row(0), row(0), row(mc), row(mc + 1), row(mc + 2),
                  pl.BlockSpec((1, 3 * D_MODEL), lambda i: (0, 0)),
                  wspec, wspec, wspec, wspec,
                  pl.BlockSpec((1, D_MODEL), lambda i: (0, 0))],
        out_specs=[row(0), pl.BlockSpec((D_MODEL, tm), lambda i: (0, i))],
        out_shape=[jax.ShapeDtypeStruct((m, D_MODEL), F32), jax.ShapeDtypeStruct((D_MODEL, m), BF16)],
        compiler_params=_cparams("parallel"),
        name="merge_outproj",
    )(h, ya, yb, yc, z, z, z, merge_b, pa, pb, pc, wo, norm_w)


_CAND_SLABS = (
    (16, ((0, 0, 16),)),
    (16, ((1, 0, 8), (2, 8, 5), (4, 13, 3))),
    (16, ((3, 0, 4), (5, 4, 2), (6, 6, 2), (7, 8, 2), (8, 10, 1), (9, 11, 1), (10, 12, 1), (11, 13, 1),
          (12, 14, 1), (13, 15, 1))),
    (8, ((14, 0, 1), (15, 1, 1))),
)


def _odd_even_merge_sort_pairs(n):
    pairs = []
    p = 1
    while p < n:
        k = p
        while k >= 1:
            for j in range(k % p, n - k, 2 * k):
                for i in range(min(k, n - j - k)):
                    if (i + j) // (2 * p) == (i + j + k) // (2 * p):
                        pairs.append((i + j, i + j + k))
            k //= 2
        p *= 2
    return tuple(pairs)


_SORT_PAIRS = _odd_even_merge_sort_pairs(PEER_TOPK)


def _candidate_sums(a1, a2):
    slabs = []
    for nrows, pieces in _CAND_SLABS:
        base = a2[:nrows]
        rows = lax.broadcasted_iota(jnp.int32, base.shape, 0)
        out = None
        end = 0
        for p, off, cnt in pieces:
            val = a1[p:p + 1] + (base if off == 0 else pltpu.roll(base, off, 0))
            out = val if out is None else jnp.where(rows >= off, val, out)
            end = off + cnt
        if end < nrows:
            out = jnp.where(rows >= end, -BIG, out)
        slabs.append(out)
    return jnp.concatenate(slabs, axis=0)


def _peer_topk_body(xnt_ref, wqt_ref, keys_ref, n_ref, e1_ref, r2_ref, e2_ref, q_scr):
    tt = xnt_ref.shape[1]
    nk = PEER_NKEYS
    topk = PEER_TOPK
    q_scr[...] = jnp.dot(wqt_ref[...], xnt_ref[...], preferred_element_type=F32)
    lanes = 128
    sub = 8
    assert nk == topk * sub
    row8 = lax.broadcasted_iota(jnp.int32, (sub, lanes), 0)

    def top_sorted(s):
        v = [s[k * sub:(k + 1) * sub] for k in range(topk)]
        for i, j in _SORT_PAIRS:
            v[i], v[j] = jnp.maximum(v[i], v[j]), jnp.minimum(v[i], v[j])
        for shift in (4, 2, 1):
            other = [pltpu.roll(x, shift, 0) for x in v]
            v = [jnp.maximum(v[k], other[topk - 1 - k]) for k in range(topk)]
            d = topk // 2
            while d >= 1:
                for i in range(topk):
                    if i & d == 0:
                        v[i], v[i + d] = jnp.maximum(v[i], v[i + d]), jnp.minimum(v[i], v[i + d])
                d //= 2
        return v

    def compact(v):
        tiles = []
        for t0 in range(0, topk, sub):
            out = v[t0]
            for k in range(1, sub):
                out = jnp.where(row8 == k, v[t0 + k], out)
            tiles.append(out)
        return jnp.concatenate(tiles, axis=0)

    def head(h, carry):
        o1 = pl.multiple_of(h * (2 * nk), 2 * nk)
        s1 = _mm(keys_ref[2 * h], q_scr[pl.ds(o1, nk), :])
        s2 = _mm(keys_ref[2 * h + 1], q_scr[pl.ds(o1 + nk, nk), :])
        chunks = [slice(l0, l0 + lanes) for l0 in range(0, tt, lanes)]
        a1 = [top_sorted(s1[:, ls]) for ls in chunks]
        a2 = [top_sorted(s2[:, ls]) for ls in chunks]
        a1c = jnp.concatenate([compact(v) for v in a1], axis=1)
        a2c = jnp.concatenate([compact(v) for v in a2], axis=1)
        cand = _candidate_sums(a1c, a2c)
        cmax = a1c[0:1] + a2c[0:1]

        def cbody(rnd, carry):
            cnd, zsum, c_in, c_out = carry
            mx = jnp.max(cnd, axis=0, keepdims=True)
            zsum = zsum + jnp.where(rnd < topk, jnp.exp(mx - cmax), 0.0)
            c_in = jnp.where(rnd == topk - 1, mx, c_in)
            c_out = jnp.where(rnd == topk, mx, c_out)
            return jnp.where(cnd == mx, -BIG, cnd), zsum, c_in, c_out

        zero = jnp.zeros((1, tt), F32)
        _, zsum, c_in, c_out = lax.fori_loop(0, topk + 1, cbody, (cand, zero, zero, zero))
        tau = 0.5 * (c_in + c_out)
        inv_z = 1.0 / zsum
        pack = 16

        def rep(x):
            return jnp.concatenate([x] * (pack // sub), axis=0)

        for ci, ls in enumerate(chunks):
            tau8 = jnp.broadcast_to(tau[:, ls], (sub, lanes))
            need = [rep(tau8 - a2[ci][qq]) for qq in range(topk)]
            val2 = [rep(a2[ci][qq]) for qq in range(topk)]
            last1 = rep(a1[ci][topk - 1])
            top1 = rep(a1[ci][0])
            top2 = val2[0]
            for k in range(nk // pack):
                rows = slice(k * pack, (k + 1) * pack)
                x1 = s1[rows, ls]
                x2 = s2[rows, ls]
                n_sel = jnp.zeros((pack, lanes), F32)
                r2 = jnp.zeros((pack, lanes), F32)
                for qq in range(topk):
                    n_sel = jnp.where(x1 >= need[qq], n_sel + 1.0, n_sel)
                    r2 = jnp.where(x2 < val2[qq], r2 + 1.0, r2)
                n_ref[h, rows, ls] = jnp.where(x1 >= last1, n_sel, 0.0)
                e1_ref[h, rows, ls] = jnp.exp(x1 - top1)
                r2_ref[h, rows, ls] = r2.astype(BF16)
                e2_ref[h, rows, ls] = (jnp.exp(x2 - top2) * inv_z[:, ls]).astype(BF16)
        return carry

    lax.fori_loop(0, PEER_HEADS, head, 0)


def _peer_topk(xnt, wqt, keys):
    m = xnt.shape[1]
    tt = 512 if m % 512 == 0 else 256
    shape = (PEER_HEADS, PEER_NKEYS, m)
    ospec = pl.BlockSpec((PEER_HEADS, PEER_NKEYS, tt), lambda i: (0, 0, i))
    return pl.pallas_call(
        _peer_topk_body,
        grid=(m // tt,),
        in_specs=[pl.BlockSpec((D_MODEL, tt), lambda i: (0, i)),
                  pl.BlockSpec(wqt.shape, lambda i: (0, 0)),
                  pl.BlockSpec(keys.shape, lambda i: (0, 0, 0))],
        out_specs=[ospec] * 4,
        out_shape=[jax.ShapeDtypeStruct(shape, F32), jax.ShapeDtypeStruct(shape, F32),
                   jax.ShapeDtypeStruct(shape, BF16), jax.ShapeDtypeStruct(shape, BF16)],
        scratch_shapes=[pltpu.VMEM((wqt.shape[0], tt), F32)],
        compiler_params=_cparams("parallel"),
        name="peer_topk",
    )(xnt, wqt, keys)


PEER_TOKEN_CHUNK = 256


def _peer_dense_body(u_ref, xnt_ref, vt_ref, n_ref, e1_ref, r2_ref, e2_ref, y_ref, acc_scr):
    nk = PEER_NKEYS
    tt = xnt_ref.shape[1]
    tc = PEER_TOKEN_CHUNK
    n_blk = u_ref.shape[0] // nk
    pack = 16

    @pl.when(pl.program_id(1) == 0)
    def _():
        acc_scr[...] = jnp.zeros_like(acc_scr)

    def rows_bf16(ref, h, ii, ls):
        row = jnp.broadcast_to(ref[h, ii:ii + 1, ls], (pack, tc)).astype(BF16)
        return jnp.concatenate([row] * (nk // pack), axis=0)

    def select_weights(ii, c0):
        ls = slice(c0, c0 + tc)
        w = None
        for h in range(PEER_HEADS):
            sel = jnp.where(r2_ref[h, :, ls] < rows_bf16(n_ref, h, ii, ls), e2_ref[h, :, ls],
                            jnp.zeros((), BF16))
            term = sel * rows_bf16(e1_ref, h, ii, ls)
            w = term if w is None else w + term
        return w

    def gelu_tanh(x):
        c = 0.7978845608028654
        inner = x * ((x * x) * (c * 0.044715) + c)
        hx = 0.5 * x
        return hx * jnp.tanh(inner) + hx

    starts = list(range(0, tt, tc))
    pre = [jnp.dot(u_ref[...], xnt_ref[:, c0:c0 + tc], preferred_element_type=F32) for c0 in starts]
    for idx, c0 in enumerate(starts):
        act = gelu_tanh(pre[idx].astype(BF16))
        a = jnp.concatenate([act[ii * nk:(ii + 1) * nk] * select_weights(ii, c0) for ii in range(n_blk)], axis=0)
        acc_scr[:, c0:c0 + tc] += jnp.dot(vt_ref[...], a, preferred_element_type=F32)

    @pl.when(pl.program_id(1) == pl.num_programs(1) - 1)
    def _():
        y_ref[...] = jnp.transpose(acc_scr[...])


def _peer_dense(xnt, u_bf, vt_bf, n_sel, e1, r2, e2, tt):
    m = xnt.shape[1]
    te = 2048
    ib = te // PEER_NKEYS
    sel_i = pl.BlockSpec((PEER_HEADS, ib, tt), lambda i, e: (0, e, i))
    sel_all = pl.BlockSpec((PEER_HEADS, PEER_NKEYS, tt), lambda i, e: (0, 0, i))
    return pl.pallas_call(
        _peer_dense_body,
        grid=(m // tt, PEER_EXPERTS // te),
        in_specs=[pl.BlockSpec((te, D_MODEL), lambda i, e: (e, 0)),
                  pl.BlockSpec((D_MODEL, tt), lambda i, e: (0, i)),
                  pl.BlockSpec((D_MODEL, te), lambda i, e: (0, e)),
                  sel_i, sel_i, sel_all, sel_all],
        out_specs=pl.BlockSpec((tt, D_MODEL), lambda i, e: (i, 0)),
        out_shape=jax.ShapeDtypeStruct((m, D_MODEL), F32),
        scratch_shapes=[pltpu.VMEM((D_MODEL, tt), F32)],
        compiler_params=_cparams("parallel", "arbitrary"),
        name="peer_dense",
    )(u_bf, xnt, vt_bf, n_sel, e1, r2, e2)


def _final_body(h_ref, y_ref, nw_ref, o_ref):
    o_ref[...] = _rms(h_ref[...] + y_ref[...], nw_ref[...])


def _final(h, y, norm_w):
    m = h.shape[0]
    tm = 512
    row = pl.BlockSpec((tm, D_MODEL), lambda i: (i, 0))
    return pl.pallas_call(
        _final_body,
        grid=(m // tm,),
        in_specs=[row, row, pl.BlockSpec((1, D_MODEL), lambda i: (0, 0))],
        out_specs=row,
        out_shape=jax.ShapeDtypeStruct((m, D_MODEL), F32),
        compiler_params=_cparams("parallel"),
        name="final_norm",
    )(h, y, norm_w.reshape(1, D_MODEL))


def _pad_rows(rows, width, n_rows=8):
    out = jnp.zeros((n_rows, width), F32)
    for i, r in enumerate(rows):
        flat = r.reshape(-1).astype(F32)
        out = out.at[i, :flat.shape[0]].set(flat)
    return out


def _chunk_rows(ab):
    m, n = ab.shape
    return jnp.transpose(ab.reshape(m // GDN_CHUNK, GDN_CHUNK, n), (0, 2, 1)).astype(F32)


def _block_diag_groups(w):
    per = LRU_GROUP // LRU_BLOCK_DIM
    n_grp = w.shape[0] // per
    out = jnp.zeros((n_grp, LRU_GROUP, LRU_GROUP), F32)
    for g in range(n_grp):
        for j in range(per):
            o = j * LRU_BLOCK_DIM
            out = out.at[g, o:o + LRU_BLOCK_DIM, o:o + LRU_BLOCK_DIM].set(w[g * per + j])
    return out.astype(BF16)


def _prep_w_in(w):
    pad = jnp.zeros((D_MODEL, IN_WIDTH_PAD - COL_AB - 2 * GDN_HEADS), w.dtype)
    return jnp.concatenate([w[:, :6144], w[:, 9488:12560], w[:, 6160:9488], w[:, 6144:6160], pad],
                           axis=1).astype(BF16)


def _layer(l, h, y, bsz, seq, p):
    z_h, z = _inproj(h, y, p["norm_mix_w"][l], _prep_w_in(p["w_in"][l]))
    h = z_h

    lru_vecs = _pad_rows([p["lru_conv_b"][l], p["lru_b_a"][l], p["lru_b_x"][l], p["lru_lambda"][l]], LRU_WIDTH)
    ya = _lru(z, bsz, seq, p["lru_conv_w"][l], lru_vecs,
              _block_diag_groups(p["lru_w_a"][l]), _block_diag_groups(p["lru_w_x"][l]))

    abt = _chunk_rows(z[:, COL_AB:COL_AB + 2 * GDN_HEADS])
    gp = _pad_rows([p["gdn_a_log"][l], p["gdn_dt_bias"][l]], 128)
    gpt = jnp.zeros((8, 128), F32).at[:, 0].set(p["gdn_a_log"][l]).at[:, 1].set(p["gdn_dt_bias"][l])
    yb = _gdn(z, abt, bsz, seq, p["gdn_conv_w"][l], gp, gpt, p["gdn_norm_w"][l].reshape(1, GDN_DV))

    rw_vecs = _pad_rows([p["rwkv_w0"][l], p["rwkv_a0"][l], p["rwkv_k_k"][l], p["rwkv_k_a"][l],
                         p["rwkv_r_k"][l], p["rwkv_lnx_w"][l], p["rwkv_lnx_b"][l]], RWKV_WIDTH)
    zl = jnp.zeros((RWKV_LORA, RWKV_WIDTH), F32)
    wup = zl.at[0:64].set(p["rwkv_w_up"][l]).astype(BF16)
    aup = zl.at[64:128].set(p["rwkv_a_up"][l]).astype(BF16)
    gup = zl.at[128:256].set(p["rwkv_g_up"][l]).astype(BF16)
    yc = _rwkv(z, bsz, seq, p["rwkv_mu"][l].reshape(1, -1), rw_vecs, wup, aup, gup)

    h, xnt = _merge(h, z, ya, yb, yc, p["merge_b"][l].reshape(1, -1), p["p_lru"][l].astype(BF16),
                    p["p_gdn"][l].astype(BF16), p["p_rwkv"][l].astype(BF16), p["w_out"][l].astype(BF16),
                    p["norm_ffn_w"][l].reshape(1, D_MODEL))

    wqt = jnp.transpose(p["peer_wq"][l]).astype(BF16)
    keys = p["peer_keys"][l].reshape(2 * PEER_HEADS, PEER_NKEYS, PEER_HALF).astype(BF16)
    n_sel, e1, r2, e2 = _peer_topk(xnt, wqt, keys)
    tt = 512 if xnt.shape[1] % 512 == 0 else 256
    y = _peer_dense(xnt, p["peer_u"][l].astype(BF16), jnp.transpose(p["peer_v"][l]).astype(BF16),
                    n_sel, e1, r2, e2, tt)
    return h, y


def kernel(x, norm_mix_w, norm_ffn_w, final_norm_w, w_in, lru_conv_w, lru_conv_b, lru_w_a, lru_b_a,
           lru_w_x, lru_b_x, lru_lambda, gdn_conv_w, gdn_a_log, gdn_dt_bias, gdn_norm_w, rwkv_mu, rwkv_w0,
           rwkv_w_up, rwkv_a0, rwkv_a_up, rwkv_g_up, rwkv_k_k, rwkv_k_a, rwkv_r_k, rwkv_lnx_w, rwkv_lnx_b,
           merge_b, p_lru, p_gdn, p_rwkv, w_out, peer_wq, peer_keys, peer_u, peer_v):
    p = dict(locals())
    bsz, seq, dim = x.shape
    h = x.reshape(bsz * seq, dim)
    y = None
    for l in range(DEPTH):
        h, y = _layer(l, h, y, bsz, seq, p)
    return _final(h, y, final_norm_w).reshape(bsz, seq, dim)
```

```python
import functools

import jax
import jax.numpy as jnp
from jax import lax
from jax.experimental import pallas as pl
from jax.experimental.pallas import tpu as pltpu

F32 = jnp.float32
BF16 = jnp.bfloat16

D_MODEL = 1024
DEPTH = 2
RMS_EPS = 1e-6

LRU_WIDTH = 1024
LRU_BLOCK_DIM = 64
LRU_C = 8.0
LRU_GROUP = 256

GDN_HEADS = 8
GDN_DK = 128
GDN_DV = 128
GDN_CHUNK = 64

RWKV_HEADS = 16
RWKV_HD = 64
RWKV_WIDTH = 1024
RWKV_GN_EPS = 64e-5
RWKV_CHUNK = 64
RWKV_GROUP = 256
RWKV_LORA = 256

PEER_HEADS = 8
PEER_NKEYS = 128
PEER_EXPERTS = PEER_NKEYS * PEER_NKEYS
PEER_HALF = 128
PEER_TOPK = 16

COL_U = 0
COL_GATE = 1024
COL_QKV = 2048
COL_ZG = 5120
COL_MERGE = 6144
COL_RWKV = 9216
COL_AB = 12544
IN_WIDTH_PAD = 12672

VMEM_LIMIT = 48 * 1024 * 1024
BIG = 3.0e38


def _cparams(*sem):
    return pltpu.CompilerParams(dimension_semantics=sem, vmem_limit_bytes=VMEM_LIMIT)


def _mm(a, b):
    return jnp.dot(a.astype(BF16), b.astype(BF16), preferred_element_type=F32)


def _mm_nt(a, b):
    return lax.dot_general(a.astype(BF16), b.astype(BF16), (((1,), (1,)), ((), ())),
                           preferred_element_type=F32)


def _mm_tn(a, b):
    return lax.dot_general(a.astype(BF16), b.astype(BF16), (((0,), (0,)), ((), ())),
                           preferred_element_type=F32)


def _softplus(x):
    return jnp.maximum(x, 0.0) + jnp.log1p(jnp.exp(-jnp.abs(x)))


def _rms(x, w):
    return x * lax.rsqrt(jnp.mean(x * x, axis=-1, keepdims=True) + RMS_EPS) * w


def _shift_prev(x, prev8, s):
    r = pltpu.roll(x, s, 0)
    pr = pltpu.roll(prev8, s, 0)
    rows8 = lax.broadcasted_iota(jnp.int32, prev8.shape, 0)
    head = jnp.where(rows8 < s, pr, r[:8])
    return jnp.concatenate([head, r[8:]], axis=0)


def _shift_fill(x, d, fill):
    n, c = x.shape
    if d % 8 == 0:
        return jnp.concatenate([jnp.full((d, c), fill, x.dtype), x[:n - d]], axis=0)
    r = pltpu.roll(x, d, 0)
    rows8 = lax.broadcasted_iota(jnp.int32, (8, c), 0)
    head = jnp.where(rows8 < d, fill, r[:8])
    return jnp.concatenate([head, r[8:]], axis=0)


def _tri(n, strict):
    i = lax.broadcasted_iota(jnp.int32, (n, n), 0)
    j = lax.broadcasted_iota(jnp.int32, (n, n), 1)
    return (j < i) if strict else (j <= i)


def _split_bf16(x, terms):
    parts = []
    rem = x
    for i in range(terms):
        p = rem.astype(BF16)
        parts.append(p)
        if i + 1 < terms:
            rem = rem - p.astype(F32)
    return parts


def _mask_mm(mask_bf, x, terms=3):
    return sum(jnp.dot(mask_bf, p, preferred_element_type=F32) for p in _split_bf16(x, terms))


def _mm_mask(x, mask_bf, terms=3):
    return sum(jnp.dot(p, mask_bf, preferred_element_type=F32) for p in _split_bf16(x, terms))


def _inproj_body(add_y, *refs):
    if add_y:
        h_ref, y_ref, nw_ref, w_ref, hout_ref, z_ref, ab_ref, xn_scr = refs
    else:
        h_ref, nw_ref, w_ref, z_ref, ab_ref, xn_scr = refs

    @pl.when(pl.program_id(1) == 0)
    def _():
        h = h_ref[...]
        if add_y:
            h = h + y_ref[...]
            hout_ref[...] = h
        xn_scr[...] = _rms(h, nw_ref[...]).astype(BF16)

    res = jnp.dot(xn_scr[...], w_ref[...], preferred_element_type=F32)
    z_ref[...] = res.astype(z_ref.dtype)

    @pl.when(pl.program_id(1) == pl.num_programs(1) - 1)
    def _():
        ab_ref[...] = res[:, res.shape[1] - ab_ref.shape[1]:]


def _inproj(h, y, norm_w, w_pad):
    m = h.shape[0]
    tm, tn = (1024 if m % 1024 == 0 else 512), 1152
    add_y = y is not None
    row = pl.BlockSpec((tm, D_MODEL), lambda i, j: (i, 0))
    in_specs = [row] + ([row] if add_y else []) + [
        pl.BlockSpec((1, D_MODEL), lambda i, j: (0, 0)),
        pl.BlockSpec((D_MODEL, tn), lambda i, j: (0, j)),
    ]
    assert IN_WIDTH_PAD - COL_AB == 128 and IN_WIDTH_PAD % tn == 0
    z_spec = pl.BlockSpec((tm, tn), lambda i, j: (i, j))
    z_shape = jax.ShapeDtypeStruct((m, IN_WIDTH_PAD), BF16)
    ab_spec = pl.BlockSpec((tm, 128), lambda i, j: (i, 0))
    ab_shape = jax.ShapeDtypeStruct((m, 128), F32)
    args = (h,) + ((y,) if add_y else ()) + (norm_w.reshape(1, D_MODEL), w_pad)
    out = pl.pallas_call(
        functools.partial(_inproj_body, add_y),
        grid=(m // tm, IN_WIDTH_PAD // tn),
        in_specs=in_specs,
        out_specs=([row] if add_y else []) + [z_spec, ab_spec],
        out_shape=([jax.ShapeDtypeStruct((m, D_MODEL), F32)] if add_y else []) + [z_shape, ab_shape],
        scratch_shapes=[pltpu.VMEM((tm, D_MODEL), BF16)],
        compiler_params=_cparams("parallel", "arbitrary"),
        name="norm_inproj",
    )(*args)
    return tuple(out) if add_y else (h,) + tuple(out)


def _lru_body(zu_ref, zg_ref, cw_ref, vp_ref, wa_ref, wx_ref, o_ref, prev_scr, hc_scr):
    tt = zu_ref.shape[0]

    @pl.when(pl.program_id(1) == 0)
    def _():
        prev_scr[...] = jnp.zeros_like(prev_scr)
        hc_scr[...] = jnp.zeros_like(hc_scr)

    u = zu_ref[...].astype(F32)
    prev = prev_scr[...]
    cw = cw_ref[...]
    vp = vp_ref[...]
    xc = (cw[3:4] * u + cw[2:3] * _shift_prev(u, prev, 1) + cw[1:2] * _shift_prev(u, prev, 2)
          + cw[0:1] * _shift_prev(u, prev, 3) + vp[0:1])
    prev_scr[...] = u[tt - 8:]

    xcb = xc.astype(BF16)
    n_grp = LRU_WIDTH // LRU_GROUP
    pre_a = jnp.concatenate(
        [jnp.dot(xcb[:, g * LRU_GROUP:(g + 1) * LRU_GROUP], wa_ref[g], preferred_element_type=F32)
         for g in range(n_grp)], axis=1)
    pre_x = jnp.concatenate(
        [jnp.dot(xcb[:, g * LRU_GROUP:(g + 1) * LRU_GROUP], wx_ref[g], preferred_element_type=F32)
         for g in range(n_grp)], axis=1)
    r = jax.nn.sigmoid(pre_a + vp[1:2])
    i = jax.nn.sigmoid(pre_x + vp[2:3])
    log_a = (-LRU_C * r) * _softplus(-vp[3:4])
    a = jnp.exp(log_a)
    b = jnp.sqrt(-jnp.tanh(log_a) * (a * a + 1.0)) * (i * xc)

    d = 1
    while d < tt:
        b = a * _shift_fill(b, d, 0.0) + b
        a = a * _shift_fill(a, d, 1.0)
        d *= 2
    hcar = hc_scr[...]
    hval = b + a * hcar[0:1]
    hc_scr[...] = jnp.broadcast_to(hval[tt - 1:tt], hcar.shape)
    o_ref[...] = (hval * jax.nn.gelu(zg_ref[...].astype(F32))).astype(o_ref.dtype)


def _lru(z, bsz, seq, conv_w, vecs, wa_bd, wx_bd):
    m = z.shape[0]
    tt = 256
    nt = seq // tt
    rowmap = lambda c: (lambda b, t: (b * nt + t, c))
    const2 = lambda b, t: (0, 0)
    return pl.pallas_call(
        _lru_body,
        grid=(bsz, nt),
        in_specs=[
            pl.BlockSpec((tt, LRU_WIDTH), rowmap(COL_U // LRU_WIDTH)),
            pl.BlockSpec((tt, LRU_WIDTH), rowmap(COL_GATE // LRU_WIDTH)),
            pl.BlockSpec((4, LRU_WIDTH), const2),
            pl.BlockSpec((8, LRU_WIDTH), const2),
            pl.BlockSpec(wa_bd.shape, lambda b, t: (0, 0, 0)),
            pl.BlockSpec(wx_bd.shape, lambda b, t: (0, 0, 0)),
        ],
        out_specs=pl.BlockSpec((tt, LRU_WIDTH), rowmap(0)),
        out_shape=jax.ShapeDtypeStruct((m, LRU_WIDTH), BF16),
        scratch_shapes=[pltpu.VMEM((8, LRU_WIDTH), F32), pltpu.VMEM((8, LRU_WIDTH), F32)],
        compiler_params=_cparams("parallel", "arbitrary"),
        name="rglru",
    )(z, z, conv_w, vecs, wa_bd, wx_bd)


def _gdn_body(q_ref, k_ref, v_ref, zg_ref, ab_ref, abt_ref, cw_ref, gp_ref, gpt_ref, nw_ref,
              o_ref, pq_scr, pk_scr, pv_scr, st_scr):
    tt = q_ref.shape[0]
    c = GDN_CHUNK
    width = GDN_HEADS * GDN_DK

    @pl.when(pl.program_id(1) == 0)
    def _():
        for s in (pq_scr, pk_scr, pv_scr, st_scr):
            s[...] = jnp.zeros_like(s)

    cw = cw_ref[...]

    def conv_silu(x_ref, p_scr, w):
        x = x_ref[...].astype(F32)
        prev = p_scr[...]
        y = (w[3:4] * x + w[2:3] * _shift_prev(x, prev, 1) + w[1:2] * _shift_prev(x, prev, 2)
             + w[0:1] * _shift_prev(x, prev, 3))
        p_scr[...] = x[tt - 8:]
        return jax.nn.silu(y)

    q = conv_silu(q_ref, pq_scr, cw[:, 0:width])
    k = conv_silu(k_ref, pk_scr, cw[:, width:2 * width])
    v = conv_silu(v_ref, pv_scr, cw[:, 2 * width:])

    ab = ab_ref[...].astype(F32)
    gp = gp_ref[...]
    gpt = gpt_ref[...]
    g_all = -jnp.exp(gp[0:1]) * _softplus(ab + gp[1:2])
    beta_all = jax.nn.sigmoid(ab)

    incl = _tri(c, False)
    strict = _tri(c, True)
    cum_mask = jnp.concatenate([incl.astype(BF16), jnp.ones((c, c), BF16)], axis=0)
    triu_bf = (lax.broadcasted_iota(jnp.int32, (c, c), 0) <= lax.broadcasted_iota(jnp.int32, (c, c), 1)).astype(BF16)
    nw = nw_ref[...]

    n_chunks = tt // c
    heads = range(GDN_HEADS)
    inst = [(ci, h) for ci in range(n_chunks) for h in heads]
    sls = [slice(ci * c, (ci + 1) * c) for ci in range(n_chunks)]
    hss = [slice(h * GDN_DK, (h + 1) * GDN_DK) for h in heads]
    gcs = [_mask_mm(cum_mask, g_all[sl]) for sl in sls]
    gr_all = [_mm_mask(-jnp.exp(gpt[:, 0:1]) * _softplus(abt_ref[ci][0:GDN_HEADS] + gpt[:, 1:2]), triu_bf)
              for ci in range(n_chunks)]

    qs, ks, kbs, decays, egcs, kdecs, gtots, bcols = {}, {}, {}, {}, {}, {}, {}, {}
    for ci, h in inst:
        sl, hs = sls[ci], hss[h]
        qc, kc = q[sl, hs], k[sl, hs]
        qs[ci, h] = qc * lax.rsqrt(jnp.sum(qc * qc, axis=-1, keepdims=True) + RMS_EPS) * (GDN_DK ** -0.5)
        ks[ci, h] = kc * lax.rsqrt(jnp.sum(kc * kc, axis=-1, keepdims=True) + RMS_EPS)
        gcol = gcs[ci][:c, h:h + 1]
        g_last = gcs[ci][c:, h:h + 1]
        grow = gr_all[ci][h:h + 1, :]
        bcols[ci, h] = beta_all[sl, GDN_HEADS + h:GDN_HEADS + h + 1]
        decays[ci, h] = jnp.exp(jnp.where(incl, gcol - grow, -BIG))
        kbs[ci, h] = ks[ci, h] * bcols[ci, h]
        egcs[ci, h] = jnp.exp(gcol)
        kdecs[ci, h] = ks[ci, h] * jnp.exp(g_last - gcol)
        gtots[ci, h] = jnp.exp(jnp.concatenate([g_last] * (GDN_DK // c), axis=0))

    kk = {i: _mm_nt(kbs[i], ks[i]) for i in inst}
    qk = {i: _mm_nt(qs[i], ks[i]) * decays[i] for i in inst}
    p = {i: -jnp.where(strict, kk[i] * decays[i], 0.0) for i in inst}
    tm1 = dict(p)
    for _ in range(5):
        p = {i: _mm(p[i], p[i]) for i in inst}
        tp = {i: _mm(tm1[i], p[i]) for i in inst}
        tm1 = {i: tm1[i] + tp[i] + p[i] for i in inst}
    rhs = {i: jnp.concatenate([v[sls[i[0]], hss[i[1]]] * bcols[i], kbs[i] * egcs[i]], axis=1) for i in inst}
    tr = {i: _mm(tm1[i], rhs[i]) for i in inst}
    sol = {i: rhs[i] + tr[i] for i in inst}

    state = [st_scr[h] for h in heads]
    for ci in range(n_chunks):
        ws = [_mm(sol[ci, h][:, GDN_DV:], state[h]) for h in heads]
        qst = [_mm(qs[ci, h] * egcs[ci, h], state[h]) for h in heads]
        v_new = [sol[ci, h][:, :GDN_DV] - ws[h] for h in heads]
        qv = [_mm(qk[ci, h], v_new[h]) for h in heads]
        kv = [_mm_tn(kdecs[ci, h], v_new[h]) for h in heads]
        state = [state[h] * gtots[ci, h] + kv[h] for h in heads]
        for h in heads:
            o = qst[h] + qv[h]
            o = o * lax.rsqrt(jnp.mean(o * o, axis=-1, keepdims=True) + RMS_EPS) * nw
            o = o * jax.nn.silu(zg_ref[sls[ci], hss[h]].astype(F32))
            o_ref[sls[ci], hss[h]] = o.astype(o_ref.dtype)
    for h in heads:
        st_scr[h] = state[h]


def _gdn(z, ab, abt, bsz, seq, conv_w, gp, gpt, norm_w):
    m = z.shape[0]
    tt = 256
    nt = seq // tt
    width = GDN_HEADS * GDN_DK
    zblk = lambda c0: pl.BlockSpec((tt, width), lambda b, t: (b * nt + t, c0 // width))
    const2 = lambda b, t: (0, 0)
    return pl.pallas_call(
        _gdn_body,
        grid=(bsz, nt),
        in_specs=[
            zblk(COL_QKV), zblk(COL_QKV + width), zblk(COL_QKV + 2 * width), zblk(COL_ZG),
            pl.BlockSpec((tt, 128), lambda b, t: (b * nt + t, 0)),
            pl.BlockSpec((tt // GDN_CHUNK, 2 * GDN_HEADS, GDN_CHUNK), lambda b, t: (b * nt + t, 0, 0)),
            pl.BlockSpec((4, 3 * width), const2),
            pl.BlockSpec((8, 128), const2),
            pl.BlockSpec((8, 128), const2),
            pl.BlockSpec((1, 128), const2),
        ],
        out_specs=pl.BlockSpec((tt, width), lambda b, t: (b * nt + t, 0)),
        out_shape=jax.ShapeDtypeStruct((m, width), BF16),
        scratch_shapes=[pltpu.VMEM((8, width), F32)] * 3 + [pltpu.VMEM((GDN_HEADS, GDN_DK, GDN_DV), F32)],
        compiler_params=_cparams("parallel", "arbitrary"),
        name="gated_deltanet",
    )(z, z, z, z, ab, abt, conv_w, gp, gpt, norm_w)


def _rwkv_body(r_ref, k_ref, v_ref, lo_ref, mu_ref, vp_ref, wup_ref, aup_ref, gup_ref, o_ref,
               pr_scr, pk_scr, pv_scr, plo_scr, st_scr):
    tt = r_ref.shape[0]
    c = RWKV_CHUNK
    gw = RWKV_GROUP
    hd = RWKV_HD
    width = RWKV_WIDTH

    @pl.when(pl.program_id(1) == 0)
    def _():
        for s in (pr_scr, pk_scr, pv_scr, plo_scr, st_scr):
            s[...] = jnp.zeros_like(s)

    mu = mu_ref[...]

    def tshift(x_ref, p_scr, m):
        x = x_ref[...].astype(F32)
        xs = x + m * (_shift_prev(x, p_scr[...], 1) - x)
        p_scr[...] = x[tt - 8:]
        return xs

    r = tshift(r_ref, pr_scr, mu[:, 0:width])
    k = tshift(k_ref, pk_scr, mu[:, width:2 * width])
    v = tshift(v_ref, pv_scr, mu[:, 2 * width:3 * width])
    lo = tshift(lo_ref, plo_scr, mu[:, 3 * width:])
    vp = vp_ref[...]
    w0, a0, k_k, k_a, r_k, lnx_w, lnx_b = (vp[i:i + 1] for i in range(7))

    lane = lax.broadcasted_iota(jnp.int32, lo.shape, 1)
    lo_act = jnp.where(lane < 64, jnp.tanh(lo), jnp.where(lane < 128, lo, jax.nn.sigmoid(lo)))
    w_pre = _mm(lo_act, wup_ref[...])
    a_pre = _mm(lo_act, aup_ref[...])
    gate = _mm(lo_act, gup_ref[...])
    w_log = -_softplus(-(w0 + w_pre)) - 0.5
    lw = -jnp.exp(w_log)
    a = jax.nn.sigmoid(a0 + a_pre)

    bi = lax.broadcasted_iota(jnp.int32, (gw, gw), 0) // hd
    bj = lax.broadcasted_iota(jnp.int32, (gw, gw), 1) // hd
    bdmask = bi == bj
    ones_bd = bdmask.astype(BF16)

    def head_sum(x):
        return _mm_mask(x, ones_bd, terms=2)

    def bd(x):
        return jnp.where(bdmask, jnp.concatenate([x] * (gw // c), axis=0), 0.0)

    ti = lax.broadcasted_iota(jnp.int32, (c, gw), 0)
    sj = lax.broadcasted_iota(jnp.int32, (c, gw), 1) % c
    strict = sj < ti
    incl = sj <= ti
    tril_bf = _tri(c, False).astype(BF16)

    kk_all = k * k_k
    k2_all = k * (1.0 + (a - 1.0) * k_a)

    n_chunks = tt // c
    groups = range(width // gw)
    inst = [(ci, g) for ci in range(n_chunks) for g in groups]
    sls = [slice(ci * c, (ci + 1) * c) for ci in range(n_chunks)]
    gss = [slice(g * gw, (g + 1) * gw) for g in groups]
    cl_all = [_mask_mm(tril_bf, lw[sl]) for sl in sls]

    def at(x, i):
        return x[sls[i[0]], gss[i[1]]]

    kk_raw = {i: at(kk_all, i) for i in inst}
    kk_ss = {i: head_sum(kk_raw[i] * kk_raw[i]) for i in inst}
    a_h, b_h, k_h, br, w_end = {}, {}, {}, {}, {}
    for i in inst:
        cl = cl_all[i[0]][:, gss[i[1]]]
        kk = kk_raw[i] * lax.rsqrt(kk_ss[i] + RMS_EPS)
        wcum = jnp.exp(cl)
        inv_w = jnp.exp(-cl)
        a_h[i] = kk * at(a, i) * inv_w
        b_h[i] = kk * jnp.exp(cl - at(lw, i))
        k_h[i] = at(k2_all, i) * inv_w
        br[i] = jnp.concatenate([b_h[i], at(r, i) * wcum], axis=0)
        w_end[i] = wcum[c - 1:c]
    v_bd = {i: bd(at(v, i)) for i in inst}
    xa = {i: _mm_nt(br[i], bd(a_h[i])) for i in inst}
    xk = {i: _mm_nt(br[i], bd(k_h[i])) for i in inst}
    l_k = {i: jnp.where(strict, xk[i][:c], 0.0) for i in inst}
    ra = {i: jnp.where(incl, xa[i][c:], 0.0) for i in inst}
    rk = {i: jnp.where(incl, xk[i][c:], 0.0) for i in inst}
    p = {i: -jnp.where(strict, xa[i][:c], 0.0) for i in inst}
    tm1 = dict(p)
    for _ in range(5):
        p = {i: _mm(p[i], bd(p[i])) for i in inst}
        tp = {i: _mm(tm1[i], bd(p[i])) for i in inst}
        tm1 = {i: tm1[i] + tp[i] + p[i] for i in inst}
    lkv = {i: _mm(l_k[i], v_bd[i]) for i in inst}
    rkv = {i: _mm(rk[i], v_bd[i]) for i in inst}
    so = {i: head_sum(at(r, i) * at(k2_all, i) * r_k[:, gss[i[1]]]) for i in inst}

    state_t = [st_scr[g] for g in groups]
    for ci in range(n_chunks):
        ids = [(ci, g) for g in groups]
        brh = [_mm_nt(br[i], state_t[i[1]]) for i in ids]
        rhs = [brh[g][:c] + lkv[ci, g] for g in groups]
        tu = [_mm(tm1[ci, g], bd(rhs[g])) for g in groups]
        u = [rhs[g] + tu[g] for g in groups]
        rau = [_mm(ra[ci, g], bd(u[g])) for g in groups]
        upd = [_mm_tn(jnp.concatenate([u[g], at(v, (ci, g))], axis=0),
                      jnp.concatenate([-(a_h[ci, g] * w_end[ci, g]), k_h[ci, g] * w_end[ci, g]], axis=0))
               for g in groups]
        state_t = [w_end[ci, g] * state_t[g] + jnp.where(bdmask, upd[g], 0.0) for g in groups]
        o = [brh[g][c:] - rau[g] + rkv[ci, g] for g in groups]
        osum = [head_sum(o[g]) for g in groups]
        cen = [o[g] - osum[g] * (1.0 / hd) for g in groups]
        var = [head_sum(cen[g] * cen[g]) * (1.0 / hd) for g in groups]
        for g in groups:
            gs = gss[g]
            on = cen[g] * lax.rsqrt(var[g] + RWKV_GN_EPS) * lnx_w[:, gs] + lnx_b[:, gs]
            bonus = so[ci, g] * at(v, (ci, g))
            o_ref[sls[ci], gs] = ((on + bonus) * gate[sls[ci], gs]).astype(o_ref.dtype)
    for g in groups:
        st_scr[g] = state_t[g]


def _rwkv(z, bsz, seq, mu, vecs, wup_pad, aup_pad, gup_pad):
    m = z.shape[0]
    tt = 256
    nt = seq // tt
    gw = RWKV_GROUP
    width = RWKV_WIDTH
    zblk = lambda c0: pl.BlockSpec((tt, width), lambda b, t: (b * nt + t, c0 // width))
    lora_col = (COL_RWKV + 3 * width) // RWKV_LORA
    const2 = lambda b, t: (0, 0)
    return pl.pallas_call(
        _rwkv_body,
        grid=(bsz, nt),
        in_specs=[
            zblk(COL_RWKV), zblk(COL_RWKV + width), zblk(COL_RWKV + 2 * width),
            pl.BlockSpec((tt, RWKV_LORA), lambda b, t: (b * nt + t, lora_col)),
            pl.BlockSpec(mu.shape, const2),
            pl.BlockSpec((8, width), const2),
            pl.BlockSpec((RWKV_LORA, width), const2),
            pl.BlockSpec((RWKV_LORA, width), const2),
            pl.BlockSpec((RWKV_LORA, width), const2),
        ],
        out_specs=pl.BlockSpec((tt, width), lambda b, t: (b * nt + t, 0)),
        out_shape=jax.ShapeDtypeStruct((m, width), BF16),
        scratch_shapes=[pltpu.VMEM((8, width), F32)] * 3 + [pltpu.VMEM((8, RWKV_LORA), F32),
                                                            pltpu.VMEM((width // gw, gw, gw), F32)],
        compiler_params=_cparams("parallel", "arbitrary"),
        name="rwkv7",
    )(z, z, z, z, mu, vecs, wup_pad, aup_pad, gup_pad)


def _merge_body(h_ref, ya_ref, yb_ref, yc_ref, za_ref, zb_ref, zc_ref, mb_ref, pa_ref, pb_ref,
                pc_ref, wo_ref, nw_ref, hout_ref, xnt_ref):
    mb = mb_ref[...]
    ga = jax.nn.sigmoid(za_ref[...].astype(F32) + mb[:, 0:D_MODEL])
    gb = jax.nn.sigmoid(zb_ref[...].astype(F32) + mb[:, D_MODEL:2 * D_MODEL])
    gc = jax.nn.sigmoid(zc_ref[...].astype(F32) + mb[:, 2 * D_MODEL:])
    merged = (ga * jnp.dot(ya_ref[...], pa_ref[...], preferred_element_type=F32)
              + gb * jnp.dot(yb_ref[...], pb_ref[...], preferred_element_type=F32)
              + gc * jnp.dot(yc_ref[...], pc_ref[...], preferred_element_type=F32))
    h = h_ref[...] + _mm(merged, wo_ref[...])
    hout_ref[...] = h
    xnt_ref[...] = jnp.transpose(_rms(h, nw_ref[...])).astype(BF16)


def _merge(h, z, ya, yb, yc, merge_b, pa, pb, pc, wo, norm_w):
    m = h.shape[0]
    tm = 256
    row = lambda c: pl.BlockSpec((tm, D_MODEL), lambda i: (i, c))
    wspec = pl.BlockSpec((D_MODEL, D_MODEL), lambda i: (0, 0))
    mc = COL_MERGE // D_MODEL
    return pl.pallas_call(
        _merge_body,
        grid=(m // tm,),
        in_specs=[row(0), row(0), row(0), row(0), row(mc), row(mc + 1), row(mc + 2),
                  pl.BlockSpec((1, 3 * D_MODEL), lambda i: (0, 0)),
                  wspec, wspec, wspec, wspec,
                  pl.BlockSpec((1, D_MODEL), lambda i: (0, 0))],
        out_specs=[row(0), pl.BlockSpec((D_MODEL, tm), lambda i: (0, i))],
        out_shape=[jax.ShapeDtypeStruct((m, D_MODEL), F32), jax.ShapeDtypeStruct((D_MODEL, m), BF16)],
        compiler_params=_cparams("parallel"),
        name="merge_outproj",
    )(h, ya, yb, yc, z, z, z, merge_b, pa, pb, pc, wo, norm_w)


_CAND_SLABS = (
    (16, ((0, 0, 16),)),
    (16, ((1, 0, 8), (2, 8, 5), (4, 13, 3))),
    (16, ((3, 0, 4), (5, 4, 2), (6, 6, 2), (7, 8, 2), (8, 10, 1), (9, 11, 1), (10, 12, 1), (11, 13, 1),
          (12, 14, 1), (13, 15, 1))),
    (8, ((14, 0, 1), (15, 1, 1))),
)


def _odd_even_merge_sort_pairs(n):
    pairs = []
    p = 1
    while p < n:
        k = p
        while k >= 1:
            for j in range(k % p, n - k, 2 * k):
                for i in range(min(k, n - j - k)):
                    if (i + j) // (2 * p) == (i + j + k) // (2 * p):
                        pairs.append((i + j, i + j + k))
            k //= 2
        p *= 2
    return tuple(pairs)


_SORT_PAIRS = _odd_even_merge_sort_pairs(PEER_TOPK)


def _candidate_sums(a1, a2):
    slabs = []
    for nrows, pieces in _CAND_SLABS:
        base = a2[:nrows]
        rows = lax.broadcasted_iota(jnp.int32, base.shape, 0)
        out = None
        end = 0
        for p, off, cnt in pieces:
            val = a1[p:p + 1] + (base if off == 0 else pltpu.roll(base, off, 0))
            out = val if out is None else jnp.where(rows >= off, val, out)
            end = off + cnt
        if end < nrows:
            out = jnp.where(rows >= end, -BIG, out)
        slabs.append(out)
    return jnp.concatenate(slabs, axis=0)


def _peer_topk_body(xnt_ref, wqt_ref, keys_ref, n_ref, e1_ref, r2_ref, e2_ref, q_scr):
    tt = xnt_ref.shape[1]
    nk = PEER_NKEYS
    topk = PEER_TOPK
    q_scr[...] = jnp.dot(wqt_ref[...], xnt_ref[...], preferred_element_type=F32)
    lanes = 128
    sub = 8
    assert nk == topk * sub
    row8 = lax.broadcasted_iota(jnp.int32, (sub, lanes), 0)

    def top_sorted(s):
        v = [s[k * sub:(k + 1) * sub] for k in range(topk)]
        for i, j in _SORT_PAIRS:
            v[i], v[j] = jnp.maximum(v[i], v[j]), jnp.minimum(v[i], v[j])
        for shift in (4, 2, 1):
            other = [pltpu.roll(x, shift, 0) for x in v]
            v = [jnp.maximum(v[k], other[topk - 1 - k]) for k in range(topk)]
            d = topk // 2
            while d >= 1:
                for i in range(topk):
                    if i & d == 0:
                        v[i], v[i + d] = jnp.maximum(v[i], v[i + d]), jnp.minimum(v[i], v[i + d])
                d //= 2
        return v

    def compact(v):
        tiles = []
        for t0 in range(0, topk, sub):
            out = v[t0]
            for k in range(1, sub):
                out = jnp.where(row8 == k, v[t0 + k], out)
            tiles.append(out)
        return jnp.concatenate(tiles, axis=0)

    def head(h, carry):
        o1 = pl.multiple_of(h * (2 * nk), 2 * nk)
        s1 = _mm(keys_ref[2 * h], q_scr[pl.ds(o1, nk), :])
        s2 = _mm(keys_ref[2 * h + 1], q_scr[pl.ds(o1 + nk, nk), :])
        chunks = [slice(l0, l0 + lanes) for l0 in range(0, tt, lanes)]
        a1 = [top_sorted(s1[:, ls]) for ls in chunks]
        a2 = [top_sorted(s2[:, ls]) for ls in chunks]
        a1c = jnp.concatenate([compact(v) for v in a1], axis=1)
        a2c = jnp.concatenate([compact(v) for v in a2], axis=1)
        cand = _candidate_sums(a1c, a2c)
        cmax = a1c[0:1] + a2c[0:1]

        def cbody(rnd, carry):
            cnd, zsum, c_in, c_out = carry
            mx = jnp.max(cnd, axis=0, keepdims=True)
            zsum = zsum + jnp.where(rnd < topk, jnp.exp(mx - cmax), 0.0)
            c_in = jnp.where(rnd == topk - 1, mx, c_in)
            c_out = jnp.where(rnd == topk, mx, c_out)
            return jnp.where(cnd == mx, -BIG, cnd), zsum, c_in, c_out

        zero = jnp.zeros((1, tt), F32)
        _, zsum, c_in, c_out = lax.fori_loop(0, topk + 1, cbody, (cand, zero, zero, zero))
        tau = 0.5 * (c_in + c_out)
        inv_z = 1.0 / zsum
        pack = 16

        def rep(x):
            return jnp.concatenate([x] * (pack // sub), axis=0)

        for ci, ls in enumerate(chunks):
            tau8 = jnp.broadcast_to(tau[:, ls], (sub, lanes))
            need = [rep(tau8 - a2[ci][qq]) for qq in range(topk)]
            val2 = [rep(a2[ci][qq]) for qq in range(topk)]
            last1 = rep(a1[ci][topk - 1])
            top1 = rep(a1[ci][0])
            top2 = val2[0]
            for k in range(nk // pack):
                rows = slice(k * pack, (k + 1) * pack)
                x1 = s1[rows, ls]
                x2 = s2[rows, ls]
                n_sel = jnp.zeros((pack, lanes), F32)
                r2 = jnp.zeros((pack, lanes), F32)
                for qq in range(topk):
                    n_sel = jnp.where(x1 >= need[qq], n_sel + 1.0, n_sel)
                    r2 = jnp.where(x2 < val2[qq], r2 + 1.0, r2)
                n_ref[h, rows, ls] = jnp.where(x1 >= last1, n_sel, 0.0)
                e1_ref[h, rows, ls] = jnp.exp(x1 - top1)
                r2_ref[h, rows, ls] = r2.astype(BF16)
                e2_ref[h, rows, ls] = (jnp.exp(x2 - top2) * inv_z[:, ls]).astype(BF16)
        return carry

    lax.fori_loop(0, PEER_HEADS, head, 0)


def _peer_topk(xnt, wqt, keys):
    m = xnt.shape[1]
    tt = 512 if m % 512 == 0 else 256
    shape = (PEER_HEADS, PEER_NKEYS, m)
    ospec = pl.BlockSpec((PEER_HEADS, PEER_NKEYS, tt), lambda i: (0, 0, i))
    return pl.pallas_call(
        _peer_topk_body,
        grid=(m // tt,),
        in_specs=[pl.BlockSpec((D_MODEL, tt), lambda i: (0, i)),
                  pl.BlockSpec(wqt.shape, lambda i: (0, 0)),
                  pl.BlockSpec(keys.shape, lambda i: (0, 0, 0))],
        out_specs=[ospec] * 4,
        out_shape=[jax.ShapeDtypeStruct(shape, F32), jax.ShapeDtypeStruct(shape, F32),
                   jax.ShapeDtypeStruct(shape, BF16), jax.ShapeDtypeStruct(shape, BF16)],
        scratch_shapes=[pltpu.VMEM((wqt.shape[0], tt), F32)],
        compiler_params=_cparams("parallel"),
        name="peer_topk",
    )(xnt, wqt, keys)


PEER_TOKEN_CHUNK = 256


def _peer_dense_body(u_ref, xnt_ref, vt_ref, n_ref, e1_ref, r2_ref, e2_ref, y_ref, acc_scr):
    nk = PEER_NKEYS
    tt = xnt_ref.shape[1]
    tc = PEER_TOKEN_CHUNK
    n_blk = u_ref.shape[0] // nk
    pack = 16

    @pl.when(pl.program_id(1) == 0)
    def _():
        acc_scr[...] = jnp.zeros_like(acc_scr)

    def rows_bf16(ref, h, ii, ls):
        row = jnp.broadcast_to(ref[h, ii:ii + 1, ls], (pack, tc)).astype(BF16)
        return jnp.concatenate([row] * (nk // pack), axis=0)

    def select_weights(ii, c0):
        ls = slice(c0, c0 + tc)
        w = None
        for h in range(PEER_HEADS):
            sel = jnp.where(r2_ref[h, :, ls] < rows_bf16(n_ref, h, ii, ls), e2_ref[h, :, ls],
                            jnp.zeros((), BF16))
            term = sel * rows_bf16(e1_ref, h, ii, ls)
            w = term if w is None else w + term
        return w

    def gelu_tanh(x):
        c = 0.7978845608028654
        inner = x * ((x * x) * (c * 0.044715) + c)
        hx = 0.5 * x
        return hx * jnp.tanh(inner) + hx

    starts = list(range(0, tt, tc))
    pre = [jnp.dot(u_ref[...], xnt_ref[:, c0:c0 + tc], preferred_element_type=F32) for c0 in starts]
    for idx, c0 in enumerate(starts):
        act = gelu_tanh(pre[idx].astype(BF16))
        a = jnp.concatenate([act[ii * nk:(ii + 1) * nk] * select_weights(ii, c0) for ii in range(n_blk)], axis=0)
        acc_scr[:, c0:c0 + tc] += jnp.dot(vt_ref[...], a, preferred_element_type=F32)

    @pl.when(pl.program_id(1) == pl.num_programs(1) - 1)
    def _():
        y_ref[...] = jnp.transpose(acc_scr[...])


def _peer_dense(xnt, u_bf, vt_bf, n_sel, e1, r2, e2, tt):
    m = xnt.shape[1]
    te = 2048
    ib = te // PEER_NKEYS
    sel_i = pl.BlockSpec((PEER_HEADS, ib, tt), lambda i, e: (0, e, i))
    sel_all = pl.BlockSpec((PEER_HEADS, PEER_NKEYS, tt), lambda i, e: (0, 0, i))
    return pl.pallas_call(
        _peer_dense_body,
        grid=(m // tt, PEER_EXPERTS // te),
        in_specs=[pl.BlockSpec((te, D_MODEL), lambda i, e: (e, 0)),
                  pl.BlockSpec((D_MODEL, tt), lambda i, e: (0, i)),
                  pl.BlockSpec((D_MODEL, te), lambda i, e: (0, e)),
                  sel_i, sel_i, sel_all, sel_all],
        out_specs=pl.BlockSpec((tt, D_MODEL), lambda i, e: (i, 0)),
        out_shape=jax.ShapeDtypeStruct((m, D_MODEL), F32),
        scratch_shapes=[pltpu.VMEM((D_MODEL, tt), F32)],
        compiler_params=_cparams("parallel", "arbitrary"),
        name="peer_dense",
    )(u_bf, xnt, vt_bf, n_sel, e1, r2, e2)


def _final_body(h_ref, y_ref, nw_ref, o_ref):
    o_ref[...] = _rms(h_ref[...] + y_ref[...], nw_ref[...])


def _final(h, y, norm_w):
    m = h.shape[0]
    tm = 512
    row = pl.BlockSpec((tm, D_MODEL), lambda i: (i, 0))
    return pl.pallas_call(
        _final_body,
        grid=(m // tm,),
        in_specs=[row, row, pl.BlockSpec((1, D_MODEL), lambda i: (0, 0))],
        out_specs=row,
        out_shape=jax.ShapeDtypeStruct((m, D_MODEL), F32),
        compiler_params=_cparams("parallel"),
        name="final_norm",
    )(h, y, norm_w.reshape(1, D_MODEL))


def _pad_rows(rows, width, n_rows=8):
    flat = [jnp.pad(r.reshape(-1).astype(F32), (0, width - r.size)) for r in rows]
    return jnp.pad(jnp.stack(flat), ((0, n_rows - len(rows)), (0, 0)))


def _chunk_rows(ab):
    m, n = ab.shape
    return jnp.transpose(ab.reshape(m // GDN_CHUNK, GDN_CHUNK, n), (0, 2, 1)).astype(F32)


def _block_diag_groups(w):
    per = LRU_GROUP // LRU_BLOCK_DIM
    n_grp = w.shape[0] // per
    wg = w.reshape(n_grp, per, LRU_BLOCK_DIM, LRU_BLOCK_DIM)
    eye = jnp.eye(per, dtype=w.dtype)
    out = wg[:, :, :, None, :] * eye[None, :, None, :, None]
    return out.reshape(n_grp, LRU_GROUP, LRU_GROUP).astype(BF16)


def _prep_w_in(w):
    pad = jnp.zeros((D_MODEL, IN_WIDTH_PAD - COL_AB - 2 * GDN_HEADS), w.dtype)
    return jnp.concatenate([w[:, :6144], w[:, 9488:12560], w[:, 6160:9488], w[:, 6144:6160], pad],
                           axis=1).astype(BF16)


def _layer(l, h, y, bsz, seq, p):
    h, z, ab = _inproj(h, y, p["norm_mix_w"][l], _prep_w_in(p["w_in"][l]))

    lru_vecs = _pad_rows([p["lru_conv_b"][l], p["lru_b_a"][l], p["lru_b_x"][l], p["lru_lambda"][l]], LRU_WIDTH)
    ya = _lru(z, bsz, seq, p["lru_conv_w"][l], lru_vecs,
              _block_diag_groups(p["lru_w_a"][l]), _block_diag_groups(p["lru_w_x"][l]))

    abt = _chunk_rows(ab[:, :2 * GDN_HEADS])
    gp = _pad_rows([p["gdn_a_log"][l], p["gdn_dt_bias"][l]], 128)
    gpt = jnp.pad(jnp.stack([p["gdn_a_log"][l], p["gdn_dt_bias"][l]], axis=1), ((0, 0), (0, 126)))
    yb = _gdn(z, ab, abt, bsz, seq, p["gdn_conv_w"][l], gp, gpt, p["gdn_norm_w"][l].reshape(1, GDN_DV))

    rw_vecs = _pad_rows([p["rwkv_w0"][l], p["rwkv_a0"][l], p["rwkv_k_k"][l], p["rwkv_k_a"][l],
                         p["rwkv_r_k"][l], p["rwkv_lnx_w"][l], p["rwkv_lnx_b"][l]], RWKV_WIDTH)
    wup = jnp.pad(p["rwkv_w_up"][l], ((0, 192), (0, 0))).astype(BF16)
    aup = jnp.pad(p["rwkv_a_up"][l], ((64, 128), (0, 0))).astype(BF16)
    gup = jnp.pad(p["rwkv_g_up"][l], ((128, 0), (0, 0))).astype(BF16)
    yc = _rwkv(z, bsz, seq, p["rwkv_mu"][l].reshape(1, -1), rw_vecs, wup, aup, gup)

    h, xnt = _merge(h, z, ya, yb, yc, p["merge_b"][l].reshape(1, -1), p["p_lru"][l].astype(BF16),
                    p["p_gdn"][l].astype(BF16), p["p_rwkv"][l].astype(BF16), p["w_out"][l].astype(BF16),
                    p["norm_ffn_w"][l].reshape(1, D_MODEL))

    wqt = jnp.transpose(p["peer_wq"][l]).astype(BF16)
    keys = p["peer_keys"][l].reshape(2 * PEER_HEADS, PEER_NKEYS, PEER_HALF).astype(BF16)
    n_sel, e1, r2, e2 = _peer_topk(xnt, wqt, keys)
    tt = 512 if xnt.shape[1] % 512 == 0 else 256
    y = _peer_dense(xnt, p["peer_u"][l].astype(BF16), jnp.transpose(p["peer_v"][l]).astype(BF16),
                    n_sel, e1, r2, e2, tt)
    return h, y


def kernel(x, norm_mix_w, norm_ffn_w, final_norm_w, w_in, lru_conv_w, lru_conv_b, lru_w_a, lru_b_a,
           lru_w_x, lru_b_x, lru_lambda, gdn_conv_w, gdn_a_log, gdn_dt_bias, gdn_norm_w, rwkv_mu, rwkv_w0,
           rwkv_w_up, rwkv_a0, rwkv_a_up, rwkv_g_up, rwkv_k_k, rwkv_k_a, rwkv_r_k, rwkv_lnx_w, rwkv_lnx_b,
           merge_b, p_lru, p_gdn, p_rwkv, w_out, peer_wq, peer_keys, peer_u, peer_v):
    p = dict(locals())
    bsz, seq, dim = x.shape
    h = x.reshape(bsz * seq, dim)
    y = None
    for l in range(DEPTH):
        h, y = _layer(l, h, y, bsz, seq, p)
    return _final(h, y, final_norm_w).reshape(bsz, seq, dim)
```

```python
import functools

import jax
import jax.numpy as jnp
from jax import lax
from jax.experimental import pallas as pl
from jax.experimental.pallas import tpu as pltpu

F32 = jnp.float32
BF16 = jnp.bfloat16

D_MODEL = 1024
DEPTH = 2
RMS_EPS = 1e-6

LRU_WIDTH = 1024
LRU_BLOCK_DIM = 64
LRU_C = 8.0
LRU_GROUP = 256

GDN_HEADS = 8
GDN_DK = 128
GDN_DV = 128
GDN_CHUNK = 64

RWKV_HEADS = 16
RWKV_HD = 64
RWKV_WIDTH = 1024
RWKV_GN_EPS = 64e-5
RWKV_CHUNK = 64
RWKV_GROUP = 256
RWKV_LORA = 256

PEER_HEADS = 8
PEER_NKEYS = 128
PEER_EXPERTS = PEER_NKEYS * PEER_NKEYS
PEER_HALF = 128
PEER_TOPK = 16

COL_U = 0
COL_GATE = 1024
COL_QKV = 2048
COL_ZG = 5120
COL_MERGE = 6144
COL_RWKV = 9216
COL_AB = 12544
IN_WIDTH_PAD = 12672

VMEM_LIMIT = 48 * 1024 * 1024
BIG = 3.0e38


def _cparams(*sem):
    return pltpu.CompilerParams(dimension_semantics=sem, vmem_limit_bytes=VMEM_LIMIT)


def _mm(a, b):
    return jnp.dot(a.astype(BF16), b.astype(BF16), preferred_element_type=F32)


def _mm_nt(a, b):
    return lax.dot_general(a.astype(BF16), b.astype(BF16), (((1,), (1,)), ((), ())),
                           preferred_element_type=F32)


def _mm_tn(a, b):
    return lax.dot_general(a.astype(BF16), b.astype(BF16), (((0,), (0,)), ((), ())),
                           preferred_element_type=F32)


def _softplus(x):
    return jnp.maximum(x, 0.0) + jnp.log1p(jnp.exp(-jnp.abs(x)))


def _rms(x, w):
    return x * lax.rsqrt(jnp.mean(x * x, axis=-1, keepdims=True) + RMS_EPS) * w


def _shift_prev(x, prev8, s):
    r = pltpu.roll(x, s, 0)
    pr = pltpu.roll(prev8, s, 0)
    rows8 = lax.broadcasted_iota(jnp.int32, prev8.shape, 0)
    head = jnp.where(rows8 < s, pr, r[:8])
    return jnp.concatenate([head, r[8:]], axis=0)


def _shift_fill(x, d, fill):
    n, c = x.shape
    if d % 8 == 0:
        return jnp.concatenate([jnp.full((d, c), fill, x.dtype), x[:n - d]], axis=0)
    r = pltpu.roll(x, d, 0)
    rows8 = lax.broadcasted_iota(jnp.int32, (8, c), 0)
    head = jnp.where(rows8 < d, fill, r[:8])
    return jnp.concatenate([head, r[8:]], axis=0)


def _tri(n, strict):
    i = lax.broadcasted_iota(jnp.int32, (n, n), 0)
    j = lax.broadcasted_iota(jnp.int32, (n, n), 1)
    return (j < i) if strict else (j <= i)


def _split_bf16(x, terms):
    parts = []
    rem = x
    for i in range(terms):
        p = rem.astype(BF16)
        parts.append(p)
        if i + 1 < terms:
            rem = rem - p.astype(F32)
    return parts


def _mask_mm(mask_bf, x, terms=3):
    return sum(jnp.dot(mask_bf, p, preferred_element_type=F32) for p in _split_bf16(x, terms))


def _mm_mask(x, mask_bf, terms=3):
    return sum(jnp.dot(p, mask_bf, preferred_element_type=F32) for p in _split_bf16(x, terms))


def _inproj_body(add_y, *refs):
    if add_y:
        h_ref, y_ref, nw_ref, w_ref, hout_ref, z_ref, ab_ref, xn_scr = refs
    else:
        h_ref, nw_ref, w_ref, z_ref, ab_ref, xn_scr = refs

    @pl.when(pl.program_id(1) == 0)
    def _():
        h = h_ref[...]
        if add_y:
            h = h + y_ref[...]
            hout_ref[...] = h
        xn_scr[...] = _rms(h, nw_ref[...]).astype(BF16)

    res = jnp.dot(xn_scr[...], w_ref[...], preferred_element_type=F32)
    z_ref[...] = res.astype(z_ref.dtype)

    @pl.when(pl.program_id(1) == pl.num_programs(1) - 1)
    def _():
        ab_ref[...] = res[:, res.shape[1] - ab_ref.shape[1]:]


def _inproj(h, y, norm_w, w_pad):
    m = h.shape[0]
    tm, tn = (1024 if m % 1024 == 0 else 512), 1152
    add_y = y is not None
    row = pl.BlockSpec((tm, D_MODEL), lambda i, j: (i, 0))
    in_specs = [row] + ([row] if add_y else []) + [
        pl.BlockSpec((1, D_MODEL), lambda i, j: (0, 0)),
        pl.BlockSpec((D_MODEL, tn), lambda i, j: (0, j)),
    ]
    assert IN_WIDTH_PAD - COL_AB == 128 and IN_WIDTH_PAD % tn == 0
    z_spec = pl.BlockSpec((tm, tn), lambda i, j: (i, j))
    z_shape = jax.ShapeDtypeStruct((m, IN_WIDTH_PAD), BF16)
    ab_spec = pl.BlockSpec((tm, 128), lambda i, j: (i, 0))
    ab_shape = jax.ShapeDtypeStruct((m, 128), F32)
    args = (h,) + ((y,) if add_y else ()) + (norm_w.reshape(1, D_MODEL), w_pad)
    out = pl.pallas_call(
        functools.partial(_inproj_body, add_y),
        grid=(m // tm, IN_WIDTH_PAD // tn),
        in_specs=in_specs,
        out_specs=([row] if add_y else []) + [z_spec, ab_spec],
        out_shape=([jax.ShapeDtypeStruct((m, D_MODEL), F32)] if add_y else []) + [z_shape, ab_shape],
        scratch_shapes=[pltpu.VMEM((tm, D_MODEL), BF16)],
        compiler_params=_cparams("parallel", "arbitrary"),
        name="norm_inproj",
    )(*args)
    return tuple(out) if add_y else (h,) + tuple(out)


def _lru_body(zu_ref, zg_ref, cw_ref, vp_ref, wa_ref, wx_ref, o_ref, prev_scr, hc_scr):
    tt = zu_ref.shape[0]

    @pl.when(pl.program_id(1) == 0)
    def _():
        prev_scr[...] = jnp.zeros_like(prev_scr)
        hc_scr[...] = jnp.zeros_like(hc_scr)

    u = zu_ref[...].astype(F32)
    prev = prev_scr[...]
    cw = cw_ref[...]
    vp = vp_ref[...]
    xc = (cw[3:4] * u + cw[2:3] * _shift_prev(u, prev, 1) + cw[1:2] * _shift_prev(u, prev, 2)
          + cw[0:1] * _shift_prev(u, prev, 3) + vp[0:1])
    prev_scr[...] = u[tt - 8:]

    xcb = xc.astype(BF16)
    n_grp = LRU_WIDTH // LRU_GROUP
    pre_a = jnp.concatenate(
        [jnp.dot(xcb[:, g * LRU_GROUP:(g + 1) * LRU_GROUP], wa_ref[g], preferred_element_type=F32)
         for g in range(n_grp)], axis=1)
    pre_x = jnp.concatenate(
        [jnp.dot(xcb[:, g * LRU_GROUP:(g + 1) * LRU_GROUP], wx_ref[g], preferred_element_type=F32)
         for g in range(n_grp)], axis=1)
    r = jax.nn.sigmoid(pre_a + vp[1:2])
    i = jax.nn.sigmoid(pre_x + vp[2:3])
    log_a = (-LRU_C * r) * _softplus(-vp[3:4])
    a = jnp.exp(log_a)
    b = jnp.sqrt(-jnp.tanh(log_a) * (a * a + 1.0)) * (i * xc)

    d = 1
    while d < tt:
        b = a * _shift_fill(b, d, 0.0) + b
        a = a * _shift_fill(a, d, 1.0)
        d *= 2
    hcar = hc_scr[...]
    hval = b + a * hcar[0:1]
    hc_scr[...] = jnp.broadcast_to(hval[tt - 1:tt], hcar.shape)
    o_ref[...] = (hval * jax.nn.gelu(zg_ref[...].astype(F32))).astype(o_ref.dtype)


def _lru(z, bsz, seq, conv_w, vecs, wa_bd, wx_bd):
    m = z.shape[0]
    tt = 256
    nt = seq // tt
    rowmap = lambda c: (lambda b, t: (b * nt + t, c))
    const2 = lambda b, t: (0, 0)
    return pl.pallas_call(
        _lru_body,
        grid=(bsz, nt),
        in_specs=[
            pl.BlockSpec((tt, LRU_WIDTH), rowmap(COL_U // LRU_WIDTH)),
            pl.BlockSpec((tt, LRU_WIDTH), rowmap(COL_GATE // LRU_WIDTH)),
            pl.BlockSpec((4, LRU_WIDTH), const2),
            pl.BlockSpec((8, LRU_WIDTH), const2),
            pl.BlockSpec(wa_bd.shape, lambda b, t: (0, 0, 0)),
            pl.BlockSpec(wx_bd.shape, lambda b, t: (0, 0, 0)),
        ],
        out_specs=pl.BlockSpec((tt, LRU_WIDTH), rowmap(0)),
        out_shape=jax.ShapeDtypeStruct((m, LRU_WIDTH), BF16),
        scratch_shapes=[pltpu.VMEM((8, LRU_WIDTH), F32), pltpu.VMEM((8, LRU_WIDTH), F32)],
        compiler_params=_cparams("parallel", "arbitrary"),
        name="rglru",
    )(z, z, conv_w, vecs, wa_bd, wx_bd)


def _gdn_body(q_ref, k_ref, v_ref, zg_ref, ab_ref, abt_ref, cw_ref, gp_ref, gpt_ref, nw_ref,
              o_ref, pq_scr, pk_scr, pv_scr, st_scr):
    tt = q_ref.shape[0]
    c = GDN_CHUNK
    width = GDN_HEADS * GDN_DK

    @pl.when(pl.program_id(1) == 0)
    def _():
        for s in (pq_scr, pk_scr, pv_scr, st_scr):
            s[...] = jnp.zeros_like(s)

    cw = cw_ref[...]

    def conv_silu(x_ref, p_scr, w):
        x = x_ref[...].astype(F32)
        prev = p_scr[...]
        y = (w[3:4] * x + w[2:3] * _shift_prev(x, prev, 1) + w[1:2] * _shift_prev(x, prev, 2)
             + w[0:1] * _shift_prev(x, prev, 3))
        p_scr[...] = x[tt - 8:]
        return jax.nn.silu(y)

    q = conv_silu(q_ref, pq_scr, cw[:, 0:width])
    k = conv_silu(k_ref, pk_scr, cw[:, width:2 * width])
    v = conv_silu(v_ref, pv_scr, cw[:, 2 * width:])

    ab = ab_ref[...].astype(F32)
    gp = gp_ref[...]
    gpt = gpt_ref[...]
    g_all = -jnp.exp(gp[0:1]) * _softplus(ab + gp[1:2])
    beta_all = jax.nn.sigmoid(ab)

    incl = _tri(c, False)
    strict = _tri(c, True)
    cum_mask = jnp.concatenate([incl.astype(BF16), jnp.ones((c, c), BF16)], axis=0)
    triu_bf = (lax.broadcasted_iota(jnp.int32, (c, c), 0) <= lax.broadcasted_iota(jnp.int32, (c, c), 1)).astype(BF16)
    nw = nw_ref[...]

    n_chunks = tt // c
    heads = range(GDN_HEADS)
    inst = [(ci, h) for ci in range(n_chunks) for h in heads]
    sls = [slice(ci * c, (ci + 1) * c) for ci in range(n_chunks)]
    hss = [slice(h * GDN_DK, (h + 1) * GDN_DK) for h in heads]
    gcs = [_mask_mm(cum_mask, g_all[sl]) for sl in sls]
    gr_all = [_mm_mask(-jnp.exp(gpt[:, 0:1]) * _softplus(abt_ref[ci][0:GDN_HEADS] + gpt[:, 1:2]), triu_bf)
              for ci in range(n_chunks)]

    qs, ks, kbs, decays, egcs, kdecs, gtots, bcols = {}, {}, {}, {}, {}, {}, {}, {}
    for ci, h in inst:
        sl, hs = sls[ci], hss[h]
        qc, kc = q[sl, hs], k[sl, hs]
        qs[ci, h] = qc * lax.rsqrt(jnp.sum(qc * qc, axis=-1, keepdims=True) + RMS_EPS) * (GDN_DK ** -0.5)
        ks[ci, h] = kc * lax.rsqrt(jnp.sum(kc * kc, axis=-1, keepdims=True) + RMS_EPS)
        gcol = gcs[ci][:c, h:h + 1]
        g_last = gcs[ci][c:, h:h + 1]
        grow = gr_all[ci][h:h + 1, :]
        bcols[ci, h] = beta_all[sl, GDN_HEADS + h:GDN_HEADS + h + 1]
        decays[ci, h] = jnp.exp(jnp.where(incl, gcol - grow, -BIG))
        kbs[ci, h] = ks[ci, h] * bcols[ci, h]
        egcs[ci, h] = jnp.exp(gcol)
        kdecs[ci, h] = ks[ci, h] * jnp.exp(g_last - gcol)
        gtots[ci, h] = jnp.exp(jnp.concatenate([g_last] * (GDN_DK // c), axis=0))

    kk = {i: _mm_nt(kbs[i], ks[i]) for i in inst}
    qk = {i: _mm_nt(qs[i], ks[i]) * decays[i] for i in inst}
    p = {i: -jnp.where(strict, kk[i] * decays[i], 0.0) for i in inst}
    tm1 = dict(p)
    for _ in range(5):
        p = {i: _mm(p[i], p[i]) for i in inst}
        tp = {i: _mm(tm1[i], p[i]) for i in inst}
        tm1 = {i: tm1[i] + tp[i] + p[i] for i in inst}
    rhs = {i: jnp.concatenate([v[sls[i[0]], hss[i[1]]] * bcols[i], kbs[i] * egcs[i]], axis=1) for i in inst}
    tr = {i: _mm(tm1[i], rhs[i]) for i in inst}
    sol = {i: rhs[i] + tr[i] for i in inst}

    state = [st_scr[h] for h in heads]
    for ci in range(n_chunks):
        ws = [_mm(sol[ci, h][:, GDN_DV:], state[h]) for h in heads]
        qst = [_mm(qs[ci, h] * egcs[ci, h], state[h]) for h in heads]
        v_new = [sol[ci, h][:, :GDN_DV] - ws[h] for h in heads]
        qv = [_mm(qk[ci, h], v_new[h]) for h in heads]
        kv = [_mm_tn(kdecs[ci, h], v_new[h]) for h in heads]
        state = [state[h] * gtots[ci, h] + kv[h] for h in heads]
        for h in heads:
            o = qst[h] + qv[h]
            o = o * lax.rsqrt(jnp.mean(o * o, axis=-1, keepdims=True) + RMS_EPS) * nw
            o = o * jax.nn.silu(zg_ref[sls[ci], hss[h]].astype(F32))
            o_ref[sls[ci], hss[h]] = o.astype(o_ref.dtype)
    for h in heads:
        st_scr[h] = state[h]


def _gdn(z, ab, abt, bsz, seq, conv_w, gp, gpt, norm_w):
    m = z.shape[0]
    tt = 256
    nt = seq // tt
    width = GDN_HEADS * GDN_DK
    zblk = lambda c0: pl.BlockSpec((tt, width), lambda b, t: (b * nt + t, c0 // width))
    const2 = lambda b, t: (0, 0)
    return pl.pallas_call(
        _gdn_body,
        grid=(bsz, nt),
        in_specs=[
            zblk(COL_QKV), zblk(COL_QKV + width), zblk(COL_QKV + 2 * width), zblk(COL_ZG),
            pl.BlockSpec((tt, 128), lambda b, t: (b * nt + t, 0)),
            pl.BlockSpec((tt // GDN_CHUNK, 2 * GDN_HEADS, GDN_CHUNK), lambda b, t: (b * nt + t, 0, 0)),
            pl.BlockSpec((4, 3 * width), const2),
            pl.BlockSpec((8, 128), const2),
            pl.BlockSpec((8, 128), const2),
            pl.BlockSpec((1, 128), const2),
        ],
        out_specs=pl.BlockSpec((tt, width), lambda b, t: (b * nt + t, 0)),
        out_shape=jax.ShapeDtypeStruct((m, width), BF16),
        scratch_shapes=[pltpu.VMEM((8, width), F32)] * 3 + [pltpu.VMEM((GDN_HEADS, GDN_DK, GDN_DV), F32)],
        compiler_params=_cparams("parallel", "arbitrary"),
        name="gated_deltanet",
    )(z, z, z, z, ab, abt, conv_w, gp, gpt, norm_w)


def _rwkv_body(r_ref, k_ref, v_ref, lo_ref, mu_ref, vp_ref, wup_ref, aup_ref, gup_ref, o_ref,
               pr_scr, pk_scr, pv_scr, plo_scr, st_scr):
    tt = r_ref.shape[0]
    c = RWKV_CHUNK
    gw = RWKV_GROUP
    hd = RWKV_HD
    width = RWKV_WIDTH

    @pl.when(pl.program_id(1) == 0)
    def _():
        for s in (pr_scr, pk_scr, pv_scr, plo_scr, st_scr):
            s[...] = jnp.zeros_like(s)

    mu = mu_ref[...]

    def tshift(x_ref, p_scr, m):
        x = x_ref[...].astype(F32)
        xs = x + m * (_shift_prev(x, p_scr[...], 1) - x)
        p_scr[...] = x[tt - 8:]
        return xs

    r = tshift(r_ref, pr_scr, mu[:, 0:width])
    k = tshift(k_ref, pk_scr, mu[:, width:2 * width])
    v = tshift(v_ref, pv_scr, mu[:, 2 * width:3 * width])
    lo = tshift(lo_ref, plo_scr, mu[:, 3 * width:])
    vp = vp_ref[...]
    w0, a0, k_k, k_a, r_k, lnx_w, lnx_b = (vp[i:i + 1] for i in range(7))

    lane = lax.broadcasted_iota(jnp.int32, lo.shape, 1)
    lo_act = jnp.where(lane < 64, jnp.tanh(lo), jnp.where(lane < 128, lo, jax.nn.sigmoid(lo)))
    w_pre = _mm(lo_act, wup_ref[...])
    a_pre = _mm(lo_act, aup_ref[...])
    gate = _mm(lo_act, gup_ref[...])
    w_log = -_softplus(-(w0 + w_pre)) - 0.5
    lw = -jnp.exp(w_log)
    a = jax.nn.sigmoid(a0 + a_pre)

    bi = lax.broadcasted_iota(jnp.int32, (gw, gw), 0) // hd
    bj = lax.broadcasted_iota(jnp.int32, (gw, gw), 1) // hd
    bdmask = bi == bj
    ones_bd = bdmask.astype(BF16)

    def head_sum(x):
        return _mm_mask(x, ones_bd, terms=2)

    def bd(x):
        return jnp.where(bdmask, jnp.concatenate([x] * (gw // c), axis=0), 0.0)

    ti = lax.broadcasted_iota(jnp.int32, (c, gw), 0)
    sj = lax.broadcasted_iota(jnp.int32, (c, gw), 1) % c
    strict = sj < ti
    incl = sj <= ti
    tril_bf = _tri(c, False).astype(BF16)

    kk_all = k * k_k
    k2_all = k * (1.0 + (a - 1.0) * k_a)

    n_chunks = tt // c
    groups = range(width // gw)
    inst = [(ci, g) for ci in range(n_chunks) for g in groups]
    sls = [slice(ci * c, (ci + 1) * c) for ci in range(n_chunks)]
    gss = [slice(g * gw, (g + 1) * gw) for g in groups]
    cl_all = [_mask_mm(tril_bf, lw[sl]) for sl in sls]

    def at(x, i):
        return x[sls[i[0]], gss[i[1]]]

    kk_raw = {i: at(kk_all, i) for i in inst}
    kk_ss = {i: head_sum(kk_raw[i] * kk_raw[i]) for i in inst}
    a_h, b_h, k_h, br, w_end = {}, {}, {}, {}, {}
    for i in inst:
        cl = cl_all[i[0]][:, gss[i[1]]]
        kk = kk_raw[i] * lax.rsqrt(kk_ss[i] + RMS_EPS)
        wcum = jnp.exp(cl)
        inv_w = jnp.exp(-cl)
        a_h[i] = kk * at(a, i) * inv_w
        b_h[i] = kk * jnp.exp(cl - at(lw, i))
        k_h[i] = at(k2_all, i) * inv_w
        br[i] = jnp.concatenate([b_h[i], at(r, i) * wcum], axis=0)
        w_end[i] = wcum[c - 1:c]
    v_bd = {i: bd(at(v, i)) for i in inst}
    xa = {i: _mm_nt(br[i], bd(a_h[i])) for i in inst}
    xk = {i: _mm_nt(br[i], bd(k_h[i])) for i in inst}
    l_k = {i: jnp.where(strict, xk[i][:c], 0.0) for i in inst}
    ra = {i: jnp.where(incl, xa[i][c:], 0.0) for i in inst}
    rk = {i: jnp.where(incl, xk[i][c:], 0.0) for i in inst}
    p = {i: -jnp.where(strict, xa[i][:c], 0.0) for i in inst}
    tm1 = dict(p)
    for _ in range(5):
        p = {i: _mm(p[i], bd(p[i])) for i in inst}
        tp = {i: _mm(tm1[i], bd(p[i])) for i in inst}
        tm1 = {i: tm1[i] + tp[i] + p[i] for i in inst}
    lkv = {i: _mm(l_k[i], v_bd[i]) for i in inst}
    rkv = {i: _mm(rk[i], v_bd[i]) for i in inst}
    so = {i: head_sum(at(r, i) * at(k2_all, i) * r_k[:, gss[i[1]]]) for i in inst}

    state_t = [st_scr[g] for g in groups]
    for ci in range(n_chunks):
        ids = [(ci, g) for g in groups]
        brh = [_mm_nt(br[i], state_t[i[1]]) for i in ids]
        rhs = [brh[g][:c] + lkv[ci, g] for g in groups]
        tu = [_mm(tm1[ci, g], bd(rhs[g])) for g in groups]
        u = [rhs[g] + tu[g] for g in groups]
        rau = [_mm(ra[ci, g], bd(u[g])) for g in groups]
        upd = [_mm_tn(jnp.concatenate([u[g], at(v, (ci, g))], axis=0),
                      jnp.concatenate([-(a_h[ci, g] * w_end[ci, g]), k_h[ci, g] * w_end[ci, g]], axis=0))
               for g in groups]
        state_t = [w_end[ci, g] * state_t[g] + jnp.where(bdmask, upd[g], 0.0) for g in groups]
        o = [brh[g][c:] - rau[g] + rkv[ci, g] for g in groups]
        osum = [head_sum(o[g]) for g in groups]
        cen = [o[g] - osum[g] * (1.0 / hd) for g in groups]
        var = [head_sum(cen[g] * cen[g]) * (1.0 / hd) for g in groups]
        for g in groups:
            gs = gss[g]
            on = cen[g] * lax.rsqrt(var[g] + RWKV_GN_EPS) * lnx_w[:, gs] + lnx_b[:, gs]
            bonus = so[ci, g] * at(v, (ci, g))
            o_ref[sls[ci], gs] = ((on + bonus) * gate[sls[ci], gs]).astype(o_ref.dtype)
    for g in groups:
        st_scr[g] = state_t[g]


def _rwkv(z, bsz, seq, mu, vecs, wup_pad, aup_pad, gup_pad):
    m = z.shape[0]
    tt = 256
    nt = seq // tt
    gw = RWKV_GROUP
    width = RWKV_WIDTH
    zblk = lambda c0: pl.BlockSpec((tt, width), lambda b, t: (b * nt + t, c0 // width))
    lora_col = (COL_RWKV + 3 * width) // RWKV_LORA
    const2 = lambda b, t: (0, 0)
    return pl.pallas_call(
        _rwkv_body,
        grid=(bsz, nt),
        in_specs=[
            zblk(COL_RWKV), zblk(COL_RWKV + width), zblk(COL_RWKV + 2 * width),
            pl.BlockSpec((tt, RWKV_LORA), lambda b, t: (b * nt + t, lora_col)),
            pl.BlockSpec(mu.shape, const2),
            pl.BlockSpec((8, width), const2),
            pl.BlockSpec((RWKV_LORA, width), const2),
            pl.BlockSpec((RWKV_LORA, width), const2),
            pl.BlockSpec((RWKV_LORA, width), const2),
        ],
        out_specs=pl.BlockSpec((tt, width), lambda b, t: (b * nt + t, 0)),
        out_shape=jax.ShapeDtypeStruct((m, width), BF16),
        scratch_shapes=[pltpu.VMEM((8, width), F32)] * 3 + [pltpu.VMEM((8, RWKV_LORA), F32),
                                                            pltpu.VMEM((width // gw, gw, gw), F32)],
        compiler_params=_cparams("parallel", "arbitrary"),
        name="rwkv7",
    )(z, z, z, z, mu, vecs, wup_pad, aup_pad, gup_pad)


def _merge_body(h_ref, ya_ref, yb_ref, yc_ref, za_ref, zb_ref, zc_ref, mb_ref, pa_ref, pb_ref,
                pc_ref, wo_ref, nw_ref, hout_ref, xnt_ref):
    mb = mb_ref[...]
    ga = jax.nn.sigmoid(za_ref[...].astype(F32) + mb[:, 0:D_MODEL])
    gb = jax.nn.sigmoid(zb_ref[...].astype(F32) + mb[:, D_MODEL:2 * D_MODEL])
    gc = jax.nn.sigmoid(zc_ref[...].astype(F32) + mb[:, 2 * D_MODEL:])
    merged = (ga * jnp.dot(ya_ref[...], pa_ref[...], preferred_element_type=F32)
              + gb * jnp.dot(yb_ref[...], pb_ref[...], preferred_element_type=F32)
              + gc * jnp.dot(yc_ref[...], pc_ref[...], preferred_element_type=F32))
    h = h_ref[...] + _mm(merged, wo_ref[...])
    hout_ref[...] = h
    xnt_ref[...] = jnp.transpose(_rms(h, nw_ref[...])).astype(BF16)


def _merge(h, z, ya, yb, yc, merge_b, pa, pb, pc, wo, norm_w):
    m = h.shape[0]
    tm = 256
    row = lambda c: pl.BlockSpec((tm, D_MODEL), lambda i: (i, c))
    wspec = pl.BlockSpec((D_MODEL, D_MODEL), lambda i: (0, 0))
    mc = COL_MERGE // D_MODEL
    return pl.pallas_call(
        _merge_body,
        grid=(m // tm,),
        in_specs=[row(0), row(0), row(0), row(0), row(mc), row(mc + 1), row(mc + 2),
                  pl.BlockSpec((1, 3 * D_MODEL), lambda i: (0, 0)),
                  wspec, wspec, wspec, wspec,
                  pl.BlockSpec((1, D_MODEL), lambda i: (0, 0))],
        out_specs=[row(0), pl.BlockSpec((D_MODEL, tm), lambda i: (0, i))],
        out_shape=[jax.ShapeDtypeStruct((m, D_MODEL), F32), jax.ShapeDtypeStruct((D_MODEL, m), BF16)],
        compiler_params=_cparams("parallel"),
        name="merge_outproj",
    )(h, ya, yb, yc, z, z, z, merge_b, pa, pb, pc, wo, norm_w)


_CAND_SLABS = (
    (16, ((0, 0, 16),)),
    (16, ((1, 0, 8), (2, 8, 5), (4, 13, 3))),
    (16, ((3, 0, 4), (5, 4, 2), (6, 6, 2), (7, 8, 2), (8, 10, 1), (9, 11, 1), (10, 12, 1), (11, 13, 1),
          (12, 14, 1), (13, 15, 1))),
    (8, ((14, 0, 1), (15, 1, 1))),
)


def _odd_even_merge_sort_pairs(n):
    pairs = []
    p = 1
    while p < n:
        k = p
        while k >= 1:
            for j in range(k % p, n - k, 2 * k):
                for i in range(min(k, n - j - k)):
                    if (i + j) // (2 * p) == (i + j + k) // (2 * p):
                        pairs.append((i + j, i + j + k))
            k //= 2
        p *= 2
    return tuple(pairs)


_SORT_PAIRS = _odd_even_merge_sort_pairs(PEER_TOPK)


def _candidate_sums(a1, a2):
    slabs = []
    for nrows, pieces in _CAND_SLABS:
        base = a2[:nrows]
        rows = lax.broadcasted_iota(jnp.int32, base.shape, 0)
        out = None
        end = 0
        for p, off, cnt in pieces:
            val = a1[p:p + 1] + (base if off == 0 else pltpu.roll(base, off, 0))
            out = val if out is None else jnp.where(rows >= off, val, out)
            end = off + cnt
        if end < nrows:
            out = jnp.where(rows >= end, -BIG, out)
        slabs.append(out)
    return jnp.concatenate(slabs, axis=0)


def _peer_topk_body(xnt_ref, wqt_ref, keys_ref, n_ref, e1_ref, r2_ref, e2_ref, q_scr):
    tt = xnt_ref.shape[1]
    nk = PEER_NKEYS
    topk = PEER_TOPK
    q_scr[...] = jnp.dot(wqt_ref[...], xnt_ref[...], preferred_element_type=F32)
    lanes = 128
    sub = 8
    assert nk == topk * sub
    row8 = lax.broadcasted_iota(jnp.int32, (sub, lanes), 0)

    def top_sorted(s):
        v = [s[k * sub:(k + 1) * sub] for k in range(topk)]
        for i, j in _SORT_PAIRS:
            v[i], v[j] = jnp.maximum(v[i], v[j]), jnp.minimum(v[i], v[j])
        for shift in (4, 2, 1):
            other = [pltpu.roll(x, shift, 0) for x in v]
            v = [jnp.maximum(v[k], other[topk - 1 - k]) for k in range(topk)]
            d = topk // 2
            while d >= 1:
                for i in range(topk):
                    if i & d == 0:
                        v[i], v[i + d] = jnp.maximum(v[i], v[i + d]), jnp.minimum(v[i], v[i + d])
                d //= 2
        return v

    def compact(v):
        tiles = []
        for t0 in range(0, topk, sub):
            out = v[t0]
            for k in range(1, sub):
                out = jnp.where(row8 == k, v[t0 + k], out)
            tiles.append(out)
        return jnp.concatenate(tiles, axis=0)

    def head(h, carry):
        o1 = pl.multiple_of(h * (2 * nk), 2 * nk)
        s1 = _mm(keys_ref[2 * h], q_scr[pl.ds(o1, nk), :])
        s2 = _mm(keys_ref[2 * h + 1], q_scr[pl.ds(o1 + nk, nk), :])
        chunks = [slice(l0, l0 + lanes) for l0 in range(0, tt, lanes)]
        a1 = [top_sorted(s1[:, ls]) for ls in chunks]
        a2 = [top_sorted(s2[:, ls]) for ls in chunks]
        a1c = jnp.concatenate([compact(v) for v in a1], axis=1)
        a2c = jnp.concatenate([compact(v) for v in a2], axis=1)
        cand = _candidate_sums(a1c, a2c)
        cmax = a1c[0:1] + a2c[0:1]

        def cbody(rnd, carry):
            cnd, zsum, c_in, c_out = carry
            mx = jnp.max(cnd, axis=0, keepdims=True)
            zsum = zsum + jnp.where(rnd < topk, jnp.exp(mx - cmax), 0.0)
            c_in = jnp.where(rnd == topk - 1, mx, c_in)
            c_out = jnp.where(rnd == topk, mx, c_out)
            return jnp.where(cnd == mx, -BIG, cnd), zsum, c_in, c_out

        zero = jnp.zeros((1, tt), F32)
        _, zsum, c_in, c_out = lax.fori_loop(0, topk + 1, cbody, (cand, zero, zero, zero))
        tau = 0.5 * (c_in + c_out)
        inv_z = 1.0 / zsum
        pack = 16

        def rep(x):
            return jnp.concatenate([x] * (pack // sub), axis=0)

        for ci, ls in enumerate(chunks):
            tau8 = jnp.broadcast_to(tau[:, ls], (sub, lanes))
            need = [rep(tau8 - a2[ci][qq]) for qq in range(topk)]
            val2 = [rep(a2[ci][qq]) for qq in range(topk)]
            last1 = rep(a1[ci][topk - 1])
            top1 = rep(a1[ci][0])
            top2 = val2[0]
            for k in range(nk // pack):
                rows = slice(k * pack, (k + 1) * pack)
                x1 = s1[rows, ls]
                x2 = s2[rows, ls]
                n_sel = jnp.zeros((pack, lanes), F32)
                r2 = jnp.zeros((pack, lanes), F32)
                for qq in range(topk):
                    n_sel = jnp.where(x1 >= need[qq], n_sel + 1.0, n_sel)
                    r2 = jnp.where(x2 < val2[qq], r2 + 1.0, r2)
                n_ref[h, rows, ls] = jnp.where(x1 >= last1, n_sel, 0.0)
                e1_ref[h, rows, ls] = jnp.exp(x1 - top1)
                r2_ref[h, rows, ls] = r2.astype(BF16)
                e2_ref[h, rows, ls] = (jnp.exp(x2 - top2) * inv_z[:, ls]).astype(BF16)
        return carry

    lax.fori_loop(0, PEER_HEADS, head, 0)


def _peer_topk(xnt, wqt, keys):
    m = xnt.shape[1]
    tt = 512 if m % 512 == 0 else 256
    shape = (PEER_HEADS, PEER_NKEYS, m)
    ospec = pl.BlockSpec((PEER_HEADS, PEER_NKEYS, tt), lambda i: (0, 0, i))
    return pl.pallas_call(
        _peer_topk_body,
        grid=(m // tt,),
        in_specs=[pl.BlockSpec((D_MODEL, tt), lambda i: (0, i)),
                  pl.BlockSpec(wqt.shape, lambda i: (0, 0)),
                  pl.BlockSpec(keys.shape, lambda i: (0, 0, 0))],
        out_specs=[ospec] * 4,
        out_shape=[jax.ShapeDtypeStruct(shape, F32), jax.ShapeDtypeStruct(shape, F32),
                   jax.ShapeDtypeStruct(shape, BF16), jax.ShapeDtypeStruct(shape, BF16)],
        scratch_shapes=[pltpu.VMEM((wqt.shape[0], tt), F32)],
        compiler_params=_cparams("parallel"),
        name="peer_topk",
    )(xnt, wqt, keys)


PEER_TOKEN_CHUNK = 256


def _peer_dense_body(u_ref, xnt_ref, vt_ref, n_ref, e1_ref, r2_ref, e2_ref, y_ref, acc_scr):
    nk = PEER_NKEYS
    tt = xnt_ref.shape[1]
    tc = PEER_TOKEN_CHUNK
    n_blk = u_ref.shape[0] // nk
    pack = 16

    @pl.when(pl.program_id(1) == 0)
    def _():
        acc_scr[...] = jnp.zeros_like(acc_scr)

    def rows_bf16(ref, h, ii, ls):
        row = jnp.broadcast_to(ref[h, ii:ii + 1, ls], (pack, tc)).astype(BF16)
        return jnp.concatenate([row] * (nk // pack), axis=0)

    def select_weights(ii, c0):
        ls = slice(c0, c0 + tc)
        w = None
        for h in range(PEER_HEADS):
            sel = jnp.where(r2_ref[h, :, ls] < rows_bf16(n_ref, h, ii, ls), e2_ref[h, :, ls],
                            jnp.zeros((), BF16))
            term = sel * rows_bf16(e1_ref, h, ii, ls)
            w = term if w is None else w + term
        return w

    def gelu_tanh(x):
        c = 0.7978845608028654
        inner = x * ((x * x) * (c * 0.044715) + c)
        hx = 0.5 * x
        return hx * jnp.tanh(inner) + hx

    starts = list(range(0, tt, tc))
    pre = [jnp.dot(u_ref[...], xnt_ref[:, c0:c0 + tc], preferred_element_type=F32) for c0 in starts]
    for idx, c0 in enumerate(starts):
        act = gelu_tanh(pre[idx].astype(BF16))
        a = jnp.concatenate([act[ii * nk:(ii + 1) * nk] * select_weights(ii, c0) for ii in range(n_blk)], axis=0)
        acc_scr[:, c0:c0 + tc] += jnp.dot(vt_ref[...], a, preferred_element_type=F32)

    @pl.when(pl.program_id(1) == pl.num_programs(1) - 1)
    def _():
        y_ref[...] = jnp.transpose(acc_scr[...])


def _peer_dense(xnt, u_bf, vt_bf, n_sel, e1, r2, e2, tt):
    m = xnt.shape[1]
    te = 2048
    ib = te // PEER_NKEYS
    sel_i = pl.BlockSpec((PEER_HEADS, ib, tt), lambda i, e: (0, e, i))
    sel_all = pl.BlockSpec((PEER_HEADS, PEER_NKEYS, tt), lambda i, e: (0, 0, i))
    return pl.pallas_call(
        _peer_dense_body,
        grid=(m // tt, PEER_EXPERTS // te),
        in_specs=[pl.BlockSpec((te, D_MODEL), lambda i, e: (e, 0)),
                  pl.BlockSpec((D_MODEL, tt), lambda i, e: (0, i)),
                  pl.BlockSpec((D_MODEL, te), lambda i, e: (0, e)),
                  sel_i, sel_i, sel_all, sel_all],
        out_specs=pl.BlockSpec((tt, D_MODEL), lambda i, e: (i, 0)),
        out_shape=jax.ShapeDtypeStruct((m, D_MODEL), F32),
        scratch_shapes=[pltpu.VMEM((D_MODEL, tt), F32)],
        compiler_params=_cparams("parallel", "arbitrary"),
        name="peer_dense",
    )(u_bf, xnt, vt_bf, n_sel, e1, r2, e2)


def _table_body(transpose, x_ref, o_ref):
    x = x_ref[...]
    o_ref[...] = (jnp.transpose(x) if transpose else x).astype(o_ref.dtype)


def _expert_table(tables, l, transpose):
    _, rows, cols = tables.shape
    tr = 512
    return pl.pallas_call(
        functools.partial(_table_body, transpose),
        grid=(rows // tr,),
        in_specs=[pl.BlockSpec((None, tr, cols), lambda i: (l, i, 0))],
        out_specs=pl.BlockSpec((cols, tr), lambda i: (0, i)) if transpose else pl.BlockSpec((tr, cols), lambda i: (i, 0)),
        out_shape=jax.ShapeDtypeStruct((cols, rows) if transpose else (rows, cols), BF16),
        compiler_params=_cparams("parallel"),
        name="expert_table",
    )(tables)


def _final_body(h_ref, y_ref, nw_ref, o_ref):
    o_ref[...] = _rms(h_ref[...] + y_ref[...], nw_ref[...])


def _final(h, y, norm_w):
    m = h.shape[0]
    tm = 512
    row = pl.BlockSpec((tm, D_MODEL), lambda i: (i, 0))
    return pl.pallas_call(
        _final_body,
        grid=(m // tm,),
        in_specs=[row, row, pl.BlockSpec((1, D_MODEL), lambda i: (0, 0))],
        out_specs=row,
        out_shape=jax.ShapeDtypeStruct((m, D_MODEL), F32),
        compiler_params=_cparams("parallel"),
        name="final_norm",
    )(h, y, norm_w.reshape(1, D_MODEL))


def _pad_rows(rows, width, n_rows=8):
    flat = [jnp.pad(r.reshape(-1).astype(F32), (0, width - r.size)) for r in rows]
    return jnp.pad(jnp.stack(flat), ((0, n_rows - len(rows)), (0, 0)))


def _chunk_rows(ab):
    m, n = ab.shape
    return jnp.transpose(ab.reshape(m // GDN_CHUNK, GDN_CHUNK, n), (0, 2, 1)).astype(F32)


def _block_diag_groups(w):
    per = LRU_GROUP // LRU_BLOCK_DIM
    n_grp = w.shape[0] // per
    wg = w.reshape(n_grp, per, LRU_BLOCK_DIM, LRU_BLOCK_DIM)
    eye = jnp.eye(per, dtype=w.dtype)
    out = wg[:, :, :, None, :] * eye[None, :, None, :, None]
    return out.reshape(n_grp, LRU_GROUP, LRU_GROUP).astype(BF16)


def _prep_w_in(w_all, l):
    pad = jnp.zeros((D_MODEL, IN_WIDTH_PAD - COL_AB - 2 * GDN_HEADS), BF16)
    cols = ((0, 6144), (9488, 12560), (6160, 9488), (6144, 6160))
    return jnp.concatenate([w_all[l, :, a:b].astype(BF16) for a, b in cols] + [pad], axis=1)


def _layer(l, h, y, bsz, seq, p):
    h, z, ab = _inproj(h, y, p["norm_mix_w"][l], _prep_w_in(p["w_in"], l))

    lru_vecs = _pad_rows([p["lru_conv_b"][l], p["lru_b_a"][l], p["lru_b_x"][l], p["lru_lambda"][l]], LRU_WIDTH)
    ya = _lru(z, bsz, seq, p["lru_conv_w"][l], lru_vecs,
              _block_diag_groups(p["lru_w_a"][l]), _block_diag_groups(p["lru_w_x"][l]))

    abt = _chunk_rows(ab[:, :2 * GDN_HEADS])
    gp = _pad_rows([p["gdn_a_log"][l], p["gdn_dt_bias"][l]], 128)
    gpt = jnp.pad(jnp.stack([p["gdn_a_log"][l], p["gdn_dt_bias"][l]], axis=1), ((0, 0), (0, 126)))
    yb = _gdn(z, ab, abt, bsz, seq, p["gdn_conv_w"][l], gp, gpt, p["gdn_norm_w"][l].reshape(1, GDN_DV))

    rw_vecs = _pad_rows([p["rwkv_w0"][l], p["rwkv_a0"][l], p["rwkv_k_k"][l], p["rwkv_k_a"][l],
                         p["rwkv_r_k"][l], p["rwkv_lnx_w"][l], p["rwkv_lnx_b"][l]], RWKV_WIDTH)
    wup = jnp.pad(p["rwkv_w_up"][l], ((0, 192), (0, 0))).astype(BF16)
    aup = jnp.pad(p["rwkv_a_up"][l], ((64, 128), (0, 0))).astype(BF16)
    gup = jnp.pad(p["rwkv_g_up"][l], ((128, 0), (0, 0))).astype(BF16)
    yc = _rwkv(z, bsz, seq, p["rwkv_mu"][l].reshape(1, -1), rw_vecs, wup, aup, gup)

    h, xnt = _merge(h, z, ya, yb, yc, p["merge_b"][l].reshape(1, -1), p["p_lru"][l].astype(BF16),
                    p["p_gdn"][l].astype(BF16), p["p_rwkv"][l].astype(BF16), p["w_out"][l].astype(BF16),
                    p["norm_ffn_w"][l].reshape(1, D_MODEL))

    wqt = jnp.transpose(p["peer_wq"][l]).astype(BF16)
    keys = p["peer_keys"][l].reshape(2 * PEER_HEADS, PEER_NKEYS, PEER_HALF).astype(BF16)
    n_sel, e1, r2, e2 = _peer_topk(xnt, wqt, keys)
    tt = 512 if xnt.shape[1] % 512 == 0 else 256
    y = _peer_dense(xnt, _expert_table(p["peer_u"], l, False), _expert_table(p["peer_v"], l, True),
                    n_sel, e1, r2, e2, tt)
    return h, y


def kernel(x, norm_mix_w, norm_ffn_w, final_norm_w, w_in, lru_conv_w, lru_conv_b, lru_w_a, lru_b_a,
           lru_w_x, lru_b_x, lru_lambda, gdn_conv_w, gdn_a_log, gdn_dt_bias, gdn_norm_w, rwkv_mu, rwkv_w0,
           rwkv_w_up, rwkv_a0, rwkv_a_up, rwkv_g_up, rwkv_k_k, rwkv_k_a, rwkv_r_k, rwkv_lnx_w, rwkv_lnx_b,
           merge_b, p_lru, p_gdn, p_rwkv, w_out, peer_wq, peer_keys, peer_u, peer_v):
    p = dict(locals())
    bsz, seq, dim = x.shape
    h = x.reshape(bsz * seq, dim)
    y = None
    for l in range(DEPTH):
        h, y = _layer(l, h, y, bsz, seq, p)
    return _final(h, y, final_norm_w).reshape(bsz, seq, dim)
```

```python
import functools

import jax
import jax.numpy as jnp
from jax import lax
from jax.experimental import pallas as pl
from jax.experimental.pallas import tpu as pltpu

F32 = jnp.float32
BF16 = jnp.bfloat16

D_MODEL = 1024
DEPTH = 2
RMS_EPS = 1e-6

LRU_WIDTH = 1024
LRU_BLOCK_DIM = 64
LRU_C = 8.0
LRU_GROUP = 256

GDN_HEADS = 8
GDN_DK = 128
GDN_DV = 128
GDN_CHUNK = 64

RWKV_HEADS = 16
RWKV_HD = 64
RWKV_WIDTH = 1024
RWKV_GN_EPS = 64e-5
RWKV_CHUNK = 64
RWKV_GROUP = 256
RWKV_LORA = 256

PEER_HEADS = 8
PEER_NKEYS = 128
PEER_EXPERTS = PEER_NKEYS * PEER_NKEYS
PEER_HALF = 128
PEER_TOPK = 16

COL_U = 0
COL_GATE = 1024
COL_QKV = 2048
COL_ZG = 5120
COL_MERGE = 6144
COL_RWKV = 9216
COL_AB = 12544
IN_WIDTH_PAD = 12800
IN_TILE_N = 1280

VMEM_LIMIT = 48 * 1024 * 1024
BIG = 3.0e38


def _cparams(*sem):
    return pltpu.CompilerParams(dimension_semantics=sem, vmem_limit_bytes=VMEM_LIMIT)


def _mm(a, b):
    return jnp.dot(a.astype(BF16), b.astype(BF16), preferred_element_type=F32)


def _mm_nt(a, b):
    return lax.dot_general(a.astype(BF16), b.astype(BF16), (((1,), (1,)), ((), ())),
                           preferred_element_type=F32)


def _mm_tn(a, b):
    return lax.dot_general(a.astype(BF16), b.astype(BF16), (((0,), (0,)), ((), ())),
                           preferred_element_type=F32)


def _softplus(x):
    return jnp.maximum(x, 0.0) + jnp.log1p(jnp.exp(-jnp.abs(x)))


def _rms(x, w):
    return x * lax.rsqrt(jnp.mean(x * x, axis=-1, keepdims=True) + RMS_EPS) * w


def _shift_prev(x, prev8, s):
    r = pltpu.roll(x, s, 0)
    pr = pltpu.roll(prev8, s, 0)
    rows8 = lax.broadcasted_iota(jnp.int32, prev8.shape, 0)
    head = jnp.where(rows8 < s, pr, r[:8])
    return jnp.concatenate([head, r[8:]], axis=0)


def _shift_fill(x, d, fill):
    n, c = x.shape
    if d % 8 == 0:
        return jnp.concatenate([jnp.full((d, c), fill, x.dtype), x[:n - d]], axis=0)
    r = pltpu.roll(x, d, 0)
    rows8 = lax.broadcasted_iota(jnp.int32, (8, c), 0)
    head = jnp.where(rows8 < d, fill, r[:8])
    return jnp.concatenate([head, r[8:]], axis=0)


def _tri(n, strict):
    i = lax.broadcasted_iota(jnp.int32, (n, n), 0)
    j = lax.broadcasted_iota(jnp.int32, (n, n), 1)
    return (j < i) if strict else (j <= i)


def _split_bf16(x, terms):
    parts = []
    rem = x
    for i in range(terms):
        p = rem.astype(BF16)
        parts.append(p)
        if i + 1 < terms:
            rem = rem - p.astype(F32)
    return parts


def _mask_mm(mask_bf, x, terms=3):
    return sum(jnp.dot(mask_bf, p, preferred_element_type=F32) for p in _split_bf16(x, terms))


def _mm_mask(x, mask_bf, terms=3):
    return sum(jnp.dot(p, mask_bf, preferred_element_type=F32) for p in _split_bf16(x, terms))


def _inproj_body(add_y, *refs):
    if add_y:
        h_ref, y_ref, nw_ref, w_ref, hout_ref, z_ref, ab_ref, xn_scr = refs
    else:
        h_ref, nw_ref, w_ref, z_ref, ab_ref, xn_scr = refs

    @pl.when(pl.program_id(1) == 0)
    def _():
        h = h_ref[...]
        if add_y:
            h = h + y_ref[...]
            hout_ref[...] = h
        xn_scr[...] = _rms(h, nw_ref[...]).astype(BF16)

    res = jnp.dot(xn_scr[...], w_ref[...], preferred_element_type=F32)
    z_ref[...] = res.astype(z_ref.dtype)

    @pl.when(pl.program_id(1) == pl.num_programs(1) - 1)
    def _():
        c0 = COL_AB - (IN_WIDTH_PAD - res.shape[1])
        ab_ref[...] = res[:, c0:c0 + ab_ref.shape[1]]


def _inproj(h, y, norm_w, w_pad):
    m = h.shape[0]
    tm, tn = (1024 if m % 1024 == 0 else 512), IN_TILE_N
    add_y = y is not None
    row = pl.BlockSpec((tm, D_MODEL), lambda i, j: (i, 0))
    in_specs = [row] + ([row] if add_y else []) + [
        pl.BlockSpec((1, D_MODEL), lambda i, j: (0, 0)),
        pl.BlockSpec((D_MODEL, tn), lambda i, j: (0, j)),
    ]
    assert IN_WIDTH_PAD % tn == 0 and COL_AB >= IN_WIDTH_PAD - tn
    z_spec = pl.BlockSpec((tm, tn), lambda i, j: (i, j))
    z_shape = jax.ShapeDtypeStruct((m, IN_WIDTH_PAD), BF16)
    ab_spec = pl.BlockSpec((tm, 128), lambda i, j: (i, 0))
    ab_shape = jax.ShapeDtypeStruct((m, 128), F32)
    args = (h,) + ((y,) if add_y else ()) + (norm_w.reshape(1, D_MODEL), w_pad)
    out = pl.pallas_call(
        functools.partial(_inproj_body, add_y),
        grid=(m // tm, IN_WIDTH_PAD // tn),
        in_specs=in_specs,
        out_specs=([row] if add_y else []) + [z_spec, ab_spec],
        out_shape=([jax.ShapeDtypeStruct((m, D_MODEL), F32)] if add_y else []) + [z_shape, ab_shape],
        scratch_shapes=[pltpu.VMEM((tm, D_MODEL), BF16)],
        compiler_params=_cparams("parallel", "arbitrary"),
        name="norm_inproj",
    )(*args)
    return tuple(out) if add_y else (h,) + tuple(out)


def _lru_body(zu_ref, zg_ref, cw_ref, vp_ref, wa_ref, wx_ref, o_ref, prev_scr, hc_scr):
    tt = zu_ref.shape[0]

    @pl.when(pl.program_id(1) == 0)
    def _():
        prev_scr[...] = jnp.zeros_like(prev_scr)
        hc_scr[...] = jnp.zeros_like(hc_scr)

    u = zu_ref[...].astype(F32)
    prev = prev_scr[...]
    cw = cw_ref[...]
    vp = vp_ref[...]
    xc = (cw[3:4] * u + cw[2:3] * _shift_prev(u, prev, 1) + cw[1:2] * _shift_prev(u, prev, 2)
          + cw[0:1] * _shift_prev(u, prev, 3) + vp[0:1])
    prev_scr[...] = u[tt - 8:]

    xcb = xc.astype(BF16)
    n_grp = LRU_WIDTH // LRU_GROUP
    pre_a = jnp.concatenate(
        [jnp.dot(xcb[:, g * LRU_GROUP:(g + 1) * LRU_GROUP], wa_ref[g], preferred_element_type=F32)
         for g in range(n_grp)], axis=1)
    pre_x = jnp.concatenate(
        [jnp.dot(xcb[:, g * LRU_GROUP:(g + 1) * LRU_GROUP], wx_ref[g], preferred_element_type=F32)
         for g in range(n_grp)], axis=1)
    r = jax.nn.sigmoid(pre_a + vp[1:2])
    i = jax.nn.sigmoid(pre_x + vp[2:3])
    log_a = (-LRU_C * r) * _softplus(-vp[3:4])
    a = jnp.exp(log_a)
    b = jnp.sqrt(-jnp.tanh(log_a) * (a * a + 1.0)) * (i * xc)

    d = 1
    while d < tt:
        b = a * _shift_fill(b, d, 0.0) + b
        a = a * _shift_fill(a, d, 1.0)
        d *= 2
    hcar = hc_scr[...]
    hval = b + a * hcar[0:1]
    hc_scr[...] = jnp.broadcast_to(hval[tt - 1:tt], hcar.shape)
    o_ref[...] = (hval * jax.nn.gelu(zg_ref[...].astype(F32))).astype(o_ref.dtype)


def _lru(z, bsz, seq, conv_w, vecs, wa_bd, wx_bd):
    m = z.shape[0]
    tt = 256
    nt = seq // tt
    rowmap = lambda c: (lambda b, t: (b * nt + t, c))
    const2 = lambda b, t: (0, 0)
    return pl.pallas_call(
        _lru_body,
        grid=(bsz, nt),
        in_specs=[
            pl.BlockSpec((tt, LRU_WIDTH), rowmap(COL_U // LRU_WIDTH)),
            pl.BlockSpec((tt, LRU_WIDTH), rowmap(COL_GATE // LRU_WIDTH)),
            pl.BlockSpec((4, LRU_WIDTH), const2),
            pl.BlockSpec((8, LRU_WIDTH), const2),
            pl.BlockSpec(wa_bd.shape, lambda b, t: (0, 0, 0)),
            pl.BlockSpec(wx_bd.shape, lambda b, t: (0, 0, 0)),
        ],
        out_specs=pl.BlockSpec((tt, LRU_WIDTH), rowmap(0)),
        out_shape=jax.ShapeDtypeStruct((m, LRU_WIDTH), BF16),
        scratch_shapes=[pltpu.VMEM((8, LRU_WIDTH), F32), pltpu.VMEM((8, LRU_WIDTH), F32)],
        compiler_params=_cparams("parallel", "arbitrary"),
        name="rglru",
    )(z, z, conv_w, vecs, wa_bd, wx_bd)


def _gdn_body(q_ref, k_ref, v_ref, zg_ref, ab_ref, abt_ref, cw_ref, gp_ref, gpt_ref, nw_ref,
              o_ref, pq_scr, pk_scr, pv_scr, st_scr):
    tt = q_ref.shape[0]
    c = GDN_CHUNK
    width = GDN_HEADS * GDN_DK

    @pl.when(pl.program_id(1) == 0)
    def _():
        for s in (pq_scr, pk_scr, pv_scr, st_scr):
            s[...] = jnp.zeros_like(s)

    cw = cw_ref[...]

    def conv_silu(x_ref, p_scr, w):
        x = x_ref[...].astype(F32)
        prev = p_scr[...]
        y = (w[3:4] * x + w[2:3] * _shift_prev(x, prev, 1) + w[1:2] * _shift_prev(x, prev, 2)
             + w[0:1] * _shift_prev(x, prev, 3))
        p_scr[...] = x[tt - 8:]
        return jax.nn.silu(y)

    q = conv_silu(q_ref, pq_scr, cw[:, 0:width])
    k = conv_silu(k_ref, pk_scr, cw[:, width:2 * width])
    v = conv_silu(v_ref, pv_scr, cw[:, 2 * width:])

    ab = ab_ref[...].astype(F32)
    gp = gp_ref[...]
    gpt = gpt_ref[...]
    g_all = -jnp.exp(gp[0:1]) * _softplus(ab + gp[1:2])
    beta_all = jax.nn.sigmoid(ab)

    incl = _tri(c, False)
    strict = _tri(c, True)
    cum_mask = jnp.concatenate([incl.astype(BF16), jnp.ones((c, c), BF16)], axis=0)
    triu_bf = (lax.broadcasted_iota(jnp.int32, (c, c), 0) <= lax.broadcasted_iota(jnp.int32, (c, c), 1)).astype(BF16)
    nw = nw_ref[...]

    n_chunks = tt // c
    heads = range(GDN_HEADS)
    inst = [(ci, h) for ci in range(n_chunks) for h in heads]
    sls = [slice(ci * c, (ci + 1) * c) for ci in range(n_chunks)]
    hss = [slice(h * GDN_DK, (h + 1) * GDN_DK) for h in heads]
    gcs = [_mask_mm(cum_mask, g_all[sl]) for sl in sls]
    gr_all = [_mm_mask(-jnp.exp(gpt[:, 0:1]) * _softplus(abt_ref[ci][0:GDN_HEADS] + gpt[:, 1:2]), triu_bf)
              for ci in range(n_chunks)]

    qs, ks, kbs, decays, egcs, kdecs, gtots, bcols = {}, {}, {}, {}, {}, {}, {}, {}
    for ci, h in inst:
        sl, hs = sls[ci], hss[h]
        qc, kc = q[sl, hs], k[sl, hs]
        qs[ci, h] = qc * lax.rsqrt(jnp.sum(qc * qc, axis=-1, keepdims=True) + RMS_EPS) * (GDN_DK ** -0.5)
        ks[ci, h] = kc * lax.rsqrt(jnp.sum(kc * kc, axis=-1, keepdims=True) + RMS_EPS)
        gcol = gcs[ci][:c, h:h + 1]
        g_last = gcs[ci][c:, h:h + 1]
        grow = gr_all[ci][h:h + 1, :]
        bcols[ci, h] = beta_all[sl, GDN_HEADS + h:GDN_HEADS + h + 1]
        decays[ci, h] = jnp.exp(jnp.where(incl, gcol - grow, -BIG))
        kbs[ci, h] = ks[ci, h] * bcols[ci, h]
        egcs[ci, h] = jnp.exp(gcol)
        kdecs[ci, h] = ks[ci, h] * jnp.exp(g_last - gcol)
        gtots[ci, h] = jnp.exp(jnp.concatenate([g_last] * (GDN_DK // c), axis=0))

    kk = {i: _mm_nt(kbs[i], ks[i]) for i in inst}
    qk = {i: _mm_nt(qs[i], ks[i]) * decays[i] for i in inst}
    p = {i: -jnp.where(strict, kk[i] * decays[i], 0.0) for i in inst}
    tm1 = dict(p)
    for _ in range(5):
        p = {i: _mm(p[i], p[i]) for i in inst}
        tp = {i: _mm(tm1[i], p[i]) for i in inst}
        tm1 = {i: tm1[i] + tp[i] + p[i] for i in inst}
    rhs = {i: jnp.concatenate([v[sls[i[0]], hss[i[1]]] * bcols[i], kbs[i] * egcs[i]], axis=1) for i in inst}
    tr = {i: _mm(tm1[i], rhs[i]) for i in inst}
    sol = {i: rhs[i] + tr[i] for i in inst}

    state = [st_scr[h] for h in heads]
    for ci in range(n_chunks):
        ws = [_mm(sol[ci, h][:, GDN_DV:], state[h]) for h in heads]
        qst = [_mm(qs[ci, h] * egcs[ci, h], state[h]) for h in heads]
        v_new = [sol[ci, h][:, :GDN_DV] - ws[h] for h in heads]
        qv = [_mm(qk[ci, h], v_new[h]) for h in heads]
        kv = [_mm_tn(kdecs[ci, h], v_new[h]) for h in heads]
        state = [state[h] * gtots[ci, h] + kv[h] for h in heads]
        for h in heads:
            o = qst[h] + qv[h]
            o = o * lax.rsqrt(jnp.mean(o * o, axis=-1, keepdims=True) + RMS_EPS) * nw
            o = o * jax.nn.silu(zg_ref[sls[ci], hss[h]].astype(F32))
            o_ref[sls[ci], hss[h]] = o.astype(o_ref.dtype)
    for h in heads:
        st_scr[h] = state[h]


def _gdn(z, ab, abt, bsz, seq, conv_w, gp, gpt, norm_w):
    m = z.shape[0]
    tt = 256
    nt = seq // tt
    width = GDN_HEADS * GDN_DK
    zblk = lambda c0: pl.BlockSpec((tt, width), lambda b, t: (b * nt + t, c0 // width))
    const2 = lambda b, t: (0, 0)
    return pl.pallas_call(
        _gdn_body,
        grid=(bsz, nt),
        in_specs=[
            zblk(COL_QKV), zblk(COL_QKV + width), zblk(COL_QKV + 2 * width), zblk(COL_ZG),
            pl.BlockSpec((tt, 128), lambda b, t: (b * nt + t, 0)),
            pl.BlockSpec((tt // GDN_CHUNK, 2 * GDN_HEADS, GDN_CHUNK), lambda b, t: (b * nt + t, 0, 0)),
            pl.BlockSpec((4, 3 * width), const2),
            pl.BlockSpec((8, 128), const2),
            pl.BlockSpec((8, 128), const2),
            pl.BlockSpec((1, 128), const2),
        ],
        out_specs=pl.BlockSpec((tt, width), lambda b, t: (b * nt + t, 0)),
        out_shape=jax.ShapeDtypeStruct((m, width), BF16),
        scratch_shapes=[pltpu.VMEM((8, width), F32)] * 3 + [pltpu.VMEM((GDN_HEADS, GDN_DK, GDN_DV), F32)],
        compiler_params=_cparams("parallel", "arbitrary"),
        name="gated_deltanet",
    )(z, z, z, z, ab, abt, conv_w, gp, gpt, norm_w)


def _rwkv_body(r_ref, k_ref, v_ref, lo_ref, mu_ref, vp_ref, wup_ref, aup_ref, gup_ref, o_ref,
               pr_scr, pk_scr, pv_scr, plo_scr, st_scr):
    tt = r_ref.shape[0]
    c = RWKV_CHUNK
    gw = RWKV_GROUP
    hd = RWKV_HD
    width = RWKV_WIDTH

    @pl.when(pl.program_id(1) == 0)
    def _():
        for s in (pr_scr, pk_scr, pv_scr, plo_scr, st_scr):
            s[...] = jnp.zeros_like(s)

    mu = mu_ref[...]

    def tshift(x_ref, p_scr, m):
        x = x_ref[...].astype(F32)
        xs = x + m * (_shift_prev(x, p_scr[...], 1) - x)
        p_scr[...] = x[tt - 8:]
        return xs

    r = tshift(r_ref, pr_scr, mu[:, 0:width])
    k = tshift(k_ref, pk_scr, mu[:, width:2 * width])
    v = tshift(v_ref, pv_scr, mu[:, 2 * width:3 * width])
    lo = tshift(lo_ref, plo_scr, mu[:, 3 * width:])
    vp = vp_ref[...]
    w0, a0, k_k, k_a, r_k, lnx_w, lnx_b = (vp[i:i + 1] for i in range(7))

    lane = lax.broadcasted_iota(jnp.int32, lo.shape, 1)
    lo_act = jnp.where(lane < 64, jnp.tanh(lo), jnp.where(lane < 128, lo, jax.nn.sigmoid(lo)))
    w_pre = _mm(lo_act, wup_ref[...])
    a_pre = _mm(lo_act, aup_ref[...])
    gate = _mm(lo_act, gup_ref[...])
    w_log = -_softplus(-(w0 + w_pre)) - 0.5
    lw = -jnp.exp(w_log)
    a = jax.nn.sigmoid(a0 + a_pre)

    bi = lax.broadcasted_iota(jnp.int32, (gw, gw), 0) // hd
    bj = lax.broadcasted_iota(jnp.int32, (gw, gw), 1) // hd
    bdmask = bi == bj
    ones_bd = bdmask.astype(BF16)

    def head_sum(x):
        return _mm_mask(x, ones_bd, terms=1)

    def bd(x):
        return jnp.where(bdmask, jnp.concatenate([x] * (gw // c), axis=0), 0.0)

    ti = lax.broadcasted_iota(jnp.int32, (c, gw), 0)
    sj = lax.broadcasted_iota(jnp.int32, (c, gw), 1) % c
    strict = sj < ti
    incl = sj <= ti
    tril_bf = _tri(c, False).astype(BF16)

    kk_all = k * k_k
    k2_all = k * (1.0 + (a - 1.0) * k_a)

    n_chunks = tt // c
    groups = range(width // gw)
    inst = [(ci, g) for ci in range(n_chunks) for g in groups]
    sls = [slice(ci * c, (ci + 1) * c) for ci in range(n_chunks)]
    gss = [slice(g * gw, (g + 1) * gw) for g in groups]
    cl_all = [_mask_mm(tril_bf, lw[sl]) for sl in sls]

    def at(x, i):
        return x[sls[i[0]], gss[i[1]]]

    kk_raw = {i: at(kk_all, i) for i in inst}
    kk_ss = {i: head_sum(kk_raw[i] * kk_raw[i]) for i in inst}
    a_h, b_h, k_h, br, w_end = {}, {}, {}, {}, {}
    for i in inst:
        cl = cl_all[i[0]][:, gss[i[1]]]
        kk = kk_raw[i] * lax.rsqrt(kk_ss[i] + RMS_EPS)
        wcum = jnp.exp(cl)
        inv_w = jnp.exp(-cl)
        a_h[i] = kk * at(a, i) * inv_w
        b_h[i] = kk * jnp.exp(cl - at(lw, i))
        k_h[i] = at(k2_all, i) * inv_w
        br[i] = jnp.concatenate([b_h[i], at(r, i) * wcum], axis=0)
        w_end[i] = wcum[c - 1:c]
    v_bd = {i: bd(at(v, i)) for i in inst}
    xa = {i: _mm_nt(br[i], bd(a_h[i])) for i in inst}
    xk = {i: _mm_nt(br[i], bd(k_h[i])) for i in inst}
    l_k = {i: jnp.where(strict, xk[i][:c], 0.0) for i in inst}
    ra = {i: jnp.where(incl, xa[i][c:], 0.0) for i in inst}
    rk = {i: jnp.where(incl, xk[i][c:], 0.0) for i in inst}
    p = {i: -jnp.where(strict, xa[i][:c], 0.0) for i in inst}
    tm1 = dict(p)
    for _ in range(5):
        p = {i: _mm(p[i], bd(p[i])) for i in inst}
        tp = {i: _mm(tm1[i], bd(p[i])) for i in inst}
        tm1 = {i: tm1[i] + tp[i] + p[i] for i in inst}
    lkv = {i: _mm(l_k[i], v_bd[i]) for i in inst}
    rkv = {i: _mm(rk[i], v_bd[i]) for i in inst}
    so = {i: head_sum(at(r, i) * at(k2_all, i) * r_k[:, gss[i[1]]]) for i in inst}

    state_t = [st_scr[g] for g in groups]
    for ci in range(n_chunks):
        ids = [(ci, g) for g in groups]
        brh = [_mm_nt(br[i], state_t[i[1]]) for i in ids]
        rhs = [brh[g][:c] + lkv[ci, g] for g in groups]
        tu = [_mm(tm1[ci, g], bd(rhs[g])) for g in groups]
        u = [rhs[g] + tu[g] for g in groups]
        rau = [_mm(ra[ci, g], bd(u[g])) for g in groups]
        upd = [_mm_tn(jnp.concatenate([u[g], at(v, (ci, g))], axis=0),
                      jnp.concatenate([-(a_h[ci, g] * w_end[ci, g]), k_h[ci, g] * w_end[ci, g]], axis=0))
               for g in groups]
        state_t = [w_end[ci, g] * state_t[g] + jnp.where(bdmask, upd[g], 0.0) for g in groups]
        o = [brh[g][c:] - rau[g] + rkv[ci, g] for g in groups]
        osum = [head_sum(o[g]) for g in groups]
        cen = [o[g] - osum[g] * (1.0 / hd) for g in groups]
        var = [head_sum(cen[g] * cen[g]) * (1.0 / hd) for g in groups]
        for g in groups:
            gs = gss[g]
            on = cen[g] * lax.rsqrt(var[g] + RWKV_GN_EPS) * lnx_w[:, gs] + lnx_b[:, gs]
            bonus = so[ci, g] * at(v, (ci, g))
            o_ref[sls[ci], gs] = ((on + bonus) * gate[sls[ci], gs]).astype(o_ref.dtype)
    for g in groups:
        st_scr[g] = state_t[g]


def _rwkv(z, bsz, seq, mu, vecs, wup_pad, aup_pad, gup_pad):
    m = z.shape[0]
    tt = 256
    nt = seq // tt
    gw = RWKV_GROUP
    width = RWKV_WIDTH
    zblk = lambda c0: pl.BlockSpec((tt, width), lambda b, t: (b * nt + t, c0 // width))
    lora_col = (COL_RWKV + 3 * width) // RWKV_LORA
    const2 = lambda b, t: (0, 0)
    return pl.pallas_call(
        _rwkv_body,
        grid=(bsz, nt),
        in_specs=[
            zblk(COL_RWKV), zblk(COL_RWKV + width), zblk(COL_RWKV + 2 * width),
            pl.BlockSpec((tt, RWKV_LORA), lambda b, t: (b * nt + t, lora_col)),
            pl.BlockSpec(mu.shape, const2),
            pl.BlockSpec((8, width), const2),
            pl.BlockSpec((RWKV_LORA, width), const2),
            pl.BlockSpec((RWKV_LORA, width), const2),
            pl.BlockSpec((RWKV_LORA, width), const2),
        ],
        out_specs=pl.BlockSpec((tt, width), lambda b, t: (b * nt + t, 0)),
        out_shape=jax.ShapeDtypeStruct((m, width), BF16),
        scratch_shapes=[pltpu.VMEM((8, width), F32)] * 3 + [pltpu.VMEM((8, RWKV_LORA), F32),
                                                            pltpu.VMEM((width // gw, gw, gw), F32)],
        compiler_params=_cparams("parallel", "arbitrary"),
        name="rwkv7",
    )(z, z, z, z, mu, vecs, wup_pad, aup_pad, gup_pad)


def _merge_body(h_ref, ya_ref, yb_ref, yc_ref, za_ref, zb_ref, zc_ref, mb_ref, pa_ref, pb_ref,
                pc_ref, wo_ref, nw_ref, hout_ref, xnt_ref):
    mb = mb_ref[...]
    ga = jax.nn.sigmoid(za_ref[...].astype(F32) + mb[:, 0:D_MODEL])
    gb = jax.nn.sigmoid(zb_ref[...].astype(F32) + mb[:, D_MODEL:2 * D_MODEL])
    gc = jax.nn.sigmoid(zc_ref[...].astype(F32) + mb[:, 2 * D_MODEL:])
    merged = (ga * jnp.dot(ya_ref[...], pa_ref[...], preferred_element_type=F32)
              + gb * jnp.dot(yb_ref[...], pb_ref[...], preferred_element_type=F32)
              + gc * jnp.dot(yc_ref[...], pc_ref[...], preferred_element_type=F32))
    h = h_ref[...] + _mm(merged, wo_ref[...])
    hout_ref[...] = h
    xnt_ref[...] = jnp.transpose(_rms(h, nw_ref[...])).astype(BF16)


def _merge(h, z, ya, yb, yc, merge_b, pa, pb, pc, wo, norm_w):
    m = h.shape[0]
    tm = 256
    row = lambda c: pl.BlockSpec((tm, D_MODEL), lambda i: (i, c))
    wspec = pl.BlockSpec((D_MODEL, D_MODEL), lambda i: (0, 0))
    mc = COL_MERGE // D_MODEL
    return pl.pallas_call(
        _merge_body,
        grid=(m // tm,),
        in_specs=[row(0), row(0), row(0), row(0), row(mc), row(mc + 1), row(mc + 2),
                  pl.BlockSpec((1, 3 * D_MODEL), lambda i: (0, 0)),
                  wspec, wspec, wspec, wspec,
                  pl.BlockSpec((1, D_MODEL), lambda i: (0, 0))],
        out_specs=[row(0), pl.BlockSpec((D_MODEL, tm), lambda i: (0, i))],
        out_shape=[jax.ShapeDtypeStruct((m, D_MODEL), F32), jax.ShapeDtypeStruct((D_MODEL, m), BF16)],
        compiler_params=_cparams("parallel"),
        name="merge_outproj",
    )(h, ya, yb, yc, z, z, z, merge_b, pa, pb, pc, wo, norm_w)


_CAND_SLABS = (
    (16, ((0, 0, 16),)),
    (16, ((1, 0, 8), (2, 8, 5), (4, 13, 3))),
    (16, ((3, 0, 4), (5, 4, 2), (6, 6, 2), (7, 8, 2), (8, 10, 1), (9, 11, 1), (10, 12, 1), (11, 13, 1),
          (12, 14, 1), (13, 15, 1))),
    (8, ((14, 0, 1), (15, 1, 1))),
)


def _odd_even_merge_sort_pairs(n):
    pairs = []
    p = 1
    while p < n:
        k = p
        while k >= 1:
            for j in range(k % p, n - k, 2 * k):
                for i in range(min(k, n - j - k)):
                    if (i + j) // (2 * p) == (i + j + k) // (2 * p):
                        pairs.append((i + j, i + j + k))
            k //= 2
        p *= 2
    return tuple(pairs)


_SORT_PAIRS = _odd_even_merge_sort_pairs(PEER_TOPK)


def _candidate_sums(a1, a2):
    slabs = []
    for nrows, pieces in _CAND_SLABS:
        base = a2[:nrows]
        rows = lax.broadcasted_iota(jnp.int32, base.shape, 0)
        out = None
        end = 0
        for p, off, cnt in pieces:
            val = a1[p:p + 1] + (base if off == 0 else pltpu.roll(base, off, 0))
            out = val if out is None else jnp.where(rows >= off, val, out)
            end = off + cnt
        if end < nrows:
            out = jnp.where(rows >= end, -BIG, out)
        slabs.append(out)
    return jnp.concatenate(slabs, axis=0)


def _peer_topk_body(xnt_ref, wqt_ref, keys_ref, n_ref, e1_ref, r2_ref, e2_ref, q_scr):
    tt = xnt_ref.shape[1]
    nk = PEER_NKEYS
    topk = PEER_TOPK
    q_scr[...] = jnp.dot(wqt_ref[...], xnt_ref[...], preferred_element_type=F32)
    lanes = 128
    sub = 8
    assert nk == topk * sub
    row8 = lax.broadcasted_iota(jnp.int32, (sub, lanes), 0)

    def top_sorted(s):
        v = [s[k * sub:(k + 1) * sub] for k in range(topk)]
        for i, j in _SORT_PAIRS:
            v[i], v[j] = jnp.maximum(v[i], v[j]), jnp.minimum(v[i], v[j])
        for shift in (4, 2, 1):
            other = [pltpu.roll(x, shift, 0) for x in v]
            v = [jnp.maximum(v[k], other[topk - 1 - k]) for k in range(topk)]
            d = topk // 2
            while d >= 1:
                for i in range(topk):
                    if i & d == 0:
                        v[i], v[i + d] = jnp.maximum(v[i], v[i + d]), jnp.minimum(v[i], v[i + d])
                d //= 2
        return v

    def compact(v):
        tiles = []
        for t0 in range(0, topk, sub):
            out = v[t0]
            for k in range(1, sub):
                out = jnp.where(row8 == k, v[t0 + k], out)
            tiles.append(out)
        return jnp.concatenate(tiles, axis=0)

    def head(h, carry):
        o1 = pl.multiple_of(h * (2 * nk), 2 * nk)
        s1 = _mm(keys_ref[2 * h], q_scr[pl.ds(o1, nk), :])
        s2 = _mm(keys_ref[2 * h + 1], q_scr[pl.ds(o1 + nk, nk), :])
        chunks = [slice(l0, l0 + lanes) for l0 in range(0, tt, lanes)]
        a1 = [top_sorted(s1[:, ls]) for ls in chunks]
        a2 = [top_sorted(s2[:, ls]) for ls in chunks]
        a1c = jnp.concatenate([compact(v) for v in a1], axis=1)
        a2c = jnp.concatenate([compact(v) for v in a2], axis=1)
        cand = _candidate_sums(a1c, a2c)
        cmax = a1c[0:1] + a2c[0:1]

        def cbody(rnd, carry):
            cnd, zsum, c_in, c_out = carry
            mx = jnp.max(cnd, axis=0, keepdims=True)
            zsum = zsum + jnp.where(rnd < topk, jnp.exp(mx - cmax), 0.0)
            c_in = jnp.where(rnd == topk - 1, mx, c_in)
            c_out = jnp.where(rnd == topk, mx, c_out)
            return jnp.where(cnd == mx, -BIG, cnd), zsum, c_in, c_out

        zero = jnp.zeros((1, tt), F32)
        _, zsum, c_in, c_out = lax.fori_loop(0, topk + 1, cbody, (cand, zero, zero, zero))
        tau = 0.5 * (c_in + c_out)
        inv_z = 1.0 / zsum
        pack = 16

        def rep(x):
            return jnp.concatenate([x] * (pack // sub), axis=0)

        for ci, ls in enumerate(chunks):
            tau8 = jnp.broadcast_to(tau[:, ls], (sub, lanes))
            need = [rep(tau8 - a2[ci][qq]) for qq in range(topk)]
            val2 = [rep(a2[ci][qq]) for qq in range(topk)]
            last1 = rep(a1[ci][topk - 1])
            top1 = rep(a1[ci][0])
            top2 = val2[0]
            for k in range(nk // pack):
                rows = slice(k * pack, (k + 1) * pack)
                x1 = s1[rows, ls]
                x2 = s2[rows, ls]
                n_sel = jnp.zeros((pack, lanes), F32)
                r2 = jnp.zeros((pack, lanes), F32)
                for qq in range(topk):
                    n_sel = jnp.where(x1 >= need[qq], n_sel + 1.0, n_sel)
                    r2 = jnp.where(x2 < val2[qq], r2 + 1.0, r2)
                n_ref[h, rows, ls] = jnp.where(x1 >= last1, n_sel, 0.0)
                e1_ref[h, rows, ls] = jnp.exp(x1 - top1)
                r2_ref[h, rows, ls] = r2.astype(BF16)
                e2_ref[h, rows, ls] = (jnp.exp(x2 - top2) * inv_z[:, ls]).astype(BF16)
        return carry

    lax.fori_loop(0, PEER_HEADS, head, 0)


def _peer_topk(xnt, wqt, keys):
    m = xnt.shape[1]
    tt = 512 if m % 512 == 0 else 256
    shape = (PEER_HEADS, PEER_NKEYS, m)
    ospec = pl.BlockSpec((PEER_HEADS, PEER_NKEYS, tt), lambda i: (0, 0, i))
    return pl.pallas_call(
        _peer_topk_body,
        grid=(m // tt,),
        in_specs=[pl.BlockSpec((D_MODEL, tt), lambda i: (0, i)),
                  pl.BlockSpec(wqt.shape, lambda i: (0, 0)),
                  pl.BlockSpec(keys.shape, lambda i: (0, 0, 0))],
        out_specs=[ospec] * 4,
        out_shape=[jax.ShapeDtypeStruct(shape, F32), jax.ShapeDtypeStruct(shape, F32),
                   jax.ShapeDtypeStruct(shape, BF16), jax.ShapeDtypeStruct(shape, BF16)],
        scratch_shapes=[pltpu.VMEM((wqt.shape[0], tt), F32)],
        compiler_params=_cparams("parallel"),
        name="peer_topk",
    )(xnt, wqt, keys)


PEER_TOKEN_CHUNK = 256


def _peer_dense_body(u_ref, xnt_ref, vt_ref, n_ref, e1_ref, r2_ref, e2_ref, y_ref, acc_scr):
    nk = PEER_NKEYS
    tt = xnt_ref.shape[1]
    tc = PEER_TOKEN_CHUNK
    n_blk = u_ref.shape[0] // nk
    pack = 16

    @pl.when(pl.program_id(1) == 0)
    def _():
        acc_scr[...] = jnp.zeros_like(acc_scr)

    def rows_bf16(ref, h, ii, ls):
        row = jnp.broadcast_to(ref[h, ii:ii + 1, ls], (pack, tc)).astype(BF16)
        return jnp.concatenate([row] * (nk // pack), axis=0)

    def select_weights(ii, c0):
        ls = slice(c0, c0 + tc)
        w = None
        for h in range(PEER_HEADS):
            sel = jnp.where(r2_ref[h, :, ls] < rows_bf16(n_ref, h, ii, ls), e2_ref[h, :, ls],
                            jnp.zeros((), BF16))
            term = sel * rows_bf16(e1_ref, h, ii, ls)
            w = term if w is None else w + term
        return w

    def gelu_tanh(x):
        c = 0.7978845608028654
        inner = x * ((x * x) * (c * 0.044715) + c)
        hx = 0.5 * x
        return hx * jnp.tanh(inner) + hx

    starts = list(range(0, tt, tc))
    pre = [jnp.dot(u_ref[...], xnt_ref[:, c0:c0 + tc], preferred_element_type=F32) for c0 in starts]
    for idx, c0 in enumerate(starts):
        act = gelu_tanh(pre[idx].astype(BF16))
        a = jnp.concatenate([act[ii * nk:(ii + 1) * nk] * select_weights(ii, c0) for ii in range(n_blk)], axis=0)
        acc_scr[:, c0:c0 + tc] += jnp.dot(vt_ref[...], a, preferred_element_type=F32)

    @pl.when(pl.program_id(1) == pl.num_programs(1) - 1)
    def _():
        y_ref[...] = jnp.transpose(acc_scr[...])


def _peer_dense(xnt, u_bf, vt_bf, n_sel, e1, r2, e2, tt):
    m = xnt.shape[1]
    te = 2048
    ib = te // PEER_NKEYS
    sel_i = pl.BlockSpec((PEER_HEADS, ib, tt), lambda i, e: (0, e, i))
    sel_all = pl.BlockSpec((PEER_HEADS, PEER_NKEYS, tt), lambda i, e: (0, 0, i))
    return pl.pallas_call(
        _peer_dense_body,
        grid=(m // tt, PEER_EXPERTS // te),
        in_specs=[pl.BlockSpec((te, D_MODEL), lambda i, e: (e, 0)),
                  pl.BlockSpec((D_MODEL, tt), lambda i, e: (0, i)),
                  pl.BlockSpec((D_MODEL, te), lambda i, e: (0, e)),
                  sel_i, sel_i, sel_all, sel_all],
        out_specs=pl.BlockSpec((tt, D_MODEL), lambda i, e: (i, 0)),
        out_shape=jax.ShapeDtypeStruct((m, D_MODEL), F32),
        scratch_shapes=[pltpu.VMEM((D_MODEL, tt), F32)],
        compiler_params=_cparams("parallel", "arbitrary"),
        name="peer_dense",
    )(u_bf, xnt, vt_bf, n_sel, e1, r2, e2)


def _table_body(transpose, x_ref, o_ref):
    x = x_ref[...]
    o_ref[...] = (jnp.transpose(x) if transpose else x).astype(o_ref.dtype)


def _expert_table(tables, l, transpose):
    _, rows, cols = tables.shape
    tr = 512
    return pl.pallas_call(
        functools.partial(_table_body, transpose),
        grid=(rows // tr,),
        in_specs=[pl.BlockSpec((None, tr, cols), lambda i: (l, i, 0))],
        out_specs=pl.BlockSpec((cols, tr), lambda i: (0, i)) if transpose else pl.BlockSpec((tr, cols), lambda i: (i, 0)),
        out_shape=jax.ShapeDtypeStruct((cols, rows) if transpose else (rows, cols), BF16),
        compiler_params=_cparams("parallel"),
        name="expert_table",
    )(tables)


def _final_body(h_ref, y_ref, nw_ref, o_ref):
    o_ref[...] = _rms(h_ref[...] + y_ref[...], nw_ref[...])


def _final(h, y, norm_w):
    m = h.shape[0]
    tm = 512
    row = pl.BlockSpec((tm, D_MODEL), lambda i: (i, 0))
    return pl.pallas_call(
        _final_body,
        grid=(m // tm,),
        in_specs=[row, row, pl.BlockSpec((1, D_MODEL), lambda i: (0, 0))],
        out_specs=row,
        out_shape=jax.ShapeDtypeStruct((m, D_MODEL), F32),
        compiler_params=_cparams("parallel"),
        name="final_norm",
    )(h, y, norm_w.reshape(1, D_MODEL))


def _pad_rows(rows, width, n_rows=8):
    flat = [jnp.pad(r.reshape(-1).astype(F32), (0, width - r.size)) for r in rows]
    return jnp.pad(jnp.stack(flat), ((0, n_rows - len(rows)), (0, 0)))


def _chunk_rows(ab):
    m, n = ab.shape
    return jnp.transpose(ab.reshape(m // GDN_CHUNK, GDN_CHUNK, n), (0, 2, 1)).astype(F32)


def _block_diag_groups(w):
    per = LRU_GROUP // LRU_BLOCK_DIM
    n_grp = w.shape[0] // per
    wg = w.reshape(n_grp, per, LRU_BLOCK_DIM, LRU_BLOCK_DIM)
    eye = jnp.eye(per, dtype=w.dtype)
    out = wg[:, :, :, None, :] * eye[None, :, None, :, None]
    return out.reshape(n_grp, LRU_GROUP, LRU_GROUP).astype(BF16)


def _prep_w_in(w_all, l):
    pad = jnp.zeros((D_MODEL, IN_WIDTH_PAD - COL_AB - 2 * GDN_HEADS), BF16)
    cols = ((0, 6144), (9488, 12560), (6160, 9488), (6144, 6160))
    return jnp.concatenate([w_all[l, :, a:b].astype(BF16) for a, b in cols] + [pad], axis=1)


def _layer(l, h, y, bsz, seq, p):
    h, z, ab = _inproj(h, y, p["norm_mix_w"][l], _prep_w_in(p["w_in"], l))

    lru_vecs = _pad_rows([p["lru_conv_b"][l], p["lru_b_a"][l], p["lru_b_x"][l], p["lru_lambda"][l]], LRU_WIDTH)
    ya = _lru(z, bsz, seq, p["lru_conv_w"][l], lru_vecs,
              _block_diag_groups(p["lru_w_a"][l]), _block_diag_groups(p["lru_w_x"][l]))

    abt = _chunk_rows(ab[:, :2 * GDN_HEADS])
    gp = _pad_rows([p["gdn_a_log"][l], p["gdn_dt_bias"][l]], 128)
    gpt = jnp.pad(jnp.stack([p["gdn_a_log"][l], p["gdn_dt_bias"][l]], axis=1), ((0, 0), (0, 126)))
    yb = _gdn(z, ab, abt, bsz, seq, p["gdn_conv_w"][l], gp, gpt, p["gdn_norm_w"][l].reshape(1, GDN_DV))

    rw_vecs = _pad_rows([p["rwkv_w0"][l], p["rwkv_a0"][l], p["rwkv_k_k"][l], p["rwkv_k_a"][l],
                         p["rwkv_r_k"][l], p["rwkv_lnx_w"][l], p["rwkv_lnx_b"][l]], RWKV_WIDTH)
    wup = jnp.pad(p["rwkv_w_up"][l], ((0, 192), (0, 0))).astype(BF16)
    aup = jnp.pad(p["rwkv_a_up"][l], ((64, 128), (0, 0))).astype(BF16)
    gup = jnp.pad(p["rwkv_g_up"][l], ((128, 0), (0, 0))).astype(BF16)
    yc = _rwkv(z, bsz, seq, p["rwkv_mu"][l].reshape(1, -1), rw_vecs, wup, aup, gup)

    h, xnt = _merge(h, z, ya, yb, yc, p["merge_b"][l].reshape(1, -1), p["p_lru"][l].astype(BF16),
                    p["p_gdn"][l].astype(BF16), p["p_rwkv"][l].astype(BF16), p["w_out"][l].astype(BF16),
                    p["norm_ffn_w"][l].reshape(1, D_MODEL))

    wqt = jnp.transpose(p["peer_wq"][l]).astype(BF16)
    keys = p["peer_keys"][l].reshape(2 * PEER_HEADS, PEER_NKEYS, PEER_HALF).astype(BF16)
    n_sel, e1, r2, e2 = _peer_topk(xnt, wqt, keys)
    tt = 512 if xnt.shape[1] % 512 == 0 else 256
    y = _peer_dense(xnt, _expert_table(p["peer_u"], l, False), _expert_table(p["peer_v"], l, True),
                    n_sel, e1, r2, e2, tt)
    return h, y


def kernel(x, norm_mix_w, norm_ffn_w, final_norm_w, w_in, lru_conv_w, lru_conv_b, lru_w_a, lru_b_a,
           lru_w_x, lru_b_x, lru_lambda, gdn_conv_w, gdn_a_log, gdn_dt_bias, gdn_norm_w, rwkv_mu, rwkv_w0,
           rwkv_w_up, rwkv_a0, rwkv_a_up, rwkv_g_up, rwkv_k_k, rwkv_k_a, rwkv_r_k, rwkv_lnx_w, rwkv_lnx_b,
           merge_b, p_lru, p_gdn, p_rwkv, w_out, peer_wq, peer_keys, peer_u, peer_v):
    p = dict(locals())
    bsz, seq, dim = x.shape
    h = x.reshape(bsz * seq, dim)
    y = None
    for l in range(DEPTH):
        h, y = _layer(l, h, y, bsz, seq, p)
    return _final(h, y, final_norm_w).reshape(bsz, seq, dim)
```

```python
import functools

import jax
import jax.numpy as jnp
from jax import lax
from jax.experimental import pallas as pl
from jax.experimental.pallas import tpu as pltpu

F32 = jnp.float32
BF16 = jnp.bfloat16

D_MODEL = 1024
DEPTH = 2
RMS_EPS = 1e-6

LRU_WIDTH = 1024
LRU_BLOCK_DIM = 64
LRU_C = 8.0
LRU_GROUP = 256

GDN_HEADS = 8
GDN_DK = 128
GDN_DV = 128
GDN_CHUNK = 64

RWKV_HEADS = 16
RWKV_HD = 64
RWKV_WIDTH = 1024
RWKV_GN_EPS = 64e-5
RWKV_CHUNK = 64
RWKV_GROUP = 256
RWKV_LORA = 256

PEER_HEADS = 8
PEER_NKEYS = 128
PEER_EXPERTS = PEER_NKEYS * PEER_NKEYS
PEER_HALF = 128
PEER_TOPK = 16

COL_U = 0
COL_GATE = 1024
COL_QKV = 2048
COL_ZG = 5120
COL_MERGE = 6144
COL_RWKV = 9216
COL_AB = 12544
IN_WIDTH_PAD = 12800
IN_TILE_N = 1280

VMEM_LIMIT = 48 * 1024 * 1024
BIG = 3.0e38


def _cparams(*sem):
    return pltpu.CompilerParams(dimension_semantics=sem, vmem_limit_bytes=VMEM_LIMIT)


def _mm(a, b):
    return jnp.dot(a.astype(BF16), b.astype(BF16), preferred_element_type=F32)


def _mm_nt(a, b):
    return lax.dot_general(a.astype(BF16), b.astype(BF16), (((1,), (1,)), ((), ())),
                           preferred_element_type=F32)


def _mm_tn(a, b):
    return lax.dot_general(a.astype(BF16), b.astype(BF16), (((0,), (0,)), ((), ())),
                           preferred_element_type=F32)


def _softplus(x):
    return jnp.maximum(x, 0.0) + jnp.log1p(jnp.exp(-jnp.abs(x)))


def _rms(x, w):
    return x * lax.rsqrt(jnp.mean(x * x, axis=-1, keepdims=True) + RMS_EPS) * w


def _shift_prev(x, prev8, s):
    r = pltpu.roll(x, s, 0)
    pr = pltpu.roll(prev8, s, 0)
    rows8 = lax.broadcasted_iota(jnp.int32, prev8.shape, 0)
    head = jnp.where(rows8 < s, pr, r[:8])
    return jnp.concatenate([head, r[8:]], axis=0)


def _shift_fill(x, d, fill):
    n, c = x.shape
    if d % 8 == 0:
        return jnp.concatenate([jnp.full((d, c), fill, x.dtype), x[:n - d]], axis=0)
    r = pltpu.roll(x, d, 0)
    rows8 = lax.broadcasted_iota(jnp.int32, (8, c), 0)
    head = jnp.where(rows8 < d, fill, r[:8])
    return jnp.concatenate([head, r[8:]], axis=0)


def _tri(n, strict):
    i = lax.broadcasted_iota(jnp.int32, (n, n), 0)
    j = lax.broadcasted_iota(jnp.int32, (n, n), 1)
    return (j < i) if strict else (j <= i)


def _split_bf16(x, terms):
    parts = []
    rem = x
    for i in range(terms):
        p = rem.astype(BF16)
        parts.append(p)
        if i + 1 < terms:
            rem = rem - p.astype(F32)
    return parts


def _mask_mm(mask_bf, x, terms=3):
    return sum(jnp.dot(mask_bf, p, preferred_element_type=F32) for p in _split_bf16(x, terms))


def _mm_mask(x, mask_bf, terms=3):
    return sum(jnp.dot(p, mask_bf, preferred_element_type=F32) for p in _split_bf16(x, terms))


def _inproj_body(add_y, *refs):
    if add_y:
        h_ref, y_ref, nw_ref, w_ref, hout_ref, z_ref, ab_ref, xn_scr = refs
    else:
        h_ref, nw_ref, w_ref, z_ref, ab_ref, xn_scr = refs

    @pl.when(pl.program_id(1) == 0)
    def _():
        h = h_ref[...]
        if add_y:
            h = h + y_ref[...]
            hout_ref[...] = h
        xn_scr[...] = _rms(h, nw_ref[...]).astype(BF16)

    res = jnp.dot(xn_scr[...], w_ref[...], preferred_element_type=F32)
    z_ref[...] = res.astype(z_ref.dtype)

    @pl.when(pl.program_id(1) == pl.num_programs(1) - 1)
    def _():
        c0 = COL_AB - (IN_WIDTH_PAD - res.shape[1])
        ab_ref[...] = res[:, c0:c0 + ab_ref.shape[1]]


def _inproj(h, y, norm_w, w_pad):
    m = h.shape[0]
    tm, tn = (1024 if m % 1024 == 0 else 512), IN_TILE_N
    add_y = y is not None
    row = pl.BlockSpec((tm, D_MODEL), lambda i, j: (i, 0))
    in_specs = [row] + ([row] if add_y else []) + [
        pl.BlockSpec((1, D_MODEL), lambda i, j: (0, 0)),
        pl.BlockSpec((D_MODEL, tn), lambda i, j: (0, j)),
    ]
    assert IN_WIDTH_PAD % tn == 0 and COL_AB >= IN_WIDTH_PAD - tn
    z_spec = pl.BlockSpec((tm, tn), lambda i, j: (i, j))
    z_shape = jax.ShapeDtypeStruct((m, IN_WIDTH_PAD), BF16)
    ab_spec = pl.BlockSpec((tm, 128), lambda i, j: (i, 0))
    ab_shape = jax.ShapeDtypeStruct((m, 128), F32)
    args = (h,) + ((y,) if add_y else ()) + (norm_w.reshape(1, D_MODEL), w_pad)
    out = pl.pallas_call(
        functools.partial(_inproj_body, add_y),
        grid=(m // tm, IN_WIDTH_PAD // tn),
        in_specs=in_specs,
        out_specs=([row] if add_y else []) + [z_spec, ab_spec],
        out_shape=([jax.ShapeDtypeStruct((m, D_MODEL), F32)] if add_y else []) + [z_shape, ab_shape],
        scratch_shapes=[pltpu.VMEM((tm, D_MODEL), BF16)],
        compiler_params=_cparams("parallel", "arbitrary"),
        name="norm_inproj",
    )(*args)
    return tuple(out) if add_y else (h,) + tuple(out)


def _lru_body(zu_ref, zg_ref, cw_ref, vp_ref, wa_ref, wx_ref, o_ref, prev_scr, hc_scr):
    tt = zu_ref.shape[0]

    @pl.when(pl.program_id(1) == 0)
    def _():
        prev_scr[...] = jnp.zeros_like(prev_scr)
        hc_scr[...] = jnp.zeros_like(hc_scr)

    u = zu_ref[...].astype(F32)
    prev = prev_scr[...]
    cw = cw_ref[...]
    vp = vp_ref[...]
    xc = (cw[3:4] * u + cw[2:3] * _shift_prev(u, prev, 1) + cw[1:2] * _shift_prev(u, prev, 2)
          + cw[0:1] * _shift_prev(u, prev, 3) + vp[0:1])
    prev_scr[...] = u[tt - 8:]

    xcb = xc.astype(BF16)
    n_grp = LRU_WIDTH // LRU_GROUP
    pre_a = jnp.concatenate(
        [jnp.dot(xcb[:, g * LRU_GROUP:(g + 1) * LRU_GROUP], wa_ref[g], preferred_element_type=F32)
         for g in range(n_grp)], axis=1)
    pre_x = jnp.concatenate(
        [jnp.dot(xcb[:, g * LRU_GROUP:(g + 1) * LRU_GROUP], wx_ref[g], preferred_element_type=F32)
         for g in range(n_grp)], axis=1)
    r = jax.nn.sigmoid(pre_a + vp[1:2])
    i = jax.nn.sigmoid(pre_x + vp[2:3])
    log_a = (-LRU_C * r) * _softplus(-vp[3:4])
    a = jnp.exp(log_a)
    b = jnp.sqrt(-jnp.tanh(log_a) * (a * a + 1.0)) * (i * xc)

    d = 1
    while d < tt:
        b = a * _shift_fill(b, d, 0.0) + b
        a = a * _shift_fill(a, d, 1.0)
        d *= 2
    hcar = hc_scr[...]
    hval = b + a * hcar[0:1]
    hc_scr[...] = jnp.broadcast_to(hval[tt - 1:tt], hcar.shape)
    o_ref[...] = (hval * jax.nn.gelu(zg_ref[...].astype(F32))).astype(o_ref.dtype)


def _lru(z, bsz, seq, conv_w, vecs, wa_bd, wx_bd):
    m = z.shape[0]
    tt = 256
    nt = seq // tt
    rowmap = lambda c: (lambda b, t: (b * nt + t, c))
    const2 = lambda b, t: (0, 0)
    return pl.pallas_call(
        _lru_body,
        grid=(bsz, nt),
        in_specs=[
            pl.BlockSpec((tt, LRU_WIDTH), rowmap(COL_U // LRU_WIDTH)),
            pl.BlockSpec((tt, LRU_WIDTH), rowmap(COL_GATE // LRU_WIDTH)),
            pl.BlockSpec((4, LRU_WIDTH), const2),
            pl.BlockSpec((8, LRU_WIDTH), const2),
            pl.BlockSpec(wa_bd.shape, lambda b, t: (0, 0, 0)),
            pl.BlockSpec(wx_bd.shape, lambda b, t: (0, 0, 0)),
        ],
        out_specs=pl.BlockSpec((tt, LRU_WIDTH), rowmap(0)),
        out_shape=jax.ShapeDtypeStruct((m, LRU_WIDTH), BF16),
        scratch_shapes=[pltpu.VMEM((8, LRU_WIDTH), F32), pltpu.VMEM((8, LRU_WIDTH), F32)],
        compiler_params=_cparams("parallel", "arbitrary"),
        name="rglru",
    )(z, z, conv_w, vecs, wa_bd, wx_bd)


def _gdn_body(q_ref, k_ref, v_ref, zg_ref, ab_ref, abt_ref, cw_ref, gp_ref, gpt_ref, nw_ref,
              o_ref, pq_scr, pk_scr, pv_scr, st_scr):
    tt = q_ref.shape[0]
    c = GDN_CHUNK
    width = GDN_HEADS * GDN_DK

    @pl.when(pl.program_id(1) == 0)
    def _():
        for s in (pq_scr, pk_scr, pv_scr, st_scr):
            s[...] = jnp.zeros_like(s)

    cw = cw_ref[...]

    def conv_silu(x_ref, p_scr, w):
        x = x_ref[...].astype(F32)
        prev = p_scr[...]
        y = (w[3:4] * x + w[2:3] * _shift_prev(x, prev, 1) + w[1:2] * _shift_prev(x, prev, 2)
             + w[0:1] * _shift_prev(x, prev, 3))
        p_scr[...] = x[tt - 8:]
        return jax.nn.silu(y)

    q = conv_silu(q_ref, pq_scr, cw[:, 0:width])
    k = conv_silu(k_ref, pk_scr, cw[:, width:2 * width])
    v = conv_silu(v_ref, pv_scr, cw[:, 2 * width:])

    ab = ab_ref[...].astype(F32)
    gp = gp_ref[...]
    gpt = gpt_ref[...]
    g_all = -jnp.exp(gp[0:1]) * _softplus(ab + gp[1:2])
    beta_all = jax.nn.sigmoid(ab)

    incl = _tri(c, False)
    strict = _tri(c, True)
    cum_mask = jnp.concatenate([incl.astype(BF16), jnp.ones((c, c), BF16)], axis=0)
    triu_bf = (lax.broadcasted_iota(jnp.int32, (c, c), 0) <= lax.broadcasted_iota(jnp.int32, (c, c), 1)).astype(BF16)
    nw = nw_ref[...]

    n_chunks = tt // c
    heads = range(GDN_HEADS)
    inst = [(ci, h) for ci in range(n_chunks) for h in heads]
    sls = [slice(ci * c, (ci + 1) * c) for ci in range(n_chunks)]
    hss = [slice(h * GDN_DK, (h + 1) * GDN_DK) for h in heads]
    gcs = [_mask_mm(cum_mask, g_all[sl]) for sl in sls]
    gr_all = [_mm_mask(-jnp.exp(gpt[:, 0:1]) * _softplus(abt_ref[ci][0:GDN_HEADS] + gpt[:, 1:2]), triu_bf)
              for ci in range(n_chunks)]

    qs, ks, kbs, decays, egcs, kdecs, gtots, bcols = {}, {}, {}, {}, {}, {}, {}, {}
    for ci, h in inst:
        sl, hs = sls[ci], hss[h]
        qc, kc = q[sl, hs], k[sl, hs]
        qs[ci, h] = qc * lax.rsqrt(jnp.sum(qc * qc, axis=-1, keepdims=True) + RMS_EPS) * (GDN_DK ** -0.5)
        ks[ci, h] = kc * lax.rsqrt(jnp.sum(kc * kc, axis=-1, keepdims=True) + RMS_EPS)
        gcol = gcs[ci][:c, h:h + 1]
        g_last = gcs[ci][c:, h:h + 1]
        grow = gr_all[ci][h:h + 1, :]
        bcols[ci, h] = beta_all[sl, GDN_HEADS + h:GDN_HEADS + h + 1]
        decays[ci, h] = jnp.exp(jnp.where(incl, gcol - grow, -BIG))
        kbs[ci, h] = ks[ci, h] * bcols[ci, h]
        egcs[ci, h] = jnp.exp(gcol)
        kdecs[ci, h] = ks[ci, h] * jnp.exp(g_last - gcol)
        gtots[ci, h] = jnp.exp(jnp.concatenate([g_last] * (GDN_DK // c), axis=0))

    kk = {i: _mm_nt(kbs[i], ks[i]) for i in inst}
    qk = {i: _mm_nt(qs[i], ks[i]) * decays[i] for i in inst}
    p = {i: -jnp.where(strict, kk[i] * decays[i], 0.0) for i in inst}
    tm1 = dict(p)
    for _ in range(5):
        p = {i: _mm(p[i], p[i]) for i in inst}
        tp = {i: _mm(tm1[i], p[i]) for i in inst}
        tm1 = {i: tm1[i] + tp[i] + p[i] for i in inst}
    rhs = {i: jnp.concatenate([v[sls[i[0]], hss[i[1]]] * bcols[i], kbs[i] * egcs[i]], axis=1) for i in inst}
    tr = {i: _mm(tm1[i], rhs[i]) for i in inst}
    sol = {i: rhs[i] + tr[i] for i in inst}

    state = [st_scr[h] for h in heads]
    for ci in range(n_chunks):
        ws = [_mm(sol[ci, h][:, GDN_DV:], state[h]) for h in heads]
        qst = [_mm(qs[ci, h] * egcs[ci, h], state[h]) for h in heads]
        v_new = [sol[ci, h][:, :GDN_DV] - ws[h] for h in heads]
        qv = [_mm(qk[ci, h], v_new[h]) for h in heads]
        kv = [_mm_tn(kdecs[ci, h], v_new[h]) for h in heads]
        state = [state[h] * gtots[ci, h] + kv[h] for h in heads]
        for h in heads:
            o = qst[h] + qv[h]
            o = o * lax.rsqrt(jnp.mean(o * o, axis=-1, keepdims=True) + RMS_EPS) * nw
            o = o * jax.nn.silu(zg_ref[sls[ci], hss[h]].astype(F32))
            o_ref[sls[ci], hss[h]] = o.astype(o_ref.dtype)
    for h in heads:
        st_scr[h] = state[h]


def _gdn(z, ab, abt, bsz, seq, conv_w, gp, gpt, norm_w):
    m = z.shape[0]
    tt = 256
    nt = seq // tt
    width = GDN_HEADS * GDN_DK
    zblk = lambda c0: pl.BlockSpec((tt, width), lambda b, t: (b * nt + t, c0 // width))
    const2 = lambda b, t: (0, 0)
    return pl.pallas_call(
        _gdn_body,
        grid=(bsz, nt),
        in_specs=[
            zblk(COL_QKV), zblk(COL_QKV + width), zblk(COL_QKV + 2 * width), zblk(COL_ZG),
            pl.BlockSpec((tt, 128), lambda b, t: (b * nt + t, 0)),
            pl.BlockSpec((tt // GDN_CHUNK, 2 * GDN_HEADS, GDN_CHUNK), lambda b, t: (b * nt + t, 0, 0)),
            pl.BlockSpec((4, 3 * width), const2),
            pl.BlockSpec((8, 128), const2),
            pl.BlockSpec((8, 128), const2),
            pl.BlockSpec((1, 128), const2),
        ],
        out_specs=pl.BlockSpec((tt, width), lambda b, t: (b * nt + t, 0)),
        out_shape=jax.ShapeDtypeStruct((m, width), BF16),
        scratch_shapes=[pltpu.VMEM((8, width), F32)] * 3 + [pltpu.VMEM((GDN_HEADS, GDN_DK, GDN_DV), F32)],
        compiler_params=_cparams("parallel", "arbitrary"),
        name="gated_deltanet",
    )(z, z, z, z, ab, abt, conv_w, gp, gpt, norm_w)


def _rwkv_body(r_ref, k_ref, v_ref, lo_ref, mu_ref, vp_ref, wup_ref, aup_ref, gup_ref, o_ref,
               pr_scr, pk_scr, pv_scr, plo_scr, st_scr):
    tt = r_ref.shape[0]
    c = RWKV_CHUNK
    gw = RWKV_GROUP
    hd = RWKV_HD
    width = RWKV_WIDTH

    @pl.when(pl.program_id(1) == 0)
    def _():
        for s in (pr_scr, pk_scr, pv_scr, plo_scr, st_scr):
            s[...] = jnp.zeros_like(s)

    mu = mu_ref[...]

    def tshift(x_ref, p_scr, m):
        x = x_ref[...].astype(F32)
        xs = x + m * (_shift_prev(x, p_scr[...], 1) - x)
        p_scr[...] = x[tt - 8:]
        return xs

    r = tshift(r_ref, pr_scr, mu[:, 0:width])
    k = tshift(k_ref, pk_scr, mu[:, width:2 * width])
    v = tshift(v_ref, pv_scr, mu[:, 2 * width:3 * width])
    lo = tshift(lo_ref, plo_scr, mu[:, 3 * width:])
    vp = vp_ref[...]
    w0, a0, k_k, k_a, r_k, lnx_w, lnx_b = (vp[i:i + 1] for i in range(7))

    lane = lax.broadcasted_iota(jnp.int32, lo.shape, 1)
    lo_act = jnp.where(lane < 64, jnp.tanh(lo), jnp.where(lane < 128, lo, jax.nn.sigmoid(lo)))
    w_pre = _mm(lo_act, wup_ref[...])
    a_pre = _mm(lo_act, aup_ref[...])
    gate = _mm(lo_act, gup_ref[...])
    w_log = -_softplus(-(w0 + w_pre)) - 0.5
    lw = -jnp.exp(w_log)
    a = jax.nn.sigmoid(a0 + a_pre)

    bi = lax.broadcasted_iota(jnp.int32, (gw, gw), 0) // hd
    bj = lax.broadcasted_iota(jnp.int32, (gw, gw), 1) // hd
    bdmask = bi == bj
    ones_bd = bdmask.astype(BF16)

    def head_sum(x):
        return _mm_mask(x, ones_bd, terms=1)

    def bd(x):
        return jnp.where(bdmask, jnp.concatenate([x] * (gw // c), axis=0), 0.0)

    ti = lax.broadcasted_iota(jnp.int32, (c, gw), 0)
    sj = lax.broadcasted_iota(jnp.int32, (c, gw), 1) % c
    strict = sj < ti
    incl = sj <= ti
    tril_bf = _tri(c, False).astype(BF16)

    kk_all = k * k_k
    k2_all = k * (1.0 + (a - 1.0) * k_a)

    n_chunks = tt // c
    groups = range(width // gw)
    inst = [(ci, g) for ci in range(n_chunks) for g in groups]
    sls = [slice(ci * c, (ci + 1) * c) for ci in range(n_chunks)]
    gss = [slice(g * gw, (g + 1) * gw) for g in groups]
    cl_all = [_mask_mm(tril_bf, lw[sl]) for sl in sls]

    def at(x, i):
        return x[sls[i[0]], gss[i[1]]]

    kk_raw = {i: at(kk_all, i) for i in inst}
    kk_ss = {i: head_sum(kk_raw[i] * kk_raw[i]) for i in inst}
    a_h, b_h, k_h, br, w_end = {}, {}, {}, {}, {}
    for i in inst:
        cl = cl_all[i[0]][:, gss[i[1]]]
        kk = kk_raw[i] * lax.rsqrt(kk_ss[i] + RMS_EPS)
        wcum = jnp.exp(cl)
        inv_w = jnp.exp(-cl)
        a_h[i] = kk * at(a, i) * inv_w
        b_h[i] = kk * jnp.exp(cl - at(lw, i))
        k_h[i] = at(k2_all, i) * inv_w
        br[i] = jnp.concatenate([b_h[i], at(r, i) * wcum], axis=0)
        w_end[i] = wcum[c - 1:c]
    v_bd = {i: bd(at(v, i)) for i in inst}
    xa = {i: _mm_nt(br[i], bd(a_h[i])) for i in inst}
    xk = {i: _mm_nt(br[i], bd(k_h[i])) for i in inst}
    l_k = {i: jnp.where(strict, xk[i][:c], 0.0) for i in inst}
    ra = {i: jnp.where(incl, xa[i][c:], 0.0) for i in inst}
    rk = {i: jnp.where(incl, xk[i][c:], 0.0) for i in inst}
    p = {i: -jnp.where(strict, xa[i][:c], 0.0) for i in inst}
    tm1 = dict(p)
    for _ in range(5):
        p = {i: _mm(p[i], bd(p[i])) for i in inst}
        tp = {i: _mm(tm1[i], bd(p[i])) for i in inst}
        tm1 = {i: tm1[i] + tp[i] + p[i] for i in inst}
    lkv = {i: _mm(l_k[i], v_bd[i]) for i in inst}
    rkv = {i: _mm(rk[i], v_bd[i]) for i in inst}
    so = {i: head_sum(at(r, i) * at(k2_all, i) * r_k[:, gss[i[1]]]) for i in inst}

    state_t = [st_scr[g] for g in groups]
    for ci in range(n_chunks):
        ids = [(ci, g) for g in groups]
        brh = [_mm_nt(br[i], state_t[i[1]]) for i in ids]
        rhs = [brh[g][:c] + lkv[ci, g] for g in groups]
        tu = [_mm(tm1[ci, g], bd(rhs[g])) for g in groups]
        u = [rhs[g] + tu[g] for g in groups]
        rau = [_mm(ra[ci, g], bd(u[g])) for g in groups]
        upd = [_mm_tn(jnp.concatenate([u[g], at(v, (ci, g))], axis=0),
                      jnp.concatenate([-(a_h[ci, g] * w_end[ci, g]), k_h[ci, g] * w_end[ci, g]], axis=0))
               for g in groups]
        state_t = [w_end[ci, g] * state_t[g] + jnp.where(bdmask, upd[g], 0.0) for g in groups]
        o = [brh[g][c:] - rau[g] + rkv[ci, g] for g in groups]
        osum = [head_sum(o[g]) for g in groups]
        cen = [o[g] - osum[g] * (1.0 / hd) for g in groups]
        var = [head_sum(cen[g] * cen[g]) * (1.0 / hd) for g in groups]
        for g in groups:
            gs = gss[g]
            on = cen[g] * lax.rsqrt(var[g] + RWKV_GN_EPS) * lnx_w[:, gs] + lnx_b[:, gs]
            bonus = so[ci, g] * at(v, (ci, g))
            o_ref[sls[ci], gs] = ((on + bonus) * gate[sls[ci], gs]).astype(o_ref.dtype)
    for g in groups:
        st_scr[g] = state_t[g]


def _rwkv(z, bsz, seq, mu, vecs, wup_pad, aup_pad, gup_pad):
    m = z.shape[0]
    tt = 256
    nt = seq // tt
    gw = RWKV_GROUP
    width = RWKV_WIDTH
    zblk = lambda c0: pl.BlockSpec((tt, width), lambda b, t: (b * nt + t, c0 // width))
    lora_col = (COL_RWKV + 3 * width) // RWKV_LORA
    const2 = lambda b, t: (0, 0)
    return pl.pallas_call(
        _rwkv_body,
        grid=(bsz, nt),
        in_specs=[
            zblk(COL_RWKV), zblk(COL_RWKV + width), zblk(COL_RWKV + 2 * width),
            pl.BlockSpec((tt, RWKV_LORA), lambda b, t: (b * nt + t, lora_col)),
            pl.BlockSpec(mu.shape, const2),
            pl.BlockSpec((8, width), const2),
            pl.BlockSpec((RWKV_LORA, width), const2),
            pl.BlockSpec((RWKV_LORA, width), const2),
            pl.BlockSpec((RWKV_LORA, width), const2),
        ],
        out_specs=pl.BlockSpec((tt, width), lambda b, t: (b * nt + t, 0)),
        out_shape=jax.ShapeDtypeStruct((m, width), BF16),
        scratch_shapes=[pltpu.VMEM((8, width), F32)] * 3 + [pltpu.VMEM((8, RWKV_LORA), F32),
                                                            pltpu.VMEM((width // gw, gw, gw), F32)],
        compiler_params=_cparams("parallel", "arbitrary"),
        name="rwkv7",
    )(z, z, z, z, mu, vecs, wup_pad, aup_pad, gup_pad)


def _merge_body(h_ref, ya_ref, yb_ref, yc_ref, za_ref, zb_ref, zc_ref, mb_ref, pa_ref, pb_ref,
                pc_ref, wo_ref, nw_ref, hout_ref, xnt_ref):
    mb = mb_ref[...]
    ga = jax.nn.sigmoid(za_ref[...].astype(F32) + mb[:, 0:D_MODEL])
    gb = jax.nn.sigmoid(zb_ref[...].astype(F32) + mb[:, D_MODEL:2 * D_MODEL])
    gc = jax.nn.sigmoid(zc_ref[...].astype(F32) + mb[:, 2 * D_MODEL:])
    merged = (ga * jnp.dot(ya_ref[...], pa_ref[...], preferred_element_type=F32)
              + gb * jnp.dot(yb_ref[...], pb_ref[...], preferred_element_type=F32)
              + gc * jnp.dot(yc_ref[...], pc_ref[...], preferred_element_type=F32))
    h = h_ref[...] + _mm(merged, wo_ref[...])
    hout_ref[...] = h
    xnt_ref[...] = jnp.transpose(_rms(h, nw_ref[...])).astype(BF16)


def _merge(h, z, ya, yb, yc, merge_b, pa, pb, pc, wo, norm_w):
    m = h.shape[0]
    tm = 256
    row = lambda c: pl.BlockSpec((tm, D_MODEL), lambda i: (i, c))
    wspec = pl.BlockSpec((D_MODEL, D_MODEL), lambda i: (0, 0))
    mc = COL_MERGE // D_MODEL
    return pl.pallas_call(
        _merge_body,
        grid=(m // tm,),
        in_specs=[row(0), row(0), row(0), row(0), row(mc), row(mc + 1), row(mc + 2),
                  pl.BlockSpec((1, 3 * D_MODEL), lambda i: (0, 0)),
                  wspec, wspec, wspec, wspec,
                  pl.BlockSpec((1, D_MODEL), lambda i: (0, 0))],
        out_specs=[row(0), pl.BlockSpec((D_MODEL, tm), lambda i: (0, i))],
        out_shape=[jax.ShapeDtypeStruct((m, D_MODEL), F32), jax.ShapeDtypeStruct((D_MODEL, m), BF16)],
        compiler_params=_cparams("parallel"),
        name="merge_outproj",
    )(h, ya, yb, yc, z, z, z, merge_b, pa, pb, pc, wo, norm_w)


_CAND_SLABS = (
    (16, ((0, 0, 16),)),
    (16, ((1, 0, 8), (2, 8, 5), (4, 13, 3))),
    (16, ((3, 0, 4), (5, 4, 2), (6, 6, 2), (7, 8, 2), (8, 10, 1), (9, 11, 1), (10, 12, 1), (11, 13, 1),
          (12, 14, 1), (13, 15, 1))),
    (8, ((14, 0, 1), (15, 1, 1))),
)


def _odd_even_merge_sort_pairs(n):
    pairs = []
    p = 1
    while p < n:
        k = p
        while k >= 1:
            for j in range(k % p, n - k, 2 * k):
                for i in range(min(k, n - j - k)):
                    if (i + j) // (2 * p) == (i + j + k) // (2 * p):
                        pairs.append((i + j, i + j + k))
            k //= 2
        p *= 2
    return tuple(pairs)


_SORT_PAIRS = _odd_even_merge_sort_pairs(PEER_TOPK)


def _candidate_sums(a1, a2):
    slabs = []
    for nrows, pieces in _CAND_SLABS:
        base = a2[:nrows]
        rows = lax.broadcasted_iota(jnp.int32, base.shape, 0)
        out = None
        end = 0
        for p, off, cnt in pieces:
            val = a1[p:p + 1] + (base if off == 0 else pltpu.roll(base, off, 0))
            out = val if out is None else jnp.where(rows >= off, val, out)
            end = off + cnt
        if end < nrows:
            out = jnp.where(rows >= end, -BIG, out)
        slabs.append(out)
    return jnp.concatenate(slabs, axis=0)


def _count_leading(pred, thr):
    assert len(thr) == 16
    b8 = pred(thr[7])
    b4 = pred(jnp.where(b8, thr[11], thr[3]))
    b2 = pred(jnp.where(b8, jnp.where(b4, thr[13], thr[9]), jnp.where(b4, thr[5], thr[1])))
    lo = jnp.where(b4, jnp.where(b2, thr[6], thr[4]), jnp.where(b2, thr[2], thr[0]))
    hi = jnp.where(b4, jnp.where(b2, thr[14], thr[12]), jnp.where(b2, thr[10], thr[8]))
    b1 = pred(jnp.where(b8, hi, lo))
    cnt = (jnp.where(b8, 8.0, 0.0) + jnp.where(b4, 4.0, 0.0)) + (jnp.where(b2, 2.0, 0.0) + jnp.where(b1, 1.0, 0.0))
    return jnp.where(pred(thr[15]), 16.0, cnt)


def _peer_topk_body(xnt_ref, wqt_ref, keys_ref, n_ref, e1_ref, r2_ref, e2_ref, q_scr):
    tt = xnt_ref.shape[1]
    nk = PEER_NKEYS
    topk = PEER_TOPK
    q_scr[...] = jnp.dot(wqt_ref[...], xnt_ref[...], preferred_element_type=F32)
    lanes = 128
    sub = 8
    assert nk == topk * sub
    row8 = lax.broadcasted_iota(jnp.int32, (sub, lanes), 0)

    def top_sorted(s):
        v = [s[k * sub:(k + 1) * sub] for k in range(topk)]
        for i, j in _SORT_PAIRS:
            v[i], v[j] = jnp.maximum(v[i], v[j]), jnp.minimum(v[i], v[j])
        for shift in (4, 2, 1):
            other = [pltpu.roll(x, shift, 0) for x in v]
            v = [jnp.maximum(v[k], other[topk - 1 - k]) for k in range(topk)]
            d = topk // 2
            while d >= 1:
                for i in range(topk):
                    if i & d == 0:
                        v[i], v[i + d] = jnp.maximum(v[i], v[i + d]), jnp.minimum(v[i], v[i + d])
                d //= 2
        return v

    def compact(v):
        tiles = []
        for t0 in range(0, topk, sub):
            out = v[t0]
            for k in range(1, sub):
                out = jnp.where(row8 == k, v[t0 + k], out)
            tiles.append(out)
        return jnp.concatenate(tiles, axis=0)

    def head(h, carry):
        o1 = pl.multiple_of(h * (2 * nk), 2 * nk)
        s1 = _mm(keys_ref[2 * h], q_scr[pl.ds(o1, nk), :])
        s2 = _mm(keys_ref[2 * h + 1], q_scr[pl.ds(o1 + nk, nk), :])
        chunks = [slice(l0, l0 + lanes) for l0 in range(0, tt, lanes)]
        a1 = [top_sorted(s1[:, ls]) for ls in chunks]
        a2 = [top_sorted(s2[:, ls]) for ls in chunks]
        a1c = jnp.concatenate([compact(v) for v in a1], axis=1)
        a2c = jnp.concatenate([compact(v) for v in a2], axis=1)
        cand = _candidate_sums(a1c, a2c)
        cmax = a1c[0:1] + a2c[0:1]

        def cbody(rnd, carry):
            cnd, zsum, c_in, c_out = carry
            mx = jnp.max(cnd, axis=0, keepdims=True)
            zsum = zsum + jnp.where(rnd < topk, jnp.exp(mx - cmax), 0.0)
            c_in = jnp.where(rnd == topk - 1, mx, c_in)
            c_out = jnp.where(rnd == topk, mx, c_out)
            return jnp.where(cnd == mx, -BIG, cnd), zsum, c_in, c_out

        zero = jnp.zeros((1, tt), F32)
        _, zsum, c_in, c_out = lax.fori_loop(0, topk + 1, cbody, (cand, zero, zero, zero))
        tau = 0.5 * (c_in + c_out)
        inv_z = 1.0 / zsum
        pack = 16

        def rep(x):
            return jnp.concatenate([x] * (pack // sub), axis=0)

        for ci, ls in enumerate(chunks):
            tau8 = jnp.broadcast_to(tau[:, ls], (sub, lanes))
            need = [rep(tau8 - a2[ci][qq]) for qq in range(topk)]
            val2 = [rep(a2[ci][qq]) for qq in range(topk)]
            last1 = rep(a1[ci][topk - 1])
            top1 = rep(a1[ci][0])
            top2 = val2[0]
            for k in range(nk // pack):
                rows = slice(k * pack, (k + 1) * pack)
                x1 = s1[rows, ls]
                x2 = s2[rows, ls]
                n_sel = _count_leading(lambda t: x1 >= t, need)
                r2 = _count_leading(lambda t: x2 < t, val2)
                n_ref[h, rows, ls] = jnp.where(x1 >= last1, n_sel, 0.0)
                e1_ref[h, rows, ls] = jnp.exp(x1 - top1)
                r2_ref[h, rows, ls] = r2.astype(BF16)
                e2_ref[h, rows, ls] = (jnp.exp(x2 - top2) * inv_z[:, ls]).astype(BF16)
        return carry

    lax.fori_loop(0, PEER_HEADS, head, 0)


def _peer_topk(xnt, wqt, keys):
    m = xnt.shape[1]
    tt = 512 if m % 512 == 0 else 256
    shape = (PEER_HEADS, PEER_NKEYS, m)
    ospec = pl.BlockSpec((PEER_HEADS, PEER_NKEYS, tt), lambda i: (0, 0, i))
    return pl.pallas_call(
        _peer_topk_body,
        grid=(m // tt,),
        in_specs=[pl.BlockSpec((D_MODEL, tt), lambda i: (0, i)),
                  pl.BlockSpec(wqt.shape, lambda i: (0, 0)),
                  pl.BlockSpec(keys.shape, lambda i: (0, 0, 0))],
        out_specs=[ospec] * 4,
        out_shape=[jax.ShapeDtypeStruct(shape, F32), jax.ShapeDtypeStruct(shape, F32),
                   jax.ShapeDtypeStruct(shape, BF16), jax.ShapeDtypeStruct(shape, BF16)],
        scratch_shapes=[pltpu.VMEM((wqt.shape[0], tt), F32)],
        compiler_params=_cparams("parallel"),
        name="peer_topk",
    )(xnt, wqt, keys)


PEER_TOKEN_CHUNK = 256


def _peer_dense_body(u_ref, xnt_ref, vt_ref, n_ref, e1_ref, r2_ref, e2_ref, y_ref, acc_scr):
    nk = PEER_NKEYS
    tt = xnt_ref.shape[1]
    tc = PEER_TOKEN_CHUNK
    n_blk = u_ref.shape[0] // nk
    pack = 16

    @pl.when(pl.program_id(1) == 0)
    def _():
        acc_scr[...] = jnp.zeros_like(acc_scr)

    def rows_bf16(ref, h, ii, ls):
        row = jnp.broadcast_to(ref[h, ii:ii + 1, ls], (pack, tc)).astype(BF16)
        return jnp.concatenate([row] * (nk // pack), axis=0)

    def select_weights(ii, c0):
        ls = slice(c0, c0 + tc)
        w = None
        for h in range(PEER_HEADS):
            sel = jnp.where(r2_ref[h, :, ls] < rows_bf16(n_ref, h, ii, ls), e2_ref[h, :, ls],
                            jnp.zeros((), BF16))
            term = sel * rows_bf16(e1_ref, h, ii, ls)
            w = term if w is None else w + term
        return w

    def gelu_tanh(x):
        c = 0.7978845608028654
        inner = x * ((x * x) * (c * 0.044715) + c)
        hx = 0.5 * x
        return hx * jnp.tanh(inner) + hx

    starts = list(range(0, tt, tc))
    pre = [jnp.dot(u_ref[...], xnt_ref[:, c0:c0 + tc], preferred_element_type=F32) for c0 in starts]
    for idx, c0 in enumerate(starts):
        act = gelu_tanh(pre[idx].astype(BF16))
        a = jnp.concatenate([act[ii * nk:(ii + 1) * nk] * select_weights(ii, c0) for ii in range(n_blk)], axis=0)
        acc_scr[:, c0:c0 + tc] += jnp.dot(vt_ref[...], a, preferred_element_type=F32)

    @pl.when(pl.program_id(1) == pl.num_programs(1) - 1)
    def _():
        y_ref[...] = jnp.transpose(acc_scr[...])


def _peer_dense(xnt, u_bf, vt_bf, n_sel, e1, r2, e2, tt):
    m = xnt.shape[1]
    te = 2048
    ib = te // PEER_NKEYS
    sel_i = pl.BlockSpec((PEER_HEADS, ib, tt), lambda i, e: (0, e, i))
    sel_all = pl.BlockSpec((PEER_HEADS, PEER_NKEYS, tt), lambda i, e: (0, 0, i))
    return pl.pallas_call(
        _peer_dense_body,
        grid=(m // tt, PEER_EXPERTS // te),
        in_specs=[pl.BlockSpec((te, D_MODEL), lambda i, e: (e, 0)),
                  pl.BlockSpec((D_MODEL, tt), lambda i, e: (0, i)),
                  pl.BlockSpec((D_MODEL, te), lambda i, e: (0, e)),
                  sel_i, sel_i, sel_all, sel_all],
        out_specs=pl.BlockSpec((tt, D_MODEL), lambda i, e: (i, 0)),
        out_shape=jax.ShapeDtypeStruct((m, D_MODEL), F32),
        scratch_shapes=[pltpu.VMEM((D_MODEL, tt), F32)],
        compiler_params=_cparams("parallel", "arbitrary"),
        name="peer_dense",
    )(u_bf, xnt, vt_bf, n_sel, e1, r2, e2)


def _table_body(transpose, x_ref, o_ref):
    x = x_ref[...]
    o_ref[...] = (jnp.transpose(x) if transpose else x).astype(o_ref.dtype)


def _expert_table(tables, l, transpose):
    _, rows, cols = tables.shape
    tr = 512
    return pl.pallas_call(
        functools.partial(_table_body, transpose),
        grid=(rows // tr,),
        in_specs=[pl.BlockSpec((None, tr, cols), lambda i: (l, i, 0))],
        out_specs=pl.BlockSpec((cols, tr), lambda i: (0, i)) if transpose else pl.BlockSpec((tr, cols), lambda i: (i, 0)),
        out_shape=jax.ShapeDtypeStruct((cols, rows) if transpose else (rows, cols), BF16),
        compiler_params=_cparams("parallel"),
        name="expert_table",
    )(tables)


def _final_body(h_ref, y_ref, nw_ref, o_ref):
    o_ref[...] = _rms(h_ref[...] + y_ref[...], nw_ref[...])


def _final(h, y, norm_w):
    m = h.shape[0]
    tm = 512
    row = pl.BlockSpec((tm, D_MODEL), lambda i: (i, 0))
    return pl.pallas_call(
        _final_body,
        grid=(m // tm,),
        in_specs=[row, row, pl.BlockSpec((1, D_MODEL), lambda i: (0, 0))],
        out_specs=row,
        out_shape=jax.ShapeDtypeStruct((m, D_MODEL), F32),
        compiler_params=_cparams("parallel"),
        name="final_norm",
    )(h, y, norm_w.reshape(1, D_MODEL))


def _pad_rows(rows, width, n_rows=8):
    flat = [jnp.pad(r.reshape(-1).astype(F32), (0, width - r.size)) for r in rows]
    return jnp.pad(jnp.stack(flat), ((0, n_rows - len(rows)), (0, 0)))


def _chunk_rows(ab):
    m, n = ab.shape
    return jnp.transpose(ab.reshape(m // GDN_CHUNK, GDN_CHUNK, n), (0, 2, 1)).astype(F32)


def _block_diag_groups(w):
    per = LRU_GROUP // LRU_BLOCK_DIM
    n_grp = w.shape[0] // per
    wg = w.reshape(n_grp, per, LRU_BLOCK_DIM, LRU_BLOCK_DIM)
    eye = jnp.eye(per, dtype=w.dtype)
    out = wg[:, :, :, None, :] * eye[None, :, None, :, None]
    return out.reshape(n_grp, LRU_GROUP, LRU_GROUP).astype(BF16)


def _prep_w_in(w_all, l):
    pad = jnp.zeros((D_MODEL, IN_WIDTH_PAD - COL_AB - 2 * GDN_HEADS), BF16)
    cols = ((0, 6144), (9488, 12560), (6160, 9488), (6144, 6160))
    return jnp.concatenate([w_all[l, :, a:b].astype(BF16) for a, b in cols] + [pad], axis=1)


def _layer(l, h, y, bsz, seq, p):
    h, z, ab = _inproj(h, y, p["norm_mix_w"][l], _prep_w_in(p["w_in"], l))

    lru_vecs = _pad_rows([p["lru_conv_b"][l], p["lru_b_a"][l], p["lru_b_x"][l], p["lru_lambda"][l]], LRU_WIDTH)
    ya = _lru(z, bsz, seq, p["lru_conv_w"][l], lru_vecs,
              _block_diag_groups(p["lru_w_a"][l]), _block_diag_groups(p["lru_w_x"][l]))

    abt = _chunk_rows(ab[:, :2 * GDN_HEADS])
    gp = _pad_rows([p["gdn_a_log"][l], p["gdn_dt_bias"][l]], 128)
    gpt = jnp.pad(jnp.stack([p["gdn_a_log"][l], p["gdn_dt_bias"][l]], axis=1), ((0, 0), (0, 126)))
    yb = _gdn(z, ab, abt, bsz, seq, p["gdn_conv_w"][l], gp, gpt, p["gdn_norm_w"][l].reshape(1, GDN_DV))

    rw_vecs = _pad_rows([p["rwkv_w0"][l], p["rwkv_a0"][l], p["rwkv_k_k"][l], p["rwkv_k_a"][l],
                         p["rwkv_r_k"][l], p["rwkv_lnx_w"][l], p["rwkv_lnx_b"][l]], RWKV_WIDTH)
    wup = jnp.pad(p["rwkv_w_up"][l], ((0, 192), (0, 0))).astype(BF16)
    aup = jnp.pad(p["rwkv_a_up"][l], ((64, 128), (0, 0))).astype(BF16)
    gup = jnp.pad(p["rwkv_g_up"][l], ((128, 0), (0, 0))).astype(BF16)
    yc = _rwkv(z, bsz, seq, p["rwkv_mu"][l].reshape(1, -1), rw_vecs, wup, aup, gup)

    h, xnt = _merge(h, z, ya, yb, yc, p["merge_b"][l].reshape(1, -1), p["p_lru"][l].astype(BF16),
                    p["p_gdn"][l].astype(BF16), p["p_rwkv"][l].astype(BF16), p["w_out"][l].astype(BF16),
                    p["norm_ffn_w"][l].reshape(1, D_MODEL))

    wqt = jnp.transpose(p["peer_wq"][l]).astype(BF16)
    keys = p["peer_keys"][l].reshape(2 * PEER_HEADS, PEER_NKEYS, PEER_HALF).astype(BF16)
    n_sel, e1, r2, e2 = _peer_topk(xnt, wqt, keys)
    tt = 512 if xnt.shape[1] % 512 == 0 else 256
    y = _peer_dense(xnt, _expert_table(p["peer_u"], l, False), _expert_table(p["peer_v"], l, True),
                    n_sel, e1, r2, e2, tt)
    return h, y


def kernel(x, norm_mix_w, norm_ffn_w, final_norm_w, w_in, lru_conv_w, lru_conv_b, lru_w_a, lru_b_a,
           lru_w_x, lru_b_x, lru_lambda, gdn_conv_w, gdn_a_log, gdn_dt_bias, gdn_norm_w, rwkv_mu, rwkv_w0,
           rwkv_w_up, rwkv_a0, rwkv_a_up, rwkv_g_up, rwkv_k_k, rwkv_k_a, rwkv_r_k, rwkv_lnx_w, rwkv_lnx_b,
           merge_b, p_lru, p_gdn, p_rwkv, w_out, peer_wq, peer_keys, peer_u, peer_v):
    p = dict(locals())
    bsz, seq, dim = x.shape
    h = x.reshape(bsz * seq, dim)
    y = None
    for l in range(DEPTH):
        h, y = _layer(l, h, y, bsz, seq, p)
    return _final(h, y, final_norm_w).reshape(bsz, seq, dim)
```

```python
import functools

import jax
import jax.numpy as jnp
from jax import lax
from jax.experimental import pallas as pl
from jax.experimental.pallas import tpu as pltpu

F32 = jnp.float32
BF16 = jnp.bfloat16

D_MODEL = 1024
DEPTH = 2
RMS_EPS = 1e-6

LRU_WIDTH = 1024
LRU_BLOCK_DIM = 64
LRU_C = 8.0
LRU_GROUP = 256

GDN_HEADS = 8
GDN_DK = 128
GDN_DV = 128
GDN_CHUNK = 64

RWKV_HD = 64
RWKV_WIDTH = 1024
RWKV_GN_EPS = 64e-5
RWKV_CHUNK = 64
RWKV_GROUP = 256
RWKV_LORA = 256

PEER_HEADS = 8
PEER_NKEYS = 128
PEER_EXPERTS = PEER_NKEYS * PEER_NKEYS
PEER_HALF = 128
PEER_TOPK = 16

COL_U = 0
COL_GATE = 1024
COL_QKV = 2048
COL_ZG = 5120
COL_MERGE = 6144
COL_RWKV = 9216
COL_AB = 12544
IN_WIDTH_PAD = 12800
IN_TILE_N = 1280

VMEM_LIMIT = 48 * 1024 * 1024
BIG = 3.0e38


def _cparams(*sem):
    return pltpu.CompilerParams(dimension_semantics=sem, vmem_limit_bytes=VMEM_LIMIT)


def _mm(a, b):
    return jnp.dot(a.astype(BF16), b.astype(BF16), preferred_element_type=F32)


def _mm_nt(a, b):
    return lax.dot_general(a.astype(BF16), b.astype(BF16), (((1,), (1,)), ((), ())),
                           preferred_element_type=F32)


def _mm_tn(a, b):
    return lax.dot_general(a.astype(BF16), b.astype(BF16), (((0,), (0,)), ((), ())),
                           preferred_element_type=F32)


def _softplus(x):
    return jnp.maximum(x, 0.0) + jnp.log1p(jnp.exp(-jnp.abs(x)))


def _rms(x, w):
    return x * lax.rsqrt(jnp.mean(x * x, axis=-1, keepdims=True) + RMS_EPS) * w


def _shift_prev(x, prev8, s):
    r = pltpu.roll(x, s, 0)
    pr = pltpu.roll(prev8, s, 0)
    rows8 = lax.broadcasted_iota(jnp.int32, prev8.shape, 0)
    head = jnp.where(rows8 < s, pr, r[:8])
    return jnp.concatenate([head, r[8:]], axis=0)


def _shift_fill(x, d, fill):
    n, c = x.shape
    if d % 8 == 0:
        return jnp.concatenate([jnp.full((d, c), fill, x.dtype), x[:n - d]], axis=0)
    r = pltpu.roll(x, d, 0)
    rows8 = lax.broadcasted_iota(jnp.int32, (8, c), 0)
    head = jnp.where(rows8 < d, fill, r[:8])
    return jnp.concatenate([head, r[8:]], axis=0)


def _tri(n, strict):
    i = lax.broadcasted_iota(jnp.int32, (n, n), 0)
    j = lax.broadcasted_iota(jnp.int32, (n, n), 1)
    return (j < i) if strict else (j <= i)


def _split_bf16(x, terms):
    parts = []
    rem = x
    for i in range(terms):
        p = rem.astype(BF16)
        parts.append(p)
        if i + 1 < terms:
            rem = rem - p.astype(F32)
    return parts


def _mask_mm(mask_bf, x, terms=3):
    return sum(jnp.dot(mask_bf, p, preferred_element_type=F32) for p in _split_bf16(x, terms))


def _mm_mask(x, mask_bf, terms=3):
    return sum(jnp.dot(p, mask_bf, preferred_element_type=F32) for p in _split_bf16(x, terms))


def _inproj_body(h_ref, nw_ref, w_ref, z_ref, ab_ref, xn_scr):
    @pl.when(pl.program_id(1) == 0)
    def _():
        xn_scr[...] = _rms(h_ref[...], nw_ref[...]).astype(BF16)

    res = jnp.dot(xn_scr[...], w_ref[...], preferred_element_type=F32)
    z_ref[...] = res.astype(z_ref.dtype)

    @pl.when(pl.program_id(1) == pl.num_programs(1) - 1)
    def _():
        c0 = COL_AB - (IN_WIDTH_PAD - res.shape[1])
        ab_ref[...] = res[:, c0:c0 + ab_ref.shape[1]]


def _inproj(h, norm_w, w_pad):
    m = h.shape[0]
    tm, tn = (1024 if m % 1024 == 0 else 512), IN_TILE_N
    assert IN_WIDTH_PAD % tn == 0 and COL_AB >= IN_WIDTH_PAD - tn
    return pl.pallas_call(
        _inproj_body,
        grid=(m // tm, IN_WIDTH_PAD // tn),
        in_specs=[pl.BlockSpec((tm, D_MODEL), lambda i, j: (i, 0)),
                  pl.BlockSpec((1, D_MODEL), lambda i, j: (0, 0)),
                  pl.BlockSpec((D_MODEL, tn), lambda i, j: (0, j))],
        out_specs=[pl.BlockSpec((tm, tn), lambda i, j: (i, j)), pl.BlockSpec((tm, 128), lambda i, j: (i, 0))],
        out_shape=[jax.ShapeDtypeStruct((m, IN_WIDTH_PAD), BF16), jax.ShapeDtypeStruct((m, 128), F32)],
        scratch_shapes=[pltpu.VMEM((tm, D_MODEL), BF16)],
        compiler_params=_cparams("parallel", "arbitrary"),
        name="norm_inproj",
    )(h, norm_w.reshape(1, D_MODEL), w_pad)


def _lru_body(zu_ref, zg_ref, cw_ref, vp_ref, wa_ref, wx_ref, o_ref, prev_scr, hc_scr):
    tt = zu_ref.shape[0]

    @pl.when(pl.program_id(1) == 0)
    def _():
        prev_scr[...] = jnp.zeros_like(prev_scr)
        hc_scr[...] = jnp.zeros_like(hc_scr)

    u = zu_ref[...].astype(F32)
    prev = prev_scr[...]
    cw = cw_ref[...]
    vp = vp_ref[...]
    xc = (cw[3:4] * u + cw[2:3] * _shift_prev(u, prev, 1) + cw[1:2] * _shift_prev(u, prev, 2)
          + cw[0:1] * _shift_prev(u, prev, 3) + vp[0:1])
    prev_scr[...] = u[tt - 8:]

    xcb = xc.astype(BF16)
    n_grp = LRU_WIDTH // LRU_GROUP
    pre_a = jnp.concatenate(
        [jnp.dot(xcb[:, g * LRU_GROUP:(g + 1) * LRU_GROUP], wa_ref[g], preferred_element_type=F32)
         for g in range(n_grp)], axis=1)
    pre_x = jnp.concatenate(
        [jnp.dot(xcb[:, g * LRU_GROUP:(g + 1) * LRU_GROUP], wx_ref[g], preferred_element_type=F32)
         for g in range(n_grp)], axis=1)
    r = jax.nn.sigmoid(pre_a + vp[1:2])
    i = jax.nn.sigmoid(pre_x + vp[2:3])
    log_a = (-LRU_C * r) * _softplus(-vp[3:4])
    a = jnp.exp(log_a)
    b = jnp.sqrt(-jnp.tanh(log_a) * (a * a + 1.0)) * (i * xc)

    d = 1
    while d < tt:
        b = a * _shift_fill(b, d, 0.0) + b
        a = a * _shift_fill(a, d, 1.0)
        d *= 2
    hcar = hc_scr[...]
    hval = b + a * hcar[0:1]
    hc_scr[...] = jnp.broadcast_to(hval[tt - 1:tt], hcar.shape)
    o_ref[...] = (hval * jax.nn.gelu(zg_ref[...].astype(F32))).astype(o_ref.dtype)


def _lru(z, bsz, seq, conv_w, vecs, wa_bd, wx_bd):
    m = z.shape[0]
    tt = 256
    nt = seq // tt
    rowmap = lambda c: (lambda b, t: (b * nt + t, c))
    const2 = lambda b, t: (0, 0)
    return pl.pallas_call(
        _lru_body,
        grid=(bsz, nt),
        in_specs=[
            pl.BlockSpec((tt, LRU_WIDTH), rowmap(COL_U // LRU_WIDTH)),
            pl.BlockSpec((tt, LRU_WIDTH), rowmap(COL_GATE // LRU_WIDTH)),
            pl.BlockSpec((4, LRU_WIDTH), const2),
            pl.BlockSpec((8, LRU_WIDTH), const2),
            pl.BlockSpec(wa_bd.shape, lambda b, t: (0, 0, 0)),
            pl.BlockSpec(wx_bd.shape, lambda b, t: (0, 0, 0)),
        ],
        out_specs=pl.BlockSpec((tt, LRU_WIDTH), rowmap(0)),
        out_shape=jax.ShapeDtypeStruct((m, LRU_WIDTH), BF16),
        scratch_shapes=[pltpu.VMEM((8, LRU_WIDTH), F32), pltpu.VMEM((8, LRU_WIDTH), F32)],
        compiler_params=_cparams("parallel", "arbitrary"),
        name="rglru",
    )(z, z, conv_w, vecs, wa_bd, wx_bd)


def _gdn_body(q_ref, k_ref, v_ref, zg_ref, ab_ref, abt_ref, cw_ref, gp_ref, gpt_ref, nw_ref,
              o_ref, pq_scr, pk_scr, pv_scr, st_scr):
    tt = q_ref.shape[0]
    c = GDN_CHUNK
    width = GDN_HEADS * GDN_DK

    @pl.when(pl.program_id(1) == 0)
    def _():
        for s in (pq_scr, pk_scr, pv_scr, st_scr):
            s[...] = jnp.zeros_like(s)

    cw = cw_ref[...]

    def conv_silu(x_ref, p_scr, w):
        x = x_ref[...].astype(F32)
        prev = p_scr[...]
        y = (w[3:4] * x + w[2:3] * _shift_prev(x, prev, 1) + w[1:2] * _shift_prev(x, prev, 2)
             + w[0:1] * _shift_prev(x, prev, 3))
        p_scr[...] = x[tt - 8:]
        return jax.nn.silu(y)

    q = conv_silu(q_ref, pq_scr, cw[:, 0:width])
    k = conv_silu(k_ref, pk_scr, cw[:, width:2 * width])
    v = conv_silu(v_ref, pv_scr, cw[:, 2 * width:])

    ab = ab_ref[...].astype(F32)
    gp = gp_ref[...]
    gpt = gpt_ref[...]
    g_all = -jnp.exp(gp[0:1]) * _softplus(ab + gp[1:2])
    beta_all = jax.nn.sigmoid(ab)

    incl = _tri(c, False)
    strict = _tri(c, True)
    cum_mask = jnp.concatenate([incl.astype(BF16), jnp.ones((c, c), BF16)], axis=0)
    triu_bf = (lax.broadcasted_iota(jnp.int32, (c, c), 0) <= lax.broadcasted_iota(jnp.int32, (c, c), 1)).astype(BF16)
    nw = nw_ref[...]

    n_chunks = tt // c
    heads = range(GDN_HEADS)
    inst = [(ci, h) for ci in range(n_chunks) for h in heads]
    sls = [slice(ci * c, (ci + 1) * c) for ci in range(n_chunks)]
    hss = [slice(h * GDN_DK, (h + 1) * GDN_DK) for h in heads]
    gcs = [_mask_mm(cum_mask, g_all[sl]) for sl in sls]
    gr_all = [_mm_mask(-jnp.exp(gpt[:, 0:1]) * _softplus(abt_ref[ci][0:GDN_HEADS] + gpt[:, 1:2]), triu_bf)
              for ci in range(n_chunks)]

    qs, ks, kbs, decays, egcs, kdecs, gtots, bcols = {}, {}, {}, {}, {}, {}, {}, {}
    for ci, h in inst:
        sl, hs = sls[ci], hss[h]
        qc, kc = q[sl, hs], k[sl, hs]
        qs[ci, h] = qc * lax.rsqrt(jnp.sum(qc * qc, axis=-1, keepdims=True) + RMS_EPS) * (GDN_DK ** -0.5)
        ks[ci, h] = kc * lax.rsqrt(jnp.sum(kc * kc, axis=-1, keepdims=True) + RMS_EPS)
        gcol = gcs[ci][:c, h:h + 1]
        g_last = gcs[ci][c:, h:h + 1]
        grow = gr_all[ci][h:h + 1, :]
        bcols[ci, h] = beta_all[sl, GDN_HEADS + h:GDN_HEADS + h + 1]
        decays[ci, h] = jnp.exp(jnp.where(incl, gcol - grow, -BIG))
        kbs[ci, h] = ks[ci, h] * bcols[ci, h]
        egcs[ci, h] = jnp.exp(gcol)
        kdecs[ci, h] = ks[ci, h] * jnp.exp(g_last - gcol)
        gtots[ci, h] = jnp.exp(jnp.concatenate([g_last] * (GDN_DK // c), axis=0))

    kk = {i: _mm_nt(kbs[i], ks[i]) for i in inst}
    qk = {i: _mm_nt(qs[i], ks[i]) * decays[i] for i in inst}
    p = {i: -jnp.where(strict, kk[i] * decays[i], 0.0) for i in inst}
    tm1 = dict(p)
    for _ in range(5):
        p = {i: _mm(p[i], p[i]) for i in inst}
        tp = {i: _mm(tm1[i], p[i]) for i in inst}
        tm1 = {i: tm1[i] + tp[i] + p[i] for i in inst}
    rhs = {i: jnp.concatenate([v[sls[i[0]], hss[i[1]]] * bcols[i], kbs[i] * egcs[i]], axis=1) for i in inst}
    tr = {i: _mm(tm1[i], rhs[i]) for i in inst}
    sol = {i: rhs[i] + tr[i] for i in inst}

    state = [st_scr[h] for h in heads]
    for ci in range(n_chunks):
        ws = [_mm(sol[ci, h][:, GDN_DV:], state[h]) for h in heads]
        qst = [_mm(qs[ci, h] * egcs[ci, h], state[h]) for h in heads]
        v_new = [sol[ci, h][:, :GDN_DV] - ws[h] for h in heads]
        qv = [_mm(qk[ci, h], v_new[h]) for h in heads]
        kv = [_mm_tn(kdecs[ci, h], v_new[h]) for h in heads]
        state = [state[h] * gtots[ci, h] + kv[h] for h in heads]
        for h in heads:
            o = qst[h] + qv[h]
            o = o * lax.rsqrt(jnp.mean(o * o, axis=-1, keepdims=True) + RMS_EPS) * nw
            o = o * jax.nn.silu(zg_ref[sls[ci], hss[h]].astype(F32))
            o_ref[sls[ci], hss[h]] = o.astype(o_ref.dtype)
    for h in heads:
        st_scr[h] = state[h]


def _gdn(z, ab, abt, bsz, seq, conv_w, gp, gpt, norm_w):
    m = z.shape[0]
    tt = 256
    nt = seq // tt
    width = GDN_HEADS * GDN_DK
    zblk = lambda c0: pl.BlockSpec((tt, width), lambda b, t: (b * nt + t, c0 // width))
    const2 = lambda b, t: (0, 0)
    return pl.pallas_call(
        _gdn_body,
        grid=(bsz, nt),
        in_specs=[
            zblk(COL_QKV), zblk(COL_QKV + width), zblk(COL_QKV + 2 * width), zblk(COL_ZG),
            pl.BlockSpec((tt, 128), lambda b, t: (b * nt + t, 0)),
            pl.BlockSpec((tt // GDN_CHUNK, 2 * GDN_HEADS, GDN_CHUNK), lambda b, t: (b * nt + t, 0, 0)),
            pl.BlockSpec((4, 3 * width), const2),
            pl.BlockSpec((8, 128), const2),
            pl.BlockSpec((8, 128), const2),
            pl.BlockSpec((1, 128), const2),
        ],
        out_specs=pl.BlockSpec((tt, width), lambda b, t: (b * nt + t, 0)),
        out_shape=jax.ShapeDtypeStruct((m, width), BF16),
        scratch_shapes=[pltpu.VMEM((8, width), F32)] * 3 + [pltpu.VMEM((GDN_HEADS, GDN_DK, GDN_DV), F32)],
        compiler_params=_cparams("parallel", "arbitrary"),
        name="gated_deltanet",
    )(z, z, z, z, ab, abt, conv_w, gp, gpt, norm_w)


def _rwkv_body(r_ref, k_ref, v_ref, lo_ref, mu_ref, vp_ref, wup_ref, aup_ref, gup_ref, o_ref,
               pr_scr, pk_scr, pv_scr, plo_scr, st_scr):
    tt = r_ref.shape[0]
    c = RWKV_CHUNK
    gw = RWKV_GROUP
    hd = RWKV_HD
    width = RWKV_WIDTH

    @pl.when(pl.program_id(1) == 0)
    def _():
        for s in (pr_scr, pk_scr, pv_scr, plo_scr, st_scr):
            s[...] = jnp.zeros_like(s)

    mu = mu_ref[...]

    def tshift(x_ref, p_scr, m):
        x = x_ref[...].astype(F32)
        xs = x + m * (_shift_prev(x, p_scr[...], 1) - x)
        p_scr[...] = x[tt - 8:]
        return xs

    r = tshift(r_ref, pr_scr, mu[:, 0:width])
    k = tshift(k_ref, pk_scr, mu[:, width:2 * width])
    v = tshift(v_ref, pv_scr, mu[:, 2 * width:3 * width])
    lo = tshift(lo_ref, plo_scr, mu[:, 3 * width:])
    vp = vp_ref[...]
    w0, a0, k_k, k_a, r_k, lnx_w, lnx_b = (vp[i:i + 1] for i in range(7))

    lane = lax.broadcasted_iota(jnp.int32, lo.shape, 1)
    lo_act = jnp.where(lane < 64, jnp.tanh(lo), jnp.where(lane < 128, lo, jax.nn.sigmoid(lo)))
    w_pre = _mm(lo_act, wup_ref[...])
    a_pre = _mm(lo_act, aup_ref[...])
    gate = _mm(lo_act, gup_ref[...])
    w_log = -_softplus(-(w0 + w_pre)) - 0.5
    lw = -jnp.exp(w_log)
    a = jax.nn.sigmoid(a0 + a_pre)

    bi = lax.broadcasted_iota(jnp.int32, (gw, gw), 0) // hd
    bj = lax.broadcasted_iota(jnp.int32, (gw, gw), 1) // hd
    bdmask = bi == bj
    ones_bd = bdmask.astype(BF16)

    def head_sum(x):
        return _mm_mask(x, ones_bd, terms=1)

    def bd(x):
        return jnp.where(bdmask, jnp.concatenate([x] * (gw // c), axis=0), 0.0)

    ti = lax.broadcasted_iota(jnp.int32, (c, gw), 0)
    sj = lax.broadcasted_iota(jnp.int32, (c, gw), 1) % c
    strict = sj < ti
    incl = sj <= ti
    tril_bf = _tri(c, False).astype(BF16)

    kk_all = k * k_k
    k2_all = k * (1.0 + (a - 1.0) * k_a)

    n_chunks = tt // c
    groups = range(width // gw)
    inst = [(ci, g) for ci in range(n_chunks) for g in groups]
    sls = [slice(ci * c, (ci + 1) * c) for ci in range(n_chunks)]
    gss = [slice(g * gw, (g + 1) * gw) for g in groups]
    cl_all = [_mask_mm(tril_bf, lw[sl]) for sl in sls]

    def at(x, i):
        return x[sls[i[0]], gss[i[1]]]

    kk_raw = {i: at(kk_all, i) for i in inst}
    kk_ss = {i: head_sum(kk_raw[i] * kk_raw[i]) for i in inst}
    a_h, b_h, k_h, br, w_end = {}, {}, {}, {}, {}
    for i in inst:
        cl = cl_all[i[0]][:, gss[i[1]]]
        kk = kk_raw[i] * lax.rsqrt(kk_ss[i] + RMS_EPS)
        wcum = jnp.exp(cl)
        inv_w = jnp.exp(-cl)
        a_h[i] = kk * at(a, i) * inv_w
        b_h[i] = kk * jnp.exp(cl - at(lw, i))
        k_h[i] = at(k2_all, i) * inv_w
        br[i] = jnp.concatenate([b_h[i], at(r, i) * wcum], axis=0)
        w_end[i] = wcum[c - 1:c]
    v_bd = {i: bd(at(v, i)) for i in inst}
    xa = {i: _mm_nt(br[i], bd(a_h[i])) for i in inst}
    xk = {i: _mm_nt(br[i], bd(k_h[i])) for i in inst}
    l_k = {i: jnp.where(strict, xk[i][:c], 0.0) for i in inst}
    ra = {i: jnp.where(incl, xa[i][c:], 0.0) for i in inst}
    rk = {i: jnp.where(incl, xk[i][c:], 0.0) for i in inst}
    p = {i: -jnp.where(strict, xa[i][:c], 0.0) for i in inst}
    tm1 = dict(p)
    for _ in range(5):
        p = {i: _mm(p[i], bd(p[i])) for i in inst}
        tp = {i: _mm(tm1[i], bd(p[i])) for i in inst}
        tm1 = {i: tm1[i] + tp[i] + p[i] for i in inst}
    lkv = {i: _mm(l_k[i], v_bd[i]) for i in inst}
    rkv = {i: _mm(rk[i], v_bd[i]) for i in inst}
    so = {i: head_sum(at(r, i) * at(k2_all, i) * r_k[:, gss[i[1]]]) for i in inst}

    state_t = [st_scr[g] for g in groups]
    for ci in range(n_chunks):
        ids = [(ci, g) for g in groups]
        brh = [_mm_nt(br[i], state_t[i[1]]) for i in ids]
        rhs = [brh[g][:c] + lkv[ci, g] for g in groups]
        tu = [_mm(tm1[ci, g], bd(rhs[g])) for g in groups]
        u = [rhs[g] + tu[g] for g in groups]
        rau = [_mm(ra[ci, g], bd(u[g])) for g in groups]
        upd = [_mm_tn(jnp.concatenate([u[g], at(v, (ci, g))], axis=0),
                      jnp.concatenate([-(a_h[ci, g] * w_end[ci, g]), k_h[ci, g] * w_end[ci, g]], axis=0))
               for g in groups]
        state_t = [w_end[ci, g] * state_t[g] + jnp.where(bdmask, upd[g], 0.0) for g in groups]
        o = [brh[g][c:] - rau[g] + rkv[ci, g] for g in groups]
        osum = [head_sum(o[g]) for g in groups]
        cen = [o[g] - osum[g] * (1.0 / hd) for g in groups]
        var = [head_sum(cen[g] * cen[g]) * (1.0 / hd) for g in groups]
        for g in groups:
            gs = gss[g]
            on = cen[g] * lax.rsqrt(var[g] + RWKV_GN_EPS) * lnx_w[:, gs] + lnx_b[:, gs]
            bonus = so[ci, g] * at(v, (ci, g))
            o_ref[sls[ci], gs] = ((on + bonus) * gate[sls[ci], gs]).astype(o_ref.dtype)
    for g in groups:
        st_scr[g] = state_t[g]


def _rwkv(z, bsz, seq, mu, vecs, wup_pad, aup_pad, gup_pad):
    m = z.shape[0]
    tt = 256
    nt = seq // tt
    gw = RWKV_GROUP
    width = RWKV_WIDTH
    zblk = lambda c0: pl.BlockSpec((tt, width), lambda b, t: (b * nt + t, c0 // width))
    lora_col = (COL_RWKV + 3 * width) // RWKV_LORA
    const2 = lambda b, t: (0, 0)
    return pl.pallas_call(
        _rwkv_body,
        grid=(bsz, nt),
        in_specs=[
            zblk(COL_RWKV), zblk(COL_RWKV + width), zblk(COL_RWKV + 2 * width),
            pl.BlockSpec((tt, RWKV_LORA), lambda b, t: (b * nt + t, lora_col)),
            pl.BlockSpec(mu.shape, const2),
            pl.BlockSpec((8, width), const2),
            pl.BlockSpec((RWKV_LORA, width), const2),
            pl.BlockSpec((RWKV_LORA, width), const2),
            pl.BlockSpec((RWKV_LORA, width), const2),
        ],
        out_specs=pl.BlockSpec((tt, width), lambda b, t: (b * nt + t, 0)),
        out_shape=jax.ShapeDtypeStruct((m, width), BF16),
        scratch_shapes=[pltpu.VMEM((8, width), F32)] * 3 + [pltpu.VMEM((8, RWKV_LORA), F32),
                                                            pltpu.VMEM((width // gw, gw, gw), F32)],
        compiler_params=_cparams("parallel", "arbitrary"),
        name="rwkv7",
    )(z, z, z, z, mu, vecs, wup_pad, aup_pad, gup_pad)


def _merge_body(h_ref, ya_ref, yb_ref, yc_ref, za_ref, zb_ref, zc_ref, mb_ref, pa_ref, pb_ref,
                pc_ref, wo_ref, nw_ref, hout_ref, xnt_ref):
    mb = mb_ref[...]
    ga = jax.nn.sigmoid(za_ref[...].astype(F32) + mb[:, 0:D_MODEL])
    gb = jax.nn.sigmoid(zb_ref[...].astype(F32) + mb[:, D_MODEL:2 * D_MODEL])
    gc = jax.nn.sigmoid(zc_ref[...].astype(F32) + mb[:, 2 * D_MODEL:])
    merged = (ga * jnp.dot(ya_ref[...], pa_ref[...], preferred_element_type=F32)
              + gb * jnp.dot(yb_ref[...], pb_ref[...], preferred_element_type=F32)
              + gc * jnp.dot(yc_ref[...], pc_ref[...], preferred_element_type=F32))
    h = h_ref[...] + _mm(merged, wo_ref[...])
    hout_ref[...] = h
    xnt_ref[...] = jnp.transpose(_rms(h, nw_ref[...])).astype(BF16)


def _merge(h, z, ya, yb, yc, merge_b, pa, pb, pc, wo, norm_w):
    m = h.shape[0]
    tm = 256
    row = lambda c: pl.BlockSpec((tm, D_MODEL), lambda i: (i, c))
    wspec = pl.BlockSpec((D_MODEL, D_MODEL), lambda i: (0, 0))
    mc = COL_MERGE // D_MODEL
    return pl.pallas_call(
        _merge_body,
        grid=(m // tm,),
        in_specs=[row(0), row(0), row(0), row(0), row(mc), row(mc + 1), row(mc + 2),
                  pl.BlockSpec((1, 3 * D_MODEL), lambda i: (0, 0)),
                  wspec, wspec, wspec, wspec,
                  pl.BlockSpec((1, D_MODEL), lambda i: (0, 0))],
        out_specs=[row(0), pl.BlockSpec((D_MODEL, tm), lambda i: (0, i))],
        out_shape=[jax.ShapeDtypeStruct((m, D_MODEL), F32), jax.ShapeDtypeStruct((D_MODEL, m), BF16)],
        compiler_params=_cparams("parallel"),
        name="merge_outproj",
    )(h, ya, yb, yc, z, z, z, merge_b, pa, pb, pc, wo, norm_w)


_CAND_SLABS = (
    (16, ((0, 0, 16),)),
    (16, ((1, 0, 8), (2, 8, 5), (4, 13, 3))),
    (16, ((3, 0, 4), (5, 4, 2), (6, 6, 2), (7, 8, 2), (8, 10, 1), (9, 11, 1), (10, 12, 1), (11, 13, 1),
          (12, 14, 1), (13, 15, 1))),
    (8, ((14, 0, 1), (15, 1, 1))),
)


def _odd_even_merge_sort_pairs(n):
    pairs = []
    p = 1
    while p < n:
        k = p
        while k >= 1:
            for j in range(k % p, n - k, 2 * k):
                for i in range(min(k, n - j - k)):
                    if (i + j) // (2 * p) == (i + j + k) // (2 * p):
                        pairs.append((i + j, i + j + k))
            k //= 2
        p *= 2
    return tuple(pairs)


_SORT_PAIRS = _odd_even_merge_sort_pairs(PEER_TOPK)


def _candidate_sums(a1, a2):
    slabs = []
    for nrows, pieces in _CAND_SLABS:
        base = a2[:nrows]
        rows = lax.broadcasted_iota(jnp.int32, base.shape, 0)
        out = None
        end = 0
        for p, off, cnt in pieces:
            val = a1[p:p + 1] + (base if off == 0 else pltpu.roll(base, off, 0))
            out = val if out is None else jnp.where(rows >= off, val, out)
            end = off + cnt
        if end < nrows:
            out = jnp.where(rows >= end, -BIG, out)
        slabs.append(out)
    return jnp.concatenate(slabs, axis=0)


def _count_leading(pred, thr):
    assert len(thr) == 16
    b8 = pred(thr[7])
    b4 = pred(jnp.where(b8, thr[11], thr[3]))
    b2 = pred(jnp.where(b8, jnp.where(b4, thr[13], thr[9]), jnp.where(b4, thr[5], thr[1])))
    lo = jnp.where(b4, jnp.where(b2, thr[6], thr[4]), jnp.where(b2, thr[2], thr[0]))
    hi = jnp.where(b4, jnp.where(b2, thr[14], thr[12]), jnp.where(b2, thr[10], thr[8]))
    b1 = pred(jnp.where(b8, hi, lo))
    cnt = (jnp.where(b8, 8.0, 0.0) + jnp.where(b4, 4.0, 0.0)) + (jnp.where(b2, 2.0, 0.0) + jnp.where(b1, 1.0, 0.0))
    return jnp.where(pred(thr[15]), 16.0, cnt)


def _peer_topk_body(xnt_ref, wqt_ref, keys_ref, n_ref, e1_ref, r2_ref, e2_ref, q_scr):
    tt = xnt_ref.shape[1]
    nk = PEER_NKEYS
    topk = PEER_TOPK
    q_scr[...] = jnp.dot(wqt_ref[...], xnt_ref[...], preferred_element_type=F32)
    lanes = 128
    sub = 8
    assert nk == topk * sub
    row8 = lax.broadcasted_iota(jnp.int32, (sub, lanes), 0)

    def top_sorted(s):
        v = [s[k * sub:(k + 1) * sub] for k in range(topk)]
        for i, j in _SORT_PAIRS:
            v[i], v[j] = jnp.maximum(v[i], v[j]), jnp.minimum(v[i], v[j])
        for shift in (4, 2, 1):
            other = [pltpu.roll(x, shift, 0) for x in v]
            v = [jnp.maximum(v[k], other[topk - 1 - k]) for k in range(topk)]
            d = topk // 2
            while d >= 1:
                for i in range(topk):
                    if i & d == 0:
                        v[i], v[i + d] = jnp.maximum(v[i], v[i + d]), jnp.minimum(v[i], v[i + d])
                d //= 2
        return v

    def compact(v):
        tiles = []
        for t0 in range(0, topk, sub):
            out = v[t0]
            for k in range(1, sub):
                out = jnp.where(row8 == k, v[t0 + k], out)
            tiles.append(out)
        return jnp.concatenate(tiles, axis=0)

    def head(h, carry):
        o1 = pl.multiple_of(h * (2 * nk), 2 * nk)
        s1 = _mm(keys_ref[2 * h], q_scr[pl.ds(o1, nk), :])
        s2 = _mm(keys_ref[2 * h + 1], q_scr[pl.ds(o1 + nk, nk), :])
        chunks = [slice(l0, l0 + lanes) for l0 in range(0, tt, lanes)]
        a1 = [top_sorted(s1[:, ls]) for ls in chunks]
        a2 = [top_sorted(s2[:, ls]) for ls in chunks]
        a1c = jnp.concatenate([compact(v) for v in a1], axis=1)
        a2c = jnp.concatenate([compact(v) for v in a2], axis=1)
        cand = _candidate_sums(a1c, a2c)
        cmax = a1c[0:1] + a2c[0:1]

        def cbody(rnd, carry):
            cnd, zsum, c_in, c_out = carry
            mx = jnp.max(cnd, axis=0, keepdims=True)
            zsum = zsum + jnp.where(rnd < topk, jnp.exp(mx - cmax), 0.0)
            c_in = jnp.where(rnd == topk - 1, mx, c_in)
            c_out = jnp.where(rnd == topk, mx, c_out)
            return jnp.where(cnd == mx, -BIG, cnd), zsum, c_in, c_out

        zero = jnp.zeros((1, tt), F32)
        _, zsum, c_in, c_out = lax.fori_loop(0, topk + 1, cbody, (cand, zero, zero, zero))
        tau = 0.5 * (c_in + c_out)
        inv_z = 1.0 / zsum
        pack = 16

        def rep(x):
            return jnp.concatenate([x] * (pack // sub), axis=0)

        for ci, ls in enumerate(chunks):
            tau8 = jnp.broadcast_to(tau[:, ls], (sub, lanes))
            need = [rep(tau8 - a2[ci][qq]) for qq in range(topk)]
            val2 = [rep(a2[ci][qq]) for qq in range(topk)]
            last1 = rep(a1[ci][topk - 1])
            top1 = rep(a1[ci][0])
            top2 = val2[0]
            for k in range(nk // pack):
                rows = slice(k * pack, (k + 1) * pack)
                x1 = s1[rows, ls]
                x2 = s2[rows, ls]
                n_sel = _count_leading(lambda t: x1 >= t, need)
                r2 = _count_leading(lambda t: x2 < t, val2)
                n_ref[h, rows, ls] = jnp.where(x1 >= last1, n_sel, 0.0)
                e1_ref[h, rows, ls] = jnp.exp(x1 - top1)
                r2_ref[h, rows, ls] = r2.astype(BF16)
                e2_ref[h, rows, ls] = (jnp.exp(x2 - top2) * inv_z[:, ls]).astype(BF16)
        return carry

    lax.fori_loop(0, PEER_HEADS, head, 0)


def _peer_topk(xnt, wqt, keys):
    m = xnt.shape[1]
    tt = 512 if m % 512 == 0 else 256
    shape = (PEER_HEADS, PEER_NKEYS, m)
    ospec = pl.BlockSpec((PEER_HEADS, PEER_NKEYS, tt), lambda i: (0, 0, i))
    return pl.pallas_call(
        _peer_topk_body,
        grid=(m // tt,),
        in_specs=[pl.BlockSpec((D_MODEL, tt), lambda i: (0, i)),
                  pl.BlockSpec(wqt.shape, lambda i: (0, 0)),
                  pl.BlockSpec(keys.shape, lambda i: (0, 0, 0))],
        out_specs=[ospec] * 4,
        out_shape=[jax.ShapeDtypeStruct(shape, F32), jax.ShapeDtypeStruct(shape, F32),
                   jax.ShapeDtypeStruct(shape, BF16), jax.ShapeDtypeStruct(shape, BF16)],
        scratch_shapes=[pltpu.VMEM((wqt.shape[0], tt), F32)],
        compiler_params=_cparams("parallel"),
        name="peer_topk",
    )(xnt, wqt, keys)


PEER_TOKEN_CHUNK = 256


def _peer_dense_body(u_ref, xnt_ref, vt_ref, n_ref, e1_ref, r2_ref, e2_ref, h_ref, hout_ref, acc_scr):
    nk = PEER_NKEYS
    tt = xnt_ref.shape[1]
    tc = PEER_TOKEN_CHUNK
    n_blk = u_ref.shape[0] // nk
    pack = 16

    @pl.when(pl.program_id(1) == 0)
    def _():
        acc_scr[...] = jnp.zeros_like(acc_scr)

    def rows_bf16(ref, h, ii, ls):
        row = jnp.broadcast_to(ref[h, ii:ii + 1, ls], (pack, tc)).astype(BF16)
        return jnp.concatenate([row] * (nk // pack), axis=0)

    def select_weights(ii, c0):
        ls = slice(c0, c0 + tc)
        w = None
        for h in range(PEER_HEADS):
            sel = jnp.where(r2_ref[h, :, ls] < rows_bf16(n_ref, h, ii, ls), e2_ref[h, :, ls],
                            jnp.zeros((), BF16))
            term = sel * rows_bf16(e1_ref, h, ii, ls)
            w = term if w is None else w + term
        return w

    def gelu_tanh(x):
        c = 0.7978845608028654
        inner = x * ((x * x) * (c * 0.044715) + c)
        hx = 0.5 * x
        return hx * jnp.tanh(inner) + hx

    starts = list(range(0, tt, tc))
    pre = [jnp.dot(u_ref[...], xnt_ref[:, c0:c0 + tc], preferred_element_type=F32) for c0 in starts]
    for idx, c0 in enumerate(starts):
        act = gelu_tanh(pre[idx].astype(BF16))
        a = jnp.concatenate([act[ii * nk:(ii + 1) * nk] * select_weights(ii, c0) for ii in range(n_blk)], axis=0)
        acc_scr[:, c0:c0 + tc] += jnp.dot(vt_ref[...], a, preferred_element_type=F32)

    @pl.when(pl.program_id(1) == pl.num_programs(1) - 1)
    def _():
        hout_ref[...] = h_ref[...] + jnp.transpose(acc_scr[...])


def _peer_dense(xnt, u_bf, vt_bf, n_sel, e1, r2, e2, h, tt):
    m = xnt.shape[1]
    te = 2048
    ib = te // PEER_NKEYS
    sel_i = pl.BlockSpec((PEER_HEADS, ib, tt), lambda i, e: (0, e, i))
    sel_all = pl.BlockSpec((PEER_HEADS, PEER_NKEYS, tt), lambda i, e: (0, 0, i))
    return pl.pallas_call(
        _peer_dense_body,
        grid=(m // tt, PEER_EXPERTS // te),
        in_specs=[pl.BlockSpec((te, D_MODEL), lambda i, e: (e, 0)),
                  pl.BlockSpec((D_MODEL, tt), lambda i, e: (0, i)),
                  pl.BlockSpec((D_MODEL, te), lambda i, e: (0, e)),
                  sel_i, sel_i, sel_all, sel_all,
                  pl.BlockSpec((tt, D_MODEL), lambda i, e: (i, 0))],
        out_specs=pl.BlockSpec((tt, D_MODEL), lambda i, e: (i, 0)),
        out_shape=jax.ShapeDtypeStruct((m, D_MODEL), F32),
        scratch_shapes=[pltpu.VMEM((D_MODEL, tt), F32)],
        compiler_params=_cparams("parallel", "arbitrary"),
        name="peer_dense",
    )(u_bf, xnt, vt_bf, n_sel, e1, r2, e2, h)


def _table_body(transpose, x_ref, o_ref):
    x = x_ref[...]
    o_ref[...] = (jnp.transpose(x) if transpose else x).astype(o_ref.dtype)


def _expert_table(tables, l, transpose):
    _, rows, cols = tables.shape
    tr = 512
    return pl.pallas_call(
        functools.partial(_table_body, transpose),
        grid=(rows // tr,),
        in_specs=[pl.BlockSpec((None, tr, cols), lambda i: (l, i, 0))],
        out_specs=pl.BlockSpec((cols, tr), lambda i: (0, i)) if transpose else pl.BlockSpec((tr, cols), lambda i: (i, 0)),
        out_shape=jax.ShapeDtypeStruct((cols, rows) if transpose else (rows, cols), BF16),
        compiler_params=_cparams("parallel"),
        name="expert_table",
    )(tables)


def _final_body(h_ref, nw_ref, o_ref):
    o_ref[...] = _rms(h_ref[...], nw_ref[...])


def _final(h, norm_w):
    m = h.shape[0]
    tm = 512
    row = pl.BlockSpec((tm, D_MODEL), lambda i: (i, 0))
    return pl.pallas_call(
        _final_body,
        grid=(m // tm,),
        in_specs=[row, pl.BlockSpec((1, D_MODEL), lambda i: (0, 0))],
        out_specs=row,
        out_shape=jax.ShapeDtypeStruct((m, D_MODEL), F32),
        compiler_params=_cparams("parallel"),
        name="final_norm",
    )(h, norm_w.reshape(1, D_MODEL))


def _pad_rows(rows, width, n_rows=8):
    flat = [jnp.pad(r.reshape(-1).astype(F32), (0, width - r.size)) for r in rows]
    return jnp.pad(jnp.stack(flat), ((0, n_rows - len(rows)), (0, 0)))


def _chunk_rows(ab):
    m, n = ab.shape
    return jnp.transpose(ab.reshape(m // GDN_CHUNK, GDN_CHUNK, n), (0, 2, 1)).astype(F32)


def _block_diag_groups(w):
    per = LRU_GROUP // LRU_BLOCK_DIM
    n_grp = w.shape[0] // per
    wg = w.reshape(n_grp, per, LRU_BLOCK_DIM, LRU_BLOCK_DIM)
    eye = jnp.eye(per, dtype=w.dtype)
    out = wg[:, :, :, None, :] * eye[None, :, None, :, None]
    return out.reshape(n_grp, LRU_GROUP, LRU_GROUP).astype(BF16)


def _prep_w_in(w_all, l):
    pad = jnp.zeros((D_MODEL, IN_WIDTH_PAD - COL_AB - 2 * GDN_HEADS), BF16)
    cols = ((0, 6144), (9488, 12560), (6160, 9488), (6144, 6160))
    return jnp.concatenate([w_all[l, :, a:b].astype(BF16) for a, b in cols] + [pad], axis=1)


def _layer(l, h, bsz, seq, p):
    z, ab = _inproj(h, p["norm_mix_w"][l], _prep_w_in(p["w_in"], l))

    lru_vecs = _pad_rows([p["lru_conv_b"][l], p["lru_b_a"][l], p["lru_b_x"][l], p["lru_lambda"][l]], LRU_WIDTH)
    ya = _lru(z, bsz, seq, p["lru_conv_w"][l], lru_vecs,
              _block_diag_groups(p["lru_w_a"][l]), _block_diag_groups(p["lru_w_x"][l]))

    abt = _chunk_rows(ab[:, :2 * GDN_HEADS])
    gp = _pad_rows([p["gdn_a_log"][l], p["gdn_dt_bias"][l]], 128)
    gpt = jnp.pad(jnp.stack([p["gdn_a_log"][l], p["gdn_dt_bias"][l]], axis=1), ((0, 0), (0, 126)))
    yb = _gdn(z, ab, abt, bsz, seq, p["gdn_conv_w"][l], gp, gpt, p["gdn_norm_w"][l].reshape(1, GDN_DV))

    rw_vecs = _pad_rows([p["rwkv_w0"][l], p["rwkv_a0"][l], p["rwkv_k_k"][l], p["rwkv_k_a"][l],
                         p["rwkv_r_k"][l], p["rwkv_lnx_w"][l], p["rwkv_lnx_b"][l]], RWKV_WIDTH)
    wup = jnp.pad(p["rwkv_w_up"][l], ((0, 192), (0, 0))).astype(BF16)
    aup = jnp.pad(p["rwkv_a_up"][l], ((64, 128), (0, 0))).astype(BF16)
    gup = jnp.pad(p["rwkv_g_up"][l], ((128, 0), (0, 0))).astype(BF16)
    yc = _rwkv(z, bsz, seq, p["rwkv_mu"][l].reshape(1, -1), rw_vecs, wup, aup, gup)

    h, xnt = _merge(h, z, ya, yb, yc, p["merge_b"][l].reshape(1, -1), p["p_lru"][l].astype(BF16),
                    p["p_gdn"][l].astype(BF16), p["p_rwkv"][l].astype(BF16), p["w_out"][l].astype(BF16),
                    p["norm_ffn_w"][l].reshape(1, D_MODEL))

    wqt = jnp.transpose(p["peer_wq"][l]).astype(BF16)
    keys = p["peer_keys"][l].reshape(2 * PEER_HEADS, PEER_NKEYS, PEER_HALF).astype(BF16)
    n_sel, e1, r2, e2 = _peer_topk(xnt, wqt, keys)
    tt = 512 if xnt.shape[1] % 512 == 0 else 256
    return _peer_dense(xnt, _expert_table(p["peer_u"], l, False), _expert_table(p["peer_v"], l, True),
                       n_sel, e1, r2, e2, h, tt)


def kernel(x, norm_mix_w, norm_ffn_w, final_norm_w, w_in, lru_conv_w, lru_conv_b, lru_w_a, lru_b_a,
           lru_w_x, lru_b_x, lru_lambda, gdn_conv_w, gdn_a_log, gdn_dt_bias, gdn_norm_w, rwkv_mu, rwkv_w0,
           rwkv_w_up, rwkv_a0, rwkv_a_up, rwkv_g_up, rwkv_k_k, rwkv_k_a, rwkv_r_k, rwkv_lnx_w, rwkv_lnx_b,
           merge_b, p_lru, p_gdn, p_rwkv, w_out, peer_wq, peer_keys, peer_u, peer_v):
    p = dict(locals())
    bsz, seq, dim = x.shape
    h = x.reshape(bsz * seq, dim)
    for l in range(DEPTH):
        h = _layer(l, h, bsz, seq, p)
    return _final(h, final_norm_w).reshape(bsz, seq, dim)
```

```python
import functools

import jax
import jax.numpy as jnp
from jax import lax
from jax.experimental import pallas as pl
from jax.experimental.pallas import tpu as pltpu

F32 = jnp.float32
BF16 = jnp.bfloat16

D_MODEL = 1024
DEPTH = 2
RMS_EPS = 1e-6

LRU_WIDTH = 1024
LRU_BLOCK_DIM = 64
LRU_C = 8.0
LRU_GROUP = 256

GDN_HEADS = 8
GDN_DK = 128
GDN_DV = 128
GDN_CHUNK = 64

RWKV_HD = 64
RWKV_WIDTH = 1024
RWKV_GN_EPS = 64e-5
RWKV_CHUNK = 64
RWKV_GROUP = 256
RWKV_LORA = 256

PEER_HEADS = 8
PEER_NKEYS = 128
PEER_EXPERTS = PEER_NKEYS * PEER_NKEYS
PEER_HALF = 128
PEER_TOPK = 16

COL_U = 0
COL_GATE = 1024
COL_QKV = 2048
COL_ZG = 5120
COL_MERGE = 6144
COL_RWKV = 9216
COL_AB = 12544
IN_WIDTH_PAD = 12800
IN_TILE_N = 1280

VMEM_LIMIT = 48 * 1024 * 1024
BIG = 3.0e38


def _cparams(*sem):
    return pltpu.CompilerParams(dimension_semantics=sem, vmem_limit_bytes=VMEM_LIMIT)


def _mm(a, b):
    return jnp.dot(a.astype(BF16), b.astype(BF16), preferred_element_type=F32)


def _mm_nt(a, b):
    return lax.dot_general(a.astype(BF16), b.astype(BF16), (((1,), (1,)), ((), ())),
                           preferred_element_type=F32)


def _mm_tn(a, b):
    return lax.dot_general(a.astype(BF16), b.astype(BF16), (((0,), (0,)), ((), ())),
                           preferred_element_type=F32)


def _softplus(x):
    return jnp.maximum(x, 0.0) + jnp.log1p(jnp.exp(-jnp.abs(x)))


def _rms(x, w):
    return x * lax.rsqrt(jnp.mean(x * x, axis=-1, keepdims=True) + RMS_EPS) * w


def _shift_prev(x, prev8, s):
    r = pltpu.roll(x, s, 0)
    pr = pltpu.roll(prev8, s, 0)
    rows8 = lax.broadcasted_iota(jnp.int32, prev8.shape, 0)
    head = jnp.where(rows8 < s, pr, r[:8])
    return jnp.concatenate([head, r[8:]], axis=0)


def _shift_fill(x, d, fill):
    n, c = x.shape
    if d % 8 == 0:
        return jnp.concatenate([jnp.full((d, c), fill, x.dtype), x[:n - d]], axis=0)
    r = pltpu.roll(x, d, 0)
    rows8 = lax.broadcasted_iota(jnp.int32, (8, c), 0)
    head = jnp.where(rows8 < d, fill, r[:8])
    return jnp.concatenate([head, r[8:]], axis=0)


def _tri(n, strict):
    i = lax.broadcasted_iota(jnp.int32, (n, n), 0)
    j = lax.broadcasted_iota(jnp.int32, (n, n), 1)
    return (j < i) if strict else (j <= i)


def _split_bf16(x, terms):
    parts = []
    rem = x
    for i in range(terms):
        p = rem.astype(BF16)
        parts.append(p)
        if i + 1 < terms:
            rem = rem - p.astype(F32)
    return parts


def _mask_mm(mask_bf, x, terms=3):
    return sum(jnp.dot(mask_bf, p, preferred_element_type=F32) for p in _split_bf16(x, terms))


def _mm_mask(x, mask_bf, terms=3):
    return sum(jnp.dot(p, mask_bf, preferred_element_type=F32) for p in _split_bf16(x, terms))


def _inproj_body(h_ref, nw_ref, w_ref, z_ref, ab_ref, xn_scr):
    @pl.when(pl.program_id(1) == 0)
    def _():
        xn_scr[...] = _rms(h_ref[...], nw_ref[...]).astype(BF16)

    res = jnp.dot(xn_scr[...], w_ref[...], preferred_element_type=F32)
    z_ref[...] = res.astype(z_ref.dtype)

    @pl.when(pl.program_id(1) == pl.num_programs(1) - 1)
    def _():
        c0 = COL_AB - (IN_WIDTH_PAD - res.shape[1])
        ab_ref[...] = res[:, c0:c0 + ab_ref.shape[1]]


def _inproj(h, norm_w, w_pad):
    m = h.shape[0]
    tm, tn = (1024 if m % 1024 == 0 else 512), IN_TILE_N
    assert IN_WIDTH_PAD % tn == 0 and COL_AB >= IN_WIDTH_PAD - tn
    return pl.pallas_call(
        _inproj_body,
        grid=(m // tm, IN_WIDTH_PAD // tn),
        in_specs=[pl.BlockSpec((tm, D_MODEL), lambda i, j: (i, 0)),
                  pl.BlockSpec((1, D_MODEL), lambda i, j: (0, 0)),
                  pl.BlockSpec((D_MODEL, tn), lambda i, j: (0, j))],
        out_specs=[pl.BlockSpec((tm, tn), lambda i, j: (i, j)), pl.BlockSpec((tm, 128), lambda i, j: (i, 0))],
        out_shape=[jax.ShapeDtypeStruct((m, IN_WIDTH_PAD), BF16), jax.ShapeDtypeStruct((m, 128), F32)],
        scratch_shapes=[pltpu.VMEM((tm, D_MODEL), BF16)],
        compiler_params=_cparams("parallel", "arbitrary"),
        name="norm_inproj",
    )(h, norm_w.reshape(1, D_MODEL), w_pad)


def _lru_body(zu_ref, zg_ref, cw_ref, vp_ref, wa_ref, wx_ref, o_ref, prev_scr, hc_scr):
    tt = zu_ref.shape[0]

    @pl.when(pl.program_id(1) == 0)
    def _():
        prev_scr[...] = jnp.zeros_like(prev_scr)
        hc_scr[...] = jnp.zeros_like(hc_scr)

    u = zu_ref[...].astype(F32)
    prev = prev_scr[...]
    cw = cw_ref[...]
    vp = vp_ref[...]
    xc = (cw[3:4] * u + cw[2:3] * _shift_prev(u, prev, 1) + cw[1:2] * _shift_prev(u, prev, 2)
          + cw[0:1] * _shift_prev(u, prev, 3) + vp[0:1])
    prev_scr[...] = u[tt - 8:]

    xcb = xc.astype(BF16)
    n_grp = LRU_WIDTH // LRU_GROUP
    pre_a = jnp.concatenate(
        [jnp.dot(xcb[:, g * LRU_GROUP:(g + 1) * LRU_GROUP], wa_ref[g], preferred_element_type=F32)
         for g in range(n_grp)], axis=1)
    pre_x = jnp.concatenate(
        [jnp.dot(xcb[:, g * LRU_GROUP:(g + 1) * LRU_GROUP], wx_ref[g], preferred_element_type=F32)
         for g in range(n_grp)], axis=1)
    r = jax.nn.sigmoid(pre_a + vp[1:2])
    i = jax.nn.sigmoid(pre_x + vp[2:3])
    log_a = (-LRU_C * r) * _softplus(-vp[3:4])
    a = jnp.exp(log_a)
    b = jnp.sqrt(-jnp.tanh(log_a) * (a * a + 1.0)) * (i * xc)

    d = 1
    while d < tt:
        b = a * _shift_fill(b, d, 0.0) + b
        a = a * _shift_fill(a, d, 1.0)
        d *= 2
    hcar = hc_scr[...]
    hval = b + a * hcar[0:1]
    hc_scr[...] = jnp.broadcast_to(hval[tt - 1:tt], hcar.shape)
    o_ref[...] = (hval * jax.nn.gelu(zg_ref[...].astype(F32))).astype(o_ref.dtype)


def _lru(z, bsz, seq, conv_w, vecs, wa_bd, wx_bd):
    m = z.shape[0]
    tt = 256
    nt = seq // tt
    rowmap = lambda c: (lambda b, t: (b * nt + t, c))
    const2 = lambda b, t: (0, 0)
    return pl.pallas_call(
        _lru_body,
        grid=(bsz, nt),
        in_specs=[
            pl.BlockSpec((tt, LRU_WIDTH), rowmap(COL_U // LRU_WIDTH)),
            pl.BlockSpec((tt, LRU_WIDTH), rowmap(COL_GATE // LRU_WIDTH)),
            pl.BlockSpec((4, LRU_WIDTH), const2),
            pl.BlockSpec((8, LRU_WIDTH), const2),
            pl.BlockSpec(wa_bd.shape, lambda b, t: (0, 0, 0)),
            pl.BlockSpec(wx_bd.shape, lambda b, t: (0, 0, 0)),
        ],
        out_specs=pl.BlockSpec((tt, LRU_WIDTH), rowmap(0)),
        out_shape=jax.ShapeDtypeStruct((m, LRU_WIDTH), BF16),
        scratch_shapes=[pltpu.VMEM((8, LRU_WIDTH), F32), pltpu.VMEM((8, LRU_WIDTH), F32)],
        compiler_params=_cparams("parallel", "arbitrary"),
        name="rglru",
    )(z, z, conv_w, vecs, wa_bd, wx_bd)


def _gdn_body(q_ref, k_ref, v_ref, zg_ref, ab_ref, abt_ref, cw_ref, gp_ref, gpt_ref, nw_ref,
              o_ref, pq_scr, pk_scr, pv_scr, st_scr):
    tt = q_ref.shape[0]
    c = GDN_CHUNK
    width = GDN_HEADS * GDN_DK

    @pl.when(pl.program_id(1) == 0)
    def _():
        for s in (pq_scr, pk_scr, pv_scr, st_scr):
            s[...] = jnp.zeros_like(s)

    cw = cw_ref[...]

    def conv_silu(x_ref, p_scr, w):
        x = x_ref[...].astype(F32)
        prev = p_scr[...]
        y = (w[3:4] * x + w[2:3] * _shift_prev(x, prev, 1) + w[1:2] * _shift_prev(x, prev, 2)
             + w[0:1] * _shift_prev(x, prev, 3))
        p_scr[...] = x[tt - 8:]
        return jax.nn.silu(y)

    q = conv_silu(q_ref, pq_scr, cw[:, 0:width])
    k = conv_silu(k_ref, pk_scr, cw[:, width:2 * width])
    v = conv_silu(v_ref, pv_scr, cw[:, 2 * width:])

    ab = ab_ref[...].astype(F32)
    gp = gp_ref[...]
    gpt = gpt_ref[...]
    g_all = -jnp.exp(gp[0:1]) * _softplus(ab + gp[1:2])
    beta_all = jax.nn.sigmoid(ab)

    incl = _tri(c, False)
    strict = _tri(c, True)
    cum_mask = jnp.concatenate([incl.astype(BF16), jnp.ones((c, c), BF16)], axis=0)
    triu_bf = (lax.broadcasted_iota(jnp.int32, (c, c), 0) <= lax.broadcasted_iota(jnp.int32, (c, c), 1)).astype(BF16)
    nw = nw_ref[...]

    n_chunks = tt // c
    heads = range(GDN_HEADS)
    inst = [(ci, h) for ci in range(n_chunks) for h in heads]
    sls = [slice(ci * c, (ci + 1) * c) for ci in range(n_chunks)]
    hss = [slice(h * GDN_DK, (h + 1) * GDN_DK) for h in heads]
    gcs = [_mask_mm(cum_mask, g_all[sl]) for sl in sls]
    gr_all = [_mm_mask(-jnp.exp(gpt[:, 0:1]) * _softplus(abt_ref[ci][0:GDN_HEADS] + gpt[:, 1:2]), triu_bf)
              for ci in range(n_chunks)]

    qs, ks, kbs, decays, egcs, kdecs, gtots, bcols = {}, {}, {}, {}, {}, {}, {}, {}
    for ci, h in inst:
        sl, hs = sls[ci], hss[h]
        qc, kc = q[sl, hs], k[sl, hs]
        qs[ci, h] = qc * lax.rsqrt(jnp.sum(qc * qc, axis=-1, keepdims=True) + RMS_EPS) * (GDN_DK ** -0.5)
        ks[ci, h] = kc * lax.rsqrt(jnp.sum(kc * kc, axis=-1, keepdims=True) + RMS_EPS)
        gcol = gcs[ci][:c, h:h + 1]
        g_last = gcs[ci][c:, h:h + 1]
        grow = gr_all[ci][h:h + 1, :]
        bcols[ci, h] = beta_all[sl, GDN_HEADS + h:GDN_HEADS + h + 1]
        decays[ci, h] = jnp.exp(jnp.where(incl, gcol - grow, -BIG))
        kbs[ci, h] = ks[ci, h] * bcols[ci, h]
        egcs[ci, h] = jnp.exp(gcol)
        kdecs[ci, h] = ks[ci, h] * jnp.exp(g_last - gcol)
        gtots[ci, h] = jnp.exp(jnp.concatenate([g_last] * (GDN_DK // c), axis=0))

    kk = {i: _mm_nt(kbs[i], ks[i]) for i in inst}
    qk = {i: _mm_nt(qs[i], ks[i]) * decays[i] for i in inst}
    p = {i: -jnp.where(strict, kk[i] * decays[i], 0.0) for i in inst}
    tm1 = dict(p)
    for _ in range(5):
        p = {i: _mm(p[i], p[i]) for i in inst}
        tp = {i: _mm(tm1[i], p[i]) for i in inst}
        tm1 = {i: tm1[i] + tp[i] + p[i] for i in inst}
    rhs = {i: jnp.concatenate([v[sls[i[0]], hss[i[1]]] * bcols[i], kbs[i] * egcs[i]], axis=1) for i in inst}
    tr = {i: _mm(tm1[i], rhs[i]) for i in inst}
    sol = {i: rhs[i] + tr[i] for i in inst}

    state = [st_scr[h] for h in heads]
    for ci in range(n_chunks):
        ws = [_mm(sol[ci, h][:, GDN_DV:], state[h]) for h in heads]
        qst = [_mm(qs[ci, h] * egcs[ci, h], state[h]) for h in heads]
        v_new = [sol[ci, h][:, :GDN_DV] - ws[h] for h in heads]
        qv = [_mm(qk[ci, h], v_new[h]) for h in heads]
        kv = [_mm_tn(kdecs[ci, h], v_new[h]) for h in heads]
        state = [state[h] * gtots[ci, h] + kv[h] for h in heads]
        for h in heads:
            o = qst[h] + qv[h]
            o = o * lax.rsqrt(jnp.mean(o * o, axis=-1, keepdims=True) + RMS_EPS) * nw
            o = o * jax.nn.silu(zg_ref[sls[ci], hss[h]].astype(F32))
            o_ref[sls[ci], hss[h]] = o.astype(o_ref.dtype)
    for h in heads:
        st_scr[h] = state[h]


def _gdn(z, ab, abt, bsz, seq, conv_w, gp, gpt, norm_w):
    m = z.shape[0]
    tt = 256
    nt = seq // tt
    width = GDN_HEADS * GDN_DK
    zblk = lambda c0: pl.BlockSpec((tt, width), lambda b, t: (b * nt + t, c0 // width))
    const2 = lambda b, t: (0, 0)
    return pl.pallas_call(
        _gdn_body,
        grid=(bsz, nt),
        in_specs=[
            zblk(COL_QKV), zblk(COL_QKV + width), zblk(COL_QKV + 2 * width), zblk(COL_ZG),
            pl.BlockSpec((tt, 128), lambda b, t: (b * nt + t, 0)),
            pl.BlockSpec((tt // GDN_CHUNK, 2 * GDN_HEADS, GDN_CHUNK), lambda b, t: (b * nt + t, 0, 0)),
            pl.BlockSpec((4, 3 * width), const2),
            pl.BlockSpec((8, 128), const2),
            pl.BlockSpec((8, 128), const2),
            pl.BlockSpec((1, 128), const2),
        ],
        out_specs=pl.BlockSpec((tt, width), lambda b, t: (b * nt + t, 0)),
        out_shape=jax.ShapeDtypeStruct((m, width), BF16),
        scratch_shapes=[pltpu.VMEM((8, width), F32)] * 3 + [pltpu.VMEM((GDN_HEADS, GDN_DK, GDN_DV), F32)],
        compiler_params=_cparams("parallel", "arbitrary"),
        name="gated_deltanet",
    )(z, z, z, z, ab, abt, conv_w, gp, gpt, norm_w)


def _rwkv_body(r_ref, k_ref, v_ref, lo_ref, mu_ref, vp_ref, wup_ref, aup_ref, gup_ref, o_ref,
               pr_scr, pk_scr, pv_scr, plo_scr, st_scr):
    tt = r_ref.shape[0]
    c = RWKV_CHUNK
    gw = RWKV_GROUP
    hd = RWKV_HD
    width = RWKV_WIDTH

    @pl.when(pl.program_id(1) == 0)
    def _():
        for s in (pr_scr, pk_scr, pv_scr, plo_scr, st_scr):
            s[...] = jnp.zeros_like(s)

    mu = mu_ref[...]

    def tshift(x_ref, p_scr, m):
        x = x_ref[...].astype(F32)
        xs = x + m * (_shift_prev(x, p_scr[...], 1) - x)
        p_scr[...] = x[tt - 8:]
        return xs

    r = tshift(r_ref, pr_scr, mu[:, 0:width])
    k = tshift(k_ref, pk_scr, mu[:, width:2 * width])
    v = tshift(v_ref, pv_scr, mu[:, 2 * width:3 * width])
    lo = tshift(lo_ref, plo_scr, mu[:, 3 * width:])
    vp = vp_ref[...]
    w0, a0, k_k, k_a, r_k, lnx_w, lnx_b = (vp[i:i + 1] for i in range(7))

    lane = lax.broadcasted_iota(jnp.int32, lo.shape, 1)
    lo_act = jnp.where(lane < 64, jnp.tanh(lo), jnp.where(lane < 128, lo, jax.nn.sigmoid(lo)))
    w_pre = _mm(lo_act, wup_ref[...])
    a_pre = _mm(lo_act, aup_ref[...])
    gate = _mm(lo_act, gup_ref[...])
    w_log = -_softplus(-(w0 + w_pre)) - 0.5
    lw = -jnp.exp(w_log)
    a = jax.nn.sigmoid(a0 + a_pre)

    bi = lax.broadcasted_iota(jnp.int32, (gw, gw), 0) // hd
    bj = lax.broadcasted_iota(jnp.int32, (gw, gw), 1) // hd
    bdmask = bi == bj
    ones_bd = bdmask.astype(BF16)

    def head_sum(x):
        return _mm_mask(x, ones_bd, terms=1)

    def bd(x):
        return jnp.where(bdmask, jnp.concatenate([x] * (gw // c), axis=0), 0.0)

    ti = lax.broadcasted_iota(jnp.int32, (c, gw), 0)
    sj = lax.broadcasted_iota(jnp.int32, (c, gw), 1) % c
    strict = sj < ti
    incl = sj <= ti
    tril_bf = _tri(c, False).astype(BF16)

    kk_all = k * k_k
    k2_all = k * (1.0 + (a - 1.0) * k_a)

    n_chunks = tt // c
    groups = range(width // gw)
    inst = [(ci, g) for ci in range(n_chunks) for g in groups]
    sls = [slice(ci * c, (ci + 1) * c) for ci in range(n_chunks)]
    gss = [slice(g * gw, (g + 1) * gw) for g in groups]
    cl_all = [_mask_mm(tril_bf, lw[sl]) for sl in sls]

    def at(x, i):
        return x[sls[i[0]], gss[i[1]]]

    kk_raw = {i: at(kk_all, i) for i in inst}
    kk_ss = {i: head_sum(kk_raw[i] * kk_raw[i]) for i in inst}
    a_h, b_h, k_h, br, w_end = {}, {}, {}, {}, {}
    for i in inst:
        cl = cl_all[i[0]][:, gss[i[1]]]
        kk = kk_raw[i] * lax.rsqrt(kk_ss[i] + RMS_EPS)
        wcum = jnp.exp(cl)
        inv_w = jnp.exp(-cl)
        a_h[i] = kk * at(a, i) * inv_w
        b_h[i] = kk * jnp.exp(cl - at(lw, i))
        k_h[i] = at(k2_all, i) * inv_w
        br[i] = jnp.concatenate([b_h[i], at(r, i) * wcum], axis=0)
        w_end[i] = wcum[c - 1:c]
    v_bd = {i: bd(at(v, i)) for i in inst}
    xa = {i: _mm_nt(br[i], bd(a_h[i])) for i in inst}
    xk = {i: _mm_nt(br[i], bd(k_h[i])) for i in inst}
    l_k = {i: jnp.where(strict, xk[i][:c], 0.0) for i in inst}
    ra = {i: jnp.where(incl, xa[i][c:], 0.0) for i in inst}
    rk = {i: jnp.where(incl, xk[i][c:], 0.0) for i in inst}
    p = {i: -jnp.where(strict, xa[i][:c], 0.0) for i in inst}
    tm1 = dict(p)
    for _ in range(5):
        p = {i: _mm(p[i], bd(p[i])) for i in inst}
        tp = {i: _mm(tm1[i], bd(p[i])) for i in inst}
        tm1 = {i: tm1[i] + tp[i] + p[i] for i in inst}
    lkv = {i: _mm(l_k[i], v_bd[i]) for i in inst}
    rkv = {i: _mm(rk[i], v_bd[i]) for i in inst}
    so = {i: head_sum(at(r, i) * at(k2_all, i) * r_k[:, gss[i[1]]]) for i in inst}

    state_t = [st_scr[g] for g in groups]
    for ci in range(n_chunks):
        ids = [(ci, g) for g in groups]
        brh = [_mm_nt(br[i], state_t[i[1]]) for i in ids]
        rhs = [brh[g][:c] + lkv[ci, g] for g in groups]
        tu = [_mm(tm1[ci, g], bd(rhs[g])) for g in groups]
        u = [rhs[g] + tu[g] for g in groups]
        rau = [_mm(ra[ci, g], bd(u[g])) for g in groups]
        upd = [_mm_tn(jnp.concatenate([u[g], at(v, (ci, g))], axis=0),
                      jnp.concatenate([-(a_h[ci, g] * w_end[ci, g]), k_h[ci, g] * w_end[ci, g]], axis=0))
               for g in groups]
        state_t = [w_end[ci, g] * state_t[g] + jnp.where(bdmask, upd[g], 0.0) for g in groups]
        o = [brh[g][c:] - rau[g] + rkv[ci, g] for g in groups]
        osum = [head_sum(o[g]) for g in groups]
        cen = [o[g] - osum[g] * (1.0 / hd) for g in groups]
        var = [head_sum(cen[g] * cen[g]) * (1.0 / hd) for g in groups]
        for g in groups:
            gs = gss[g]
            on = cen[g] * lax.rsqrt(var[g] + RWKV_GN_EPS) * lnx_w[:, gs] + lnx_b[:, gs]
            bonus = so[ci, g] * at(v, (ci, g))
            o_ref[sls[ci], gs] = ((on + bonus) * gate[sls[ci], gs]).astype(o_ref.dtype)
    for g in groups:
        st_scr[g] = state_t[g]


def _rwkv(z, bsz, seq, mu, vecs, wup_pad, aup_pad, gup_pad):
    m = z.shape[0]
    tt = 256
    nt = seq // tt
    gw = RWKV_GROUP
    width = RWKV_WIDTH
    zblk = lambda c0: pl.BlockSpec((tt, width), lambda b, t: (b * nt + t, c0 // width))
    lora_col = (COL_RWKV + 3 * width) // RWKV_LORA
    const2 = lambda b, t: (0, 0)
    return pl.pallas_call(
        _rwkv_body,
        grid=(bsz, nt),
        in_specs=[
            zblk(COL_RWKV), zblk(COL_RWKV + width), zblk(COL_RWKV + 2 * width),
            pl.BlockSpec((tt, RWKV_LORA), lambda b, t: (b * nt + t, lora_col)),
            pl.BlockSpec(mu.shape, const2),
            pl.BlockSpec((8, width), const2),
            pl.BlockSpec((RWKV_LORA, width), const2),
            pl.BlockSpec((RWKV_LORA, width), const2),
            pl.BlockSpec((RWKV_LORA, width), const2),
        ],
        out_specs=pl.BlockSpec((tt, width), lambda b, t: (b * nt + t, 0)),
        out_shape=jax.ShapeDtypeStruct((m, width), BF16),
        scratch_shapes=[pltpu.VMEM((8, width), F32)] * 3 + [pltpu.VMEM((8, RWKV_LORA), F32),
                                                            pltpu.VMEM((width // gw, gw, gw), F32)],
        compiler_params=_cparams("parallel", "arbitrary"),
        name="rwkv7",
    )(z, z, z, z, mu, vecs, wup_pad, aup_pad, gup_pad)


def _merge_body(h_ref, ya_ref, yb_ref, yc_ref, za_ref, zb_ref, zc_ref, mb_ref, pa_ref, pb_ref,
                pc_ref, wo_ref, nw_ref, hout_ref, xnt_ref):
    mb = mb_ref[...]
    ga = jax.nn.sigmoid(za_ref[...].astype(F32) + mb[:, 0:D_MODEL])
    gb = jax.nn.sigmoid(zb_ref[...].astype(F32) + mb[:, D_MODEL:2 * D_MODEL])
    gc = jax.nn.sigmoid(zc_ref[...].astype(F32) + mb[:, 2 * D_MODEL:])
    merged = (ga * jnp.dot(ya_ref[...], pa_ref[...], preferred_element_type=F32)
              + gb * jnp.dot(yb_ref[...], pb_ref[...], preferred_element_type=F32)
              + gc * jnp.dot(yc_ref[...], pc_ref[...], preferred_element_type=F32))
    h = h_ref[...] + _mm(merged, wo_ref[...])
    hout_ref[...] = h
    xnt_ref[...] = jnp.transpose(_rms(h, nw_ref[...])).astype(BF16)


def _merge(h, z, ya, yb, yc, merge_b, pa, pb, pc, wo, norm_w):
    m = h.shape[0]
    tm = 256
    row = lambda c: pl.BlockSpec((tm, D_MODEL), lambda i: (i, c))
    wspec = pl.BlockSpec((D_MODEL, D_MODEL), lambda i: (0, 0))
    mc = COL_MERGE // D_MODEL
    return pl.pallas_call(
        _merge_body,
        grid=(m // tm,),
        in_specs=[row(0), row(0), row(0), row(0), row(mc), row(mc + 1), row(mc + 2),
                  pl.BlockSpec((1, 3 * D_MODEL), lambda i: (0, 0)),
                  wspec, wspec, wspec, wspec,
                  pl.BlockSpec((1, D_MODEL), lambda i: (0, 0))],
        out_specs=[row(0), pl.BlockSpec((D_MODEL, tm), lambda i: (0, i))],
        out_shape=[jax.ShapeDtypeStruct((m, D_MODEL), F32), jax.ShapeDtypeStruct((D_MODEL, m), BF16)],
        compiler_params=_cparams("parallel"),
        name="merge_outproj",
    )(h, ya, yb, yc, z, z, z, merge_b, pa, pb, pc, wo, norm_w)


_CAND_SLABS = (
    (16, ((0, 0, 16),)),
    (16, ((1, 0, 8), (2, 8, 5), (4, 13, 3))),
    (16, ((3, 0, 4), (5, 4, 2), (6, 6, 2), (7, 8, 2), (8, 10, 1), (9, 11, 1), (10, 12, 1), (11, 13, 1),
          (12, 14, 1), (13, 15, 1))),
    (8, ((14, 0, 1), (15, 1, 1))),
)


def _odd_even_merge_sort_pairs(n):
    pairs = []
    p = 1
    while p < n:
        k = p
        while k >= 1:
            for j in range(k % p, n - k, 2 * k):
                for i in range(min(k, n - j - k)):
                    if (i + j) // (2 * p) == (i + j + k) // (2 * p):
                        pairs.append((i + j, i + j + k))
            k //= 2
        p *= 2
    return tuple(pairs)


_SORT_PAIRS = _odd_even_merge_sort_pairs(PEER_TOPK)


def _candidate_sums(a1, a2):
    slabs = []
    for nrows, pieces in _CAND_SLABS:
        base = a2[:nrows]
        rows = lax.broadcasted_iota(jnp.int32, base.shape, 0)
        out = None
        end = 0
        for p, off, cnt in pieces:
            val = a1[p:p + 1] + (base if off == 0 else pltpu.roll(base, off, 0))
            out = val if out is None else jnp.where(rows >= off, val, out)
            end = off + cnt
        if end < nrows:
            out = jnp.where(rows >= end, -BIG, out)
        slabs.append(out)
    return jnp.concatenate(slabs, axis=0)


def _count_leading(pred, thr):
    assert len(thr) == 16
    b8 = pred(thr[7])
    b4 = pred(jnp.where(b8, thr[11], thr[3]))
    b2 = pred(jnp.where(b8, jnp.where(b4, thr[13], thr[9]), jnp.where(b4, thr[5], thr[1])))
    lo = jnp.where(b4, jnp.where(b2, thr[6], thr[4]), jnp.where(b2, thr[2], thr[0]))
    hi = jnp.where(b4, jnp.where(b2, thr[14], thr[12]), jnp.where(b2, thr[10], thr[8]))
    b1 = pred(jnp.where(b8, hi, lo))
    cnt = (jnp.where(b8, 8.0, 0.0) + jnp.where(b4, 4.0, 0.0)) + (jnp.where(b2, 2.0, 0.0) + jnp.where(b1, 1.0, 0.0))
    return jnp.where(pred(thr[15]), 16.0, cnt)


def _peer_topk_body(xnt_ref, wqt_ref, keys_ref, n_ref, e1_ref, r2_ref, e2_ref, q_scr):
    tt = xnt_ref.shape[1]
    nk = PEER_NKEYS
    topk = PEER_TOPK
    q_scr[...] = jnp.dot(wqt_ref[...], xnt_ref[...], preferred_element_type=F32)
    lanes = 128
    sub = 8
    assert nk == topk * sub
    row8 = lax.broadcasted_iota(jnp.int32, (sub, lanes), 0)

    def top_sorted(s):
        v = [s[k * sub:(k + 1) * sub] for k in range(topk)]
        for i, j in _SORT_PAIRS:
            v[i], v[j] = jnp.maximum(v[i], v[j]), jnp.minimum(v[i], v[j])
        for shift in (4, 2, 1):
            other = [pltpu.roll(x, shift, 0) for x in v]
            v = [jnp.maximum(v[k], other[topk - 1 - k]) for k in range(topk)]
            d = topk // 2
            while d >= 1:
                for i in range(topk):
                    if i & d == 0:
                        v[i], v[i + d] = jnp.maximum(v[i], v[i + d]), jnp.minimum(v[i], v[i + d])
                d //= 2
        return v

    def compact(v):
        tiles = []
        for t0 in range(0, topk, sub):
            out = v[t0]
            for k in range(1, sub):
                out = jnp.where(row8 == k, v[t0 + k], out)
            tiles.append(out)
        return jnp.concatenate(tiles, axis=0)

    def head(h, carry):
        o1 = pl.multiple_of(h * (2 * nk), 2 * nk)
        s1 = _mm(keys_ref[2 * h], q_scr[pl.ds(o1, nk), :])
        s2 = _mm(keys_ref[2 * h + 1], q_scr[pl.ds(o1 + nk, nk), :])
        chunks = [slice(l0, l0 + lanes) for l0 in range(0, tt, lanes)]
        a1 = [top_sorted(s1[:, ls]) for ls in chunks]
        a2 = [top_sorted(s2[:, ls]) for ls in chunks]
        a1c = jnp.concatenate([compact(v) for v in a1], axis=1)
        a2c = jnp.concatenate([compact(v) for v in a2], axis=1)
        cand = _candidate_sums(a1c, a2c)
        cmax = a1c[0:1] + a2c[0:1]

        def cbody(rnd, carry):
            cnd, zsum, c_in, c_out = carry
            mx = jnp.max(cnd, axis=0, keepdims=True)
            zsum = zsum + jnp.where(rnd < topk, jnp.exp(mx - cmax), 0.0)
            c_in = jnp.where(rnd == topk - 1, mx, c_in)
            c_out = jnp.where(rnd == topk, mx, c_out)
            return jnp.where(cnd == mx, -BIG, cnd), zsum, c_in, c_out

        zero = jnp.zeros((1, tt), F32)
        _, zsum, c_in, c_out = lax.fori_loop(0, topk + 1, cbody, (cand, zero, zero, zero))
        tau = 0.5 * (c_in + c_out)
        inv_z = 1.0 / zsum
        pack = 16

        def rep(x):
            return jnp.concatenate([x] * (pack // sub), axis=0)

        for ci, ls in enumerate(chunks):
            tau8 = jnp.broadcast_to(tau[:, ls], (sub, lanes))
            need = [rep(tau8 - a2[ci][qq]) for qq in range(topk)]
            val2 = [rep(a2[ci][qq]) for qq in range(topk)]
            last1 = rep(a1[ci][topk - 1])
            top1 = rep(a1[ci][0])
            top2 = val2[0]
            for k in range(nk // pack):
                rows = slice(k * pack, (k + 1) * pack)
                x1 = s1[rows, ls]
                x2 = s2[rows, ls]
                n_sel = _count_leading(lambda t: x1 >= t, need)
                r2 = _count_leading(lambda t: x2 < t, val2)
                n_ref[h, rows, ls] = jnp.where(x1 >= last1, n_sel, 0.0)
                e1_ref[h, rows, ls] = jnp.exp(x1 - top1)
                r2_ref[h, rows, ls] = r2.astype(BF16)
                e2_ref[h, rows, ls] = (jnp.exp(x2 - top2) * inv_z[:, ls]).astype(BF16)
        return carry

    lax.fori_loop(0, PEER_HEADS, head, 0)


def _peer_topk(xnt, wqt, keys):
    m = xnt.shape[1]
    tt = 512 if m % 512 == 0 else 256
    shape = (PEER_HEADS, PEER_NKEYS, m)
    ospec = pl.BlockSpec((PEER_HEADS, PEER_NKEYS, tt), lambda i: (0, 0, i))
    return pl.pallas_call(
        _peer_topk_body,
        grid=(m // tt,),
        in_specs=[pl.BlockSpec((D_MODEL, tt), lambda i: (0, i)),
                  pl.BlockSpec(wqt.shape, lambda i: (0, 0)),
                  pl.BlockSpec(keys.shape, lambda i: (0, 0, 0))],
        out_specs=[ospec] * 4,
        out_shape=[jax.ShapeDtypeStruct(shape, F32), jax.ShapeDtypeStruct(shape, F32),
                   jax.ShapeDtypeStruct(shape, BF16), jax.ShapeDtypeStruct(shape, BF16)],
        scratch_shapes=[pltpu.VMEM((wqt.shape[0], tt), F32)],
        compiler_params=_cparams("parallel"),
        name="peer_topk",
    )(xnt, wqt, keys)


PEER_TOKEN_CHUNK = 256


def _peer_dense_body(u_ref, xnt_ref, vt_ref, n_ref, e1_ref, r2_ref, e2_ref, h_ref, hout_ref, acc_scr):
    nk = PEER_NKEYS
    tt = xnt_ref.shape[1]
    tc = PEER_TOKEN_CHUNK
    n_blk = u_ref.shape[0] // nk
    pack = 16

    @pl.when(pl.program_id(1) == 0)
    def _():
        acc_scr[...] = jnp.zeros_like(acc_scr)

    def rows_bf16(ref, h, ii, ls):
        row = jnp.broadcast_to(ref[h, ii:ii + 1, ls], (pack, tc)).astype(BF16)
        return jnp.concatenate([row] * (nk // pack), axis=0)

    def select_weights(ii, c0):
        ls = slice(c0, c0 + tc)
        w = None
        for h in range(PEER_HEADS):
            sel = jnp.where(r2_ref[h, :, ls] < rows_bf16(n_ref, h, ii, ls), e2_ref[h, :, ls],
                            jnp.zeros((), BF16))
            term = sel * rows_bf16(e1_ref, h, ii, ls)
            w = term if w is None else w + term
        return w

    def gelu_tanh(x):
        c = 0.7978845608028654
        inner = x * ((x * x) * (c * 0.044715) + c)
        hx = 0.5 * x
        return hx * jnp.tanh(inner) + hx

    starts = list(range(0, tt, tc))
    pre = [jnp.dot(u_ref[...], xnt_ref[:, c0:c0 + tc], preferred_element_type=F32) for c0 in starts]
    for idx, c0 in enumerate(starts):
        act = gelu_tanh(pre[idx].astype(BF16))
        a = jnp.concatenate([act[ii * nk:(ii + 1) * nk] * select_weights(ii, c0) for ii in range(n_blk)], axis=0)
        acc_scr[:, c0:c0 + tc] += jnp.dot(vt_ref[...], a, preferred_element_type=F32)

    @pl.when(pl.program_id(1) == pl.num_programs(1) - 1)
    def _():
        hout_ref[...] = h_ref[...] + jnp.transpose(acc_scr[...])


def _peer_dense(xnt, u_bf, vt_bf, n_sel, e1, r2, e2, h, tt):
    m = xnt.shape[1]
    te = 2048
    ib = te // PEER_NKEYS
    sel_i = pl.BlockSpec((PEER_HEADS, ib, tt), lambda i, e: (0, e, i))
    sel_all = pl.BlockSpec((PEER_HEADS, PEER_NKEYS, tt), lambda i, e: (0, 0, i))
    return pl.pallas_call(
        _peer_dense_body,
        grid=(m // tt, PEER_EXPERTS // te),
        in_specs=[pl.BlockSpec((te, D_MODEL), lambda i, e: (e, 0)),
                  pl.BlockSpec((D_MODEL, tt), lambda i, e: (0, i)),
                  pl.BlockSpec((D_MODEL, te), lambda i, e: (0, e)),
                  sel_i, sel_i, sel_all, sel_all,
                  pl.BlockSpec((tt, D_MODEL), lambda i, e: (i, 0))],
        out_specs=pl.BlockSpec((tt, D_MODEL), lambda i, e: (i, 0)),
        out_shape=jax.ShapeDtypeStruct((m, D_MODEL), F32),
        scratch_shapes=[pltpu.VMEM((D_MODEL, tt), F32)],
        compiler_params=_cparams("parallel", "arbitrary"),
        name="peer_dense",
    )(u_bf, xnt, vt_bf, n_sel, e1, r2, e2, h)


def _table_body(transpose, x_ref, o_ref):
    x = x_ref[...]
    o_ref[...] = (jnp.transpose(x) if transpose else x).astype(o_ref.dtype)


def _expert_table(tables, l, transpose):
    _, rows, cols = tables.shape
    tr = 512
    return pl.pallas_call(
        functools.partial(_table_body, transpose),
        grid=(rows // tr,),
        in_specs=[pl.BlockSpec((None, tr, cols), lambda i: (l, i, 0))],
        out_specs=pl.BlockSpec((cols, tr), lambda i: (0, i)) if transpose else pl.BlockSpec((tr, cols), lambda i: (i, 0)),
        out_shape=jax.ShapeDtypeStruct((cols, rows) if transpose else (rows, cols), BF16),
        compiler_params=_cparams("parallel"),
        name="expert_table",
    )(tables)


def _final_body(h_ref, nw_ref, o_ref):
    o_ref[...] = _rms(h_ref[...], nw_ref[...])


def _final(h, norm_w):
    m = h.shape[0]
    tm = 512
    row = pl.BlockSpec((tm, D_MODEL), lambda i: (i, 0))
    return pl.pallas_call(
        _final_body,
        grid=(m // tm,),
        in_specs=[row, pl.BlockSpec((1, D_MODEL), lambda i: (0, 0))],
        out_specs=row,
        out_shape=jax.ShapeDtypeStruct((m, D_MODEL), F32),
        compiler_params=_cparams("parallel"),
        name="final_norm",
    )(h, norm_w.reshape(1, D_MODEL))


def _pad_rows(rows, width, n_rows=8):
    flat = [jnp.pad(r.reshape(-1).astype(F32), (0, width - r.size)) for r in rows]
    return jnp.pad(jnp.stack(flat), ((0, n_rows - len(rows)), (0, 0)))


def _chunk_rows(ab):
    m, n = ab.shape
    return jnp.transpose(ab.reshape(m // GDN_CHUNK, GDN_CHUNK, n), (0, 2, 1)).astype(F32)


def _block_diag_groups(w):
    per = LRU_GROUP // LRU_BLOCK_DIM
    n_grp = w.shape[0] // per
    wg = w.reshape(n_grp, per, LRU_BLOCK_DIM, LRU_BLOCK_DIM)
    eye = jnp.eye(per, dtype=w.dtype)
    out = wg[:, :, :, None, :] * eye[None, :, None, :, None]
    return out.reshape(n_grp, LRU_GROUP, LRU_GROUP).astype(BF16)


_W_IN_PIECES = ((0, 6144, 0), (9488, 12560, COL_MERGE), (6160, 9488, COL_RWKV), (6144, 6160, COL_AB))


def _w_in_body(w_ref, o_ref):
    o_ref[:, COL_AB:] = jnp.zeros((o_ref.shape[0], o_ref.shape[1] - COL_AB), o_ref.dtype)
    for a, b, d in _W_IN_PIECES:
        o_ref[:, d:d + b - a] = w_ref[:, a:b].astype(o_ref.dtype)


def _prep_w_in(w_all, l):
    _, rows, cols = w_all.shape
    tr = 256
    return pl.pallas_call(
        _w_in_body,
        grid=(rows // tr,),
        in_specs=[pl.BlockSpec((None, tr, cols), lambda i: (l, i, 0))],
        out_specs=pl.BlockSpec((tr, IN_WIDTH_PAD), lambda i: (i, 0)),
        out_shape=jax.ShapeDtypeStruct((rows, IN_WIDTH_PAD), BF16),
        compiler_params=_cparams("parallel"),
        name="w_in_layout",
    )(w_all)


def _layer(l, h, bsz, seq, p):
    z, ab = _inproj(h, p["norm_mix_w"][l], _prep_w_in(p["w_in"], l))

    lru_vecs = _pad_rows([p["lru_conv_b"][l], p["lru_b_a"][l], p["lru_b_x"][l], p["lru_lambda"][l]], LRU_WIDTH)
    ya = _lru(z, bsz, seq, p["lru_conv_w"][l], lru_vecs,
              _block_diag_groups(p["lru_w_a"][l]), _block_diag_groups(p["lru_w_x"][l]))

    abt = _chunk_rows(ab[:, :2 * GDN_HEADS])
    gp = _pad_rows([p["gdn_a_log"][l], p["gdn_dt_bias"][l]], 128)
    gpt = jnp.pad(jnp.stack([p["gdn_a_log"][l], p["gdn_dt_bias"][l]], axis=1), ((0, 0), (0, 126)))
    yb = _gdn(z, ab, abt, bsz, seq, p["gdn_conv_w"][l], gp, gpt, p["gdn_norm_w"][l].reshape(1, GDN_DV))

    rw_vecs = _pad_rows([p["rwkv_w0"][l], p["rwkv_a0"][l], p["rwkv_k_k"][l], p["rwkv_k_a"][l],
                         p["rwkv_r_k"][l], p["rwkv_lnx_w"][l], p["rwkv_lnx_b"][l]], RWKV_WIDTH)
    wup = jnp.pad(p["rwkv_w_up"][l], ((0, 192), (0, 0))).astype(BF16)
    aup = jnp.pad(p["rwkv_a_up"][l], ((64, 128), (0, 0))).astype(BF16)
    gup = jnp.pad(p["rwkv_g_up"][l], ((128, 0), (0, 0))).astype(BF16)
    yc = _rwkv(z, bsz, seq, p["rwkv_mu"][l].reshape(1, -1), rw_vecs, wup, aup, gup)

    h, xnt = _merge(h, z, ya, yb, yc, p["merge_b"][l].reshape(1, -1), p["p_lru"][l].astype(BF16),
                    p["p_gdn"][l].astype(BF16), p["p_rwkv"][l].astype(BF16), p["w_out"][l].astype(BF16),
                    p["norm_ffn_w"][l].reshape(1, D_MODEL))

    wqt = jnp.transpose(p["peer_wq"][l]).astype(BF16)
    keys = p["peer_keys"][l].reshape(2 * PEER_HEADS, PEER_NKEYS, PEER_HALF).astype(BF16)
    n_sel, e1, r2, e2 = _peer_topk(xnt, wqt, keys)
    tt = 512 if xnt.shape[1] % 512 == 0 else 256
    return _peer_dense(xnt, _expert_table(p["peer_u"], l, False), _expert_table(p["peer_v"], l, True),
                       n_sel, e1, r2, e2, h, tt)


def kernel(x, norm_mix_w, norm_ffn_w, final_norm_w, w_in, lru_conv_w, lru_conv_b, lru_w_a, lru_b_a,
           lru_w_x, lru_b_x, lru_lambda, gdn_conv_w, gdn_a_log, gdn_dt_bias, gdn_norm_w, rwkv_mu, rwkv_w0,
           rwkv_w_up, rwkv_a0, rwkv_a_up, rwkv_g_up, rwkv_k_k, rwkv_k_a, rwkv_r_k, rwkv_lnx_w, rwkv_lnx_b,
           merge_b, p_lru, p_gdn, p_rwkv, w_out, peer_wq, peer_keys, peer_u, peer_v):
    p = dict(locals())
    bsz, seq, dim = x.shape
    h = x.reshape(bsz * seq, dim)
    for l in range(DEPTH):
        h = _layer(l, h, bsz, seq, p)
    return _final(h, final_norm_w).reshape(bsz, seq, dim)
```

```python
import functools

import jax
import jax.numpy as jnp
from jax import lax
from jax.experimental import pallas as pl
from jax.experimental.pallas import tpu as pltpu

F32 = jnp.float32
BF16 = jnp.bfloat16

D_MODEL = 1024
DEPTH = 2
RMS_EPS = 1e-6

LRU_WIDTH = 1024
LRU_BLOCK_DIM = 64
LRU_C = 8.0
LRU_GROUP = 256

GDN_HEADS = 8
GDN_DK = 128
GDN_DV = 128
GDN_CHUNK = 64

RWKV_HD = 64
RWKV_WIDTH = 1024
RWKV_GN_EPS = 64e-5
RWKV_CHUNK = 64
RWKV_GROUP = 256
RWKV_LORA = 256

PEER_HEADS = 8
PEER_NKEYS = 128
PEER_EXPERTS = PEER_NKEYS * PEER_NKEYS
PEER_HALF = 128
PEER_TOPK = 16

COL_U = 0
COL_GATE = 1024
COL_QKV = 2048
COL_ZG = 5120
COL_MERGE = 6144
COL_RWKV = 9216
COL_AB = 12544
IN_WIDTH_PAD = 12800
IN_TILE_N = 1280

VMEM_LIMIT = 48 * 1024 * 1024
BIG = 3.0e38


def _cparams(*sem):
    return pltpu.CompilerParams(dimension_semantics=sem, vmem_limit_bytes=VMEM_LIMIT)


def _mm(a, b):
    return jnp.dot(a.astype(BF16), b.astype(BF16), preferred_element_type=F32)


def _mm_nt(a, b):
    return lax.dot_general(a.astype(BF16), b.astype(BF16), (((1,), (1,)), ((), ())),
                           preferred_element_type=F32)


def _mm_tn(a, b):
    return lax.dot_general(a.astype(BF16), b.astype(BF16), (((0,), (0,)), ((), ())),
                           preferred_element_type=F32)


def _softplus(x):
    return jnp.maximum(x, 0.0) + jnp.log1p(jnp.exp(-jnp.abs(x)))


def _rms(x, w):
    return x * lax.rsqrt(jnp.mean(x * x, axis=-1, keepdims=True) + RMS_EPS) * w


def _shift_prev(x, prev8, s):
    r = pltpu.roll(x, s, 0)
    pr = pltpu.roll(prev8, s, 0)
    rows8 = lax.broadcasted_iota(jnp.int32, prev8.shape, 0)
    head = jnp.where(rows8 < s, pr, r[:8])
    return jnp.concatenate([head, r[8:]], axis=0)


def _shift_fill(x, d, fill):
    n, c = x.shape
    if d % 8 == 0:
        return jnp.concatenate([jnp.full((d, c), fill, x.dtype), x[:n - d]], axis=0)
    r = pltpu.roll(x, d, 0)
    rows8 = lax.broadcasted_iota(jnp.int32, (8, c), 0)
    head = jnp.where(rows8 < d, fill, r[:8])
    return jnp.concatenate([head, r[8:]], axis=0)


def _tri(n, strict):
    i = lax.broadcasted_iota(jnp.int32, (n, n), 0)
    j = lax.broadcasted_iota(jnp.int32, (n, n), 1)
    return (j < i) if strict else (j <= i)


def _split_bf16(x, terms):
    parts = []
    rem = x
    for i in range(terms):
        p = rem.astype(BF16)
        parts.append(p)
        if i + 1 < terms:
            rem = rem - p.astype(F32)
    return parts


def _mask_mm(mask_bf, x, terms=3):
    return sum(jnp.dot(mask_bf, p, preferred_element_type=F32) for p in _split_bf16(x, terms))


def _mm_mask(x, mask_bf, terms=3):
    return sum(jnp.dot(p, mask_bf, preferred_element_type=F32) for p in _split_bf16(x, terms))


def _inproj_body(h_ref, nw_ref, w_ref, z_ref, ab_ref, xn_scr):
    @pl.when(pl.program_id(1) == 0)
    def _():
        xn_scr[...] = _rms(h_ref[...], nw_ref[...]).astype(BF16)

    res = jnp.dot(xn_scr[...], w_ref[...], preferred_element_type=F32)
    z_ref[...] = res.astype(z_ref.dtype)

    @pl.when(pl.program_id(1) == pl.num_programs(1) - 1)
    def _():
        c0 = COL_AB - (IN_WIDTH_PAD - res.shape[1])
        ab_ref[...] = res[:, c0:c0 + ab_ref.shape[1]]


def _inproj(h, norm_w, w_pad):
    m = h.shape[0]
    tm, tn = (1024 if m % 1024 == 0 else 512), IN_TILE_N
    assert IN_WIDTH_PAD % tn == 0 and COL_AB >= IN_WIDTH_PAD - tn
    return pl.pallas_call(
        _inproj_body,
        grid=(m // tm, IN_WIDTH_PAD // tn),
        in_specs=[pl.BlockSpec((tm, D_MODEL), lambda i, j: (i, 0)),
                  pl.BlockSpec((1, D_MODEL), lambda i, j: (0, 0)),
                  pl.BlockSpec((D_MODEL, tn), lambda i, j: (0, j))],
        out_specs=[pl.BlockSpec((tm, tn), lambda i, j: (i, j)), pl.BlockSpec((tm, 128), lambda i, j: (i, 0))],
        out_shape=[jax.ShapeDtypeStruct((m, IN_WIDTH_PAD), BF16), jax.ShapeDtypeStruct((m, 128), F32)],
        scratch_shapes=[pltpu.VMEM((tm, D_MODEL), BF16)],
        compiler_params=_cparams("parallel", "arbitrary"),
        name="norm_inproj",
    )(h, norm_w.reshape(1, D_MODEL), w_pad)


def _lru_body(zu_ref, zg_ref, cw_ref, vp_ref, wa_ref, wx_ref, o_ref, prev_scr, hc_scr):
    tt = zu_ref.shape[0]

    @pl.when(pl.program_id(1) == 0)
    def _():
        prev_scr[...] = jnp.zeros_like(prev_scr)
        hc_scr[...] = jnp.zeros_like(hc_scr)

    u = zu_ref[...].astype(F32)
    prev = prev_scr[...]
    cw = cw_ref[...]
    vp = vp_ref[...]
    xc = (cw[3:4] * u + cw[2:3] * _shift_prev(u, prev, 1) + cw[1:2] * _shift_prev(u, prev, 2)
          + cw[0:1] * _shift_prev(u, prev, 3) + vp[0:1])
    prev_scr[...] = u[tt - 8:]

    xcb = xc.astype(BF16)
    n_grp = LRU_WIDTH // LRU_GROUP
    pre_a = jnp.concatenate(
        [jnp.dot(xcb[:, g * LRU_GROUP:(g + 1) * LRU_GROUP], wa_ref[g], preferred_element_type=F32)
         for g in range(n_grp)], axis=1)
    pre_x = jnp.concatenate(
        [jnp.dot(xcb[:, g * LRU_GROUP:(g + 1) * LRU_GROUP], wx_ref[g], preferred_element_type=F32)
         for g in range(n_grp)], axis=1)
    r = jax.nn.sigmoid(pre_a + vp[1:2])
    i = jax.nn.sigmoid(pre_x + vp[2:3])
    log_a = (-LRU_C * r) * _softplus(-vp[3:4])
    a = jnp.exp(log_a)
    b = jnp.sqrt(-jnp.tanh(log_a) * (a * a + 1.0)) * (i * xc)

    d = 1
    while d < tt:
        b = a * _shift_fill(b, d, 0.0) + b
        a = a * _shift_fill(a, d, 1.0)
        d *= 2
    hcar = hc_scr[...]
    hval = b + a * hcar[0:1]
    hc_scr[...] = jnp.broadcast_to(hval[tt - 1:tt], hcar.shape)
    o_ref[...] = (hval * jax.nn.gelu(zg_ref[...].astype(F32))).astype(o_ref.dtype)


def _lru(z, bsz, seq, conv_w, vecs, wa_bd, wx_bd):
    m = z.shape[0]
    tt = 256
    nt = seq // tt
    rowmap = lambda c: (lambda b, t: (b * nt + t, c))
    const2 = lambda b, t: (0, 0)
    return pl.pallas_call(
        _lru_body,
        grid=(bsz, nt),
        in_specs=[
            pl.BlockSpec((tt, LRU_WIDTH), rowmap(COL_U // LRU_WIDTH)),
            pl.BlockSpec((tt, LRU_WIDTH), rowmap(COL_GATE // LRU_WIDTH)),
            pl.BlockSpec((4, LRU_WIDTH), const2),
            pl.BlockSpec((8, LRU_WIDTH), const2),
            pl.BlockSpec(wa_bd.shape, lambda b, t: (0, 0, 0)),
            pl.BlockSpec(wx_bd.shape, lambda b, t: (0, 0, 0)),
        ],
        out_specs=pl.BlockSpec((tt, LRU_WIDTH), rowmap(0)),
        out_shape=jax.ShapeDtypeStruct((m, LRU_WIDTH), BF16),
        scratch_shapes=[pltpu.VMEM((8, LRU_WIDTH), F32), pltpu.VMEM((8, LRU_WIDTH), F32)],
        compiler_params=_cparams("parallel", "arbitrary"),
        name="rglru",
    )(z, z, conv_w, vecs, wa_bd, wx_bd)


def _gdn_body(q_ref, k_ref, v_ref, zg_ref, ab_ref, abt_ref, cw_ref, gp_ref, gpt_ref, nw_ref,
              o_ref, pq_scr, pk_scr, pv_scr, st_scr):
    tt = q_ref.shape[0]
    c = GDN_CHUNK
    width = GDN_HEADS * GDN_DK

    @pl.when(pl.program_id(1) == 0)
    def _():
        for s in (pq_scr, pk_scr, pv_scr, st_scr):
            s[...] = jnp.zeros_like(s)

    cw = cw_ref[...]

    def conv_silu(x_ref, p_scr, w):
        x = x_ref[...].astype(F32)
        prev = p_scr[...]
        y = (w[3:4] * x + w[2:3] * _shift_prev(x, prev, 1) + w[1:2] * _shift_prev(x, prev, 2)
             + w[0:1] * _shift_prev(x, prev, 3))
        p_scr[...] = x[tt - 8:]
        return jax.nn.silu(y)

    q = conv_silu(q_ref, pq_scr, cw[:, 0:width])
    k = conv_silu(k_ref, pk_scr, cw[:, width:2 * width])
    v = conv_silu(v_ref, pv_scr, cw[:, 2 * width:])

    ab = ab_ref[...].astype(F32)
    gp = gp_ref[...]
    gpt = gpt_ref[...]
    g_all = -jnp.exp(gp[0:1]) * _softplus(ab + gp[1:2])
    beta_all = jax.nn.sigmoid(ab)

    incl = _tri(c, False)
    strict = _tri(c, True)
    cum_mask = jnp.concatenate([incl.astype(BF16), jnp.ones((c, c), BF16)], axis=0)
    triu_bf = (lax.broadcasted_iota(jnp.int32, (c, c), 0) <= lax.broadcasted_iota(jnp.int32, (c, c), 1)).astype(BF16)
    nw = nw_ref[...]

    n_chunks = tt // c
    heads = range(GDN_HEADS)
    inst = [(ci, h) for ci in range(n_chunks) for h in heads]
    sls = [slice(ci * c, (ci + 1) * c) for ci in range(n_chunks)]
    hss = [slice(h * GDN_DK, (h + 1) * GDN_DK) for h in heads]
    gcs = [_mask_mm(cum_mask, g_all[sl]) for sl in sls]
    gr_all = [_mm_mask(-jnp.exp(gpt[:, 0:1]) * _softplus(abt_ref[ci][0:GDN_HEADS] + gpt[:, 1:2]), triu_bf)
              for ci in range(n_chunks)]

    qs, ks, kbs, decays, egcs, kdecs, gtots, bcols = {}, {}, {}, {}, {}, {}, {}, {}
    for ci, h in inst:
        sl, hs = sls[ci], hss[h]
        qc, kc = q[sl, hs], k[sl, hs]
        qs[ci, h] = qc * lax.rsqrt(jnp.sum(qc * qc, axis=-1, keepdims=True) + RMS_EPS) * (GDN_DK ** -0.5)
        ks[ci, h] = kc * lax.rsqrt(jnp.sum(kc * kc, axis=-1, keepdims=True) + RMS_EPS)
        gcol = gcs[ci][:c, h:h + 1]
        g_last = gcs[ci][c:, h:h + 1]
        grow = gr_all[ci][h:h + 1, :]
        bcols[ci, h] = beta_all[sl, GDN_HEADS + h:GDN_HEADS + h + 1]
        decays[ci, h] = jnp.exp(jnp.where(incl, gcol - grow, -BIG))
        kbs[ci, h] = ks[ci, h] * bcols[ci, h]
        egcs[ci, h] = jnp.exp(gcol)
        kdecs[ci, h] = ks[ci, h] * jnp.exp(g_last - gcol)
        gtots[ci, h] = jnp.exp(jnp.concatenate([g_last] * (GDN_DK // c), axis=0))

    kk = {i: _mm_nt(kbs[i], ks[i]) for i in inst}
    qk = {i: _mm_nt(qs[i], ks[i]) * decays[i] for i in inst}
    p = {i: -jnp.where(strict, kk[i] * decays[i], 0.0) for i in inst}
    tm1 = dict(p)
    for _ in range(5):
        p = {i: _mm(p[i], p[i]) for i in inst}
        tp = {i: _mm(tm1[i], p[i]) for i in inst}
        tm1 = {i: tm1[i] + tp[i] + p[i] for i in inst}
    rhs = {i: jnp.concatenate([v[sls[i[0]], hss[i[1]]] * bcols[i], kbs[i] * egcs[i]], axis=1) for i in inst}
    tr = {i: _mm(tm1[i], rhs[i]) for i in inst}
    sol = {i: rhs[i] + tr[i] for i in inst}

    state = [st_scr[h] for h in heads]
    for ci in range(n_chunks):
        ws = [_mm(sol[ci, h][:, GDN_DV:], state[h]) for h in heads]
        qst = [_mm(qs[ci, h] * egcs[ci, h], state[h]) for h in heads]
        v_new = [sol[ci, h][:, :GDN_DV] - ws[h] for h in heads]
        qv = [_mm(qk[ci, h], v_new[h]) for h in heads]
        kv = [_mm_tn(kdecs[ci, h], v_new[h]) for h in heads]
        state = [state[h] * gtots[ci, h] + kv[h] for h in heads]
        for h in heads:
            o = qst[h] + qv[h]
            o = o * lax.rsqrt(jnp.mean(o * o, axis=-1, keepdims=True) + RMS_EPS) * nw
            o = o * jax.nn.silu(zg_ref[sls[ci], hss[h]].astype(F32))
            o_ref[sls[ci], hss[h]] = o.astype(o_ref.dtype)
    for h in heads:
        st_scr[h] = state[h]


def _gdn(z, ab, abt, bsz, seq, conv_w, gp, gpt, norm_w):
    m = z.shape[0]
    tt = 256
    nt = seq // tt
    width = GDN_HEADS * GDN_DK
    zblk = lambda c0: pl.BlockSpec((tt, width), lambda b, t: (b * nt + t, c0 // width))
    const2 = lambda b, t: (0, 0)
    return pl.pallas_call(
        _gdn_body,
        grid=(bsz, nt),
        in_specs=[
            zblk(COL_QKV), zblk(COL_QKV + width), zblk(COL_QKV + 2 * width), zblk(COL_ZG),
            pl.BlockSpec((tt, 128), lambda b, t: (b * nt + t, 0)),
            pl.BlockSpec((tt // GDN_CHUNK, 2 * GDN_HEADS, GDN_CHUNK), lambda b, t: (b * nt + t, 0, 0)),
            pl.BlockSpec((4, 3 * width), const2),
            pl.BlockSpec((8, 128), const2),
            pl.BlockSpec((8, 128), const2),
            pl.BlockSpec((1, 128), const2),
        ],
        out_specs=pl.BlockSpec((tt, width), lambda b, t: (b * nt + t, 0)),
        out_shape=jax.ShapeDtypeStruct((m, width), BF16),
        scratch_shapes=[pltpu.VMEM((8, width), F32)] * 3 + [pltpu.VMEM((GDN_HEADS, GDN_DK, GDN_DV), F32)],
        compiler_params=_cparams("parallel", "arbitrary"),
        name="gated_deltanet",
    )(z, z, z, z, ab, abt, conv_w, gp, gpt, norm_w)


def _rwkv_body(r_ref, k_ref, v_ref, lo_ref, mu_ref, vp_ref, wup_ref, aup_ref, gup_ref, o_ref,
               pr_scr, pk_scr, pv_scr, plo_scr, st_scr):
    tt = r_ref.shape[0]
    c = RWKV_CHUNK
    gw = RWKV_GROUP
    hd = RWKV_HD
    width = RWKV_WIDTH

    @pl.when(pl.program_id(1) == 0)
    def _():
        for s in (pr_scr, pk_scr, pv_scr, plo_scr, st_scr):
            s[...] = jnp.zeros_like(s)

    mu = mu_ref[...]

    def tshift(x_ref, p_scr, m):
        x = x_ref[...].astype(F32)
        xs = x + m * (_shift_prev(x, p_scr[...], 1) - x)
        p_scr[...] = x[tt - 8:]
        return xs

    r = tshift(r_ref, pr_scr, mu[:, 0:width])
    k = tshift(k_ref, pk_scr, mu[:, width:2 * width])
    v = tshift(v_ref, pv_scr, mu[:, 2 * width:3 * width])
    lo = tshift(lo_ref, plo_scr, mu[:, 3 * width:])
    vp = vp_ref[...]
    w0, a0, k_k, k_a, r_k, lnx_w, lnx_b = (vp[i:i + 1] for i in range(7))

    lane = lax.broadcasted_iota(jnp.int32, lo.shape, 1)
    lo_act = jnp.where(lane < 64, jnp.tanh(lo), jnp.where(lane < 128, lo, jax.nn.sigmoid(lo)))
    w_pre = _mm(lo_act, wup_ref[...])
    a_pre = _mm(lo_act, aup_ref[...])
    gate = _mm(lo_act, gup_ref[...])
    w_log = -_softplus(-(w0 + w_pre)) - 0.5
    lw = -jnp.exp(w_log)
    a = jax.nn.sigmoid(a0 + a_pre)

    bi = lax.broadcasted_iota(jnp.int32, (gw, gw), 0) // hd
    bj = lax.broadcasted_iota(jnp.int32, (gw, gw), 1) // hd
    bdmask = bi == bj
    ones_bd = bdmask.astype(BF16)

    def head_sum(x):
        return _mm_mask(x, ones_bd, terms=1)

    def bd(x):
        return jnp.where(bdmask, jnp.concatenate([x] * (gw // c), axis=0), 0.0)

    ti = lax.broadcasted_iota(jnp.int32, (c, gw), 0)
    sj = lax.broadcasted_iota(jnp.int32, (c, gw), 1) % c
    strict = sj < ti
    incl = sj <= ti
    tril_bf = _tri(c, False).astype(BF16)

    kk_all = k * k_k
    k2_all = k * (1.0 + (a - 1.0) * k_a)

    n_chunks = tt // c
    groups = range(width // gw)
    inst = [(ci, g) for ci in range(n_chunks) for g in groups]
    sls = [slice(ci * c, (ci + 1) * c) for ci in range(n_chunks)]
    gss = [slice(g * gw, (g + 1) * gw) for g in groups]
    cl_all = [_mask_mm(tril_bf, lw[sl]) for sl in sls]

    def at(x, i):
        return x[sls[i[0]], gss[i[1]]]

    kk_raw = {i: at(kk_all, i) for i in inst}
    kk_ss = {i: head_sum(kk_raw[i] * kk_raw[i]) for i in inst}
    a_h, b_h, k_h, br, w_end = {}, {}, {}, {}, {}
    for i in inst:
        cl = cl_all[i[0]][:, gss[i[1]]]
        kk = kk_raw[i] * lax.rsqrt(kk_ss[i] + RMS_EPS)
        wcum = jnp.exp(cl)
        inv_w = jnp.exp(-cl)
        a_h[i] = kk * at(a, i) * inv_w
        b_h[i] = kk * jnp.exp(cl - at(lw, i))
        k_h[i] = at(k2_all, i) * inv_w
        br[i] = jnp.concatenate([b_h[i], at(r, i) * wcum], axis=0)
        w_end[i] = wcum[c - 1:c]
    v_bd = {i: bd(at(v, i)) for i in inst}
    xa = {i: _mm_nt(br[i], bd(a_h[i])) for i in inst}
    xk = {i: _mm_nt(br[i], bd(k_h[i])) for i in inst}
    l_k = {i: jnp.where(strict, xk[i][:c], 0.0) for i in inst}
    ra = {i: jnp.where(incl, xa[i][c:], 0.0) for i in inst}
    rk = {i: jnp.where(incl, xk[i][c:], 0.0) for i in inst}
    p = {i: -jnp.where(strict, xa[i][:c], 0.0) for i in inst}
    tm1 = dict(p)
    for _ in range(5):
        p = {i: _mm(p[i], bd(p[i])) for i in inst}
        tp = {i: _mm(tm1[i], bd(p[i])) for i in inst}
        tm1 = {i: tm1[i] + tp[i] + p[i] for i in inst}
    lkv = {i: _mm(l_k[i], v_bd[i]) for i in inst}
    rkv = {i: _mm(rk[i], v_bd[i]) for i in inst}
    so = {i: head_sum(at(r, i) * at(k2_all, i) * r_k[:, gss[i[1]]]) for i in inst}

    state_t = [st_scr[g] for g in groups]
    for ci in range(n_chunks):
        ids = [(ci, g) for g in groups]
        brh = [_mm_nt(br[i], state_t[i[1]]) for i in ids]
        rhs = [brh[g][:c] + lkv[ci, g] for g in groups]
        tu = [_mm(tm1[ci, g], bd(rhs[g])) for g in groups]
        u = [rhs[g] + tu[g] for g in groups]
        rau = [_mm(ra[ci, g], bd(u[g])) for g in groups]
        upd = [_mm_tn(jnp.concatenate([u[g], at(v, (ci, g))], axis=0),
                      jnp.concatenate([-(a_h[ci, g] * w_end[ci, g]), k_h[ci, g] * w_end[ci, g]], axis=0))
               for g in groups]
        state_t = [w_end[ci, g] * state_t[g] + jnp.where(bdmask, upd[g], 0.0) for g in groups]
        o = [brh[g][c:] - rau[g] + rkv[ci, g] for g in groups]
        osum = [head_sum(o[g]) for g in groups]
        cen = [o[g] - osum[g] * (1.0 / hd) for g in groups]
        var = [head_sum(cen[g] * cen[g]) * (1.0 / hd) for g in groups]
        for g in groups:
            gs = gss[g]
            on = cen[g] * lax.rsqrt(var[g] + RWKV_GN_EPS) * lnx_w[:, gs] + lnx_b[:, gs]
            bonus = so[ci, g] * at(v, (ci, g))
            o_ref[sls[ci], gs] = ((on + bonus) * gate[sls[ci], gs]).astype(o_ref.dtype)
    for g in groups:
        st_scr[g] = state_t[g]


def _rwkv(z, bsz, seq, mu, vecs, wup_pad, aup_pad, gup_pad):
    m = z.shape[0]
    tt = 256
    nt = seq // tt
    gw = RWKV_GROUP
    width = RWKV_WIDTH
    zblk = lambda c0: pl.BlockSpec((tt, width), lambda b, t: (b * nt + t, c0 // width))
    lora_col = (COL_RWKV + 3 * width) // RWKV_LORA
    const2 = lambda b, t: (0, 0)
    return pl.pallas_call(
        _rwkv_body,
        grid=(bsz, nt),
        in_specs=[
            zblk(COL_RWKV), zblk(COL_RWKV + width), zblk(COL_RWKV + 2 * width),
            pl.BlockSpec((tt, RWKV_LORA), lambda b, t: (b * nt + t, lora_col)),
            pl.BlockSpec(mu.shape, const2),
            pl.BlockSpec((8, width), const2),
            pl.BlockSpec((RWKV_LORA, width), const2),
            pl.BlockSpec((RWKV_LORA, width), const2),
            pl.BlockSpec((RWKV_LORA, width), const2),
        ],
        out_specs=pl.BlockSpec((tt, width), lambda b, t: (b * nt + t, 0)),
        out_shape=jax.ShapeDtypeStruct((m, width), BF16),
        scratch_shapes=[pltpu.VMEM((8, width), F32)] * 3 + [pltpu.VMEM((8, RWKV_LORA), F32),
                                                            pltpu.VMEM((width // gw, gw, gw), F32)],
        compiler_params=_cparams("parallel", "arbitrary"),
        name="rwkv7",
    )(z, z, z, z, mu, vecs, wup_pad, aup_pad, gup_pad)


def _merge_body(h_ref, ya_ref, yb_ref, yc_ref, za_ref, zb_ref, zc_ref, mb_ref, pa_ref, pb_ref,
                pc_ref, wo_ref, nw_ref, hout_ref, xnt_ref):
    mb = mb_ref[...]
    ga = jax.nn.sigmoid(za_ref[...].astype(F32) + mb[:, 0:D_MODEL])
    gb = jax.nn.sigmoid(zb_ref[...].astype(F32) + mb[:, D_MODEL:2 * D_MODEL])
    gc = jax.nn.sigmoid(zc_ref[...].astype(F32) + mb[:, 2 * D_MODEL:])
    merged = (ga * jnp.dot(ya_ref[...], pa_ref[...], preferred_element_type=F32)
              + gb * jnp.dot(yb_ref[...], pb_ref[...], preferred_element_type=F32)
              + gc * jnp.dot(yc_ref[...], pc_ref[...], preferred_element_type=F32))
    h = h_ref[...] + _mm(merged, wo_ref[...])
    hout_ref[...] = h
    xnt_ref[...] = jnp.transpose(_rms(h, nw_ref[...])).astype(BF16)


def _merge(h, z, ya, yb, yc, merge_b, pa, pb, pc, wo, norm_w):
    m = h.shape[0]
    tm = 256
    row = lambda c: pl.BlockSpec((tm, D_MODEL), lambda i: (i, c))
    wspec = pl.BlockSpec((D_MODEL, D_MODEL), lambda i: (0, 0))
    mc = COL_MERGE // D_MODEL
    return pl.pallas_call(
        _merge_body,
        grid=(m // tm,),
        in_specs=[row(0), row(0), row(0), row(0), row(mc), row(mc + 1), row(mc + 2),
                  pl.BlockSpec((1, 3 * D_MODEL), lambda i: (0, 0)),
                  wspec, wspec, wspec, wspec,
                  pl.BlockSpec((1, D_MODEL), lambda i: (0, 0))],
        out_specs=[row(0), pl.BlockSpec((D_MODEL, tm), lambda i: (0, i))],
        out_shape=[jax.ShapeDtypeStruct((m, D_MODEL), F32), jax.ShapeDtypeStruct((D_MODEL, m), BF16)],
        compiler_params=_cparams("parallel"),
        name="merge_outproj",
    )(h, ya, yb, yc, z, z, z, merge_b, pa, pb, pc, wo, norm_w)


_CAND_SLABS = (
    (16, ((0, 0, 16),)),
    (16, ((1, 0, 8), (2, 8, 5), (4, 13, 3))),
    (16, ((3, 0, 4), (5, 4, 2), (6, 6, 2), (7, 8, 2), (8, 10, 1), (9, 11, 1), (10, 12, 1), (11, 13, 1),
          (12, 14, 1), (13, 15, 1))),
    (8, ((14, 0, 1), (15, 1, 1))),
)


def _odd_even_merge_sort_pairs(n):
    pairs = []
    p = 1
    while p < n:
        k = p
        while k >= 1:
            for j in range(k % p, n - k, 2 * k):
                for i in range(min(k, n - j - k)):
                    if (i + j) // (2 * p) == (i + j + k) // (2 * p):
                        pairs.append((i + j, i + j + k))
            k //= 2
        p *= 2
    return tuple(pairs)


_SORT_PAIRS = _odd_even_merge_sort_pairs(PEER_TOPK)


def _candidate_sums(a1, a2):
    slabs = []
    for nrows, pieces in _CAND_SLABS:
        base = a2[:nrows]
        rows = lax.broadcasted_iota(jnp.int32, base.shape, 0)
        out = None
        end = 0
        for p, off, cnt in pieces:
            val = a1[p:p + 1] + (base if off == 0 else pltpu.roll(base, off, 0))
            out = val if out is None else jnp.where(rows >= off, val, out)
            end = off + cnt
        if end < nrows:
            out = jnp.where(rows >= end, -BIG, out)
        slabs.append(out)
    return jnp.concatenate(slabs, axis=0)


def _count_leading(pred, thr):
    assert len(thr) == 16
    b8 = pred(thr[7])
    b4 = pred(jnp.where(b8, thr[11], thr[3]))
    b2 = pred(jnp.where(b8, jnp.where(b4, thr[13], thr[9]), jnp.where(b4, thr[5], thr[1])))
    lo = jnp.where(b4, jnp.where(b2, thr[6], thr[4]), jnp.where(b2, thr[2], thr[0]))
    hi = jnp.where(b4, jnp.where(b2, thr[14], thr[12]), jnp.where(b2, thr[10], thr[8]))
    b1 = pred(jnp.where(b8, hi, lo))
    cnt = (jnp.where(b8, 8.0, 0.0) + jnp.where(b4, 4.0, 0.0)) + (jnp.where(b2, 2.0, 0.0) + jnp.where(b1, 1.0, 0.0))
    return jnp.where(pred(thr[15]), 16.0, cnt)


def _peer_topk_body(xnt_ref, wqt_ref, keys_ref, n_ref, e1_ref, r2_ref, e2_ref, q_scr):
    tt = xnt_ref.shape[1]
    nk = PEER_NKEYS
    topk = PEER_TOPK
    q_scr[...] = jnp.dot(wqt_ref[...], xnt_ref[...], preferred_element_type=F32)
    lanes = 128
    sub = 8
    assert nk == topk * sub
    row8 = lax.broadcasted_iota(jnp.int32, (sub, lanes), 0)

    def top_sorted(s):
        v = [s[k * sub:(k + 1) * sub] for k in range(topk)]
        for i, j in _SORT_PAIRS:
            v[i], v[j] = jnp.maximum(v[i], v[j]), jnp.minimum(v[i], v[j])
        for shift in (4, 2, 1):
            other = [pltpu.roll(x, shift, 0) for x in v]
            v = [jnp.maximum(v[k], other[topk - 1 - k]) for k in range(topk)]
            d = topk // 2
            while d >= 1:
                for i in range(topk):
                    if i & d == 0:
                        v[i], v[i + d] = jnp.maximum(v[i], v[i + d]), jnp.minimum(v[i], v[i + d])
                d //= 2
        return v

    def compact(v):
        tiles = []
        for t0 in range(0, topk, sub):
            out = v[t0]
            for k in range(1, sub):
                out = jnp.where(row8 == k, v[t0 + k], out)
            tiles.append(out)
        return jnp.concatenate(tiles, axis=0)

    def head(h, carry):
        o1 = pl.multiple_of(h * (2 * nk), 2 * nk)
        s1 = _mm(keys_ref[2 * h], q_scr[pl.ds(o1, nk), :])
        s2 = _mm(keys_ref[2 * h + 1], q_scr[pl.ds(o1 + nk, nk), :])
        chunks = [slice(l0, l0 + lanes) for l0 in range(0, tt, lanes)]
        a1 = [top_sorted(s1[:, ls]) for ls in chunks]
        a2 = [top_sorted(s2[:, ls]) for ls in chunks]
        a1c = jnp.concatenate([compact(v) for v in a1], axis=1)
        a2c = jnp.concatenate([compact(v) for v in a2], axis=1)
        cand = _candidate_sums(a1c, a2c)
        cmax = a1c[0:1] + a2c[0:1]

        def cbody(rnd, carry):
            cnd, zsum, c_in, c_out = carry
            mx = jnp.max(cnd, axis=0, keepdims=True)
            zsum = zsum + jnp.where(rnd < topk, jnp.exp(mx - cmax), 0.0)
            c_in = jnp.where(rnd == topk - 1, mx, c_in)
            c_out = jnp.where(rnd == topk, mx, c_out)
            return jnp.where(cnd == mx, -BIG, cnd), zsum, c_in, c_out

        zero = jnp.zeros((1, tt), F32)
        _, zsum, c_in, c_out = lax.fori_loop(0, topk + 1, cbody, (cand, zero, zero, zero))
        tau = 0.5 * (c_in + c_out)
        inv_z = 1.0 / zsum
        pack = 16

        def rep(x):
            return jnp.concatenate([x] * (pack // sub), axis=0)

        for ci, ls in enumerate(chunks):
            tau8 = jnp.broadcast_to(tau[:, ls], (sub, lanes))
            need = [rep(tau8 - a2[ci][qq]) for qq in range(topk)]
            val2 = [rep(a2[ci][qq]) for qq in range(topk)]
            last1 = rep(a1[ci][topk - 1])
            top1 = rep(a1[ci][0])
            top2 = val2[0]
            for k in range(nk // pack):
                rows = slice(k * pack, (k + 1) * pack)
                x1 = s1[rows, ls]
                x2 = s2[rows, ls]
                n_sel = _count_leading(lambda t: x1 >= t, need)
                r2 = _count_leading(lambda t: x2 < t, val2)
                n_ref[h, rows, ls] = jnp.where(x1 >= last1, n_sel, 0.0)
                e1_ref[h, rows, ls] = jnp.exp(x1 - top1)
                r2_ref[h, rows, ls] = r2.astype(BF16)
                e2_ref[h, rows, ls] = (jnp.exp(x2 - top2) * inv_z[:, ls]).astype(BF16)
        return carry

    lax.fori_loop(0, PEER_HEADS, head, 0)


def _peer_topk(xnt, wqt, keys):
    m = xnt.shape[1]
    tt = 512 if m % 512 == 0 else 256
    shape = (PEER_HEADS, PEER_NKEYS, m)
    ospec = pl.BlockSpec((PEER_HEADS, PEER_NKEYS, tt), lambda i: (0, 0, i))
    return pl.pallas_call(
        _peer_topk_body,
        grid=(m // tt,),
        in_specs=[pl.BlockSpec((D_MODEL, tt), lambda i: (0, i)),
                  pl.BlockSpec(wqt.shape, lambda i: (0, 0)),
                  pl.BlockSpec(keys.shape, lambda i: (0, 0, 0))],
        out_specs=[ospec] * 4,
        out_shape=[jax.ShapeDtypeStruct(shape, F32), jax.ShapeDtypeStruct(shape, F32),
                   jax.ShapeDtypeStruct(shape, BF16), jax.ShapeDtypeStruct(shape, BF16)],
        scratch_shapes=[pltpu.VMEM((wqt.shape[0], tt), F32)],
        compiler_params=_cparams("parallel"),
        name="peer_topk",
    )(xnt, wqt, keys)


PEER_TOKEN_CHUNK = 256


def _peer_dense_body(u_ref, xnt_ref, vt_ref, n_ref, e1_ref, r2_ref, e2_ref, h_ref, hout_ref, acc_scr):
    nk = PEER_NKEYS
    tt = xnt_ref.shape[1]
    tc = PEER_TOKEN_CHUNK
    n_blk = u_ref.shape[0] // nk
    pack = 16

    @pl.when(pl.program_id(1) == 0)
    def _():
        acc_scr[...] = jnp.zeros_like(acc_scr)

    def rows_bf16(ref, h, ii, ls):
        row = jnp.broadcast_to(ref[h, ii:ii + 1, ls], (pack, tc)).astype(BF16)
        return jnp.concatenate([row] * (nk // pack), axis=0)

    def select_weights(ii, c0):
        ls = slice(c0, c0 + tc)
        w = None
        for h in range(PEER_HEADS):
            sel = jnp.where(r2_ref[h, :, ls] < rows_bf16(n_ref, h, ii, ls), e2_ref[h, :, ls],
                            jnp.zeros((), BF16))
            term = sel * rows_bf16(e1_ref, h, ii, ls)
            w = term if w is None else w + term
        return w

    def gelu_tanh(x):
        c = 0.7978845608028654
        inner = x * ((x * x) * (c * 0.044715) + c)
        hx = 0.5 * x
        return hx * jnp.tanh(inner) + hx

    starts = list(range(0, tt, tc))
    pre = [jnp.dot(u_ref[...], xnt_ref[:, c0:c0 + tc], preferred_element_type=F32) for c0 in starts]
    for idx, c0 in enumerate(starts):
        act = gelu_tanh(pre[idx].astype(BF16))
        a = jnp.concatenate([act[ii * nk:(ii + 1) * nk] * select_weights(ii, c0) for ii in range(n_blk)], axis=0)
        acc_scr[:, c0:c0 + tc] += jnp.dot(vt_ref[...], a, preferred_element_type=F32)

    @pl.when(pl.program_id(1) == pl.num_programs(1) - 1)
    def _():
        hout_ref[...] = h_ref[...] + jnp.transpose(acc_scr[...])


def _peer_dense(xnt, u_bf, vt_bf, n_sel, e1, r2, e2, h, tt):
    m = xnt.shape[1]
    te = 2048
    ib = te // PEER_NKEYS
    sel_i = pl.BlockSpec((PEER_HEADS, ib, tt), lambda i, e: (0, e, i))
    sel_all = pl.BlockSpec((PEER_HEADS, PEER_NKEYS, tt), lambda i, e: (0, 0, i))
    return pl.pallas_call(
        _peer_dense_body,
        grid=(m // tt, PEER_EXPERTS // te),
        in_specs=[pl.BlockSpec((te, D_MODEL), lambda i, e: (e, 0)),
                  pl.BlockSpec((D_MODEL, tt), lambda i, e: (0, i)),
                  pl.BlockSpec((D_MODEL, te), lambda i, e: (0, e)),
                  sel_i, sel_i, sel_all, sel_all,
                  pl.BlockSpec((tt, D_MODEL), lambda i, e: (i, 0))],
        out_specs=pl.BlockSpec((tt, D_MODEL), lambda i, e: (i, 0)),
        out_shape=jax.ShapeDtypeStruct((m, D_MODEL), F32),
        scratch_shapes=[pltpu.VMEM((D_MODEL, tt), F32)],
        compiler_params=_cparams("parallel", "arbitrary"),
        name="peer_dense",
    )(u_bf, xnt, vt_bf, n_sel, e1, r2, e2, h)


def _table_body(transpose, x_ref, o_ref):
    x = x_ref[...]
    o_ref[...] = (jnp.transpose(x) if transpose else x).astype(o_ref.dtype)


def _expert_table(tables, l, transpose):
    _, rows, cols = tables.shape
    tr = 512
    return pl.pallas_call(
        functools.partial(_table_body, transpose),
        grid=(rows // tr,),
        in_specs=[pl.BlockSpec((None, tr, cols), lambda i: (l, i, 0))],
        out_specs=pl.BlockSpec((cols, tr), lambda i: (0, i)) if transpose else pl.BlockSpec((tr, cols), lambda i: (i, 0)),
        out_shape=jax.ShapeDtypeStruct((cols, rows) if transpose else (rows, cols), BF16),
        compiler_params=_cparams("parallel"),
        name="expert_table",
    )(tables)


def _final_body(h_ref, nw_ref, o_ref):
    o_ref[...] = _rms(h_ref[...], nw_ref[...])


def _final(h, norm_w):
    m = h.shape[0]
    tm = 512
    row = pl.BlockSpec((tm, D_MODEL), lambda i: (i, 0))
    return pl.pallas_call(
        _final_body,
        grid=(m // tm,),
        in_specs=[row, pl.BlockSpec((1, D_MODEL), lambda i: (0, 0))],
        out_specs=row,
        out_shape=jax.ShapeDtypeStruct((m, D_MODEL), F32),
        compiler_params=_cparams("parallel"),
        name="final_norm",
    )(h, norm_w.reshape(1, D_MODEL))


def _pad_rows(rows, width, n_rows=8):
    flat = [jnp.pad(r.reshape(-1).astype(F32), (0, width - r.size)) for r in rows]
    return jnp.pad(jnp.stack(flat), ((0, n_rows - len(rows)), (0, 0)))


def _chunk_rows(ab):
    m, n = ab.shape
    return jnp.transpose(ab.reshape(m // GDN_CHUNK, GDN_CHUNK, n), (0, 2, 1)).astype(F32)


def _block_diag_groups(w):
    per = LRU_GROUP // LRU_BLOCK_DIM
    n_grp = w.shape[0] // per
    wg = w.reshape(n_grp, per, LRU_BLOCK_DIM, LRU_BLOCK_DIM)
    eye = jnp.eye(per, dtype=w.dtype)
    out = wg[:, :, :, None, :] * eye[None, :, None, :, None]
    return out.reshape(n_grp, LRU_GROUP, LRU_GROUP).astype(BF16)


_W_IN_PIECES = ((0, 6144, 0), (9488, 12560, COL_MERGE), (6160, 9488, COL_RWKV), (6144, 6160, COL_AB))


def _w_in_body(w_ref, o_ref):
    o_ref[:, COL_AB:] = jnp.zeros((o_ref.shape[0], o_ref.shape[1] - COL_AB), o_ref.dtype)
    for a, b, d in _W_IN_PIECES:
        o_ref[:, d:d + b - a] = w_ref[:, a:b].astype(o_ref.dtype)


def _prep_w_in(w_all, l):
    _, rows, cols = w_all.shape
    tr = 256
    return pl.pallas_call(
        _w_in_body,
        grid=(rows // tr,),
        in_specs=[pl.BlockSpec((None, tr, cols), lambda i: (l, i, 0))],
        out_specs=pl.BlockSpec((tr, IN_WIDTH_PAD), lambda i: (i, 0)),
        out_shape=jax.ShapeDtypeStruct((rows, IN_WIDTH_PAD), BF16),
        compiler_params=_cparams("parallel"),
        name="w_in_layout",
    )(w_all)


def _layer(l, h, bsz, seq, p):
    z, ab = _inproj(h, p["norm_mix_w"][l], _prep_w_in(p["w_in_bf16"], l))

    lru_vecs = _pad_rows([p["lru_conv_b"][l], p["lru_b_a"][l], p["lru_b_x"][l], p["lru_lambda"][l]], LRU_WIDTH)
    ya = _lru(z, bsz, seq, p["lru_conv_w"][l], lru_vecs,
              _block_diag_groups(p["lru_w_a"][l]), _block_diag_groups(p["lru_w_x"][l]))

    abt = _chunk_rows(ab[:, :2 * GDN_HEADS])
    gp = _pad_rows([p["gdn_a_log"][l], p["gdn_dt_bias"][l]], 128)
    gpt = jnp.pad(jnp.stack([p["gdn_a_log"][l], p["gdn_dt_bias"][l]], axis=1), ((0, 0), (0, 126)))
    yb = _gdn(z, ab, abt, bsz, seq, p["gdn_conv_w"][l], gp, gpt, p["gdn_norm_w"][l].reshape(1, GDN_DV))

    rw_vecs = _pad_rows([p["rwkv_w0"][l], p["rwkv_a0"][l], p["rwkv_k_k"][l], p["rwkv_k_a"][l],
                         p["rwkv_r_k"][l], p["rwkv_lnx_w"][l], p["rwkv_lnx_b"][l]], RWKV_WIDTH)
    wup = jnp.pad(p["rwkv_w_up"][l], ((0, 192), (0, 0))).astype(BF16)
    aup = jnp.pad(p["rwkv_a_up"][l], ((64, 128), (0, 0))).astype(BF16)
    gup = jnp.pad(p["rwkv_g_up"][l], ((128, 0), (0, 0))).astype(BF16)
    yc = _rwkv(z, bsz, seq, p["rwkv_mu"][l].reshape(1, -1), rw_vecs, wup, aup, gup)

    h, xnt = _merge(h, z, ya, yb, yc, p["merge_b"][l].reshape(1, -1), p["p_lru"][l].astype(BF16),
                    p["p_gdn"][l].astype(BF16), p["p_rwkv"][l].astype(BF16), p["w_out"][l].astype(BF16),
                    p["norm_ffn_w"][l].reshape(1, D_MODEL))

    wqt = jnp.transpose(p["peer_wq"][l]).astype(BF16)
    keys = p["peer_keys"][l].reshape(2 * PEER_HEADS, PEER_NKEYS, PEER_HALF).astype(BF16)
    n_sel, e1, r2, e2 = _peer_topk(xnt, wqt, keys)
    tt = 512 if xnt.shape[1] % 512 == 0 else 256
    return _peer_dense(xnt, _expert_table(p["peer_u"], l, False), _expert_table(p["peer_v"], l, True),
                       n_sel, e1, r2, e2, h, tt)


def kernel(x, norm_mix_w, norm_ffn_w, final_norm_w, w_in, lru_conv_w, lru_conv_b, lru_w_a, lru_b_a,
           lru_w_x, lru_b_x, lru_lambda, gdn_conv_w, gdn_a_log, gdn_dt_bias, gdn_norm_w, rwkv_mu, rwkv_w0,
           rwkv_w_up, rwkv_a0, rwkv_a_up, rwkv_g_up, rwkv_k_k, rwkv_k_a, rwkv_r_k, rwkv_lnx_w, rwkv_lnx_b,
           merge_b, p_lru, p_gdn, p_rwkv, w_out, peer_wq, peer_keys, peer_u, peer_v):
    p = dict(locals())
    p["w_in_bf16"] = w_in.astype(BF16)
    bsz, seq, dim = x.shape
    h = x.reshape(bsz * seq, dim)
    for l in range(DEPTH):
        h = _layer(l, h, bsz, seq, p)
    return _final(h, final_norm_w).reshape(bsz, seq, dim)
```

```python
import functools

import jax
import jax.numpy as jnp
from jax import lax
from jax.experimental import pallas as pl
from jax.experimental.pallas import tpu as pltpu

F32 = jnp.float32
BF16 = jnp.bfloat16

D_MODEL = 1024
DEPTH = 2
RMS_EPS = 1e-6

LRU_WIDTH = 1024
LRU_BLOCK_DIM = 64
LRU_C = 8.0
LRU_GROUP = 256

GDN_HEADS = 8
GDN_DK = 128
GDN_DV = 128
GDN_CHUNK = 64

RWKV_HD = 64
RWKV_WIDTH = 1024
RWKV_GN_EPS = 64e-5
RWKV_CHUNK = 64
RWKV_GROUP = 256
RWKV_LORA = 256

PEER_HEADS = 8
PEER_NKEYS = 128
PEER_EXPERTS = PEER_NKEYS * PEER_NKEYS
PEER_HALF = 128
PEER_TOPK = 16

COL_U = 0
COL_GATE = 1024
COL_QKV = 2048
COL_ZG = 5120
COL_MERGE = 6144
COL_RWKV = 9216
COL_AB = 12544
IN_WIDTH_PAD = 12800
IN_TILE_N = 1280

VMEM_LIMIT = 48 * 1024 * 1024
BIG = 3.0e38


def _cparams(*sem):
    return pltpu.CompilerParams(dimension_semantics=sem, vmem_limit_bytes=VMEM_LIMIT)


def _mm(a, b):
    return jnp.dot(a.astype(BF16), b.astype(BF16), preferred_element_type=F32)


def _mm_nt(a, b):
    return lax.dot_general(a.astype(BF16), b.astype(BF16), (((1,), (1,)), ((), ())),
                           preferred_element_type=F32)


def _mm_tn(a, b):
    return lax.dot_general(a.astype(BF16), b.astype(BF16), (((0,), (0,)), ((), ())),
                           preferred_element_type=F32)


def _softplus(x):
    return jnp.maximum(x, 0.0) + jnp.log1p(jnp.exp(-jnp.abs(x)))


def _rms(x, w):
    return x * lax.rsqrt(jnp.mean(x * x, axis=-1, keepdims=True) + RMS_EPS) * w


def _shift_prev(x, prev8, s):
    r = pltpu.roll(x, s, 0)
    pr = pltpu.roll(prev8, s, 0)
    rows8 = lax.broadcasted_iota(jnp.int32, prev8.shape, 0)
    head = jnp.where(rows8 < s, pr, r[:8])
    return jnp.concatenate([head, r[8:]], axis=0)


def _shift_fill(x, d, fill):
    n, c = x.shape
    if d % 8 == 0:
        return jnp.concatenate([jnp.full((d, c), fill, x.dtype), x[:n - d]], axis=0)
    r = pltpu.roll(x, d, 0)
    rows8 = lax.broadcasted_iota(jnp.int32, (8, c), 0)
    head = jnp.where(rows8 < d, fill, r[:8])
    return jnp.concatenate([head, r[8:]], axis=0)


def _tri(n, strict):
    i = lax.broadcasted_iota(jnp.int32, (n, n), 0)
    j = lax.broadcasted_iota(jnp.int32, (n, n), 1)
    return (j < i) if strict else (j <= i)


def _split_bf16(x, terms):
    parts = []
    rem = x
    for i in range(terms):
        p = rem.astype(BF16)
        parts.append(p)
        if i + 1 < terms:
            rem = rem - p.astype(F32)
    return parts


def _mask_mm(mask_bf, x, terms=3):
    return sum(jnp.dot(mask_bf, p, preferred_element_type=F32) for p in _split_bf16(x, terms))


def _mm_mask(x, mask_bf, terms=3):
    return sum(jnp.dot(p, mask_bf, preferred_element_type=F32) for p in _split_bf16(x, terms))


def _inproj_body(h_ref, nw_ref, w_ref, z_ref, ab_ref, xn_scr):
    @pl.when(pl.program_id(1) == 0)
    def _():
        xn_scr[...] = _rms(h_ref[...], nw_ref[...]).astype(BF16)

    res = jnp.dot(xn_scr[...], w_ref[...], preferred_element_type=F32)
    z_ref[...] = res.astype(z_ref.dtype)

    @pl.when(pl.program_id(1) == pl.num_programs(1) - 1)
    def _():
        c0 = COL_AB - (IN_WIDTH_PAD - res.shape[1])
        ab_ref[...] = res[:, c0:c0 + ab_ref.shape[1]]


def _inproj(h, norm_w, w_pad):
    m = h.shape[0]
    tm, tn = (1024 if m % 1024 == 0 else 512), IN_TILE_N
    assert IN_WIDTH_PAD % tn == 0 and COL_AB >= IN_WIDTH_PAD - tn
    return pl.pallas_call(
        _inproj_body,
        grid=(m // tm, IN_WIDTH_PAD // tn),
        in_specs=[pl.BlockSpec((tm, D_MODEL), lambda i, j: (i, 0)),
                  pl.BlockSpec((1, D_MODEL), lambda i, j: (0, 0)),
                  pl.BlockSpec((D_MODEL, tn), lambda i, j: (0, j))],
        out_specs=[pl.BlockSpec((tm, tn), lambda i, j: (i, j)), pl.BlockSpec((tm, 128), lambda i, j: (i, 0))],
        out_shape=[jax.ShapeDtypeStruct((m, IN_WIDTH_PAD), BF16), jax.ShapeDtypeStruct((m, 128), F32)],
        scratch_shapes=[pltpu.VMEM((tm, D_MODEL), BF16)],
        compiler_params=_cparams("parallel", "arbitrary"),
        name="norm_inproj",
    )(h, norm_w.reshape(1, D_MODEL), w_pad)


def _lru_body(zu_ref, zg_ref, cw_ref, vp_ref, wa_ref, wx_ref, o_ref, prev_scr, hc_scr):
    tt = zu_ref.shape[0]

    @pl.when(pl.program_id(1) == 0)
    def _():
        prev_scr[...] = jnp.zeros_like(prev_scr)
        hc_scr[...] = jnp.zeros_like(hc_scr)

    u = zu_ref[...].astype(F32)
    prev = prev_scr[...]
    cw = cw_ref[...]
    vp = vp_ref[...]
    xc = (cw[3:4] * u + cw[2:3] * _shift_prev(u, prev, 1) + cw[1:2] * _shift_prev(u, prev, 2)
          + cw[0:1] * _shift_prev(u, prev, 3) + vp[0:1])
    prev_scr[...] = u[tt - 8:]

    xcb = xc.astype(BF16)
    n_grp = LRU_WIDTH // LRU_GROUP
    pre_a = jnp.concatenate(
        [jnp.dot(xcb[:, g * LRU_GROUP:(g + 1) * LRU_GROUP], wa_ref[g], preferred_element_type=F32)
         for g in range(n_grp)], axis=1)
    pre_x = jnp.concatenate(
        [jnp.dot(xcb[:, g * LRU_GROUP:(g + 1) * LRU_GROUP], wx_ref[g], preferred_element_type=F32)
         for g in range(n_grp)], axis=1)
    r = jax.nn.sigmoid(pre_a + vp[1:2])
    i = jax.nn.sigmoid(pre_x + vp[2:3])
    log_a = (-LRU_C * r) * _softplus(-vp[3:4])
    a = jnp.exp(log_a)
    b = jnp.sqrt(-jnp.tanh(log_a) * (a * a + 1.0)) * (i * xc)

    d = 1
    while d < tt:
        b = a * _shift_fill(b, d, 0.0) + b
        a = a * _shift_fill(a, d, 1.0)
        d *= 2
    hcar = hc_scr[...]
    hval = b + a * hcar[0:1]
    hc_scr[...] = jnp.broadcast_to(hval[tt - 1:tt], hcar.shape)
    o_ref[...] = (hval * jax.nn.gelu(zg_ref[...].astype(F32))).astype(o_ref.dtype)


def _lru(z, bsz, seq, conv_w, vecs, wa_bd, wx_bd):
    m = z.shape[0]
    tt = 256
    nt = seq // tt
    rowmap = lambda c: (lambda b, t: (b * nt + t, c))
    const2 = lambda b, t: (0, 0)
    return pl.pallas_call(
        _lru_body,
        grid=(bsz, nt),
        in_specs=[
            pl.BlockSpec((tt, LRU_WIDTH), rowmap(COL_U // LRU_WIDTH)),
            pl.BlockSpec((tt, LRU_WIDTH), rowmap(COL_GATE // LRU_WIDTH)),
            pl.BlockSpec((4, LRU_WIDTH), const2),
            pl.BlockSpec((8, LRU_WIDTH), const2),
            pl.BlockSpec(wa_bd.shape, lambda b, t: (0, 0, 0)),
            pl.BlockSpec(wx_bd.shape, lambda b, t: (0, 0, 0)),
        ],
        out_specs=pl.BlockSpec((tt, LRU_WIDTH), rowmap(0)),
        out_shape=jax.ShapeDtypeStruct((m, LRU_WIDTH), BF16),
        scratch_shapes=[pltpu.VMEM((8, LRU_WIDTH), F32), pltpu.VMEM((8, LRU_WIDTH), F32)],
        compiler_params=_cparams("parallel", "arbitrary"),
        name="rglru",
    )(z, z, conv_w, vecs, wa_bd, wx_bd)


def _gdn_body(q_ref, k_ref, v_ref, zg_ref, ab_ref, abt_ref, cw_ref, gp_ref, gpt_ref, nw_ref,
              o_ref, pq_scr, pk_scr, pv_scr, st_scr):
    tt = q_ref.shape[0]
    c = GDN_CHUNK
    width = GDN_HEADS * GDN_DK

    @pl.when(pl.program_id(1) == 0)
    def _():
        for s in (pq_scr, pk_scr, pv_scr, st_scr):
            s[...] = jnp.zeros_like(s)

    cw = cw_ref[...]

    def conv_silu(x_ref, p_scr, w):
        x = x_ref[...].astype(F32)
        prev = p_scr[...]
        y = (w[3:4] * x + w[2:3] * _shift_prev(x, prev, 1) + w[1:2] * _shift_prev(x, prev, 2)
             + w[0:1] * _shift_prev(x, prev, 3))
        p_scr[...] = x[tt - 8:]
        return jax.nn.silu(y)

    q = conv_silu(q_ref, pq_scr, cw[:, 0:width])
    k = conv_silu(k_ref, pk_scr, cw[:, width:2 * width])
    v = conv_silu(v_ref, pv_scr, cw[:, 2 * width:])

    ab = ab_ref[...].astype(F32)
    gp = gp_ref[...]
    gpt = gpt_ref[...]
    g_all = -jnp.exp(gp[0:1]) * _softplus(ab + gp[1:2])
    beta_all = jax.nn.sigmoid(ab)

    incl = _tri(c, False)
    strict = _tri(c, True)
    cum_mask = jnp.concatenate([incl.astype(BF16), jnp.ones((c, c), BF16)], axis=0)
    triu_bf = (lax.broadcasted_iota(jnp.int32, (c, c), 0) <= lax.broadcasted_iota(jnp.int32, (c, c), 1)).astype(BF16)
    nw = nw_ref[...]

    n_chunks = tt // c
    heads = range(GDN_HEADS)
    inst = [(ci, h) for ci in range(n_chunks) for h in heads]
    sls = [slice(ci * c, (ci + 1) * c) for ci in range(n_chunks)]
    hss = [slice(h * GDN_DK, (h + 1) * GDN_DK) for h in heads]
    gcs = [_mask_mm(cum_mask, g_all[sl]) for sl in sls]
    gr_all = [_mm_mask(-jnp.exp(gpt[:, 0:1]) * _softplus(abt_ref[ci][0:GDN_HEADS] + gpt[:, 1:2]), triu_bf)
              for ci in range(n_chunks)]

    qs, ks, kbs, decays, egcs, kdecs, gtots, bcols = {}, {}, {}, {}, {}, {}, {}, {}
    for ci, h in inst:
        sl, hs = sls[ci], hss[h]
        qc, kc = q[sl, hs], k[sl, hs]
        qs[ci, h] = qc * lax.rsqrt(jnp.sum(qc * qc, axis=-1, keepdims=True) + RMS_EPS) * (GDN_DK ** -0.5)
        ks[ci, h] = kc * lax.rsqrt(jnp.sum(kc * kc, axis=-1, keepdims=True) + RMS_EPS)
        gcol = gcs[ci][:c, h:h + 1]
        g_last = gcs[ci][c:, h:h + 1]
        grow = gr_all[ci][h:h + 1, :]
        bcols[ci, h] = beta_all[sl, GDN_HEADS + h:GDN_HEADS + h + 1]
        decays[ci, h] = jnp.exp(jnp.where(incl, gcol - grow, -BIG))
        kbs[ci, h] = ks[ci, h] * bcols[ci, h]
        egcs[ci, h] = jnp.exp(gcol)
        kdecs[ci, h] = ks[ci, h] * jnp.exp(g_last - gcol)
        gtots[ci, h] = jnp.exp(jnp.concatenate([g_last] * (GDN_DK // c), axis=0))

    kk = {i: _mm_nt(kbs[i], ks[i]) for i in inst}
    qk = {i: _mm_nt(qs[i], ks[i]) * decays[i] for i in inst}
    p = {i: -jnp.where(strict, kk[i] * decays[i], 0.0) for i in inst}
    tm1 = dict(p)
    for _ in range(5):
        p = {i: _mm(p[i], p[i]) for i in inst}
        tp = {i: _mm(tm1[i], p[i]) for i in inst}
        tm1 = {i: tm1[i] + tp[i] + p[i] for i in inst}
    rhs = {i: jnp.concatenate([v[sls[i[0]], hss[i[1]]] * bcols[i], kbs[i] * egcs[i]], axis=1) for i in inst}
    tr = {i: _mm(tm1[i], rhs[i]) for i in inst}
    sol = {i: rhs[i] + tr[i] for i in inst}

    state = [st_scr[h] for h in heads]
    for ci in range(n_chunks):
        ws = [_mm(sol[ci, h][:, GDN_DV:], state[h]) for h in heads]
        qst = [_mm(qs[ci, h] * egcs[ci, h], state[h]) for h in heads]
        v_new = [sol[ci, h][:, :GDN_DV] - ws[h] for h in heads]
        qv = [_mm(qk[ci, h], v_new[h]) for h in heads]
        kv = [_mm_tn(kdecs[ci, h], v_new[h]) for h in heads]
        state = [state[h] * gtots[ci, h] + kv[h] for h in heads]
        for h in heads:
            o = qst[h] + qv[h]
            o = o * lax.rsqrt(jnp.mean(o * o, axis=-1, keepdims=True) + RMS_EPS) * nw
            o = o * jax.nn.silu(zg_ref[sls[ci], hss[h]].astype(F32))
            o_ref[sls[ci], hss[h]] = o.astype(o_ref.dtype)
    for h in heads:
        st_scr[h] = state[h]


def _gdn(z, ab, abt, bsz, seq, conv_w, gp, gpt, norm_w):
    m = z.shape[0]
    tt = 256
    nt = seq // tt
    width = GDN_HEADS * GDN_DK
    zblk = lambda c0: pl.BlockSpec((tt, width), lambda b, t: (b * nt + t, c0 // width))
    const2 = lambda b, t: (0, 0)
    return pl.pallas_call(
        _gdn_body,
        grid=(bsz, nt),
        in_specs=[
            zblk(COL_QKV), zblk(COL_QKV + width), zblk(COL_QKV + 2 * width), zblk(COL_ZG),
            pl.BlockSpec((tt, 128), lambda b, t: (b * nt + t, 0)),
            pl.BlockSpec((tt // GDN_CHUNK, 2 * GDN_HEADS, GDN_CHUNK), lambda b, t: (b * nt + t, 0, 0)),
            pl.BlockSpec((4, 3 * width), const2),
            pl.BlockSpec((8, 128), const2),
            pl.BlockSpec((8, 128), const2),
            pl.BlockSpec((1, 128), const2),
        ],
        out_specs=pl.BlockSpec((tt, width), lambda b, t: (b * nt + t, 0)),
        out_shape=jax.ShapeDtypeStruct((m, width), BF16),
        scratch_shapes=[pltpu.VMEM((8, width), F32)] * 3 + [pltpu.VMEM((GDN_HEADS, GDN_DK, GDN_DV), F32)],
        compiler_params=_cparams("parallel", "arbitrary"),
        name="gated_deltanet",
    )(z, z, z, z, ab, abt, conv_w, gp, gpt, norm_w)


def _rwkv_body(r_ref, k_ref, v_ref, lo_ref, mu_ref, vp_ref, wup_ref, aup_ref, gup_ref, o_ref,
               pr_scr, pk_scr, pv_scr, plo_scr, st_scr):
    tt = r_ref.shape[0]
    c = RWKV_CHUNK
    gw = RWKV_GROUP
    hd = RWKV_HD
    width = RWKV_WIDTH

    @pl.when(pl.program_id(1) == 0)
    def _():
        for s in (pr_scr, pk_scr, pv_scr, plo_scr, st_scr):
            s[...] = jnp.zeros_like(s)

    mu = mu_ref[...]

    def tshift(x_ref, p_scr, m):
        x = x_ref[...].astype(F32)
        xs = x + m * (_shift_prev(x, p_scr[...], 1) - x)
        p_scr[...] = x[tt - 8:]
        return xs

    r = tshift(r_ref, pr_scr, mu[:, 0:width])
    k = tshift(k_ref, pk_scr, mu[:, width:2 * width])
    v = tshift(v_ref, pv_scr, mu[:, 2 * width:3 * width])
    lo = tshift(lo_ref, plo_scr, mu[:, 3 * width:])
    vp = vp_ref[...]
    w0, a0, k_k, k_a, r_k, lnx_w, lnx_b = (vp[i:i + 1] for i in range(7))

    lane = lax.broadcasted_iota(jnp.int32, lo.shape, 1)
    lo_act = jnp.where(lane < 64, jnp.tanh(lo), jnp.where(lane < 128, lo, jax.nn.sigmoid(lo)))
    w_pre = _mm(lo_act, wup_ref[...])
    a_pre = _mm(lo_act, aup_ref[...])
    gate = _mm(lo_act, gup_ref[...])
    w_log = -_softplus(-(w0 + w_pre)) - 0.5
    lw = -jnp.exp(w_log)
    a = jax.nn.sigmoid(a0 + a_pre)

    bi = lax.broadcasted_iota(jnp.int32, (gw, gw), 0) // hd
    bj = lax.broadcasted_iota(jnp.int32, (gw, gw), 1) // hd
    bdmask = bi == bj
    ones_bd = bdmask.astype(BF16)

    def head_sum(x):
        return _mm_mask(x, ones_bd, terms=1)

    def bd(x):
        return jnp.where(bdmask, jnp.concatenate([x] * (gw // c), axis=0), 0.0)

    ti = lax.broadcasted_iota(jnp.int32, (c, gw), 0)
    sj = lax.broadcasted_iota(jnp.int32, (c, gw), 1) % c
    strict = sj < ti
    incl = sj <= ti
    tril_bf = _tri(c, False).astype(BF16)

    kk_all = k * k_k
    k2_all = k * (1.0 + (a - 1.0) * k_a)

    n_chunks = tt // c
    groups = range(width // gw)
    inst = [(ci, g) for ci in range(n_chunks) for g in groups]
    sls = [slice(ci * c, (ci + 1) * c) for ci in range(n_chunks)]
    gss = [slice(g * gw, (g + 1) * gw) for g in groups]
    cl_all = [_mask_mm(tril_bf, lw[sl]) for sl in sls]

    def at(x, i):
        return x[sls[i[0]], gss[i[1]]]

    kk_raw = {i: at(kk_all, i) for i in inst}
    kk_ss = {i: head_sum(kk_raw[i] * kk_raw[i]) for i in inst}
    a_h, b_h, k_h, br, w_end = {}, {}, {}, {}, {}
    for i in inst:
        cl = cl_all[i[0]][:, gss[i[1]]]
        kk = kk_raw[i] * lax.rsqrt(kk_ss[i] + RMS_EPS)
        wcum = jnp.exp(cl)
        inv_w = jnp.exp(-cl)
        a_h[i] = kk * at(a, i) * inv_w
        b_h[i] = kk * jnp.exp(cl - at(lw, i))
        k_h[i] = at(k2_all, i) * inv_w
        br[i] = jnp.concatenate([b_h[i], at(r, i) * wcum], axis=0)
        w_end[i] = wcum[c - 1:c]
    v_bd = {i: bd(at(v, i)) for i in inst}
    xa = {i: _mm_nt(br[i], bd(a_h[i])) for i in inst}
    xk = {i: _mm_nt(br[i], bd(k_h[i])) for i in inst}
    l_k = {i: jnp.where(strict, xk[i][:c], 0.0) for i in inst}
    ra = {i: jnp.where(incl, xa[i][c:], 0.0) for i in inst}
    rk = {i: jnp.where(incl, xk[i][c:], 0.0) for i in inst}
    p = {i: -jnp.where(strict, xa[i][:c], 0.0) for i in inst}
    tm1 = dict(p)
    for _ in range(5):
        p = {i: _mm(p[i], bd(p[i])) for i in inst}
        tp = {i: _mm(tm1[i], bd(p[i])) for i in inst}
        tm1 = {i: tm1[i] + tp[i] + p[i] for i in inst}
    lkv = {i: _mm(l_k[i], v_bd[i]) for i in inst}
    rkv = {i: _mm(rk[i], v_bd[i]) for i in inst}
    so = {i: head_sum(at(r, i) * at(k2_all, i) * r_k[:, gss[i[1]]]) for i in inst}

    state_t = [st_scr[g] for g in groups]
    for ci in range(n_chunks):
        ids = [(ci, g) for g in groups]
        brh = [_mm_nt(br[i], state_t[i[1]]) for i in ids]
        rhs = [brh[g][:c] + lkv[ci, g] for g in groups]
        tu = [_mm(tm1[ci, g], bd(rhs[g])) for g in groups]
        u = [rhs[g] + tu[g] for g in groups]
        rau = [_mm(ra[ci, g], bd(u[g])) for g in groups]
        upd = [_mm_tn(jnp.concatenate([u[g], at(v, (ci, g))], axis=0),
                      jnp.concatenate([-(a_h[ci, g] * w_end[ci, g]), k_h[ci, g] * w_end[ci, g]], axis=0))
               for g in groups]
        state_t = [w_end[ci, g] * state_t[g] + jnp.where(bdmask, upd[g], 0.0) for g in groups]
        o = [brh[g][c:] - rau[g] + rkv[ci, g] for g in groups]
        osum = [head_sum(o[g]) for g in groups]
        cen = [o[g] - osum[g] * (1.0 / hd) for g in groups]
        var = [head_sum(cen[g] * cen[g]) * (1.0 / hd) for g in groups]
        for g in groups:
            gs = gss[g]
            on = cen[g] * lax.rsqrt(var[g] + RWKV_GN_EPS) * lnx_w[:, gs] + lnx_b[:, gs]
            bonus = so[ci, g] * at(v, (ci, g))
            o_ref[sls[ci], gs] = ((on + bonus) * gate[sls[ci], gs]).astype(o_ref.dtype)
    for g in groups:
        st_scr[g] = state_t[g]


def _rwkv(z, bsz, seq, mu, vecs, wup_pad, aup_pad, gup_pad):
    m = z.shape[0]
    tt = 256
    nt = seq // tt
    gw = RWKV_GROUP
    width = RWKV_WIDTH
    zblk = lambda c0: pl.BlockSpec((tt, width), lambda b, t: (b * nt + t, c0 // width))
    lora_col = (COL_RWKV + 3 * width) // RWKV_LORA
    const2 = lambda b, t: (0, 0)
    return pl.pallas_call(
        _rwkv_body,
        grid=(bsz, nt),
        in_specs=[
            zblk(COL_RWKV), zblk(COL_RWKV + width), zblk(COL_RWKV + 2 * width),
            pl.BlockSpec((tt, RWKV_LORA), lambda b, t: (b * nt + t, lora_col)),
            pl.BlockSpec(mu.shape, const2),
            pl.BlockSpec((8, width), const2),
            pl.BlockSpec((RWKV_LORA, width), const2),
            pl.BlockSpec((RWKV_LORA, width), const2),
            pl.BlockSpec((RWKV_LORA, width), const2),
        ],
        out_specs=pl.BlockSpec((tt, width), lambda b, t: (b * nt + t, 0)),
        out_shape=jax.ShapeDtypeStruct((m, width), BF16),
        scratch_shapes=[pltpu.VMEM((8, width), F32)] * 3 + [pltpu.VMEM((8, RWKV_LORA), F32),
                                                            pltpu.VMEM((width // gw, gw, gw), F32)],
        compiler_params=_cparams("parallel", "arbitrary"),
        name="rwkv7",
    )(z, z, z, z, mu, vecs, wup_pad, aup_pad, gup_pad)


def _merge_body(h_ref, ya_ref, yb_ref, yc_ref, za_ref, zb_ref, zc_ref, mb_ref, pa_ref, pb_ref,
                pc_ref, wo_ref, nw_ref, hout_ref, xnt_ref):
    mb = mb_ref[...]
    ga = jax.nn.sigmoid(za_ref[...].astype(F32) + mb[:, 0:D_MODEL])
    gb = jax.nn.sigmoid(zb_ref[...].astype(F32) + mb[:, D_MODEL:2 * D_MODEL])
    gc = jax.nn.sigmoid(zc_ref[...].astype(F32) + mb[:, 2 * D_MODEL:])
    merged = (ga * jnp.dot(ya_ref[...], pa_ref[...], preferred_element_type=F32)
              + gb * jnp.dot(yb_ref[...], pb_ref[...], preferred_element_type=F32)
              + gc * jnp.dot(yc_ref[...], pc_ref[...], preferred_element_type=F32))
    h = h_ref[...] + _mm(merged, wo_ref[...])
    hout_ref[...] = h
    xnt_ref[...] = jnp.transpose(_rms(h, nw_ref[...])).astype(BF16)


def _merge(h, z, ya, yb, yc, merge_b, pa, pb, pc, wo, norm_w):
    m = h.shape[0]
    tm = 512 if m % 512 == 0 else 256
    row = lambda c: pl.BlockSpec((tm, D_MODEL), lambda i: (i, c))
    wspec = pl.BlockSpec((D_MODEL, D_MODEL), lambda i: (0, 0))
    mc = COL_MERGE // D_MODEL
    return pl.pallas_call(
        _merge_body,
        grid=(m // tm,),
        in_specs=[row(0), row(0), row(0), row(0), row(mc), row(mc + 1), row(mc + 2),
                  pl.BlockSpec((1, 3 * D_MODEL), lambda i: (0, 0)),
                  wspec, wspec, wspec, wspec,
                  pl.BlockSpec((1, D_MODEL), lambda i: (0, 0))],
        out_specs=[row(0), pl.BlockSpec((D_MODEL, tm), lambda i: (0, i))],
        out_shape=[jax.ShapeDtypeStruct((m, D_MODEL), F32), jax.ShapeDtypeStruct((D_MODEL, m), BF16)],
        compiler_params=_cparams("parallel"),
        name="merge_outproj",
    )(h, ya, yb, yc, z, z, z, merge_b, pa, pb, pc, wo, norm_w)


_CAND_SLABS = (
    (16, ((0, 0, 16),)),
    (16, ((1, 0, 8), (2, 8, 5), (4, 13, 3))),
    (16, ((3, 0, 4), (5, 4, 2), (6, 6, 2), (7, 8, 2), (8, 10, 1), (9, 11, 1), (10, 12, 1), (11, 13, 1),
          (12, 14, 1), (13, 15, 1))),
    (8, ((14, 0, 1), (15, 1, 1))),
)


def _odd_even_merge_sort_pairs(n):
    pairs = []
    p = 1
    while p < n:
        k = p
        while k >= 1:
            for j in range(k % p, n - k, 2 * k):
                for i in range(min(k, n - j - k)):
                    if (i + j) // (2 * p) == (i + j + k) // (2 * p):
                        pairs.append((i + j, i + j + k))
            k //= 2
        p *= 2
    return tuple(pairs)


_SORT_PAIRS = _odd_even_merge_sort_pairs(PEER_TOPK)


def _candidate_sums(a1, a2):
    slabs = []
    for nrows, pieces in _CAND_SLABS:
        base = a2[:nrows]
        rows = lax.broadcasted_iota(jnp.int32, base.shape, 0)
        out = None
        end = 0
        for p, off, cnt in pieces:
            val = a1[p:p + 1] + (base if off == 0 else pltpu.roll(base, off, 0))
            out = val if out is None else jnp.where(rows >= off, val, out)
            end = off + cnt
        if end < nrows:
            out = jnp.where(rows >= end, -BIG, out)
        slabs.append(out)
    return jnp.concatenate(slabs, axis=0)


def _count_leading(pred, thr):
    assert len(thr) == 16
    b8 = pred(thr[7])
    b4 = pred(jnp.where(b8, thr[11], thr[3]))
    b2 = pred(jnp.where(b8, jnp.where(b4, thr[13], thr[9]), jnp.where(b4, thr[5], thr[1])))
    lo = jnp.where(b4, jnp.where(b2, thr[6], thr[4]), jnp.where(b2, thr[2], thr[0]))
    hi = jnp.where(b4, jnp.where(b2, thr[14], thr[12]), jnp.where(b2, thr[10], thr[8]))
    b1 = pred(jnp.where(b8, hi, lo))
    cnt = (jnp.where(b8, 8.0, 0.0) + jnp.where(b4, 4.0, 0.0)) + (jnp.where(b2, 2.0, 0.0) + jnp.where(b1, 1.0, 0.0))
    return jnp.where(pred(thr[15]), 16.0, cnt)


def _peer_topk_body(xnt_ref, wqt_ref, keys_ref, n_ref, e1_ref, r2_ref, e2_ref, q_scr):
    tt = xnt_ref.shape[1]
    nk = PEER_NKEYS
    topk = PEER_TOPK
    q_scr[...] = jnp.dot(wqt_ref[...], xnt_ref[...], preferred_element_type=F32)
    lanes = 128
    sub = 8
    assert nk == topk * sub
    row8 = lax.broadcasted_iota(jnp.int32, (sub, lanes), 0)

    def top_sorted(s):
        v = [s[k * sub:(k + 1) * sub] for k in range(topk)]
        for i, j in _SORT_PAIRS:
            v[i], v[j] = jnp.maximum(v[i], v[j]), jnp.minimum(v[i], v[j])
        for shift in (4, 2, 1):
            other = [pltpu.roll(x, shift, 0) for x in v]
            v = [jnp.maximum(v[k], other[topk - 1 - k]) for k in range(topk)]
            d = topk // 2
            while d >= 1:
                for i in range(topk):
                    if i & d == 0:
                        v[i], v[i + d] = jnp.maximum(v[i], v[i + d]), jnp.minimum(v[i], v[i + d])
                d //= 2
        return v

    def compact(v):
        tiles = []
        for t0 in range(0, topk, sub):
            out = v[t0]
            for k in range(1, sub):
                out = jnp.where(row8 == k, v[t0 + k], out)
            tiles.append(out)
        return jnp.concatenate(tiles, axis=0)

    def head(h, carry):
        o1 = pl.multiple_of(h * (2 * nk), 2 * nk)
        s1 = _mm(keys_ref[2 * h], q_scr[pl.ds(o1, nk), :])
        s2 = _mm(keys_ref[2 * h + 1], q_scr[pl.ds(o1 + nk, nk), :])
        chunks = [slice(l0, l0 + lanes) for l0 in range(0, tt, lanes)]
        a1 = [top_sorted(s1[:, ls]) for ls in chunks]
        a2 = [top_sorted(s2[:, ls]) for ls in chunks]
        a1c = jnp.concatenate([compact(v) for v in a1], axis=1)
        a2c = jnp.concatenate([compact(v) for v in a2], axis=1)
        cand = _candidate_sums(a1c, a2c)
        cmax = a1c[0:1] + a2c[0:1]

        def cbody(rnd, carry):
            cnd, zsum, c_in, c_out = carry
            mx = jnp.max(cnd, axis=0, keepdims=True)
            zsum = zsum + jnp.where(rnd < topk, jnp.exp(mx - cmax), 0.0)
            c_in = jnp.where(rnd == topk - 1, mx, c_in)
            c_out = jnp.where(rnd == topk, mx, c_out)
            return jnp.where(cnd == mx, -BIG, cnd), zsum, c_in, c_out

        zero = jnp.zeros((1, tt), F32)
        _, zsum, c_in, c_out = lax.fori_loop(0, topk + 1, cbody, (cand, zero, zero, zero))
        tau = 0.5 * (c_in + c_out)
        inv_z = 1.0 / zsum
        pack = 16

        def rep(x):
            return jnp.concatenate([x] * (pack // sub), axis=0)

        for ci, ls in enumerate(chunks):
            tau8 = jnp.broadcast_to(tau[:, ls], (sub, lanes))
            need = [rep(tau8 - a2[ci][qq]) for qq in range(topk)]
            val2 = [rep(a2[ci][qq]) for qq in range(topk)]
            last1 = rep(a1[ci][topk - 1])
            top1 = rep(a1[ci][0])
            top2 = val2[0]
            for k in range(nk // pack):
                rows = slice(k * pack, (k + 1) * pack)
                x1 = s1[rows, ls]
                x2 = s2[rows, ls]
                n_sel = _count_leading(lambda t: x1 >= t, need)
                r2 = _count_leading(lambda t: x2 < t, val2)
                n_ref[h, rows, ls] = jnp.where(x1 >= last1, n_sel, 0.0)
                e1_ref[h, rows, ls] = jnp.exp(x1 - top1)
                r2_ref[h, rows, ls] = r2.astype(BF16)
                e2_ref[h, rows, ls] = (jnp.exp(x2 - top2) * inv_z[:, ls]).astype(BF16)
        return carry

    lax.fori_loop(0, PEER_HEADS, head, 0)


def _peer_topk(xnt, wqt, keys):
    m = xnt.shape[1]
    tt = 512 if m % 512 == 0 else 256
    shape = (PEER_HEADS, PEER_NKEYS, m)
    ospec = pl.BlockSpec((PEER_HEADS, PEER_NKEYS, tt), lambda i: (0, 0, i))
    return pl.pallas_call(
        _peer_topk_body,
        grid=(m // tt,),
        in_specs=[pl.BlockSpec((D_MODEL, tt), lambda i: (0, i)),
                  pl.BlockSpec(wqt.shape, lambda i: (0, 0)),
                  pl.BlockSpec(keys.shape, lambda i: (0, 0, 0))],
        out_specs=[ospec] * 4,
        out_shape=[jax.ShapeDtypeStruct(shape, F32), jax.ShapeDtypeStruct(shape, F32),
                   jax.ShapeDtypeStruct(shape, BF16), jax.ShapeDtypeStruct(shape, BF16)],
        scratch_shapes=[pltpu.VMEM((wqt.shape[0], tt), F32)],
        compiler_params=_cparams("parallel"),
        name="peer_topk",
    )(xnt, wqt, keys)


PEER_TOKEN_CHUNK = 256


def _peer_dense_body(u_ref, xnt_ref, vt_ref, n_ref, e1_ref, r2_ref, e2_ref, h_ref, hout_ref, acc_scr):
    nk = PEER_NKEYS
    tt = xnt_ref.shape[1]
    tc = PEER_TOKEN_CHUNK
    n_blk = u_ref.shape[0] // nk
    pack = 16

    @pl.when(pl.program_id(1) == 0)
    def _():
        acc_scr[...] = jnp.zeros_like(acc_scr)

    def rows_bf16(ref, h, ii, ls):
        row = jnp.broadcast_to(ref[h, ii:ii + 1, ls], (pack, tc)).astype(BF16)
        return jnp.concatenate([row] * (nk // pack), axis=0)

    def select_weights(ii, c0):
        ls = slice(c0, c0 + tc)
        w = None
        for h in range(PEER_HEADS):
            sel = jnp.where(r2_ref[h, :, ls] < rows_bf16(n_ref, h, ii, ls), e2_ref[h, :, ls],
                            jnp.zeros((), BF16))
            term = sel * rows_bf16(e1_ref, h, ii, ls)
            w = term if w is None else w + term
        return w

    def gelu_tanh(x):
        c = 0.7978845608028654
        inner = x * ((x * x) * (c * 0.044715) + c)
        hx = 0.5 * x
        return hx * jnp.tanh(inner) + hx

    starts = list(range(0, tt, tc))
    pre = [jnp.dot(u_ref[...], xnt_ref[:, c0:c0 + tc], preferred_element_type=F32) for c0 in starts]
    for idx, c0 in enumerate(starts):
        act = gelu_tanh(pre[idx].astype(BF16))
        a = jnp.concatenate([act[ii * nk:(ii + 1) * nk] * select_weights(ii, c0) for ii in range(n_blk)], axis=0)
        acc_scr[:, c0:c0 + tc] += jnp.dot(vt_ref[...], a, preferred_element_type=F32)

    @pl.when(pl.program_id(1) == pl.num_programs(1) - 1)
    def _():
        hout_ref[...] = h_ref[...] + jnp.transpose(acc_scr[...])


def _peer_dense(xnt, u_bf, vt_bf, n_sel, e1, r2, e2, h, tt):
    m = xnt.shape[1]
    te = 2048
    ib = te // PEER_NKEYS
    sel_i = pl.BlockSpec((PEER_HEADS, ib, tt), lambda i, e: (0, e, i))
    sel_all = pl.BlockSpec((PEER_HEADS, PEER_NKEYS, tt), lambda i, e: (0, 0, i))
    return pl.pallas_call(
        _peer_dense_body,
        grid=(m // tt, PEER_EXPERTS // te),
        in_specs=[pl.BlockSpec((te, D_MODEL), lambda i, e: (e, 0)),
                  pl.BlockSpec((D_MODEL, tt), lambda i, e: (0, i)),
                  pl.BlockSpec((D_MODEL, te), lambda i, e: (0, e)),
                  sel_i, sel_i, sel_all, sel_all,
                  pl.BlockSpec((tt, D_MODEL), lambda i, e: (i, 0))],
        out_specs=pl.BlockSpec((tt, D_MODEL), lambda i, e: (i, 0)),
        out_shape=jax.ShapeDtypeStruct((m, D_MODEL), F32),
        scratch_shapes=[pltpu.VMEM((D_MODEL, tt), F32)],
        compiler_params=_cparams("parallel", "arbitrary"),
        name="peer_dense",
    )(u_bf, xnt, vt_bf, n_sel, e1, r2, e2, h)


def _table_body(transpose, x_ref, o_ref):
    x = x_ref[...]
    o_ref[...] = (jnp.transpose(x) if transpose else x).astype(o_ref.dtype)


def _expert_table(tables, l, transpose):
    _, rows, cols = tables.shape
    tr = 512
    return pl.pallas_call(
        functools.partial(_table_body, transpose),
        grid=(rows // tr,),
        in_specs=[pl.BlockSpec((None, tr, cols), lambda i: (l, i, 0))],
        out_specs=pl.BlockSpec((cols, tr), lambda i: (0, i)) if transpose else pl.BlockSpec((tr, cols), lambda i: (i, 0)),
        out_shape=jax.ShapeDtypeStruct((cols, rows) if transpose else (rows, cols), BF16),
        compiler_params=_cparams("parallel"),
        name="expert_table",
    )(tables)


def _final_body(h_ref, nw_ref, o_ref):
    o_ref[...] = _rms(h_ref[...], nw_ref[...])


def _final(h, norm_w):
    m = h.shape[0]
    tm = 512
    row = pl.BlockSpec((tm, D_MODEL), lambda i: (i, 0))
    return pl.pallas_call(
        _final_body,
        grid=(m // tm,),
        in_specs=[row, pl.BlockSpec((1, D_MODEL), lambda i: (0, 0))],
        out_specs=row,
        out_shape=jax.ShapeDtypeStruct((m, D_MODEL), F32),
        compiler_params=_cparams("parallel"),
        name="final_norm",
    )(h, norm_w.reshape(1, D_MODEL))


def _pad_rows(rows, width, n_rows=8):
    flat = [jnp.pad(r.reshape(-1).astype(F32), (0, width - r.size)) for r in rows]
    return jnp.pad(jnp.stack(flat), ((0, n_rows - len(rows)), (0, 0)))


def _chunk_rows(ab):
    m, n = ab.shape
    return jnp.transpose(ab.reshape(m // GDN_CHUNK, GDN_CHUNK, n), (0, 2, 1)).astype(F32)


def _block_diag_groups(w):
    per = LRU_GROUP // LRU_BLOCK_DIM
    n_grp = w.shape[0] // per
    wg = w.reshape(n_grp, per, LRU_BLOCK_DIM, LRU_BLOCK_DIM)
    eye = jnp.eye(per, dtype=w.dtype)
    out = wg[:, :, :, None, :] * eye[None, :, None, :, None]
    return out.reshape(n_grp, LRU_GROUP, LRU_GROUP).astype(BF16)


_W_IN_PIECES = ((0, 6144, 0), (9488, 12560, COL_MERGE), (6160, 9488, COL_RWKV), (6144, 6160, COL_AB))


def _w_in_body(w_ref, o_ref):
    o_ref[:, COL_AB:] = jnp.zeros((o_ref.shape[0], o_ref.shape[1] - COL_AB), o_ref.dtype)
    for a, b, d in _W_IN_PIECES:
        o_ref[:, d:d + b - a] = w_ref[:, a:b].astype(o_ref.dtype)


def _prep_w_in(w_all, l):
    _, rows, cols = w_all.shape
    tr = 256
    return pl.pallas_call(
        _w_in_body,
        grid=(rows // tr,),
        in_specs=[pl.BlockSpec((None, tr, cols), lambda i: (l, i, 0))],
        out_specs=pl.BlockSpec((tr, IN_WIDTH_PAD), lambda i: (i, 0)),
        out_shape=jax.ShapeDtypeStruct((rows, IN_WIDTH_PAD), BF16),
        compiler_params=_cparams("parallel"),
        name="w_in_layout",
    )(w_all)


def _layer(l, h, bsz, seq, p):
    z, ab = _inproj(h, p["norm_mix_w"][l], _prep_w_in(p["w_in_bf16"], l))

    lru_vecs = _pad_rows([p["lru_conv_b"][l], p["lru_b_a"][l], p["lru_b_x"][l], p["lru_lambda"][l]], LRU_WIDTH)
    ya = _lru(z, bsz, seq, p["lru_conv_w"][l], lru_vecs,
              _block_diag_groups(p["lru_w_a"][l]), _block_diag_groups(p["lru_w_x"][l]))

    abt = _chunk_rows(ab[:, :2 * GDN_HEADS])
    gp = _pad_rows([p["gdn_a_log"][l], p["gdn_dt_bias"][l]], 128)
    gpt = jnp.pad(jnp.stack([p["gdn_a_log"][l], p["gdn_dt_bias"][l]], axis=1), ((0, 0), (0, 126)))
    yb = _gdn(z, ab, abt, bsz, seq, p["gdn_conv_w"][l], gp, gpt, p["gdn_norm_w"][l].reshape(1, GDN_DV))

    rw_vecs = _pad_rows([p["rwkv_w0"][l], p["rwkv_a0"][l], p["rwkv_k_k"][l], p["rwkv_k_a"][l],
                         p["rwkv_r_k"][l], p["rwkv_lnx_w"][l], p["rwkv_lnx_b"][l]], RWKV_WIDTH)
    wup = jnp.pad(p["rwkv_w_up"][l], ((0, 192), (0, 0))).astype(BF16)
    aup = jnp.pad(p["rwkv_a_up"][l], ((64, 128), (0, 0))).astype(BF16)
    gup = jnp.pad(p["rwkv_g_up"][l], ((128, 0), (0, 0))).astype(BF16)
    yc = _rwkv(z, bsz, seq, p["rwkv_mu"][l].reshape(1, -1), rw_vecs, wup, aup, gup)

    h, xnt = _merge(h, z, ya, yb, yc, p["merge_b"][l].reshape(1, -1), p["p_lru"][l].astype(BF16),
                    p["p_gdn"][l].astype(BF16), p["p_rwkv"][l].astype(BF16), p["w_out"][l].astype(BF16),
                    p["norm_ffn_w"][l].reshape(1, D_MODEL))

    wqt = jnp.transpose(p["peer_wq"][l]).astype(BF16)
    keys = p["peer_keys"][l].reshape(2 * PEER_HEADS, PEER_NKEYS, PEER_HALF).astype(BF16)
    n_sel, e1, r2, e2 = _peer_topk(xnt, wqt, keys)
    tt = 512 if xnt.shape[1] % 512 == 0 else 256
    return _peer_dense(xnt, _expert_table(p["peer_u"], l, False), _expert_table(p["peer_v"], l, True),
                       n_sel, e1, r2, e2, h, tt)


def kernel(x, norm_mix_w, norm_ffn_w, final_norm_w, w_in, lru_conv_w, lru_conv_b, lru_w_a, lru_b_a,
           lru_w_x, lru_b_x, lru_lambda, gdn_conv_w, gdn_a_log, gdn_dt_bias, gdn_norm_w, rwkv_mu, rwkv_w0,
           rwkv_w_up, rwkv_a0, rwkv_a_up, rwkv_g_up, rwkv_k_k, rwkv_k_a, rwkv_r_k, rwkv_lnx_w, rwkv_lnx_b,
           merge_b, p_lru, p_gdn, p_rwkv, w_out, peer_wq, peer_keys, peer_u, peer_v):
    p = dict(locals())
    p["w_in_bf16"] = w_in.astype(BF16)
    bsz, seq, dim = x.shape
    h = x.reshape(bsz * seq, dim)
    for l in range(DEPTH):
        h = _layer(l, h, bsz, seq, p)
    return _final(h, final_norm_w).reshape(bsz, seq, dim)
```

```python
import functools

import jax
import jax.numpy as jnp
from jax import lax
from jax.experimental import pallas as pl
from jax.experimental.pallas import tpu as pltpu

F32 = jnp.float32
BF16 = jnp.bfloat16

D_MODEL = 1024
DEPTH = 2
RMS_EPS = 1e-6

LRU_WIDTH = 1024
LRU_BLOCK_DIM = 64
LRU_C = 8.0
LRU_GROUP = 256

GDN_HEADS = 8
GDN_DK = 128
GDN_DV = 128
GDN_CHUNK = 64

RWKV_HD = 64
RWKV_WIDTH = 1024
RWKV_GN_EPS = 64e-5
RWKV_CHUNK = 64
RWKV_GROUP = 256
RWKV_LORA = 256

PEER_HEADS = 8
PEER_NKEYS = 128
PEER_EXPERTS = PEER_NKEYS * PEER_NKEYS
PEER_HALF = 128
PEER_TOPK = 16

COL_U = 0
COL_GATE = 1024
COL_QKV = 2048
COL_ZG = 5120
COL_MERGE = 6144
COL_RWKV = 9216
COL_AB = 12544
IN_WIDTH_PAD = 12800
IN_TILE_N = 1280

VMEM_LIMIT = 48 * 1024 * 1024
BIG = 3.0e38


def _cparams(*sem):
    return pltpu.CompilerParams(dimension_semantics=sem, vmem_limit_bytes=VMEM_LIMIT)


def _mm(a, b):
    return jnp.dot(a.astype(BF16), b.astype(BF16), preferred_element_type=F32)


def _mm_nt(a, b):
    return lax.dot_general(a.astype(BF16), b.astype(BF16), (((1,), (1,)), ((), ())),
                           preferred_element_type=F32)


def _mm_tn(a, b):
    return lax.dot_general(a.astype(BF16), b.astype(BF16), (((0,), (0,)), ((), ())),
                           preferred_element_type=F32)


def _softplus(x):
    return jnp.maximum(x, 0.0) + jnp.log1p(jnp.exp(-jnp.abs(x)))


def _rms(x, w):
    return x * lax.rsqrt(jnp.mean(x * x, axis=-1, keepdims=True) + RMS_EPS) * w


def _shift_prev(x, prev8, s):
    r = pltpu.roll(x, s, 0)
    pr = pltpu.roll(prev8, s, 0)
    rows8 = lax.broadcasted_iota(jnp.int32, prev8.shape, 0)
    head = jnp.where(rows8 < s, pr, r[:8])
    return jnp.concatenate([head, r[8:]], axis=0)


def _shift_fill(x, d, fill):
    n, c = x.shape
    if d % 8 == 0:
        return jnp.concatenate([jnp.full((d, c), fill, x.dtype), x[:n - d]], axis=0)
    r = pltpu.roll(x, d, 0)
    rows8 = lax.broadcasted_iota(jnp.int32, (8, c), 0)
    head = jnp.where(rows8 < d, fill, r[:8])
    return jnp.concatenate([head, r[8:]], axis=0)


def _tri(n, strict):
    i = lax.broadcasted_iota(jnp.int32, (n, n), 0)
    j = lax.broadcasted_iota(jnp.int32, (n, n), 1)
    return (j < i) if strict else (j <= i)


def _split_bf16(x, terms):
    parts = []
    rem = x
    for i in range(terms):
        p = rem.astype(BF16)
        parts.append(p)
        if i + 1 < terms:
            rem = rem - p.astype(F32)
    return parts


def _mask_mm(mask_bf, x, terms=3):
    return sum(jnp.dot(mask_bf, p, preferred_element_type=F32) for p in _split_bf16(x, terms))


def _mm_mask(x, mask_bf, terms=3):
    return sum(jnp.dot(p, mask_bf, preferred_element_type=F32) for p in _split_bf16(x, terms))


def _inproj_body(h_ref, nw_ref, w_ref, z_ref, ab_ref, xn_scr):
    @pl.when(pl.program_id(1) == 0)
    def _():
        xn_scr[...] = _rms(h_ref[...], nw_ref[...]).astype(BF16)

    res = jnp.dot(xn_scr[...], w_ref[...], preferred_element_type=F32)
    z_ref[...] = res.astype(z_ref.dtype)

    @pl.when(pl.program_id(1) == pl.num_programs(1) - 1)
    def _():
        c0 = COL_AB - (IN_WIDTH_PAD - res.shape[1])
        ab_ref[...] = res[:, c0:c0 + ab_ref.shape[1]]


def _inproj(h, norm_w, w_pad):
    m = h.shape[0]
    tm, tn = (1024 if m % 1024 == 0 else 512), IN_TILE_N
    assert IN_WIDTH_PAD % tn == 0 and COL_AB >= IN_WIDTH_PAD - tn
    return pl.pallas_call(
        _inproj_body,
        grid=(m // tm, IN_WIDTH_PAD // tn),
        in_specs=[pl.BlockSpec((tm, D_MODEL), lambda i, j: (i, 0)),
                  pl.BlockSpec((1, D_MODEL), lambda i, j: (0, 0)),
                  pl.BlockSpec((D_MODEL, tn), lambda i, j: (0, j))],
        out_specs=[pl.BlockSpec((tm, tn), lambda i, j: (i, j)), pl.BlockSpec((tm, 128), lambda i, j: (i, 0))],
        out_shape=[jax.ShapeDtypeStruct((m, IN_WIDTH_PAD), BF16), jax.ShapeDtypeStruct((m, 128), F32)],
        scratch_shapes=[pltpu.VMEM((tm, D_MODEL), BF16)],
        compiler_params=_cparams("parallel", "arbitrary"),
        name="norm_inproj",
    )(h, norm_w.reshape(1, D_MODEL), w_pad)


def _lru_body(zu_ref, zg_ref, cw_ref, vp_ref, wa_ref, wx_ref, o_ref, prev_scr, hc_scr):
    tt = zu_ref.shape[0]

    @pl.when(pl.program_id(1) == 0)
    def _():
        prev_scr[...] = jnp.zeros_like(prev_scr)
        hc_scr[...] = jnp.zeros_like(hc_scr)

    u = zu_ref[...].astype(F32)
    prev = prev_scr[...]
    cw = cw_ref[...]
    vp = vp_ref[...]
    xc = (cw[3:4] * u + cw[2:3] * _shift_prev(u, prev, 1) + cw[1:2] * _shift_prev(u, prev, 2)
          + cw[0:1] * _shift_prev(u, prev, 3) + vp[0:1])
    prev_scr[...] = u[tt - 8:]

    xcb = xc.astype(BF16)
    n_grp = LRU_WIDTH // LRU_GROUP
    pre_a = jnp.concatenate(
        [jnp.dot(xcb[:, g * LRU_GROUP:(g + 1) * LRU_GROUP], wa_ref[g], preferred_element_type=F32)
         for g in range(n_grp)], axis=1)
    pre_x = jnp.concatenate(
        [jnp.dot(xcb[:, g * LRU_GROUP:(g + 1) * LRU_GROUP], wx_ref[g], preferred_element_type=F32)
         for g in range(n_grp)], axis=1)
    r = jax.nn.sigmoid(pre_a + vp[1:2])
    i = jax.nn.sigmoid(pre_x + vp[2:3])
    log_a = (-LRU_C * r) * _softplus(-vp[3:4])
    a = jnp.exp(log_a)
    b = jnp.sqrt(-jnp.tanh(log_a) * (a * a + 1.0)) * (i * xc)

    d = 1
    while d < tt:
        b = a * _shift_fill(b, d, 0.0) + b
        a = a * _shift_fill(a, d, 1.0)
        d *= 2
    hcar = hc_scr[...]
    hval = b + a * hcar[0:1]
    hc_scr[...] = jnp.broadcast_to(hval[tt - 1:tt], hcar.shape)
    o_ref[...] = (hval * jax.nn.gelu(zg_ref[...].astype(F32))).astype(o_ref.dtype)


def _lru(z, bsz, seq, conv_w, vecs, wa_bd, wx_bd):
    m = z.shape[0]
    tt = 256
    nt = seq // tt
    rowmap = lambda c: (lambda b, t: (b * nt + t, c))
    const2 = lambda b, t: (0, 0)
    return pl.pallas_call(
        _lru_body,
        grid=(bsz, nt),
        in_specs=[
            pl.BlockSpec((tt, LRU_WIDTH), rowmap(COL_U // LRU_WIDTH)),
            pl.BlockSpec((tt, LRU_WIDTH), rowmap(COL_GATE // LRU_WIDTH)),
            pl.BlockSpec((4, LRU_WIDTH), const2),
            pl.BlockSpec((8, LRU_WIDTH), const2),
            pl.BlockSpec(wa_bd.shape, lambda b, t: (0, 0, 0)),
            pl.BlockSpec(wx_bd.shape, lambda b, t: (0, 0, 0)),
        ],
        out_specs=pl.BlockSpec((tt, LRU_WIDTH), rowmap(0)),
        out_shape=jax.ShapeDtypeStruct((m, LRU_WIDTH), BF16),
        scratch_shapes=[pltpu.VMEM((8, LRU_WIDTH), F32), pltpu.VMEM((8, LRU_WIDTH), F32)],
        compiler_params=_cparams("parallel", "arbitrary"),
        name="rglru",
    )(z, z, conv_w, vecs, wa_bd, wx_bd)


def _gdn_body(q_ref, k_ref, v_ref, zg_ref, ab_ref, abt_ref, cw_ref, gp_ref, gpt_ref, nw_ref,
              o_ref, pq_scr, pk_scr, pv_scr, st_scr):
    tt = q_ref.shape[0]
    c = GDN_CHUNK
    width = GDN_HEADS * GDN_DK

    @pl.when(pl.program_id(1) == 0)
    def _():
        for s in (pq_scr, pk_scr, pv_scr, st_scr):
            s[...] = jnp.zeros_like(s)

    cw = cw_ref[...]

    def conv_silu(x_ref, p_scr, w):
        x = x_ref[...].astype(F32)
        prev = p_scr[...]
        y = (w[3:4] * x + w[2:3] * _shift_prev(x, prev, 1) + w[1:2] * _shift_prev(x, prev, 2)
             + w[0:1] * _shift_prev(x, prev, 3))
        p_scr[...] = x[tt - 8:]
        return jax.nn.silu(y)

    q = conv_silu(q_ref, pq_scr, cw[:, 0:width])
    k = conv_silu(k_ref, pk_scr, cw[:, width:2 * width])
    v = conv_silu(v_ref, pv_scr, cw[:, 2 * width:])

    ab = ab_ref[...].astype(F32)
    gp = gp_ref[...]
    gpt = gpt_ref[...]
    g_all = -jnp.exp(gp[0:1]) * _softplus(ab + gp[1:2])
    beta_all = jax.nn.sigmoid(ab)

    incl = _tri(c, False)
    strict = _tri(c, True)
    cum_mask = jnp.concatenate([incl.astype(BF16), jnp.ones((c, c), BF16)], axis=0)
    triu_bf = (lax.broadcasted_iota(jnp.int32, (c, c), 0) <= lax.broadcasted_iota(jnp.int32, (c, c), 1)).astype(BF16)
    nw = nw_ref[...]

    n_chunks = tt // c
    heads = range(GDN_HEADS)
    inst = [(ci, h) for ci in range(n_chunks) for h in heads]
    sls = [slice(ci * c, (ci + 1) * c) for ci in range(n_chunks)]
    hss = [slice(h * GDN_DK, (h + 1) * GDN_DK) for h in heads]
    gcs = [_mask_mm(cum_mask, g_all[sl]) for sl in sls]
    gr_all = [_mm_mask(-jnp.exp(gpt[:, 0:1]) * _softplus(abt_ref[ci][0:GDN_HEADS] + gpt[:, 1:2]), triu_bf)
              for ci in range(n_chunks)]

    qs, ks, kbs, decays, egcs, kdecs, gtots, bcols = {}, {}, {}, {}, {}, {}, {}, {}
    for ci, h in inst:
        sl, hs = sls[ci], hss[h]
        qc, kc = q[sl, hs], k[sl, hs]
        qs[ci, h] = qc * lax.rsqrt(jnp.sum(qc * qc, axis=-1, keepdims=True) + RMS_EPS) * (GDN_DK ** -0.5)
        ks[ci, h] = kc * lax.rsqrt(jnp.sum(kc * kc, axis=-1, keepdims=True) + RMS_EPS)
        gcol = gcs[ci][:c, h:h + 1]
        g_last = gcs[ci][c:, h:h + 1]
        grow = gr_all[ci][h:h + 1, :]
        bcols[ci, h] = beta_all[sl, GDN_HEADS + h:GDN_HEADS + h + 1]
        decays[ci, h] = jnp.exp(jnp.where(incl, gcol - grow, -BIG))
        kbs[ci, h] = ks[ci, h] * bcols[ci, h]
        egcs[ci, h] = jnp.exp(gcol)
        kdecs[ci, h] = ks[ci, h] * jnp.exp(g_last - gcol)
        gtots[ci, h] = jnp.exp(jnp.concatenate([g_last] * (GDN_DK // c), axis=0))

    kk = {i: _mm_nt(kbs[i], ks[i]) for i in inst}
    qk = {i: _mm_nt(qs[i], ks[i]) * decays[i] for i in inst}
    p = {i: -jnp.where(strict, kk[i] * decays[i], 0.0) for i in inst}
    tm1 = dict(p)
    for _ in range(5):
        p = {i: _mm(p[i], p[i]) for i in inst}
        tp = {i: _mm(tm1[i], p[i]) for i in inst}
        tm1 = {i: tm1[i] + tp[i] + p[i] for i in inst}
    rhs = {i: jnp.concatenate([v[sls[i[0]], hss[i[1]]] * bcols[i], kbs[i] * egcs[i]], axis=1) for i in inst}
    tr = {i: _mm(tm1[i], rhs[i]) for i in inst}
    sol = {i: rhs[i] + tr[i] for i in inst}

    state = [st_scr[h] for h in heads]
    for ci in range(n_chunks):
        ws = [_mm(sol[ci, h][:, GDN_DV:], state[h]) for h in heads]
        qst = [_mm(qs[ci, h] * egcs[ci, h], state[h]) for h in heads]
        v_new = [sol[ci, h][:, :GDN_DV] - ws[h] for h in heads]
        qv = [_mm(qk[ci, h], v_new[h]) for h in heads]
        kv = [_mm_tn(kdecs[ci, h], v_new[h]) for h in heads]
        state = [state[h] * gtots[ci, h] + kv[h] for h in heads]
        for h in heads:
            o = qst[h] + qv[h]
            o = o * lax.rsqrt(jnp.mean(o * o, axis=-1, keepdims=True) + RMS_EPS) * nw
            o = o * jax.nn.silu(zg_ref[sls[ci], hss[h]].astype(F32))
            o_ref[sls[ci], hss[h]] = o.astype(o_ref.dtype)
    for h in heads:
        st_scr[h] = state[h]


def _gdn(z, ab, abt, bsz, seq, conv_w, gp, gpt, norm_w):
    m = z.shape[0]
    tt = 256
    nt = seq // tt
    width = GDN_HEADS * GDN_DK
    zblk = lambda c0: pl.BlockSpec((tt, width), lambda b, t: (b * nt + t, c0 // width))
    const2 = lambda b, t: (0, 0)
    return pl.pallas_call(
        _gdn_body,
        grid=(bsz, nt),
        in_specs=[
            zblk(COL_QKV), zblk(COL_QKV + width), zblk(COL_QKV + 2 * width), zblk(COL_ZG),
            pl.BlockSpec((tt, 128), lambda b, t: (b * nt + t, 0)),
            pl.BlockSpec((tt // GDN_CHUNK, 2 * GDN_HEADS, GDN_CHUNK), lambda b, t: (b * nt + t, 0, 0)),
            pl.BlockSpec((4, 3 * width), const2),
            pl.BlockSpec((8, 128), const2),
            pl.BlockSpec((8, 128), const2),
            pl.BlockSpec((1, 128), const2),
        ],
        out_specs=pl.BlockSpec((tt, width), lambda b, t: (b * nt + t, 0)),
        out_shape=jax.ShapeDtypeStruct((m, width), BF16),
        scratch_shapes=[pltpu.VMEM((8, width), F32)] * 3 + [pltpu.VMEM((GDN_HEADS, GDN_DK, GDN_DV), F32)],
        compiler_params=_cparams("parallel", "arbitrary"),
        name="gated_deltanet",
    )(z, z, z, z, ab, abt, conv_w, gp, gpt, norm_w)


def _rwkv_body(r_ref, k_ref, v_ref, lo_ref, mu_ref, vp_ref, wup_ref, aup_ref, gup_ref, o_ref,
               pr_scr, pk_scr, pv_scr, plo_scr, st_scr):
    tt = r_ref.shape[0]
    c = RWKV_CHUNK
    gw = RWKV_GROUP
    hd = RWKV_HD
    width = RWKV_WIDTH

    @pl.when(pl.program_id(1) == 0)
    def _():
        for s in (pr_scr, pk_scr, pv_scr, plo_scr, st_scr):
            s[...] = jnp.zeros_like(s)

    mu = mu_ref[...]

    def tshift(x_ref, p_scr, m):
        x = x_ref[...].astype(F32)
        xs = x + m * (_shift_prev(x, p_scr[...], 1) - x)
        p_scr[...] = x[tt - 8:]
        return xs

    r = tshift(r_ref, pr_scr, mu[:, 0:width])
    k = tshift(k_ref, pk_scr, mu[:, width:2 * width])
    v = tshift(v_ref, pv_scr, mu[:, 2 * width:3 * width])
    lo = tshift(lo_ref, plo_scr, mu[:, 3 * width:])
    vp = vp_ref[...]
    w0, a0, k_k, k_a, r_k, lnx_w, lnx_b = (vp[i:i + 1] for i in range(7))

    lane = lax.broadcasted_iota(jnp.int32, lo.shape, 1)
    lo_act = jnp.where(lane < 64, jnp.tanh(lo), jnp.where(lane < 128, lo, jax.nn.sigmoid(lo)))
    w_pre = _mm(lo_act, wup_ref[...])
    a_pre = _mm(lo_act, aup_ref[...])
    gate = _mm(lo_act, gup_ref[...])
    w_log = -_softplus(-(w0 + w_pre)) - 0.5
    lw = -jnp.exp(w_log)
    a = jax.nn.sigmoid(a0 + a_pre)

    bi = lax.broadcasted_iota(jnp.int32, (gw, gw), 0) // hd
    bj = lax.broadcasted_iota(jnp.int32, (gw, gw), 1) // hd
    bdmask = bi == bj
    ones_bd = bdmask.astype(BF16)

    def head_sum(x):
        return _mm_mask(x, ones_bd, terms=1)

    def bd(x):
        return jnp.where(bdmask, jnp.concatenate([x] * (gw // c), axis=0), 0.0)

    ti = lax.broadcasted_iota(jnp.int32, (c, gw), 0)
    sj = lax.broadcasted_iota(jnp.int32, (c, gw), 1) % c
    strict = sj < ti
    incl = sj <= ti
    tril_bf = _tri(c, False).astype(BF16)

    kk_all = k * k_k
    k2_all = k * (1.0 + (a - 1.0) * k_a)

    n_chunks = tt // c
    groups = range(width // gw)
    inst = [(ci, g) for ci in range(n_chunks) for g in groups]
    sls = [slice(ci * c, (ci + 1) * c) for ci in range(n_chunks)]
    gss = [slice(g * gw, (g + 1) * gw) for g in groups]
    cl_all = [_mask_mm(tril_bf, lw[sl]) for sl in sls]

    def at(x, i):
        return x[sls[i[0]], gss[i[1]]]

    kk_raw = {i: at(kk_all, i) for i in inst}
    kk_ss = {i: head_sum(kk_raw[i] * kk_raw[i]) for i in inst}
    a_h, b_h, k_h, br, w_end = {}, {}, {}, {}, {}
    for i in inst:
        cl = cl_all[i[0]][:, gss[i[1]]]
        kk = kk_raw[i] * lax.rsqrt(kk_ss[i] + RMS_EPS)
        wcum = jnp.exp(cl)
        inv_w = jnp.exp(-cl)
        a_h[i] = kk * at(a, i) * inv_w
        b_h[i] = kk * jnp.exp(cl - at(lw, i))
        k_h[i] = at(k2_all, i) * inv_w
        br[i] = jnp.concatenate([b_h[i], at(r, i) * wcum], axis=0)
        w_end[i] = wcum[c - 1:c]
    v_bd = {i: bd(at(v, i)) for i in inst}
    xa = {i: _mm_nt(br[i], bd(a_h[i])) for i in inst}
    xk = {i: _mm_nt(br[i], bd(k_h[i])) for i in inst}
    l_k = {i: jnp.where(strict, xk[i][:c], 0.0) for i in inst}
    ra = {i: jnp.where(incl, xa[i][c:], 0.0) for i in inst}
    rk = {i: jnp.where(incl, xk[i][c:], 0.0) for i in inst}
    p = {i: -jnp.where(strict, xa[i][:c], 0.0) for i in inst}
    tm1 = dict(p)
    for _ in range(5):
        p = {i: _mm(p[i], bd(p[i])) for i in inst}
        tp = {i: _mm(tm1[i], bd(p[i])) for i in inst}
        tm1 = {i: tm1[i] + tp[i] + p[i] for i in inst}
    lkv = {i: _mm(l_k[i], v_bd[i]) for i in inst}
    rkv = {i: _mm(rk[i], v_bd[i]) for i in inst}
    so = {i: head_sum(at(r, i) * at(k2_all, i) * r_k[:, gss[i[1]]]) for i in inst}

    state_t = [st_scr[g] for g in groups]
    for ci in range(n_chunks):
        ids = [(ci, g) for g in groups]
        brh = [_mm_nt(br[i], state_t[i[1]]) for i in ids]
        rhs = [brh[g][:c] + lkv[ci, g] for g in groups]
        tu = [_mm(tm1[ci, g], bd(rhs[g])) for g in groups]
        u = [rhs[g] + tu[g] for g in groups]
        rau = [_mm(ra[ci, g], bd(u[g])) for g in groups]
        upd = [_mm_tn(jnp.concatenate([u[g], at(v, (ci, g))], axis=0),
                      jnp.concatenate([-(a_h[ci, g] * w_end[ci, g]), k_h[ci, g] * w_end[ci, g]], axis=0))
               for g in groups]
        state_t = [w_end[ci, g] * state_t[g] + jnp.where(bdmask, upd[g], 0.0) for g in groups]
        o = [brh[g][c:] - rau[g] + rkv[ci, g] for g in groups]
        osum = [head_sum(o[g]) for g in groups]
        cen = [o[g] - osum[g] * (1.0 / hd) for g in groups]
        var = [head_sum(cen[g] * cen[g]) * (1.0 / hd) for g in groups]
        for g in groups:
            gs = gss[g]
            on = cen[g] * lax.rsqrt(var[g] + RWKV_GN_EPS) * lnx_w[:, gs] + lnx_b[:, gs]
            bonus = so[ci, g] * at(v, (ci, g))
            o_ref[sls[ci], gs] = ((on + bonus) * gate[sls[ci], gs]).astype(o_ref.dtype)
    for g in groups:
        st_scr[g] = state_t[g]


def _rwkv(z, bsz, seq, mu, vecs, wup_pad, aup_pad, gup_pad):
    m = z.shape[0]
    tt = 256
    nt = seq // tt
    gw = RWKV_GROUP
    width = RWKV_WIDTH
    zblk = lambda c0: pl.BlockSpec((tt, width), lambda b, t: (b * nt + t, c0 // width))
    lora_col = (COL_RWKV + 3 * width) // RWKV_LORA
    const2 = lambda b, t: (0, 0)
    return pl.pallas_call(
        _rwkv_body,
        grid=(bsz, nt),
        in_specs=[
            zblk(COL_RWKV), zblk(COL_RWKV + width), zblk(COL_RWKV + 2 * width),
            pl.BlockSpec((tt, RWKV_LORA), lambda b, t: (b * nt + t, lora_col)),
            pl.BlockSpec(mu.shape, const2),
            pl.BlockSpec((8, width), const2),
            pl.BlockSpec((RWKV_LORA, width), const2),
            pl.BlockSpec((RWKV_LORA, width), const2),
            pl.BlockSpec((RWKV_LORA, width), const2),
        ],
        out_specs=pl.BlockSpec((tt, width), lambda b, t: (b * nt + t, 0)),
        out_shape=jax.ShapeDtypeStruct((m, width), BF16),
        scratch_shapes=[pltpu.VMEM((8, width), F32)] * 3 + [pltpu.VMEM((8, RWKV_LORA), F32),
                                                            pltpu.VMEM((width // gw, gw, gw), F32)],
        compiler_params=_cparams("parallel", "arbitrary"),
        name="rwkv7",
    )(z, z, z, z, mu, vecs, wup_pad, aup_pad, gup_pad)


def _merge_body(h_ref, ya_ref, yb_ref, yc_ref, za_ref, zb_ref, zc_ref, mb_ref, pa_ref, pb_ref,
                pc_ref, wo_ref, nw_ref, hout_ref, xnt_ref):
    mb = mb_ref[...]
    ga = jax.nn.sigmoid(za_ref[...].astype(F32) + mb[:, 0:D_MODEL])
    gb = jax.nn.sigmoid(zb_ref[...].astype(F32) + mb[:, D_MODEL:2 * D_MODEL])
    gc = jax.nn.sigmoid(zc_ref[...].astype(F32) + mb[:, 2 * D_MODEL:])
    merged = (ga * jnp.dot(ya_ref[...], pa_ref[...], preferred_element_type=F32)
              + gb * jnp.dot(yb_ref[...], pb_ref[...], preferred_element_type=F32)
              + gc * jnp.dot(yc_ref[...], pc_ref[...], preferred_element_type=F32))
    h = h_ref[...] + _mm(merged, wo_ref[...])
    hout_ref[...] = h
    xnt_ref[...] = jnp.transpose(_rms(h, nw_ref[...])).astype(BF16)


def _merge(h, z, ya, yb, yc, merge_b, pa, pb, pc, wo, norm_w):
    m = h.shape[0]
    tm = 512 if m % 512 == 0 else 256
    row = lambda c: pl.BlockSpec((tm, D_MODEL), lambda i: (i, c))
    wspec = pl.BlockSpec((D_MODEL, D_MODEL), lambda i: (0, 0))
    mc = COL_MERGE // D_MODEL
    return pl.pallas_call(
        _merge_body,
        grid=(m // tm,),
        in_specs=[row(0), row(0), row(0), row(0), row(mc), row(mc + 1), row(mc + 2),
                  pl.BlockSpec((1, 3 * D_MODEL), lambda i: (0, 0)),
                  wspec, wspec, wspec, wspec,
                  pl.BlockSpec((1, D_MODEL), lambda i: (0, 0))],
        out_specs=[row(0), pl.BlockSpec((D_MODEL, tm), lambda i: (0, i))],
        out_shape=[jax.ShapeDtypeStruct((m, D_MODEL), F32), jax.ShapeDtypeStruct((D_MODEL, m), BF16)],
        compiler_params=_cparams("parallel"),
        name="merge_outproj",
    )(h, ya, yb, yc, z, z, z, merge_b, pa, pb, pc, wo, norm_w)


_CAND_SLABS = (
    (16, ((0, 0, 16),)),
    (16, ((1, 0, 8), (2, 8, 5), (4, 13, 3))),
    (16, ((3, 0, 4), (5, 4, 2), (6, 6, 2), (7, 8, 2), (8, 10, 1), (9, 11, 1), (10, 12, 1), (11, 13, 1),
          (12, 14, 1), (13, 15, 1))),
    (8, ((14, 0, 1), (15, 1, 1))),
)


def _odd_even_merge_sort_pairs(n):
    pairs = []
    p = 1
    while p < n:
        k = p
        while k >= 1:
            for j in range(k % p, n - k, 2 * k):
                for i in range(min(k, n - j - k)):
                    if (i + j) // (2 * p) == (i + j + k) // (2 * p):
                        pairs.append((i + j, i + j + k))
            k //= 2
        p *= 2
    return tuple(pairs)


_SORT_PAIRS = _odd_even_merge_sort_pairs(PEER_TOPK)


def _candidate_sums(a1, a2):
    slabs = []
    for nrows, pieces in _CAND_SLABS:
        base = a2[:nrows]
        rows = lax.broadcasted_iota(jnp.int32, base.shape, 0)
        out = None
        end = 0
        for p, off, cnt in pieces:
            val = a1[p:p + 1] + (base if off == 0 else pltpu.roll(base, off, 0))
            out = val if out is None else jnp.where(rows >= off, val, out)
            end = off + cnt
        if end < nrows:
            out = jnp.where(rows >= end, -BIG, out)
        slabs.append(out)
    return jnp.concatenate(slabs, axis=0)


def _count_leading(pred, thr):
    assert len(thr) == 16
    b8 = pred(thr[7])
    b4 = pred(jnp.where(b8, thr[11], thr[3]))
    b2 = pred(jnp.where(b8, jnp.where(b4, thr[13], thr[9]), jnp.where(b4, thr[5], thr[1])))
    lo = jnp.where(b4, jnp.where(b2, thr[6], thr[4]), jnp.where(b2, thr[2], thr[0]))
    hi = jnp.where(b4, jnp.where(b2, thr[14], thr[12]), jnp.where(b2, thr[10], thr[8]))
    b1 = pred(jnp.where(b8, hi, lo))
    cnt = (jnp.where(b8, 8.0, 0.0) + jnp.where(b4, 4.0, 0.0)) + (jnp.where(b2, 2.0, 0.0) + jnp.where(b1, 1.0, 0.0))
    return jnp.where(pred(thr[15]), 16.0, cnt)


def _peer_topk_body(xnt_ref, wqt_ref, keys_ref, n_ref, e1_ref, r2_ref, e2_ref, q_scr):
    tt = xnt_ref.shape[1]
    nk = PEER_NKEYS
    topk = PEER_TOPK
    q_scr[...] = jnp.dot(wqt_ref[...], xnt_ref[...], preferred_element_type=F32)
    lanes = 128
    sub = 8
    assert nk == topk * sub
    row8 = lax.broadcasted_iota(jnp.int32, (sub, lanes), 0)

    def top_sorted(s):
        v = [s[k * sub:(k + 1) * sub] for k in range(topk)]
        for i, j in _SORT_PAIRS:
            v[i], v[j] = jnp.maximum(v[i], v[j]), jnp.minimum(v[i], v[j])
        for shift in (4, 2, 1):
            other = [pltpu.roll(x, shift, 0) for x in v]
            v = [jnp.maximum(v[k], other[topk - 1 - k]) for k in range(topk)]
            d = topk // 2
            while d >= 1:
                for i in range(topk):
                    if i & d == 0:
                        v[i], v[i + d] = jnp.maximum(v[i], v[i + d]), jnp.minimum(v[i], v[i + d])
                d //= 2
        return v

    def compact(v):
        tiles = []
        for t0 in range(0, topk, sub):
            out = v[t0]
            for k in range(1, sub):
                out = jnp.where(row8 == k, v[t0 + k], out)
            tiles.append(out)
        return jnp.concatenate(tiles, axis=0)

    def head(h, carry):
        o1 = pl.multiple_of(h * (2 * nk), 2 * nk)
        s1 = _mm(keys_ref[2 * h], q_scr[pl.ds(o1, nk), :])
        s2 = _mm(keys_ref[2 * h + 1], q_scr[pl.ds(o1 + nk, nk), :])
        chunks = [slice(l0, l0 + lanes) for l0 in range(0, tt, lanes)]
        a1 = [top_sorted(s1[:, ls]) for ls in chunks]
        a2 = [top_sorted(s2[:, ls]) for ls in chunks]
        a1c = jnp.concatenate([compact(v) for v in a1], axis=1)
        a2c = jnp.concatenate([compact(v) for v in a2], axis=1)
        cand = _candidate_sums(a1c, a2c)
        cmax = a1c[0:1] + a2c[0:1]

        def cbody(rnd, carry):
            cnd, zsum, c_in, c_out = carry
            mx = jnp.max(cnd, axis=0, keepdims=True)
            zsum = zsum + jnp.where(rnd < topk, jnp.exp(mx - cmax), 0.0)
            c_in = jnp.where(rnd == topk - 1, mx, c_in)
            c_out = jnp.where(rnd == topk, mx, c_out)
            return jnp.where(cnd == mx, -BIG, cnd), zsum, c_in, c_out

        zero = jnp.zeros((1, tt), F32)
        _, zsum, c_in, c_out = lax.fori_loop(0, topk + 1, cbody, (cand, zero, zero, zero))
        tau = 0.5 * (c_in + c_out)
        inv_z = 1.0 / zsum
        pack = 16

        def rep(x):
            return jnp.concatenate([x] * (pack // sub), axis=0)

        for ci, ls in enumerate(chunks):
            tau8 = jnp.broadcast_to(tau[:, ls], (sub, lanes))
            need = [rep(tau8 - a2[ci][qq]) for qq in range(topk)]
            val2 = [rep(a2[ci][qq]) for qq in range(topk)]
            last1 = rep(a1[ci][topk - 1])
            top1 = rep(a1[ci][0])
            top2 = val2[0]
            for k in range(nk // pack):
                rows = slice(k * pack, (k + 1) * pack)
                x1 = s1[rows, ls]
                x2 = s2[rows, ls]
                n_sel = _count_leading(lambda t: x1 >= t, need)
                r2 = _count_leading(lambda t: x2 < t, val2)
                n_ref[h, rows, ls] = jnp.where(x1 >= last1, n_sel, 0.0)
                e1_ref[h, rows, ls] = jnp.exp(x1 - top1)
                r2_ref[h, rows, ls] = r2.astype(BF16)
                e2_ref[h, rows, ls] = (jnp.exp(x2 - top2) * inv_z[:, ls]).astype(BF16)
        return carry

    lax.fori_loop(0, PEER_HEADS, head, 0)


def _peer_topk(xnt, wqt, keys):
    m = xnt.shape[1]
    tt = 512 if m % 512 == 0 else 256
    shape = (PEER_HEADS, PEER_NKEYS, m)
    ospec = pl.BlockSpec((PEER_HEADS, PEER_NKEYS, tt), lambda i: (0, 0, i))
    return pl.pallas_call(
        _peer_topk_body,
        grid=(m // tt,),
        in_specs=[pl.BlockSpec((D_MODEL, tt), lambda i: (0, i)),
                  pl.BlockSpec(wqt.shape, lambda i: (0, 0)),
                  pl.BlockSpec(keys.shape, lambda i: (0, 0, 0))],
        out_specs=[ospec] * 4,
        out_shape=[jax.ShapeDtypeStruct(shape, F32), jax.ShapeDtypeStruct(shape, F32),
                   jax.ShapeDtypeStruct(shape, BF16), jax.ShapeDtypeStruct(shape, BF16)],
        scratch_shapes=[pltpu.VMEM((wqt.shape[0], tt), F32)],
        compiler_params=_cparams("parallel"),
        name="peer_topk",
    )(xnt, wqt, keys)


PEER_TOKEN_CHUNK = 256


def _peer_dense_body(last_layer, u_ref, xnt_ref, vt_ref, n_ref, e1_ref, r2_ref, e2_ref, h_ref, nw_ref,
                     hout_ref, acc_scr):
    nk = PEER_NKEYS
    tt = xnt_ref.shape[1]
    tc = PEER_TOKEN_CHUNK
    n_blk = u_ref.shape[0] // nk
    pack = 16

    @pl.when(pl.program_id(1) == 0)
    def _():
        acc_scr[...] = jnp.zeros_like(acc_scr)

    def rows_bf16(ref, h, ii, ls):
        row = jnp.broadcast_to(ref[h, ii:ii + 1, ls], (pack, tc)).astype(BF16)
        return jnp.concatenate([row] * (nk // pack), axis=0)

    def select_weights(ii, c0):
        ls = slice(c0, c0 + tc)
        w = None
        for h in range(PEER_HEADS):
            sel = jnp.where(r2_ref[h, :, ls] < rows_bf16(n_ref, h, ii, ls), e2_ref[h, :, ls],
                            jnp.zeros((), BF16))
            term = sel * rows_bf16(e1_ref, h, ii, ls)
            w = term if w is None else w + term
        return w

    def gelu_tanh(x):
        c = 0.7978845608028654
        inner = x * ((x * x) * (c * 0.044715) + c)
        hx = 0.5 * x
        return hx * jnp.tanh(inner) + hx

    starts = list(range(0, tt, tc))
    pre = [jnp.dot(u_ref[...], xnt_ref[:, c0:c0 + tc], preferred_element_type=F32) for c0 in starts]
    for idx, c0 in enumerate(starts):
        act = gelu_tanh(pre[idx].astype(BF16))
        a = jnp.concatenate([act[ii * nk:(ii + 1) * nk] * select_weights(ii, c0) for ii in range(n_blk)], axis=0)
        acc_scr[:, c0:c0 + tc] += jnp.dot(vt_ref[...], a, preferred_element_type=F32)

    @pl.when(pl.program_id(1) == pl.num_programs(1) - 1)
    def _():
        res = h_ref[...] + jnp.transpose(acc_scr[...])
        hout_ref[...] = _rms(res, nw_ref[...]) if last_layer else res


def _peer_dense(xnt, u_bf, vt_bf, n_sel, e1, r2, e2, h, final_norm_w, last_layer, tt):
    m = xnt.shape[1]
    te = 2048
    ib = te // PEER_NKEYS
    sel_i = pl.BlockSpec((PEER_HEADS, ib, tt), lambda i, e: (0, e, i))
    sel_all = pl.BlockSpec((PEER_HEADS, PEER_NKEYS, tt), lambda i, e: (0, 0, i))
    return pl.pallas_call(
        functools.partial(_peer_dense_body, last_layer),
        grid=(m // tt, PEER_EXPERTS // te),
        in_specs=[pl.BlockSpec((te, D_MODEL), lambda i, e: (e, 0)),
                  pl.BlockSpec((D_MODEL, tt), lambda i, e: (0, i)),
                  pl.BlockSpec((D_MODEL, te), lambda i, e: (0, e)),
                  sel_i, sel_i, sel_all, sel_all,
                  pl.BlockSpec((tt, D_MODEL), lambda i, e: (i, 0)),
                  pl.BlockSpec((1, D_MODEL), lambda i, e: (0, 0))],
        out_specs=pl.BlockSpec((tt, D_MODEL), lambda i, e: (i, 0)),
        out_shape=jax.ShapeDtypeStruct((m, D_MODEL), F32),
        scratch_shapes=[pltpu.VMEM((D_MODEL, tt), F32)],
        compiler_params=_cparams("parallel", "arbitrary"),
        name="peer_dense",
    )(u_bf, xnt, vt_bf, n_sel, e1, r2, e2, h, final_norm_w.reshape(1, D_MODEL))


def _table_body(transpose, x_ref, o_ref):
    x = x_ref[...]
    o_ref[...] = (jnp.transpose(x) if transpose else x).astype(o_ref.dtype)


def _expert_table(tables, l, transpose):
    _, rows, cols = tables.shape
    tr = 512
    return pl.pallas_call(
        functools.partial(_table_body, transpose),
        grid=(rows // tr,),
        in_specs=[pl.BlockSpec((None, tr, cols), lambda i: (l, i, 0))],
        out_specs=pl.BlockSpec((cols, tr), lambda i: (0, i)) if transpose else pl.BlockSpec((tr, cols), lambda i: (i, 0)),
        out_shape=jax.ShapeDtypeStruct((cols, rows) if transpose else (rows, cols), BF16),
        compiler_params=_cparams("parallel"),
        name="expert_table",
    )(tables)


def _pad_rows(rows, width, n_rows=8):
    flat = [jnp.pad(r.reshape(-1).astype(F32), (0, width - r.size)) for r in rows]
    return jnp.pad(jnp.stack(flat), ((0, n_rows - len(rows)), (0, 0)))


def _chunk_rows(ab):
    m, n = ab.shape
    return jnp.transpose(ab.reshape(m // GDN_CHUNK, GDN_CHUNK, n), (0, 2, 1)).astype(F32)


def _block_diag_groups(w):
    per = LRU_GROUP // LRU_BLOCK_DIM
    n_grp = w.shape[0] // per
    wg = w.reshape(n_grp, per, LRU_BLOCK_DIM, LRU_BLOCK_DIM)
    eye = jnp.eye(per, dtype=w.dtype)
    out = wg[:, :, :, None, :] * eye[None, :, None, :, None]
    return out.reshape(n_grp, LRU_GROUP, LRU_GROUP).astype(BF16)


_W_IN_PIECES = ((0, 6144, 0), (9488, 12560, COL_MERGE), (6160, 9488, COL_RWKV), (6144, 6160, COL_AB))


def _w_in_body(w_ref, o_ref):
    o_ref[:, COL_AB:] = jnp.zeros((o_ref.shape[0], o_ref.shape[1] - COL_AB), o_ref.dtype)
    for a, b, d in _W_IN_PIECES:
        o_ref[:, d:d + b - a] = w_ref[:, a:b].astype(o_ref.dtype)


def _prep_w_in(w_all, l):
    _, rows, cols = w_all.shape
    tr = 256
    return pl.pallas_call(
        _w_in_body,
        grid=(rows // tr,),
        in_specs=[pl.BlockSpec((None, tr, cols), lambda i: (l, i, 0))],
        out_specs=pl.BlockSpec((tr, IN_WIDTH_PAD), lambda i: (i, 0)),
        out_shape=jax.ShapeDtypeStruct((rows, IN_WIDTH_PAD), BF16),
        compiler_params=_cparams("parallel"),
        name="w_in_layout",
    )(w_all)


def _layer(l, h, bsz, seq, p):
    z, ab = _inproj(h, p["norm_mix_w"][l], _prep_w_in(p["w_in_bf16"], l))

    lru_vecs = _pad_rows([p["lru_conv_b"][l], p["lru_b_a"][l], p["lru_b_x"][l], p["lru_lambda"][l]], LRU_WIDTH)
    ya = _lru(z, bsz, seq, p["lru_conv_w"][l], lru_vecs,
              _block_diag_groups(p["lru_w_a"][l]), _block_diag_groups(p["lru_w_x"][l]))

    abt = _chunk_rows(ab[:, :2 * GDN_HEADS])
    gp = _pad_rows([p["gdn_a_log"][l], p["gdn_dt_bias"][l]], 128)
    gpt = jnp.pad(jnp.stack([p["gdn_a_log"][l], p["gdn_dt_bias"][l]], axis=1), ((0, 0), (0, 126)))
    yb = _gdn(z, ab, abt, bsz, seq, p["gdn_conv_w"][l], gp, gpt, p["gdn_norm_w"][l].reshape(1, GDN_DV))

    rw_vecs = _pad_rows([p["rwkv_w0"][l], p["rwkv_a0"][l], p["rwkv_k_k"][l], p["rwkv_k_a"][l],
                         p["rwkv_r_k"][l], p["rwkv_lnx_w"][l], p["rwkv_lnx_b"][l]], RWKV_WIDTH)
    wup = jnp.pad(p["rwkv_w_up"][l], ((0, 192), (0, 0))).astype(BF16)
    aup = jnp.pad(p["rwkv_a_up"][l], ((64, 128), (0, 0))).astype(BF16)
    gup = jnp.pad(p["rwkv_g_up"][l], ((128, 0), (0, 0))).astype(BF16)
    yc = _rwkv(z, bsz, seq, p["rwkv_mu"][l].reshape(1, -1), rw_vecs, wup, aup, gup)

    h, xnt = _merge(h, z, ya, yb, yc, p["merge_b"][l].reshape(1, -1), p["p_lru"][l].astype(BF16),
                    p["p_gdn"][l].astype(BF16), p["p_rwkv"][l].astype(BF16), p["w_out"][l].astype(BF16),
                    p["norm_ffn_w"][l].reshape(1, D_MODEL))

    wqt = jnp.transpose(p["peer_wq"][l]).astype(BF16)
    keys = p["peer_keys"][l].reshape(2 * PEER_HEADS, PEER_NKEYS, PEER_HALF).astype(BF16)
    n_sel, e1, r2, e2 = _peer_topk(xnt, wqt, keys)
    tt = 512 if xnt.shape[1] % 512 == 0 else 256
    return _peer_dense(xnt, _expert_table(p["peer_u"], l, False), _expert_table(p["peer_v"], l, True),
                       n_sel, e1, r2, e2, h, p["final_norm_w"], l == DEPTH - 1, tt)


def kernel(x, norm_mix_w, norm_ffn_w, final_norm_w, w_in, lru_conv_w, lru_conv_b, lru_w_a, lru_b_a,
           lru_w_x, lru_b_x, lru_lambda, gdn_conv_w, gdn_a_log, gdn_dt_bias, gdn_norm_w, rwkv_mu, rwkv_w0,
           rwkv_w_up, rwkv_a0, rwkv_a_up, rwkv_g_up, rwkv_k_k, rwkv_k_a, rwkv_r_k, rwkv_lnx_w, rwkv_lnx_b,
           merge_b, p_lru, p_gdn, p_rwkv, w_out, peer_wq, peer_keys, peer_u, peer_v):
    p = dict(locals())
    p["w_in_bf16"] = w_in.astype(BF16)
    bsz, seq, dim = x.shape
    h = x.reshape(bsz * seq, dim)
    for l in range(DEPTH):
        h = _layer(l, h, bsz, seq, p)
    return h.reshape(bsz, seq, dim)
```

```python
import functools

import jax
import jax.numpy as jnp
from jax import lax
from jax.experimental import pallas as pl
from jax.experimental.pallas import tpu as pltpu

F32 = jnp.float32
BF16 = jnp.bfloat16

D_MODEL = 1024
DEPTH = 2
RMS_EPS = 1e-6

LRU_WIDTH = 1024
LRU_BLOCK_DIM = 64
LRU_C = 8.0
LRU_GROUP = 256

GDN_HEADS = 8
GDN_DK = 128
GDN_DV = 128
GDN_CHUNK = 64

RWKV_HD = 64
RWKV_WIDTH = 1024
RWKV_GN_EPS = 64e-5
RWKV_CHUNK = 64
RWKV_GROUP = 256
RWKV_LORA = 256

PEER_HEADS = 8
PEER_NKEYS = 128
PEER_EXPERTS = PEER_NKEYS * PEER_NKEYS
PEER_HALF = 128
PEER_TOPK = 16

COL_U = 0
COL_GATE = 1024
COL_QKV = 2048
COL_ZG = 5120
COL_MERGE = 6144
COL_RWKV = 9216
COL_AB = 12544
IN_WIDTH_PAD = 12800
IN_TILE_N = 1280

VMEM_LIMIT = 48 * 1024 * 1024
BIG = 3.0e38


def _cparams(*sem):
    return pltpu.CompilerParams(dimension_semantics=sem, vmem_limit_bytes=VMEM_LIMIT)


def _mm(a, b):
    return jnp.dot(a.astype(BF16), b.astype(BF16), preferred_element_type=F32)


def _mm_nt(a, b):
    return lax.dot_general(a.astype(BF16), b.astype(BF16), (((1,), (1,)), ((), ())),
                           preferred_element_type=F32)


def _mm_tn(a, b):
    return lax.dot_general(a.astype(BF16), b.astype(BF16), (((0,), (0,)), ((), ())),
                           preferred_element_type=F32)


def _softplus(x):
    return jnp.maximum(x, 0.0) + jnp.log1p(jnp.exp(-jnp.abs(x)))


def _rms(x, w):
    return x * lax.rsqrt(jnp.mean(x * x, axis=-1, keepdims=True) + RMS_EPS) * w


def _shift_prev(x, prev8, s):
    r = pltpu.roll(x, s, 0)
    pr = pltpu.roll(prev8, s, 0)
    rows8 = lax.broadcasted_iota(jnp.int32, prev8.shape, 0)
    head = jnp.where(rows8 < s, pr, r[:8])
    return jnp.concatenate([head, r[8:]], axis=0)


def _shift_fill(x, d, fill):
    n, c = x.shape
    if d % 8 == 0:
        return jnp.concatenate([jnp.full((d, c), fill, x.dtype), x[:n - d]], axis=0)
    r = pltpu.roll(x, d, 0)
    rows8 = lax.broadcasted_iota(jnp.int32, (8, c), 0)
    head = jnp.where(rows8 < d, fill, r[:8])
    return jnp.concatenate([head, r[8:]], axis=0)


def _tri(n, strict):
    i = lax.broadcasted_iota(jnp.int32, (n, n), 0)
    j = lax.broadcasted_iota(jnp.int32, (n, n), 1)
    return (j < i) if strict else (j <= i)


def _split_bf16(x, terms):
    parts = []
    rem = x
    for i in range(terms):
        p = rem.astype(BF16)
        parts.append(p)
        if i + 1 < terms:
            rem = rem - p.astype(F32)
    return parts


def _mask_mm(mask_bf, x, terms=3):
    return sum(jnp.dot(mask_bf, p, preferred_element_type=F32) for p in _split_bf16(x, terms))


def _mm_mask(x, mask_bf, terms=3):
    return sum(jnp.dot(p, mask_bf, preferred_element_type=F32) for p in _split_bf16(x, terms))


def _inproj_body(h_ref, nw_ref, w_ref, z_ref, ab_ref, xn_scr):
    @pl.when(pl.program_id(1) == 0)
    def _():
        xn_scr[...] = _rms(h_ref[...], nw_ref[...]).astype(BF16)

    res = jnp.dot(xn_scr[...], w_ref[...], preferred_element_type=F32)
    z_ref[...] = res.astype(z_ref.dtype)

    @pl.when(pl.program_id(1) == pl.num_programs(1) - 1)
    def _():
        c0 = COL_AB - (IN_WIDTH_PAD - res.shape[1])
        ab_ref[...] = res[:, c0:c0 + ab_ref.shape[1]]


def _inproj(h, norm_w, w_pad):
    m = h.shape[0]
    tm, tn = (1024 if m % 1024 == 0 else 512), IN_TILE_N
    assert IN_WIDTH_PAD % tn == 0 and COL_AB >= IN_WIDTH_PAD - tn
    return pl.pallas_call(
        _inproj_body,
        grid=(m // tm, IN_WIDTH_PAD // tn),
        in_specs=[pl.BlockSpec((tm, D_MODEL), lambda i, j: (i, 0)),
                  pl.BlockSpec((1, D_MODEL), lambda i, j: (0, 0)),
                  pl.BlockSpec((D_MODEL, tn), lambda i, j: (0, j))],
        out_specs=[pl.BlockSpec((tm, tn), lambda i, j: (i, j)), pl.BlockSpec((tm, 128), lambda i, j: (i, 0))],
        out_shape=[jax.ShapeDtypeStruct((m, IN_WIDTH_PAD), BF16), jax.ShapeDtypeStruct((m, 128), F32)],
        scratch_shapes=[pltpu.VMEM((tm, D_MODEL), BF16)],
        compiler_params=_cparams("parallel", "arbitrary"),
        name="norm_inproj",
    )(h, norm_w.reshape(1, D_MODEL), w_pad)


def _lru_body(zu_ref, zg_ref, cw_ref, vp_ref, wa_ref, wx_ref, o_ref, prev_scr, hc_scr):
    tt = zu_ref.shape[0]

    @pl.when(pl.program_id(1) == 0)
    def _():
        prev_scr[...] = jnp.zeros_like(prev_scr)
        hc_scr[...] = jnp.zeros_like(hc_scr)

    u = zu_ref[...].astype(F32)
    prev = prev_scr[...]
    cw = cw_ref[...]
    vp = vp_ref[...]
    xc = (cw[3:4] * u + cw[2:3] * _shift_prev(u, prev, 1) + cw[1:2] * _shift_prev(u, prev, 2)
          + cw[0:1] * _shift_prev(u, prev, 3) + vp[0:1])
    prev_scr[...] = u[tt - 8:]

    xcb = xc.astype(BF16)
    n_grp = LRU_WIDTH // LRU_GROUP
    pre_a = jnp.concatenate(
        [jnp.dot(xcb[:, g * LRU_GROUP:(g + 1) * LRU_GROUP], wa_ref[g], preferred_element_type=F32)
         for g in range(n_grp)], axis=1)
    pre_x = jnp.concatenate(
        [jnp.dot(xcb[:, g * LRU_GROUP:(g + 1) * LRU_GROUP], wx_ref[g], preferred_element_type=F32)
         for g in range(n_grp)], axis=1)
    r = jax.nn.sigmoid(pre_a + vp[1:2])
    i = jax.nn.sigmoid(pre_x + vp[2:3])
    log_a = (-LRU_C * r) * _softplus(-vp[3:4])
    a = jnp.exp(log_a)
    b = jnp.sqrt(-jnp.tanh(log_a) * (a * a + 1.0)) * (i * xc)

    d = 1
    while d < tt:
        b = a * _shift_fill(b, d, 0.0) + b
        a = a * _shift_fill(a, d, 1.0)
        d *= 2
    hcar = hc_scr[...]
    hval = b + a * hcar[0:1]
    hc_scr[...] = jnp.broadcast_to(hval[tt - 1:tt], hcar.shape)
    o_ref[...] = (hval * jax.nn.gelu(zg_ref[...].astype(F32))).astype(o_ref.dtype)


def _lru(z, bsz, seq, conv_w, vecs, wa_bd, wx_bd):
    m = z.shape[0]
    tt = 256
    nt = seq // tt
    rowmap = lambda c: (lambda b, t: (b * nt + t, c))
    const2 = lambda b, t: (0, 0)
    return pl.pallas_call(
        _lru_body,
        grid=(bsz, nt),
        in_specs=[
            pl.BlockSpec((tt, LRU_WIDTH), rowmap(COL_U // LRU_WIDTH)),
            pl.BlockSpec((tt, LRU_WIDTH), rowmap(COL_GATE // LRU_WIDTH)),
            pl.BlockSpec((4, LRU_WIDTH), const2),
            pl.BlockSpec((8, LRU_WIDTH), const2),
            pl.BlockSpec(wa_bd.shape, lambda b, t: (0, 0, 0)),
            pl.BlockSpec(wx_bd.shape, lambda b, t: (0, 0, 0)),
        ],
        out_specs=pl.BlockSpec((tt, LRU_WIDTH), rowmap(0)),
        out_shape=jax.ShapeDtypeStruct((m, LRU_WIDTH), BF16),
        scratch_shapes=[pltpu.VMEM((8, LRU_WIDTH), F32), pltpu.VMEM((8, LRU_WIDTH), F32)],
        compiler_params=_cparams("parallel", "arbitrary"),
        name="rglru",
    )(z, z, conv_w, vecs, wa_bd, wx_bd)


def _gdn_body(q_ref, k_ref, v_ref, zg_ref, ab_ref, abt_ref, cw_ref, gp_ref, gpt_ref, nw_ref,
              o_ref, pq_scr, pk_scr, pv_scr, st_scr):
    tt = q_ref.shape[0]
    c = GDN_CHUNK
    width = GDN_HEADS * GDN_DK

    @pl.when(pl.program_id(1) == 0)
    def _():
        for s in (pq_scr, pk_scr, pv_scr, st_scr):
            s[...] = jnp.zeros_like(s)

    cw = cw_ref[...]

    def conv_silu(x_ref, p_scr, w):
        x = x_ref[...].astype(F32)
        prev = p_scr[...]
        y = (w[3:4] * x + w[2:3] * _shift_prev(x, prev, 1) + w[1:2] * _shift_prev(x, prev, 2)
             + w[0:1] * _shift_prev(x, prev, 3))
        p_scr[...] = x[tt - 8:]
        return jax.nn.silu(y)

    q = conv_silu(q_ref, pq_scr, cw[:, 0:width])
    k = conv_silu(k_ref, pk_scr, cw[:, width:2 * width])
    v = conv_silu(v_ref, pv_scr, cw[:, 2 * width:])

    ab = ab_ref[...].astype(F32)
    gp = gp_ref[...]
    gpt = gpt_ref[...]
    g_all = -jnp.exp(gp[0:1]) * _softplus(ab + gp[1:2])
    beta_all = jax.nn.sigmoid(ab)

    incl = _tri(c, False)
    strict = _tri(c, True)
    cum_mask = jnp.concatenate([incl.astype(BF16), jnp.ones((c, c), BF16)], axis=0)
    triu_bf = (lax.broadcasted_iota(jnp.int32, (c, c), 0) <= lax.broadcasted_iota(jnp.int32, (c, c), 1)).astype(BF16)
    nw = nw_ref[...]

    n_chunks = tt // c
    heads = range(GDN_HEADS)
    inst = [(ci, h) for ci in range(n_chunks) for h in heads]
    sls = [slice(ci * c, (ci + 1) * c) for ci in range(n_chunks)]
    hss = [slice(h * GDN_DK, (h + 1) * GDN_DK) for h in heads]
    gcs = [_mask_mm(cum_mask, g_all[sl]) for sl in sls]
    gr_all = [_mm_mask(-jnp.exp(gpt[:, 0:1]) * _softplus(abt_ref[ci][0:GDN_HEADS] + gpt[:, 1:2]), triu_bf)
              for ci in range(n_chunks)]

    qs, ks, kbs, decays, egcs, kdecs, gtots, bcols = {}, {}, {}, {}, {}, {}, {}, {}
    for ci, h in inst:
        sl, hs = sls[ci], hss[h]
        qc, kc = q[sl, hs], k[sl, hs]
        qs[ci, h] = qc * lax.rsqrt(jnp.sum(qc * qc, axis=-1, keepdims=True) + RMS_EPS) * (GDN_DK ** -0.5)
        ks[ci, h] = kc * lax.rsqrt(jnp.sum(kc * kc, axis=-1, keepdims=True) + RMS_EPS)
        gcol = gcs[ci][:c, h:h + 1]
        g_last = gcs[ci][c:, h:h + 1]
        grow = gr_all[ci][h:h + 1, :]
        bcols[ci, h] = beta_all[sl, GDN_HEADS + h:GDN_HEADS + h + 1]
        decays[ci, h] = jnp.exp(jnp.where(incl, gcol - grow, -BIG))
        kbs[ci, h] = ks[ci, h] * bcols[ci, h]
        egcs[ci, h] = jnp.exp(gcol)
        kdecs[ci, h] = ks[ci, h] * jnp.exp(g_last - gcol)
        gtots[ci, h] = jnp.exp(jnp.concatenate([g_last] * (GDN_DK // c), axis=0))

    kk = {i: _mm_nt(kbs[i], ks[i]) for i in inst}
    qk = {i: _mm_nt(qs[i], ks[i]) * decays[i] for i in inst}
    p = {i: -jnp.where(strict, kk[i] * decays[i], 0.0) for i in inst}
    tm1 = dict(p)
    for _ in range(5):
        p = {i: _mm(p[i], p[i]) for i in inst}
        tp = {i: _mm(tm1[i], p[i]) for i in inst}
        tm1 = {i: tm1[i] + tp[i] + p[i] for i in inst}
    rhs = {i: jnp.concatenate([v[sls[i[0]], hss[i[1]]] * bcols[i], kbs[i] * egcs[i]], axis=1) for i in inst}
    tr = {i: _mm(tm1[i], rhs[i]) for i in inst}
    sol = {i: rhs[i] + tr[i] for i in inst}

    state = [st_scr[h] for h in heads]
    for ci in range(n_chunks):
        ws = [_mm(sol[ci, h][:, GDN_DV:], state[h]) for h in heads]
        qst = [_mm(qs[ci, h] * egcs[ci, h], state[h]) for h in heads]
        v_new = [sol[ci, h][:, :GDN_DV] - ws[h] for h in heads]
        qv = [_mm(qk[ci, h], v_new[h]) for h in heads]
        kv = [_mm_tn(kdecs[ci, h], v_new[h]) for h in heads]
        state = [state[h] * gtots[ci, h] + kv[h] for h in heads]
        for h in heads:
            o = qst[h] + qv[h]
            o = o * lax.rsqrt(jnp.mean(o * o, axis=-1, keepdims=True) + RMS_EPS) * nw
            o = o * jax.nn.silu(zg_ref[sls[ci], hss[h]].astype(F32))
            o_ref[sls[ci], hss[h]] = o.astype(o_ref.dtype)
    for h in heads:
        st_scr[h] = state[h]


def _gdn(z, ab, abt, bsz, seq, conv_w, gp, gpt, norm_w):
    m = z.shape[0]
    tt = 256
    nt = seq // tt
    width = GDN_HEADS * GDN_DK
    zblk = lambda c0: pl.BlockSpec((tt, width), lambda b, t: (b * nt + t, c0 // width))
    const2 = lambda b, t: (0, 0)
    return pl.pallas_call(
        _gdn_body,
        grid=(bsz, nt),
        in_specs=[
            zblk(COL_QKV), zblk(COL_QKV + width), zblk(COL_QKV + 2 * width), zblk(COL_ZG),
            pl.BlockSpec((tt, 128), lambda b, t: (b * nt + t, 0)),
            pl.BlockSpec((tt // GDN_CHUNK, 2 * GDN_HEADS, GDN_CHUNK), lambda b, t: (b * nt + t, 0, 0)),
            pl.BlockSpec((4, 3 * width), const2),
            pl.BlockSpec((8, 128), const2),
            pl.BlockSpec((8, 128), const2),
            pl.BlockSpec((1, 128), const2),
        ],
        out_specs=pl.BlockSpec((tt, width), lambda b, t: (b * nt + t, 0)),
        out_shape=jax.ShapeDtypeStruct((m, width), BF16),
        scratch_shapes=[pltpu.VMEM((8, width), F32)] * 3 + [pltpu.VMEM((GDN_HEADS, GDN_DK, GDN_DV), F32)],
        compiler_params=_cparams("parallel", "arbitrary"),
        name="gated_deltanet",
    )(z, z, z, z, ab, abt, conv_w, gp, gpt, norm_w)


def _rwkv_body(r_ref, k_ref, v_ref, lo_ref, mu_ref, vp_ref, wup_ref, aup_ref, gup_ref, o_ref,
               pr_scr, pk_scr, pv_scr, plo_scr, st_scr):
    tt = r_ref.shape[0]
    c = RWKV_CHUNK
    gw = RWKV_GROUP
    hd = RWKV_HD
    width = RWKV_WIDTH

    @pl.when(pl.program_id(1) == 0)
    def _():
        for s in (pr_scr, pk_scr, pv_scr, plo_scr, st_scr):
            s[...] = jnp.zeros_like(s)

    mu = mu_ref[...]

    def tshift(x_ref, p_scr, m):
        x = x_ref[...].astype(F32)
        xs = x + m * (_shift_prev(x, p_scr[...], 1) - x)
        p_scr[...] = x[tt - 8:]
        return xs

    r = tshift(r_ref, pr_scr, mu[:, 0:width])
    k = tshift(k_ref, pk_scr, mu[:, width:2 * width])
    v = tshift(v_ref, pv_scr, mu[:, 2 * width:3 * width])
    lo = tshift(lo_ref, plo_scr, mu[:, 3 * width:])
    vp = vp_ref[...]
    w0, a0, k_k, k_a, r_k, lnx_w, lnx_b = (vp[i:i + 1] for i in range(7))

    lane = lax.broadcasted_iota(jnp.int32, lo.shape, 1)
    lo_act = jnp.where(lane < 64, jnp.tanh(lo), jnp.where(lane < 128, lo, jax.nn.sigmoid(lo)))
    w_pre = _mm(lo_act, wup_ref[...])
    a_pre = _mm(lo_act, aup_ref[...])
    gate = _mm(lo_act, gup_ref[...])
    w_log = -_softplus(-(w0 + w_pre)) - 0.5
    lw = -jnp.exp(w_log)
    a = jax.nn.sigmoid(a0 + a_pre)

    bi = lax.broadcasted_iota(jnp.int32, (gw, gw), 0) // hd
    bj = lax.broadcasted_iota(jnp.int32, (gw, gw), 1) // hd
    bdmask = bi == bj
    ones_bd = bdmask.astype(BF16)

    def head_sum(x):
        return _mm_mask(x, ones_bd, terms=1)

    def bd(x):
        return jnp.where(bdmask, jnp.concatenate([x] * (gw // c), axis=0), 0.0)

    ti = lax.broadcasted_iota(jnp.int32, (c, gw), 0)
    sj = lax.broadcasted_iota(jnp.int32, (c, gw), 1) % c
    strict = sj < ti
    incl = sj <= ti
    tril_bf = _tri(c, False).astype(BF16)

    kk_all = k * k_k
    k2_all = k * (1.0 + (a - 1.0) * k_a)

    n_chunks = tt // c
    all_groups = list(range(width // gw))
    sls = [slice(ci * c, (ci + 1) * c) for ci in range(n_chunks)]
    gss = [slice(g * gw, (g + 1) * gw) for g in all_groups]
    cl_all = [_mask_mm(tril_bf, lw[sl]) for sl in sls]

    def at(x, i):
        return x[sls[i[0]], gss[i[1]]]

    half = len(all_groups) // 2
    for groups in (all_groups[:half], all_groups[half:]):
        _rwkv_groups(groups, n_chunks, c, hd, sls, gss, cl_all, at, head_sum, bd, bdmask, strict, incl,
                     kk_all, k2_all, a, lw, r, v, r_k, lnx_w, lnx_b, gate, st_scr, o_ref)


def _rwkv_groups(groups, n_chunks, c, hd, sls, gss, cl_all, at, head_sum, bd, bdmask, strict, incl,
                 kk_all, k2_all, a, lw, r, v, r_k, lnx_w, lnx_b, gate, st_scr, o_ref):
    inst = [(ci, g) for ci in range(n_chunks) for g in groups]
    kk_raw = {i: at(kk_all, i) for i in inst}
    kk_ss = {i: head_sum(kk_raw[i] * kk_raw[i]) for i in inst}
    a_h, b_h, k_h, br, w_end = {}, {}, {}, {}, {}
    for i in inst:
        cl = cl_all[i[0]][:, gss[i[1]]]
        kk = kk_raw[i] * lax.rsqrt(kk_ss[i] + RMS_EPS)
        wcum = jnp.exp(cl)
        inv_w = jnp.exp(-cl)
        a_h[i] = kk * at(a, i) * inv_w
        b_h[i] = kk * jnp.exp(cl - at(lw, i))
        k_h[i] = at(k2_all, i) * inv_w
        br[i] = jnp.concatenate([b_h[i], at(r, i) * wcum], axis=0)
        w_end[i] = wcum[c - 1:c]
    v_bd = {i: bd(at(v, i)) for i in inst}
    xa = {i: _mm_nt(br[i], bd(a_h[i])) for i in inst}
    xk = {i: _mm_nt(br[i], bd(k_h[i])) for i in inst}
    l_k = {i: jnp.where(strict, xk[i][:c], 0.0) for i in inst}
    ra = {i: jnp.where(incl, xa[i][c:], 0.0) for i in inst}
    rk = {i: jnp.where(incl, xk[i][c:], 0.0) for i in inst}
    p = {i: -jnp.where(strict, xa[i][:c], 0.0) for i in inst}
    tm1 = dict(p)
    for _ in range(5):
        p = {i: _mm(p[i], bd(p[i])) for i in inst}
        tp = {i: _mm(tm1[i], bd(p[i])) for i in inst}
        tm1 = {i: tm1[i] + tp[i] + p[i] for i in inst}
    lkv = {i: _mm(l_k[i], v_bd[i]) for i in inst}
    rkv = {i: _mm(rk[i], v_bd[i]) for i in inst}
    so = {i: head_sum(at(r, i) * at(k2_all, i) * r_k[:, gss[i[1]]]) for i in inst}

    state_t = {g: st_scr[g] for g in groups}
    for ci in range(n_chunks):
        brh = {g: _mm_nt(br[ci, g], state_t[g]) for g in groups}
        rhs = {g: brh[g][:c] + lkv[ci, g] for g in groups}
        tu = {g: _mm(tm1[ci, g], bd(rhs[g])) for g in groups}
        u = {g: rhs[g] + tu[g] for g in groups}
        rau = {g: _mm(ra[ci, g], bd(u[g])) for g in groups}
        upd = {g: _mm_tn(jnp.concatenate([u[g], at(v, (ci, g))], axis=0),
                         jnp.concatenate([-(a_h[ci, g] * w_end[ci, g]), k_h[ci, g] * w_end[ci, g]], axis=0))
               for g in groups}
        state_t = {g: w_end[ci, g] * state_t[g] + jnp.where(bdmask, upd[g], 0.0) for g in groups}
        o = {g: brh[g][c:] - rau[g] + rkv[ci, g] for g in groups}
        osum = {g: head_sum(o[g]) for g in groups}
        cen = {g: o[g] - osum[g] * (1.0 / hd) for g in groups}
        var = {g: head_sum(cen[g] * cen[g]) * (1.0 / hd) for g in groups}
        for g in groups:
            gs = gss[g]
            on = cen[g] * lax.rsqrt(var[g] + RWKV_GN_EPS) * lnx_w[:, gs] + lnx_b[:, gs]
            bonus = so[ci, g] * at(v, (ci, g))
            o_ref[sls[ci], gs] = ((on + bonus) * gate[sls[ci], gs]).astype(o_ref.dtype)
    for g in groups:
        st_scr[g] = state_t[g]


def _rwkv(z, bsz, seq, mu, vecs, wup_pad, aup_pad, gup_pad):
    m = z.shape[0]
    tt = 256
    nt = seq // tt
    gw = RWKV_GROUP
    width = RWKV_WIDTH
    zblk = lambda c0: pl.BlockSpec((tt, width), lambda b, t: (b * nt + t, c0 // width))
    lora_col = (COL_RWKV + 3 * width) // RWKV_LORA
    const2 = lambda b, t: (0, 0)
    return pl.pallas_call(
        _rwkv_body,
        grid=(bsz, nt),
        in_specs=[
            zblk(COL_RWKV), zblk(COL_RWKV + width), zblk(COL_RWKV + 2 * width),
            pl.BlockSpec((tt, RWKV_LORA), lambda b, t: (b * nt + t, lora_col)),
            pl.BlockSpec(mu.shape, const2),
            pl.BlockSpec((8, width), const2),
            pl.BlockSpec((RWKV_LORA, width), const2),
            pl.BlockSpec((RWKV_LORA, width), const2),
            pl.BlockSpec((RWKV_LORA, width), const2),
        ],
        out_specs=pl.BlockSpec((tt, width), lambda b, t: (b * nt + t, 0)),
        out_shape=jax.ShapeDtypeStruct((m, width), BF16),
        scratch_shapes=[pltpu.VMEM((8, width), F32)] * 3 + [pltpu.VMEM((8, RWKV_LORA), F32),
                                                            pltpu.VMEM((width // gw, gw, gw), F32)],
        compiler_params=_cparams("parallel", "arbitrary"),
        name="rwkv7",
    )(z, z, z, z, mu, vecs, wup_pad, aup_pad, gup_pad)


def _merge_body(h_ref, ya_ref, yb_ref, yc_ref, za_ref, zb_ref, zc_ref, mb_ref, pa_ref, pb_ref,
                pc_ref, wo_ref, nw_ref, hout_ref, xnt_ref):
    mb = mb_ref[...]
    ga = jax.nn.sigmoid(za_ref[...].astype(F32) + mb[:, 0:D_MODEL])
    gb = jax.nn.sigmoid(zb_ref[...].astype(F32) + mb[:, D_MODEL:2 * D_MODEL])
    gc = jax.nn.sigmoid(zc_ref[...].astype(F32) + mb[:, 2 * D_MODEL:])
    merged = (ga * jnp.dot(ya_ref[...], pa_ref[...], preferred_element_type=F32)
              + gb * jnp.dot(yb_ref[...], pb_ref[...], preferred_element_type=F32)
              + gc * jnp.dot(yc_ref[...], pc_ref[...], preferred_element_type=F32))
    h = h_ref[...] + _mm(merged, wo_ref[...])
    hout_ref[...] = h
    xnt_ref[...] = jnp.transpose(_rms(h, nw_ref[...])).astype(BF16)


def _merge(h, z, ya, yb, yc, merge_b, pa, pb, pc, wo, norm_w):
    m = h.shape[0]
    tm = 512 if m % 512 == 0 else 256
    row = lambda c: pl.BlockSpec((tm, D_MODEL), lambda i: (i, c))
    wspec = pl.BlockSpec((D_MODEL, D_MODEL), lambda i: (0, 0))
    mc = COL_MERGE // D_MODEL
    return pl.pallas_call(
        _merge_body,
        grid=(m // tm,),
        in_specs=[row(0), row(0), row(0), row(0), row(mc), row(mc + 1), row(mc + 2),
                  pl.BlockSpec((1, 3 * D_MODEL), lambda i: (0, 0)),
                  wspec, wspec, wspec, wspec,
                  pl.BlockSpec((1, D_MODEL), lambda i: (0, 0))],
        out_specs=[row(0), pl.BlockSpec((D_MODEL, tm), lambda i: (0, i))],
        out_shape=[jax.ShapeDtypeStruct((m, D_MODEL), F32), jax.ShapeDtypeStruct((D_MODEL, m), BF16)],
        compiler_params=_cparams("parallel"),
        name="merge_outproj",
    )(h, ya, yb, yc, z, z, z, merge_b, pa, pb, pc, wo, norm_w)


_CAND_SLABS = (
    (16, ((0, 0, 16),)),
    (16, ((1, 0, 8), (2, 8, 5), (4, 13, 3))),
    (16, ((3, 0, 4), (5, 4, 2), (6, 6, 2), (7, 8, 2), (8, 10, 1), (9, 11, 1), (10, 12, 1), (11, 13, 1),
          (12, 14, 1), (13, 15, 1))),
    (8, ((14, 0, 1), (15, 1, 1))),
)


def _odd_even_merge_sort_pairs(n):
    pairs = []
    p = 1
    while p < n:
        k = p
        while k >= 1:
            for j in range(k % p, n - k, 2 * k):
                for i in range(min(k, n - j - k)):
                    if (i + j) // (2 * p) == (i + j + k) // (2 * p):
                        pairs.append((i + j, i + j + k))
            k //= 2
        p *= 2
    return tuple(pairs)


_SORT_PAIRS = _odd_even_merge_sort_pairs(PEER_TOPK)


def _candidate_sums(a1, a2):
    slabs = []
    for nrows, pieces in _CAND_SLABS:
        base = a2[:nrows]
        rows = lax.broadcasted_iota(jnp.int32, base.shape, 0)
        out = None
        end = 0
        for p, off, cnt in pieces:
            val = a1[p:p + 1] + (base if off == 0 else pltpu.roll(base, off, 0))
            out = val if out is None else jnp.where(rows >= off, val, out)
            end = off + cnt
        if end < nrows:
            out = jnp.where(rows >= end, -BIG, out)
        slabs.append(out)
    return jnp.concatenate(slabs, axis=0)


def _count_leading(pred, thr):
    assert len(thr) == 16
    b8 = pred(thr[7])
    b4 = pred(jnp.where(b8, thr[11], thr[3]))
    b2 = pred(jnp.where(b8, jnp.where(b4, thr[13], thr[9]), jnp.where(b4, thr[5], thr[1])))
    lo = jnp.where(b4, jnp.where(b2, thr[6], thr[4]), jnp.where(b2, thr[2], thr[0]))
    hi = jnp.where(b4, jnp.where(b2, thr[14], thr[12]), jnp.where(b2, thr[10], thr[8]))
    b1 = pred(jnp.where(b8, hi, lo))
    cnt = (jnp.where(b8, 8.0, 0.0) + jnp.where(b4, 4.0, 0.0)) + (jnp.where(b2, 2.0, 0.0) + jnp.where(b1, 1.0, 0.0))
    return jnp.where(pred(thr[15]), 16.0, cnt)


def _peer_topk_body(xnt_ref, wqt_ref, keys_ref, n_ref, e1_ref, r2_ref, e2_ref, q_scr):
    tt = xnt_ref.shape[1]
    nk = PEER_NKEYS
    topk = PEER_TOPK
    q_scr[...] = jnp.dot(wqt_ref[...], xnt_ref[...], preferred_element_type=F32)
    lanes = 128
    sub = 8
    assert nk == topk * sub
    row8 = lax.broadcasted_iota(jnp.int32, (sub, lanes), 0)

    def top_sorted(s):
        v = [s[k * sub:(k + 1) * sub] for k in range(topk)]
        for i, j in _SORT_PAIRS:
            v[i], v[j] = jnp.maximum(v[i], v[j]), jnp.minimum(v[i], v[j])
        for shift in (4, 2, 1):
            other = [pltpu.roll(x, shift, 0) for x in v]
            v = [jnp.maximum(v[k], other[topk - 1 - k]) for k in range(topk)]
            d = topk // 2
            while d >= 1:
                for i in range(topk):
                    if i & d == 0:
                        v[i], v[i + d] = jnp.maximum(v[i], v[i + d]), jnp.minimum(v[i], v[i + d])
                d //= 2
        return v

    def compact(v):
        tiles = []
        for t0 in range(0, topk, sub):
            out = v[t0]
            for k in range(1, sub):
                out = jnp.where(row8 == k, v[t0 + k], out)
            tiles.append(out)
        return jnp.concatenate(tiles, axis=0)

    def head(h, carry):
        o1 = pl.multiple_of(h * (2 * nk), 2 * nk)
        s1 = _mm(keys_ref[2 * h], q_scr[pl.ds(o1, nk), :])
        s2 = _mm(keys_ref[2 * h + 1], q_scr[pl.ds(o1 + nk, nk), :])
        chunks = [slice(l0, l0 + lanes) for l0 in range(0, tt, lanes)]
        a1 = [top_sorted(s1[:, ls]) for ls in chunks]
        a2 = [top_sorted(s2[:, ls]) for ls in chunks]
        a1c = jnp.concatenate([compact(v) for v in a1], axis=1)
        a2c = jnp.concatenate([compact(v) for v in a2], axis=1)
        cand = _candidate_sums(a1c, a2c)
        cmax = a1c[0:1] + a2c[0:1]

        def cbody(rnd, carry):
            cnd, zsum, c_in, c_out = carry
            mx = jnp.max(cnd, axis=0, keepdims=True)
            zsum = zsum + jnp.where(rnd < topk, jnp.exp(mx - cmax), 0.0)
            c_in = jnp.where(rnd == topk - 1, mx, c_in)
            c_out = jnp.where(rnd == topk, mx, c_out)
            return jnp.where(cnd == mx, -BIG, cnd), zsum, c_in, c_out

        zero = jnp.zeros((1, tt), F32)
        _, zsum, c_in, c_out = lax.fori_loop(0, topk + 1, cbody, (cand, zero, zero, zero))
        tau = 0.5 * (c_in + c_out)
        inv_z = 1.0 / zsum
        pack = 16

        def rep(x):
            return jnp.concatenate([x] * (pack // sub), axis=0)

        for ci, ls in enumerate(chunks):
            tau8 = jnp.broadcast_to(tau[:, ls], (sub, lanes))
            need = [rep(tau8 - a2[ci][qq]) for qq in range(topk)]
            val2 = [rep(a2[ci][qq]) for qq in range(topk)]
            last1 = rep(a1[ci][topk - 1])
            top1 = rep(a1[ci][0])
            top2 = val2[0]
            for k in range(nk // pack):
                rows = slice(k * pack, (k + 1) * pack)
                x1 = s1[rows, ls]
                x2 = s2[rows, ls]
                n_sel = _count_leading(lambda t: x1 >= t, need)
                r2 = _count_leading(lambda t: x2 < t, val2)
                n_ref[h, rows, ls] = jnp.where(x1 >= last1, n_sel, 0.0)
                e1_ref[h, rows, ls] = jnp.exp(x1 - top1)
                r2_ref[h, rows, ls] = r2.astype(BF16)
                e2_ref[h, rows, ls] = (jnp.exp(x2 - top2) * inv_z[:, ls]).astype(BF16)
        return carry

    lax.fori_loop(0, PEER_HEADS, head, 0)


def _peer_topk(xnt, wqt, keys):
    m = xnt.shape[1]
    tt = 512 if m % 512 == 0 else 256
    shape = (PEER_HEADS, PEER_NKEYS, m)
    ospec = pl.BlockSpec((PEER_HEADS, PEER_NKEYS, tt), lambda i: (0, 0, i))
    return pl.pallas_call(
        _peer_topk_body,
        grid=(m // tt,),
        in_specs=[pl.BlockSpec((D_MODEL, tt), lambda i: (0, i)),
                  pl.BlockSpec(wqt.shape, lambda i: (0, 0)),
                  pl.BlockSpec(keys.shape, lambda i: (0, 0, 0))],
        out_specs=[ospec] * 4,
        out_shape=[jax.ShapeDtypeStruct(shape, F32), jax.ShapeDtypeStruct(shape, F32),
                   jax.ShapeDtypeStruct(shape, BF16), jax.ShapeDtypeStruct(shape, BF16)],
        scratch_shapes=[pltpu.VMEM((wqt.shape[0], tt), F32)],
        compiler_params=_cparams("parallel"),
        name="peer_topk",
    )(xnt, wqt, keys)


PEER_TOKEN_CHUNK = 256


def _peer_dense_body(last_layer, u_ref, xnt_ref, vt_ref, n_ref, e1_ref, r2_ref, e2_ref, h_ref, nw_ref,
                     hout_ref, acc_scr):
    nk = PEER_NKEYS
    tt = xnt_ref.shape[1]
    tc = PEER_TOKEN_CHUNK
    n_blk = u_ref.shape[0] // nk
    pack = 16

    @pl.when(pl.program_id(1) == 0)
    def _():
        acc_scr[...] = jnp.zeros_like(acc_scr)

    def rows_bf16(ref, h, ii, ls):
        row = jnp.broadcast_to(ref[h, ii:ii + 1, ls], (pack, tc)).astype(BF16)
        return jnp.concatenate([row] * (nk // pack), axis=0)

    def select_weights(ii, c0):
        ls = slice(c0, c0 + tc)
        w = None
        for h in range(PEER_HEADS):
            sel = jnp.where(r2_ref[h, :, ls] < rows_bf16(n_ref, h, ii, ls), e2_ref[h, :, ls],
                            jnp.zeros((), BF16))
            term = sel * rows_bf16(e1_ref, h, ii, ls)
            w = term if w is None else w + term
        return w

    def gelu_tanh(x):
        c = 0.7978845608028654
        inner = x * ((x * x) * (c * 0.044715) + c)
        hx = 0.5 * x
        return hx * jnp.tanh(inner) + hx

    starts = list(range(0, tt, tc))
    pre = [jnp.dot(u_ref[...], xnt_ref[:, c0:c0 + tc], preferred_element_type=F32) for c0 in starts]
    for idx, c0 in enumerate(starts):
        act = gelu_tanh(pre[idx].astype(BF16))
        a = jnp.concatenate([act[ii * nk:(ii + 1) * nk] * select_weights(ii, c0) for ii in range(n_blk)], axis=0)
        acc_scr[:, c0:c0 + tc] += jnp.dot(vt_ref[...], a, preferred_element_type=F32)

    @pl.when(pl.program_id(1) == pl.num_programs(1) - 1)
    def _():
        res = h_ref[...] + jnp.transpose(acc_scr[...])
        hout_ref[...] = _rms(res, nw_ref[...]) if last_layer else res


def _peer_dense(xnt, u_bf, vt_bf, n_sel, e1, r2, e2, h, final_norm_w, last_layer, tt):
    m = xnt.shape[1]
    te = 2048
    ib = te // PEER_NKEYS
    sel_i = pl.BlockSpec((PEER_HEADS, ib, tt), lambda i, e: (0, e, i))
    sel_all = pl.BlockSpec((PEER_HEADS, PEER_NKEYS, tt), lambda i, e: (0, 0, i))
    return pl.pallas_call(
        functools.partial(_peer_dense_body, last_layer),
        grid=(m // tt, PEER_EXPERTS // te),
        in_specs=[pl.BlockSpec((te, D_MODEL), lambda i, e: (e, 0)),
                  pl.BlockSpec((D_MODEL, tt), lambda i, e: (0, i)),
                  pl.BlockSpec((D_MODEL, te), lambda i, e: (0, e)),
                  sel_i, sel_i, sel_all, sel_all,
                  pl.BlockSpec((tt, D_MODEL), lambda i, e: (i, 0)),
                  pl.BlockSpec((1, D_MODEL), lambda i, e: (0, 0))],
        out_specs=pl.BlockSpec((tt, D_MODEL), lambda i, e: (i, 0)),
        out_shape=jax.ShapeDtypeStruct((m, D_MODEL), F32),
        scratch_shapes=[pltpu.VMEM((D_MODEL, tt), F32)],
        compiler_params=_cparams("parallel", "arbitrary"),
        name="peer_dense",
    )(u_bf, xnt, vt_bf, n_sel, e1, r2, e2, h, final_norm_w.reshape(1, D_MODEL))


def _table_body(transpose, x_ref, o_ref):
    x = x_ref[...]
    o_ref[...] = (jnp.transpose(x) if transpose else x).astype(o_ref.dtype)


def _expert_table(tables, l, transpose):
    _, rows, cols = tables.shape
    tr = 512
    return pl.pallas_call(
        functools.partial(_table_body, transpose),
        grid=(rows // tr,),
        in_specs=[pl.BlockSpec((None, tr, cols), lambda i: (l, i, 0))],
        out_specs=pl.BlockSpec((cols, tr), lambda i: (0, i)) if transpose else pl.BlockSpec((tr, cols), lambda i: (i, 0)),
        out_shape=jax.ShapeDtypeStruct((cols, rows) if transpose else (rows, cols), BF16),
        compiler_params=_cparams("parallel"),
        name="expert_table",
    )(tables)


def _pad_rows(rows, width, n_rows=8):
    flat = [jnp.pad(r.reshape(-1).astype(F32), (0, width - r.size)) for r in rows]
    return jnp.pad(jnp.stack(flat), ((0, n_rows - len(rows)), (0, 0)))


def _chunk_rows(ab):
    m, n = ab.shape
    return jnp.transpose(ab.reshape(m // GDN_CHUNK, GDN_CHUNK, n), (0, 2, 1)).astype(F32)


def _block_diag_groups(w):
    per = LRU_GROUP // LRU_BLOCK_DIM
    n_grp = w.shape[0] // per
    wg = w.reshape(n_grp, per, LRU_BLOCK_DIM, LRU_BLOCK_DIM)
    eye = jnp.eye(per, dtype=w.dtype)
    out = wg[:, :, :, None, :] * eye[None, :, None, :, None]
    return out.reshape(n_grp, LRU_GROUP, LRU_GROUP).astype(BF16)


_W_IN_PIECES = ((0, 6144, 0), (9488, 12560, COL_MERGE), (6160, 9488, COL_RWKV), (6144, 6160, COL_AB))


def _w_in_body(w_ref, o_ref):
    o_ref[:, COL_AB:] = jnp.zeros((o_ref.shape[0], o_ref.shape[1] - COL_AB), o_ref.dtype)
    for a, b, d in _W_IN_PIECES:
        o_ref[:, d:d + b - a] = w_ref[:, a:b].astype(o_ref.dtype)


def _prep_w_in(w_all, l):
    _, rows, cols = w_all.shape
    tr = 256
    return pl.pallas_call(
        _w_in_body,
        grid=(rows // tr,),
        in_specs=[pl.BlockSpec((None, tr, cols), lambda i: (l, i, 0))],
        out_specs=pl.BlockSpec((tr, IN_WIDTH_PAD), lambda i: (i, 0)),
        out_shape=jax.ShapeDtypeStruct((rows, IN_WIDTH_PAD), BF16),
        compiler_params=_cparams("parallel"),
        name="w_in_layout",
    )(w_all)


def _layer(l, h, bsz, seq, p):
    z, ab = _inproj(h, p["norm_mix_w"][l], _prep_w_in(p["w_in_bf16"], l))

    lru_vecs = _pad_rows([p["lru_conv_b"][l], p["lru_b_a"][l], p["lru_b_x"][l], p["lru_lambda"][l]], LRU_WIDTH)
    ya = _lru(z, bsz, seq, p["lru_conv_w"][l], lru_vecs,
              _block_diag_groups(p["lru_w_a"][l]), _block_diag_groups(p["lru_w_x"][l]))

    abt = _chunk_rows(ab[:, :2 * GDN_HEADS])
    gp = _pad_rows([p["gdn_a_log"][l], p["gdn_dt_bias"][l]], 128)
    gpt = jnp.pad(jnp.stack([p["gdn_a_log"][l], p["gdn_dt_bias"][l]], axis=1), ((0, 0), (0, 126)))
    yb = _gdn(z, ab, abt, bsz, seq, p["gdn_conv_w"][l], gp, gpt, p["gdn_norm_w"][l].reshape(1, GDN_DV))

    rw_vecs = _pad_rows([p["rwkv_w0"][l], p["rwkv_a0"][l], p["rwkv_k_k"][l], p["rwkv_k_a"][l],
                         p["rwkv_r_k"][l], p["rwkv_lnx_w"][l], p["rwkv_lnx_b"][l]], RWKV_WIDTH)
    wup = jnp.pad(p["rwkv_w_up"][l], ((0, 192), (0, 0))).astype(BF16)
    aup = jnp.pad(p["rwkv_a_up"][l], ((64, 128), (0, 0))).astype(BF16)
    gup = jnp.pad(p["rwkv_g_up"][l], ((128, 0), (0, 0))).astype(BF16)
    yc = _rwkv(z, bsz, seq, p["rwkv_mu"][l].reshape(1, -1), rw_vecs, wup, aup, gup)

    h, xnt = _merge(h, z, ya, yb, yc, p["merge_b"][l].reshape(1, -1), p["p_lru"][l].astype(BF16),
                    p["p_gdn"][l].astype(BF16), p["p_rwkv"][l].astype(BF16), p["w_out"][l].astype(BF16),
                    p["norm_ffn_w"][l].reshape(1, D_MODEL))

    wqt = jnp.transpose(p["peer_wq"][l]).astype(BF16)
    keys = p["peer_keys"][l].reshape(2 * PEER_HEADS, PEER_NKEYS, PEER_HALF).astype(BF16)
    n_sel, e1, r2, e2 = _peer_topk(xnt, wqt, keys)
    tt = 512 if xnt.shape[1] % 512 == 0 else 256
    return _peer_dense(xnt, _expert_table(p["peer_u"], l, False), _expert_table(p["peer_v"], l, True),
                       n_sel, e1, r2, e2, h, p["final_norm_w"], l == DEPTH - 1, tt)


def kernel(x, norm_mix_w, norm_ffn_w, final_norm_w, w_in, lru_conv_w, lru_conv_b, lru_w_a, lru_b_a,
           lru_w_x, lru_b_x, lru_lambda, gdn_conv_w, gdn_a_log, gdn_dt_bias, gdn_norm_w, rwkv_mu, rwkv_w0,
           rwkv_w_up, rwkv_a0, rwkv_a_up, rwkv_g_up, rwkv_k_k, rwkv_k_a, rwkv_r_k, rwkv_lnx_w, rwkv_lnx_b,
           merge_b, p_lru, p_gdn, p_rwkv, w_out, peer_wq, peer_keys, peer_u, peer_v):
    p = dict(locals())
    p["w_in_bf16"] = w_in.astype(BF16)
    bsz, seq, dim = x.shape
    h = x.reshape(bsz * seq, dim)
    for l in range(DEPTH):
        h = _layer(l, h, bsz, seq, p)
    return h.reshape(bsz, seq, dim)
```
